```python
import jax, jax.numpy as jnp
from jax import lax
import numpy as np

D_MODEL = 1024
BATCH = 4
SEQ = 4096
DEPTH = 1
DEC_BATCH = 128
DEC_SEQ = 8
PAST_LEN = 16384
PAGE_SIZE = 128

CHUNK = 128
A_GROUPS = 4
A_WIDTH = D_MODEL // 2
A_CH = A_WIDTH // A_GROUPS
N_HEADS = 8
N_KV = 2
Q_PER_KV = N_HEADS // N_KV
HEAD_DIM = 64
B_WIDTH = N_HEADS * HEAD_DIM
KV_WIDTH = N_KV * HEAD_DIM
WINDOW = 128
MIX_WIDTH = A_WIDTH + B_WIDTH
IN_COLS = 2 * A_WIDTH + B_WIDTH + 2 * KV_WIDTH
N_EXPERT_GROUPS = 4
EXPERTS_PER_GROUP = 4
N_EXPERTS = N_EXPERT_GROUPS * EXPERTS_PER_GROUP
TOP_K = 2
D_EXPERT = D_MODEL // 4
EPS = 1e-6

kernel_name = "hymba_chunkmlp_swa_sink_hiermoe_step"


def rms_norm(x, g):
    xf = x.astype(jnp.float32)
    y = xf * lax.rsqrt(jnp.mean(xf * xf, axis=-1, keepdims=True) + EPS)
    return (y * g.astype(jnp.float32)).astype(x.dtype)


def alibi_slopes():
    h = jnp.arange(1, N_HEADS + 1, dtype=jnp.float32)
    return (2.0 ** (-8.0 * h / N_HEADS)).reshape(N_KV, Q_PER_KV, 1, 1)


def project(x, g_norm, w_in, g_v, g_q, g_k):
    lead = x.shape[:-1]
    z = rms_norm(x, g_norm) @ w_in
    i1 = A_WIDTH
    i2 = 2 * A_WIDTH
    i3 = i2 + B_WIDTH
    i4 = i3 + KV_WIDTH
    u = jax.nn.gelu(z[..., :i1]).reshape(*lead, A_GROUPS, A_CH)
    va = rms_norm(jax.nn.gelu(z[..., i1:i2]).reshape(*lead, A_GROUPS, A_CH), g_v)
    q = rms_norm(z[..., i2:i3].reshape(*lead, N_HEADS, HEAD_DIM), g_q)
    k = rms_norm(z[..., i3:i4].reshape(*lead, N_KV, HEAD_DIM), g_k)
    v = z[..., i4:].reshape(*lead, N_KV, HEAD_DIM)
    return u, va, q, k, v


def spatial_weights(w_s, n):
    return jnp.where(jnp.tril(jnp.ones((n, n), dtype=bool)), w_s[:, :n, :n], 0.0)


def chunk_mlp_prompt(u, va, w_s, b_s):
    bsz, slen = u.shape[:2]
    vc = va.reshape(bsz, slen // CHUNK, CHUNK, A_GROUPS, A_CH)
    mixed = jnp.einsum('gts,bnsgc->bntgc', spatial_weights(w_s, CHUNK), vc) + b_s.T[:, :, None]
    return (u * mixed.reshape(u.shape)).reshape(bsz, slen, A_WIDTH)


def chunk_mlp_sample(u, va, w_s, b_s):
    bsz, L = u.shape[:2]
    mixed = jnp.einsum('gts,bsgc->btgc', spatial_weights(w_s, L), va) + b_s[:, :L].T[:, :, None]
    return (u * mixed).reshape(bsz, L, A_WIDTH)


def sink_attention(q, k, v, dist, valid, sinks):
    s = jnp.einsum('...qkgd,...skd->...kgqs', q.astype(jnp.float32), k.astype(jnp.float32))
    s = s * (HEAD_DIM ** -0.5) - alibi_slopes() * dist.astype(jnp.float32)
    s = jnp.where(valid, s, -jnp.inf)
    sink = sinks.astype(jnp.float32).reshape(N_KV, Q_PER_KV, 1, 1)
    m = jnp.maximum(jnp.max(s, axis=-1, keepdims=True), sink)
    p = jnp.exp(s - m)
    p = p / (jnp.sum(p, axis=-1, keepdims=True) + jnp.exp(sink - m))
    return jnp.einsum('...kgqs,...skd->...qkgd', p.astype(v.dtype), v)


def swa_prompt(q, k, v, sinks):
    bsz, slen = q.shape[:2]
    nb = slen // WINDOW
    qb = q.reshape(bsz, nb, WINDOW, N_KV, Q_PER_KV, HEAD_DIM)

    def band(t):
        tb = t.reshape(bsz, nb, WINDOW, N_KV, HEAD_DIM)
        prev = jnp.concatenate([jnp.zeros_like(tb[:, :1]), tb[:, :-1]], axis=1)
        return jnp.concatenate([prev, tb], axis=2)

    t_idx = jnp.arange(WINDOW)[:, None]
    s_idx = jnp.arange(2 * WINDOW)[None, :]
    dist = t_idx + WINDOW - s_idx
    blk = jnp.arange(nb)[:, None, None]
    valid = (dist >= 0) & (dist < WINDOW) & ((blk > 0) | (s_idx >= WINDOW))
    o = sink_attention(qb, band(k), band(v), dist, valid[:, None, None], sinks)
    return o.reshape(bsz, slen, B_WIDTH)


def swa_sample(q, k_new, v_new, ck, cv, sinks):
    bsz, L = q.shape[:2]
    w_len = ck.shape[1]
    kk = jnp.concatenate([ck, k_new], axis=1)
    vv = jnp.concatenate([cv, v_new], axis=1)
    qpos = jnp.arange(L)[:, None]
    kpos = jnp.arange(w_len + L)[None, :] - w_len
    dist = qpos - kpos
    valid = (dist >= 0) & (dist < WINDOW)
    o = sink_attention(q.reshape(bsz, L, N_KV, Q_PER_KV, HEAD_DIM), kk, vv, dist, valid, sinks)
    return o.reshape(bsz, L, B_WIDTH), kk[:, -w_len:], vv[:, -w_len:]


def hier_moe(x, w_coarse, b_coarse, w_fine, b_fine, w_gate, w_up, w_down):
    lead = x.shape[:-1]
    lc = (x @ w_coarse).astype(jnp.float32) + b_coarse.astype(jnp.float32)
    g_idx = jnp.argmax(lc, axis=-1)
    p_g = jnp.take_along_axis(jax.nn.softmax(lc, axis=-1), g_idx[..., None], axis=-1)
    lf = ((x @ w_fine).astype(jnp.float32) + b_fine.astype(jnp.float32)).reshape(*lead, N_EXPERT_GROUPS, EXPERTS_PER_GROUP)
    lf_sel = jnp.take_along_axis(lf, g_idx[..., None, None], axis=-2)[..., 0, :]
    top_v, top_i = lax.top_k(lf_sel, TOP_K)
    gates = jax.nn.softmax(top_v, axis=-1) * p_g
    e_idx = g_idx[..., None] * EXPERTS_PER_GROUP + top_i
    combine = jnp.sum(jax.nn.one_hot(e_idx, N_EXPERTS, dtype=jnp.float32) * gates[..., None], axis=-2)
    h = jax.nn.silu(jnp.einsum('...d,edf->...ef', x, w_gate)) * jnp.einsum('...d,edf->...ef', x, w_up)
    h = h * combine.astype(h.dtype)[..., None]
    return jnp.einsum('...ef,efd->...d', h, w_down)


def merge_and_ffn(x, a_out, b_out, g_out_a, g_out_b, w_out, g_ffn, w_coarse, b_coarse, w_fine, b_fine, w_gate, w_up, w_down):
    mix = jnp.concatenate([rms_norm(a_out, g_out_a), rms_norm(b_out, g_out_b)], axis=-1)
    h = x + mix @ w_out
    return h + hier_moe(rms_norm(h, g_ffn), w_coarse, b_coarse, w_fine, b_fine, w_gate, w_up, w_down)


def setup_inputs(seed: int = 0) -> dict:
    key = jax.random.key(seed)
    ks = jax.random.split(key, 24)
    f32 = jnp.float32
    win = min(WINDOW, PAST_LEN)

    def nrm(k, shape, scale=1.0):
        return jax.random.normal(k, shape, f32) * scale

    def gain(k, shape):
        return 1.0 + 0.02 * jax.random.normal(k, shape, f32)

    return {
        "x_prompt": nrm(ks[0], (BATCH, SEQ, D_MODEL)),
        "x_sample": nrm(ks[1], (DEC_BATCH, DEC_SEQ, D_MODEL)),
        "cache_k": nrm(ks[2], (DEPTH, DEC_BATCH, win, N_KV, HEAD_DIM)),
        "cache_v": nrm(ks[3], (DEPTH, DEC_BATCH, win, N_KV, HEAD_DIM)),
        "g_attn_norm": gain(ks[4], (DEPTH, D_MODEL)),
        "w_in": nrm(ks[5], (DEPTH, D_MODEL, IN_COLS), D_MODEL ** -0.5),
        "g_v_a": gain(ks[6], (DEPTH, A_GROUPS, A_CH)),
        "w_spatial": nrm(ks[7], (DEPTH, A_GROUPS, CHUNK, CHUNK), CHUNK ** -0.5),
        "b_spatial": 1.0 + 0.02 * jax.random.normal(ks[8], (DEPTH, A_GROUPS, CHUNK), f32),
        "g_q": gain(ks[9], (DEPTH, HEAD_DIM)),
        "g_k": gain(ks[10], (DEPTH, HEAD_DIM)),
        "attn_sinks": nrm(ks[11], (DEPTH, N_HEADS)),
        "g_out_a": gain(ks[12], (DEPTH, A_WIDTH)),
        "g_out_b": gain(ks[13], (DEPTH, B_WIDTH)),
        "w_out": nrm(ks[14], (DEPTH, MIX_WIDTH, D_MODEL), MIX_WIDTH ** -0.5),
        "g_ffn_norm": gain(ks[15], (DEPTH, D_MODEL)),
        "w_coarse": nrm(ks[16], (DEPTH, D_MODEL, N_EXPERT_GROUPS), D_MODEL ** -0.5),
        "b_coarse": nrm(ks[17], (DEPTH, N_EXPERT_GROUPS), 0.01),
        "w_fine": nrm(ks[18], (DEPTH, D_MODEL, N_EXPERTS), D_MODEL ** -0.5),
        "b_fine": nrm(ks[19], (DEPTH, N_EXPERTS), 0.01),
        "w_gate": nrm(ks[20], (DEPTH, N_EXPERTS, D_MODEL, D_EXPERT), D_MODEL ** -0.5),
        "w_up": nrm(ks[21], (DEPTH, N_EXPERTS, D_MODEL, D_EXPERT), D_MODEL ** -0.5),
        "w_down": nrm(ks[22], (DEPTH, N_EXPERTS, D_EXPERT, D_MODEL), D_EXPERT ** -0.5),
    }


def reference(x_prompt, x_sample, cache_k, cache_v, g_attn_norm, w_in, g_v_a, w_spatial, b_spatial, g_q, g_k, attn_sinks, g_out_a, g_out_b, w_out, g_ffn_norm, w_coarse, b_coarse, w_fine, b_fine, w_gate, w_up, w_down):
    hp, hs = x_prompt, x_sample
    kp_list, vp_list, ks_list, vs_list, cv_list = [], [], [], [], []
    for l in range(DEPTH):
        moe_w = (g_ffn_norm[l], w_coarse[l], b_coarse[l], w_fine[l], b_fine[l], w_gate[l], w_up[l], w_down[l])
        u, va, q, k, v = project(hp, g_attn_norm[l], w_in[l], g_v_a[l], g_q[l], g_k[l])
        a_out = chunk_mlp_prompt(u, va, w_spatial[l], b_spatial[l])
        b_out = swa_prompt(q, k, v, attn_sinks[l])
        kp_list.append(k[:, -WINDOW:])
        vp_list.append(v[:, -WINDOW:])
        hp = merge_and_ffn(hp, a_out, b_out, g_out_a[l], g_out_b[l], w_out[l], *moe_w)
        u, va, q, k, v = project(hs, g_attn_norm[l], w_in[l], g_v_a[l], g_q[l], g_k[l])
        a_out = chunk_mlp_sample(u, va, w_spatial[l], b_spatial[l])
        b_out, k_buf, v_buf = swa_sample(q, k, v, cache_k[l], cache_v[l], attn_sinks[l])
        ks_list.append(k_buf)
        vs_list.append(v_buf)
        cv_list.append(va)
        hs = merge_and_ffn(hs, a_out, b_out, g_out_a[l], g_out_b[l], w_out[l], *moe_w)
    win_k_prompt = jnp.stack(kp_list, axis=0)
    win_v_prompt = jnp.stack(vp_list, axis=0)
    win_k_sample = jnp.stack(ks_list, axis=0)
    win_v_sample = jnp.stack(vs_list, axis=0)
    chunk_v_sample = jnp.stack(cv_list, axis=0)
    return (hp, hs, win_k_prompt, win_v_prompt, win_k_sample, win_v_sample, chunk_v_sample)
```

```python
import functools

import numpy as np
import jax
import jax.numpy as jnp
from jax import lax
from jax.experimental import pallas as pl
from jax.experimental.pallas import tpu as pltpu

D_MODEL = 1024
CHUNK = 128
A_GROUPS = 4
A_WIDTH = 512
A_CH = 128
N_HEADS = 8
N_KV = 2
Q_PER_KV = 4
HEAD_DIM = 64
B_WIDTH = 512
KV_WIDTH = 128
WINDOW = 128
IN_COLS = 2 * A_WIDTH + B_WIDTH + 2 * KV_WIDTH
N_EXPERT_GROUPS = 4
EXPERTS_PER_GROUP = 4
N_EXPERTS = 16
D_EXPERT = 256
EPS = 1e-6

LANES = 128
PAIR = 2 * CHUNK
ROUTER_LANES = 128
PROMPT_TILE = 512
SAMPLE_TILE = 256
VMEM_LIMIT_BYTES = 60 * 1024 * 1024

F32 = jnp.float32
BF16 = jnp.bfloat16
NEG_INF = float("-inf")


def _q_perm():
    perm = np.zeros((B_WIDTH,), np.int32)
    for j in range(Q_PER_KV):
        for c in range(LANES):
            kh, d = divmod(c, HEAD_DIM)
            perm[j * LANES + c] = (kh * Q_PER_KV + j) * HEAD_DIM + d
    return perm


def _slopes():
    return np.array([2.0 ** (-8.0 * (h + 1) / N_HEADS) for h in range(N_HEADS)], np.float64)


def _prompt_bias():
    t = np.arange(WINDOW)[:, None]
    s = np.arange(2 * WINDOW)[None, :]
    dist = t + WINDOW - s
    valid = (dist >= 0) & (dist < WINDOW)
    sl = _slopes()
    out = np.full((N_KV, Q_PER_KV * WINDOW, 2 * WINDOW), -np.inf, np.float32)
    for kh in range(N_KV):
        for g in range(Q_PER_KV):
            b = np.where(valid, -sl[kh * Q_PER_KV + g] * dist, -np.inf)
            out[kh, g * WINDOW:(g + 1) * WINDOW] = b
    return out


def _sample_bias(nb, dec):
    sl = _slopes()
    t = np.arange(dec)[:, None]
    j = np.arange(WINDOW)[None, :]
    dist_c = t + WINDOW - j
    valid_c = (dist_c >= 0) & (dist_c < WINDOW)
    bc = np.full((N_HEADS * dec, WINDOW), -np.inf, np.float32)
    tp = np.arange(dec)[None, :]
    dist_n = t - tp
    valid_n = dist_n >= 0
    bn = np.full((nb, N_HEADS * dec, nb * dec), -np.inf, np.float32)
    for h in range(N_HEADS):
        bc[h * dec:(h + 1) * dec] = np.where(valid_c, -sl[h] * dist_c, -np.inf)
        blk = np.where(valid_n, -sl[h] * dist_n, -np.inf)
        for b in range(nb):
            bn[b, h * dec:(h + 1) * dec, b * dec:(b + 1) * dec] = blk
    return bc, bn


def _seg_ones(width, seg):
    i = np.arange(width)
    return (i[:, None] // seg == i[None, :] // seg).astype(np.float32)


def _dot(a, b):
    return jnp.dot(a, b, preferred_element_type=F32)


def _dot_nt(a, b):
    return lax.dot_general(a, b, (((1,), (1,)), ((), ())), preferred_element_type=F32)


def _rms(x, g):
    ms = jnp.mean(x * x, axis=-1, keepdims=True)
    return (x * lax.rsqrt(ms + EPS)) * g


def _seg_rms(x, ones_bf, g, seg):
    s = x * x
    hi = s.astype(BF16)
    lo = (s - hi.astype(F32)).astype(BF16)
    ss = _dot(hi, ones_bf) + _dot(lo, ones_bf)
    return (x * lax.rsqrt(ss * (1.0 / seg) + EPS)) * g


def _project(x, g_attn_ref, w_in_ref, g_va_ref, g_q_ref, g_k_ref, segq_ref, segk_ref):
    xn = _rms(x, g_attn_ref[...]).astype(BF16)
    z = _dot(xn, w_in_ref[...])
    i1, i2, i3, i4 = A_WIDTH, 2 * A_WIDTH, 2 * A_WIDTH + B_WIDTH, 2 * A_WIDTH + B_WIDTH + KV_WIDTH
    u = jax.nn.gelu(z[:, :i1])
    va_pre = jax.nn.gelu(z[:, i1:i2])
    g_va = g_va_ref[...]
    va = jnp.concatenate(
        [_rms(va_pre[:, g * A_CH:(g + 1) * A_CH], g_va[:, g * A_CH:(g + 1) * A_CH]) for g in range(A_GROUPS)],
        axis=1)
    q = _seg_rms(z[:, i2:i3], segq_ref[...], g_q_ref[...], HEAD_DIM)
    k = _seg_rms(z[:, i3:i4], segk_ref[...], g_k_ref[...], HEAD_DIM)
    v = z[:, i4:]
    return u, va, q, k, v


def _chunk_mlp(va_bf, wsp_ref, bsp_ref, tile):
    n_pairs = tile // PAIR
    outs = []
    for g in range(A_GROUPS):
        cols = [va_bf[p * PAIR:(p + 1) * PAIR, g * A_CH:(g + 1) * A_CH] for p in range(n_pairs)]
        rhs = cols[0] if n_pairs == 1 else jnp.concatenate(cols, axis=1)
        o = _dot(wsp_ref[g], rhs)
        rows = [o[:, p * A_CH:(p + 1) * A_CH] for p in range(n_pairs)]
        outs.append(rows[0] if n_pairs == 1 else jnp.concatenate(rows, axis=0))
    return jnp.concatenate(outs, axis=1) + bsp_ref[...]


def _lane_lo():
    return lax.broadcasted_iota(jnp.int32, (1, LANES), 1) < HEAD_DIM


def _swa_prompt(q, k_bf, v_bf, kprev_ref, vprev_ref, bias_ref, sinks_ref, is_first, tile):
    lo = _lane_lo()
    col = lax.broadcasted_iota(jnp.int32, (WINDOW, 2 * WINDOW), 1)
    first_mask = jnp.where(col < WINDOW, jnp.where(is_first, NEG_INF, 0.0).astype(F32), 0.0)
    nblk = tile // WINDOW
    rows_out = []
    for i in range(nblk):
        r0, r1 = i * WINDOW, (i + 1) * WINDOW
        if i == 0:
            kp, vp = kprev_ref[...], vprev_ref[...]
        else:
            kp, vp = k_bf[r0 - WINDOW:r0], v_bf[r0 - WINDOW:r0]
        kb = jnp.concatenate([kp, k_bf[r0:r1]], axis=0)
        vb = jnp.concatenate([vp, v_bf[r0:r1]], axis=0)
        o_kh = []
        for kh in range(N_KV):
            m = lo if kh == 0 else jnp.logical_not(lo)
            lhs = jnp.concatenate(
                [jnp.where(m, q[r0:r1, g * LANES:(g + 1) * LANES], 0.0) for g in range(Q_PER_KV)],
                axis=0).astype(BF16)
            s = _dot_nt(lhs, kb) + bias_ref[kh]
            ps, linvs = [], []
            for g in range(Q_PER_KV):
                sg = s[g * WINDOW:(g + 1) * WINDOW]
                if i == 0:
                    sg = sg + first_mask
                sink = sinks_ref[kh * Q_PER_KV + g]
                mg = jnp.maximum(jnp.max(sg, axis=-1, keepdims=True), sink)
                pg = jnp.exp(sg - mg)
                lg = jnp.sum(pg, axis=-1, keepdims=True) + jnp.exp(sink - mg)
                ps.append(pg.astype(BF16))
                linvs.append(1.0 / lg)
            o = _dot(jnp.concatenate(ps, axis=0), vb)
            o_kh.append([o[g * WINDOW:(g + 1) * WINDOW] * linvs[g] for g in range(Q_PER_KV)])
        rows_out.append(jnp.concatenate(
            [jnp.where(lo, o_kh[0][g], o_kh[1][g]) for g in range(Q_PER_KV)], axis=1))
    kprev_ref[...] = k_bf[tile - WINDOW:tile]
    vprev_ref[...] = v_bf[tile - WINDOW:tile]
    return jnp.concatenate(rows_out, axis=0)


def _swa_sample(q, k_bf, v_bf, ck_ref, cv_ref, bias_c_ref, bias_n_ref, sinks_ref, nb, dec):
    lo = _lane_lo()
    hi_m = jnp.logical_not(lo)
    parts = []
    for kh in range(N_KV):
        m = lo if kh == 0 else hi_m
        for g in range(Q_PER_KV):
            parts.append(jnp.where(m, q[:, g * LANES:(g + 1) * LANES], 0.0).reshape(nb, dec, LANES))
    qb = jnp.concatenate(parts, axis=1)
    rows = N_HEADS * dec
    qb_bf = qb.astype(BF16)
    ck = ck_ref[...].astype(BF16)
    cv = cv_ref[...].astype(BF16)
    sc = jnp.einsum('bqc,bkc->bqk', qb_bf, ck, preferred_element_type=F32) + bias_c_ref[...][None]
    sn = _dot_nt(qb_bf.reshape(nb * rows, LANES), k_bf).reshape(nb, rows, nb * dec) + bias_n_ref[...]
    sink = jnp.concatenate(
        [jnp.full((1, dec, 1), sinks_ref[h], F32) for h in range(N_HEADS)], axis=1)
    m = jnp.maximum(jnp.max(sc, axis=-1, keepdims=True), jnp.max(sn, axis=-1, keepdims=True))
    m = jnp.maximum(m, sink)
    pc = jnp.exp(sc - m)
    pn = jnp.exp(sn - m)
    l = jnp.sum(pc, axis=-1, keepdims=True) + jnp.sum(pn, axis=-1, keepdims=True) + jnp.exp(sink - m)
    oc = jnp.einsum('bqk,bkc->bqc', pc.astype(BF16), cv, preferred_element_type=F32)
    on = _dot(pn.reshape(nb * rows, nb * dec).astype(BF16), v_bf).reshape(nb, rows, LANES)
    o = (oc + on) * (1.0 / l)
    half = Q_PER_KV * dec
    groups = []
    for g in range(Q_PER_KV):
        o0 = o[:, g * dec:(g + 1) * dec, :].reshape(nb * dec, LANES)
        o1 = o[:, half + g * dec:half + (g + 1) * dec, :].reshape(nb * dec, LANES)
        groups.append(jnp.where(lo, o0, o1))
    return jnp.concatenate(groups, axis=1)


def _route(hn, hn_bf, wr_hi_ref, wr_lo_ref, br_ref):
    lo_part = (hn - hn_bf.astype(F32)).astype(BF16)
    wr_hi = wr_hi_ref[...]
    logits = _dot(hn_bf, wr_hi) + _dot(lo_part, wr_hi) + _dot(hn_bf, wr_lo_ref[...]) + br_ref[...]
    lane = lax.broadcasted_iota(jnp.int32, logits.shape, 1)
    big = jnp.int32(ROUTER_LANES)
    lc = jnp.where(lane < N_EXPERT_GROUPS, logits, NEG_INF)
    mx = jnp.max(lc, axis=-1, keepdims=True)
    g_idx = jnp.min(jnp.where(lc == mx, lane, big), axis=-1, keepdims=True)
    p_g = 1.0 / jnp.sum(jnp.exp(lc - mx), axis=-1, keepdims=True)
    e_lane = lane - N_EXPERT_GROUPS
    in_group = (e_lane >= 0) & (e_lane < N_EXPERTS) & ((e_lane >> 2) == g_idx)
    lf = jnp.where(in_group, logits, NEG_INF)
    v1 = jnp.max(lf, axis=-1, keepdims=True)
    i1 = jnp.min(jnp.where(lf == v1, lane, big), axis=-1, keepdims=True)
    lf2 = jnp.where(lane == i1, NEG_INF, lf)
    v2 = jnp.max(lf2, axis=-1, keepdims=True)
    i2 = jnp.min(jnp.where(lf2 == v2, lane, big), axis=-1, keepdims=True)
    e = jnp.exp(v2 - v1)
    w1 = p_g / (1.0 + e)
    w2 = w1 * e
    return jnp.where(lane == i1, w1, jnp.where(lane == i2, w2, 0.0))


def _moe(h, g_ffn_ref, wr_hi_ref, wr_lo_ref, br_ref, wg_ref, wu_ref, wd_ref, hcat_ref):
    hn = _rms(h, g_ffn_ref[...])
    hn_bf = hn.astype(BF16)
    combine = _route(hn, hn_bf, wr_hi_ref, wr_lo_ref, br_ref)
    for e in range(N_EXPERTS):
        gate = _dot(hn_bf, wg_ref[e])
        up = _dot(hn_bf, wu_ref[e])
        c = combine[:, N_EXPERT_GROUPS + e:N_EXPERT_GROUPS + e + 1]
        he = (gate * (1.0 / (1.0 + jnp.exp(-gate)))) * up * c
        hcat_ref[:, e * D_EXPERT:(e + 1) * D_EXPERT] = he.astype(BF16)
    return _dot(hcat_ref[...], wd_ref[...])


def _merge(x, a_out, b_out, g_out_ref, w_out_ref):
    g_out = g_out_ref[...]
    mix = jnp.concatenate(
        [_rms(a_out, g_out[:, :A_WIDTH]), _rms(b_out, g_out[:, A_WIDTH:])], axis=1).astype(BF16)
    return x + _dot(mix, w_out_ref[...])


def _prompt_kernel(x_ref, g_attn_ref, w_in_ref, g_va_ref, g_q_ref, g_k_ref, segq_ref, segk_ref,
                   wsp_ref, bsp_ref, bias_ref, sinks_ref, g_out_ref, w_out_ref, g_ffn_ref,
                   wr_hi_ref, wr_lo_ref, br_ref, wg_ref, wu_ref, wd_ref,
                   y_ref, k_ref, v_ref,
                   kprev_ref, vprev_ref, hcat_ref, *, tile, tiles_per_seq):
    is_first = (pl.program_id(0) % tiles_per_seq) == 0

    @pl.when(is_first)
    def _():
        kprev_ref[...] = jnp.zeros_like(kprev_ref)
        vprev_ref[...] = jnp.zeros_like(vprev_ref)

    x = x_ref[...]
    u, va, q, k, v = _project(x, g_attn_ref, w_in_ref, g_va_ref, g_q_ref, g_k_ref, segq_ref, segk_ref)
    k_ref[...] = k
    v_ref[...] = v
    a_out = u * _chunk_mlp(va.astype(BF16), wsp_ref, bsp_ref, tile)
    b_out = _swa_prompt(q * (HEAD_DIM ** -0.5), k.astype(BF16), v.astype(BF16), kprev_ref, vprev_ref,
                        bias_ref, sinks_ref, is_first, tile)
    h = _merge(x, a_out, b_out, g_out_ref, w_out_ref)
    y_ref[...] = h + _moe(h, g_ffn_ref, wr_hi_ref, wr_lo_ref, br_ref, wg_ref, wu_ref, wd_ref, hcat_ref)


def _sample_kernel(x_ref, ck_ref, cv_ref, g_attn_ref, w_in_ref, g_va_ref, g_q_ref, g_k_ref, segq_ref,
                   segk_ref, wsp_ref, bsp_ref, bias_c_ref, bias_n_ref, sinks_ref, g_out_ref, w_out_ref,
                   g_ffn_ref, wr_hi_ref, wr_lo_ref, br_ref, wg_ref, wu_ref, wd_ref,
                   y_ref, k_ref, v_ref, va_ref,
                   hcat_ref, *, tile, nb, dec):
    x = x_ref[...]
    u, va, q, k, v = _project(x, g_attn_ref, w_in_ref, g_va_ref, g_q_ref, g_k_ref, segq_ref, segk_ref)
    k_ref[...] = k
    v_ref[...] = v
    va_ref[...] = va
    a_out = u * _chunk_mlp(va.astype(BF16), wsp_ref, bsp_ref, tile)
    b_out = _swa_sample(q * (HEAD_DIM ** -0.5), k.astype(BF16), v.astype(BF16), ck_ref, cv_ref,
                        bias_c_ref, bias_n_ref, sinks_ref, nb, dec)
    h = _merge(x, a_out, b_out, g_out_ref, w_out_ref)
    y_ref[...] = h + _moe(h, g_ffn_ref, wr_hi_ref, wr_lo_ref, br_ref, wg_ref, wu_ref, wd_ref, hcat_ref)


def _const_spec(shape):
    nd = len(shape)
    return pl.BlockSpec(shape, lambda i: (0,) * nd, pipeline_mode=pl.Buffered(1))


def _row_spec(tile, width):
    return pl.BlockSpec((tile, width), lambda i: (i, 0))


def _smem_spec():
    return pl.BlockSpec(memory_space=pltpu.SMEM)


def _layer_weights(l, g_attn_norm, w_in, g_v_a, g_q, g_k, g_out_a, g_out_b, w_out, g_ffn_norm,
                   w_coarse, b_coarse, w_fine, b_fine, w_gate, w_up, w_down):
    perm = _q_perm()
    i2, i3 = 2 * A_WIDTH, 2 * A_WIDTH + B_WIDTH
    col_perm = np.concatenate([np.arange(i2), i2 + perm, np.arange(i3, IN_COLS)]).astype(np.int32)
    row_perm = np.concatenate([np.arange(A_WIDTH), A_WIDTH + perm]).astype(np.int32)
    wr = jnp.zeros((D_MODEL, ROUTER_LANES), F32)
    wr = wr.at[:, :N_EXPERT_GROUPS].set(w_coarse[l]).at[:, N_EXPERT_GROUPS:N_EXPERT_GROUPS + N_EXPERTS].set(w_fine[l])
    wr_hi = wr.astype(BF16)
    br = jnp.zeros((1, ROUTER_LANES), F32)
    br = br.at[0, :N_EXPERT_GROUPS].set(b_coarse[l]).at[0, N_EXPERT_GROUPS:N_EXPERT_GROUPS + N_EXPERTS].set(b_fine[l])
    return dict(
        g_attn=g_attn_norm[l].reshape(1, D_MODEL),
        w_in=w_in[l][:, col_perm].astype(BF16),
        g_va=g_v_a[l].reshape(1, A_WIDTH),
        g_q=jnp.tile(g_q[l], N_HEADS).reshape(1, B_WIDTH),
        g_k=jnp.tile(g_k[l], N_KV).reshape(1, KV_WIDTH),
        segq=jnp.asarray(_seg_ones(B_WIDTH, HEAD_DIM), BF16),
        segk=jnp.asarray(_seg_ones(KV_WIDTH, HEAD_DIM), BF16),
        g_out=jnp.concatenate([g_out_a[l], g_out_b[l][perm]]).reshape(1, D_MODEL),
        w_out=w_out[l][row_perm, :].astype(BF16),
        g_ffn=g_ffn_norm[l].reshape(1, D_MODEL),
        wr_hi=wr_hi,
        wr_lo=(wr - wr_hi.astype(F32)).astype(BF16),
        br=br,
        wg=w_gate[l].astype(BF16),
        wu=w_up[l].astype(BF16),
        wd=w_down[l].astype(BF16).reshape(N_EXPERTS * D_EXPERT, D_MODEL),
    )


def _spatial_tables(ws, bs, period, tile):
    w = jnp.where(jnp.tril(jnp.ones((period, period), dtype=bool)), ws[:, :period, :period], 0.0)
    reps = PAIR // period
    eye = jnp.eye(reps, dtype=F32)
    wbd = jnp.einsum('ab,gts->gatbs', eye, w).reshape(A_GROUPS, PAIR, PAIR).astype(BF16)
    b = jnp.tile(bs[:, :period].T, (tile // period, 1))
    bsp = jnp.repeat(b, A_CH, axis=1)
    return wbd, bsp


_TAIL_NAMES = ("g_out", "w_out", "g_ffn", "wr_hi", "wr_lo", "br", "wg", "wu", "wd")
_HEAD_NAMES = ("g_attn", "w_in", "g_va", "g_q", "g_k", "segq", "segk")


def _run_prompt(x2d, lw, wbd, bsp, sinks, seq_len):
    n_tok = x2d.shape[0]
    tile = PROMPT_TILE
    bias = jnp.asarray(_prompt_bias())
    head = [lw[n] for n in _HEAD_NAMES]
    tail = [lw[n] for n in _TAIL_NAMES]
    consts = head + [wbd, bsp, bias]
    in_specs = ([_row_spec(tile, D_MODEL)] + [_const_spec(a.shape) for a in consts] + [_smem_spec()]
                + [_const_spec(a.shape) for a in tail])
    out_shape = (jax.ShapeDtypeStruct((n_tok, D_MODEL), F32),
                 jax.ShapeDtypeStruct((n_tok, KV_WIDTH), F32),
                 jax.ShapeDtypeStruct((n_tok, KV_WIDTH), F32))
    out_specs = (_row_spec(tile, D_MODEL), _row_spec(tile, KV_WIDTH), _row_spec(tile, KV_WIDTH))
    kern = functools.partial(_prompt_kernel, tile=tile, tiles_per_seq=seq_len // tile)
    return pl.pallas_call(
        kern,
        out_shape=out_shape,
        grid=(n_tok // tile,),
        in_specs=in_specs,
        out_specs=out_specs,
        scratch_shapes=[pltpu.VMEM((WINDOW, KV_WIDTH), BF16), pltpu.VMEM((WINDOW, KV_WIDTH), BF16),
                        pltpu.VMEM((tile, N_EXPERTS * D_EXPERT), BF16)],
        compiler_params=pltpu.CompilerParams(dimension_semantics=("arbitrary",),
                                             vmem_limit_bytes=VMEM_LIMIT_BYTES),
        name="layer_prompt",
    )(x2d, *consts, sinks, *tail)


def _run_sample(x2d, ck, cv, lw, wbd, bsp, sinks, dec):
    n_tok = x2d.shape[0]
    tile = SAMPLE_TILE
    nb = tile // dec
    bc, bn = _sample_bias(nb, dec)
    head = [lw[n] for n in _HEAD_NAMES]
    tail = [lw[n] for n in _TAIL_NAMES]
    consts = head + [wbd, bsp, jnp.asarray(bc), jnp.asarray(bn)]
    cache_spec = pl.BlockSpec((nb, WINDOW, KV_WIDTH), lambda i: (i, 0, 0))
    in_specs = ([_row_spec(tile, D_MODEL), cache_spec, cache_spec] + [_const_spec(a.shape) for a in consts]
                + [_smem_spec()] + [_const_spec(a.shape) for a in tail])
    out_shape = (jax.ShapeDtypeStruct((n_tok, D_MODEL), F32),
                 jax.ShapeDtypeStruct((n_tok, KV_WIDTH), F32),
                 jax.ShapeDtypeStruct((n_tok, KV_WIDTH), F32),
                 jax.ShapeDtypeStruct((n_tok, A_WIDTH), F32))
    out_specs = (_row_spec(tile, D_MODEL), _row_spec(tile, KV_WIDTH), _row_spec(tile, KV_WIDTH),
                 _row_spec(tile, A_WIDTH))
    kern = functools.partial(_sample_kernel, tile=tile, nb=nb, dec=dec)
    return pl.pallas_call(
        kern,
        out_shape=out_shape,
        grid=(n_tok // tile,),
        in_specs=in_specs,
        out_specs=out_specs,
        scratch_shapes=[pltpu.VMEM((tile, N_EXPERTS * D_EXPERT), BF16)],
        compiler_params=pltpu.CompilerParams(dimension_semantics=("arbitrary",),
                                             vmem_limit_bytes=VMEM_LIMIT_BYTES),
        name="layer_sample",
    )(x2d, ck, cv, *consts, sinks, *tail)


def kernel(x_prompt, x_sample, cache_k, cache_v, g_attn_norm, w_in, g_v_a, w_spatial, b_spatial, g_q, g_k, attn_sinks, g_out_a, g_out_b, w_out, g_ffn_norm, w_coarse, b_coarse, w_fine, b_fine, w_gate, w_up, w_down):
    depth = w_in.shape[0]
    batch, seq, _ = x_prompt.shape
    dbatch, dec, _ = x_sample.shape
    win = cache_k.shape[2]
    assert win == WINDOW and seq % PROMPT_TILE == 0 and (dbatch * dec) % SAMPLE_TILE == 0
    assert PAIR % dec == 0 and SAMPLE_TILE % dec == 0

    hp = x_prompt.reshape(batch * seq, D_MODEL)
    hs = x_sample.reshape(dbatch * dec, D_MODEL)
    kp_l, vp_l, ks_l, vs_l, cv_l = [], [], [], [], []
    for l in range(depth):
        lw = _layer_weights(l, g_attn_norm, w_in, g_v_a, g_q, g_k, g_out_a, g_out_b, w_out, g_ffn_norm,
                            w_coarse, b_coarse, w_fine, b_fine, w_gate, w_up, w_down)
        sinks = attn_sinks[l].astype(F32)
        wbd_p, bsp_p = _spatial_tables(w_spatial[l], b_spatial[l], CHUNK, PROMPT_TILE)
        wbd_s, bsp_s = _spatial_tables(w_spatial[l], b_spatial[l], dec, SAMPLE_TILE)

        hp, kp, vp = _run_prompt(hp, lw, wbd_p, bsp_p, sinks, seq)
        kp_l.append(kp.reshape(batch, seq, N_KV, HEAD_DIM)[:, -WINDOW:])
        vp_l.append(vp.reshape(batch, seq, N_KV, HEAD_DIM)[:, -WINDOW:])

        ck = cache_k[l].reshape(dbatch, win, KV_WIDTH)
        cv = cache_v[l].reshape(dbatch, win, KV_WIDTH)
        hs, ks, vs, va = _run_sample(hs, ck, cv, lw, wbd_s, bsp_s, sinks, dec)
        k_new = ks.reshape(dbatch, dec, N_KV, HEAD_DIM)
        v_new = vs.reshape(dbatch, dec, N_KV, HEAD_DIM)
        ks_l.append(jnp.concatenate([cache_k[l], k_new], axis=1)[:, -win:])
        vs_l.append(jnp.concatenate([cache_v[l], v_new], axis=1)[:, -win:])
        cv_l.append(va.reshape(dbatch, dec, A_GROUPS, A_CH))

    return (hp.reshape(batch, seq, D_MODEL), hs.reshape(dbatch, dec, D_MODEL),
            jnp.stack(kp_l, axis=0), jnp.stack(vp_l, axis=0),
            jnp.stack(ks_l, axis=0), jnp.stack(vs_l, axis=0), jnp.stack(cv_l, axis=0))
```

```python
import functools

import numpy as np
import jax
import jax.numpy as jnp
from jax import lax
from jax.experimental import pallas as pl
from jax.experimental.pallas import tpu as pltpu

D_MODEL = 1024
CHUNK = 128
A_GROUPS = 4
A_WIDTH = 512
A_CH = 128
N_HEADS = 8
N_KV = 2
Q_PER_KV = 4
HEAD_DIM = 64
B_WIDTH = 512
KV_WIDTH = 128
WINDOW = 128
IN_COLS = 2 * A_WIDTH + B_WIDTH + 2 * KV_WIDTH
N_EXPERT_GROUPS = 4
EXPERTS_PER_GROUP = 4
N_EXPERTS = 16
D_EXPERT = 256
EPS = 1e-6

LANES = 128
PAIR = 2 * CHUNK
ROUTER_LANES = 128
PROMPT_TILE = 512
SAMPLE_TILE = 128
VMEM_LIMIT_BYTES = 60 * 1024 * 1024

F32 = jnp.float32
BF16 = jnp.bfloat16
NEG_INF = float("-inf")


def _slopes():
    return np.array([2.0 ** (-8.0 * (h + 1) / N_HEADS) for h in range(N_HEADS)], np.float64)


def _prompt_bias():
    t = np.arange(WINDOW)[:, None]
    s = np.arange(2 * WINDOW)[None, :]
    dist = t + WINDOW - s
    valid = (dist >= 0) & (dist < WINDOW)
    sl = _slopes()
    out = np.full((N_KV, Q_PER_KV * WINDOW, 2 * WINDOW), -np.inf, np.float32)
    for kh in range(N_KV):
        for g in range(Q_PER_KV):
            b = np.where(valid, -sl[kh * Q_PER_KV + g] * dist, -np.inf)
            out[kh, g * WINDOW:(g + 1) * WINDOW] = b
    return out


def _sample_bias(nb, dec):
    sl = _slopes()
    t = np.arange(dec)[:, None]
    j = np.arange(WINDOW)[None, :]
    dist_c = t + WINDOW - j
    valid_c = (dist_c >= 0) & (dist_c < WINDOW)
    bc = np.full((N_HEADS * dec, WINDOW), -np.inf, np.float32)
    tp = np.arange(dec)[None, :]
    dist_n = t - tp
    valid_n = dist_n >= 0
    bn = np.full((nb, N_HEADS * dec, nb * dec), -np.inf, np.float32)
    for h in range(N_HEADS):
        bc[h * dec:(h + 1) * dec] = np.where(valid_c, -sl[h] * dist_c, -np.inf)
        blk = np.where(valid_n, -sl[h] * dist_n, -np.inf)
        for b in range(nb):
            bn[b, h * dec:(h + 1) * dec, b * dec:(b + 1) * dec] = blk
    return bc, bn


def _seg_ones(width, seg):
    i = np.arange(width)
    return (i[:, None] // seg == i[None, :] // seg).astype(np.float32)


def _causal_block_mask(period, size):
    i = np.arange(size)
    same = i[:, None] // period == i[None, :] // period
    return (same & (i[None, :] % period <= i[:, None] % period)).astype(np.float32)


def _dot(a, b):
    return jnp.dot(a, b, preferred_element_type=F32)


def _dot_nt(a, b):
    return lax.dot_general(a, b, (((1,), (1,)), ((), ())), preferred_element_type=F32)


def _rms(x, g):
    ms = jnp.mean(x * x, axis=-1, keepdims=True)
    return (x * lax.rsqrt(ms + EPS)) * g


def _seg_rms(x, ones_bf, g, seg):
    s = x * x
    hi = s.astype(BF16)
    lo = (s - hi.astype(F32)).astype(BF16)
    ss = _dot(hi, ones_bf) + _dot(lo, ones_bf)
    return (x * lax.rsqrt(ss * (1.0 / seg) + EPS)) * g


def _project(x, g_attn_ref, w_in_ref, g_va_ref, g_q_ref, g_k_ref, segq_ref, segk_ref):
    xn = _rms(x, g_attn_ref[...]).astype(BF16)
    z = _dot(xn, w_in_ref[...])
    i1, i2, i3, i4 = A_WIDTH, 2 * A_WIDTH, 2 * A_WIDTH + B_WIDTH, 2 * A_WIDTH + B_WIDTH + KV_WIDTH
    u = jax.nn.gelu(z[:, :i1])
    va_pre = jax.nn.gelu(z[:, i1:i2])
    g_va = g_va_ref[...]
    va = jnp.concatenate(
        [_rms(va_pre[:, g * A_CH:(g + 1) * A_CH], g_va[:, g * A_CH:(g + 1) * A_CH]) for g in range(A_GROUPS)],
        axis=1)
    q = _seg_rms(z[:, i2:i3], segq_ref[...], g_q_ref[...], HEAD_DIM)
    k = _seg_rms(z[:, i3:i4], segk_ref[...], g_k_ref[...], HEAD_DIM)
    v = z[:, i4:]
    return u, va, q, k, v


def _chunk_mlp(va_bf, wsp_ref, bsp_ref, tile):
    pair = wsp_ref.shape[1]
    n_pairs = tile // pair
    outs = []
    for g in range(A_GROUPS):
        cols = [va_bf[p * pair:(p + 1) * pair, g * A_CH:(g + 1) * A_CH] for p in range(n_pairs)]
        rhs = cols[0] if n_pairs == 1 else jnp.concatenate(cols, axis=1)
        o = _dot(wsp_ref[g], rhs)
        rows = [o[:, p * A_CH:(p + 1) * A_CH] for p in range(n_pairs)]
        outs.append(rows[0] if n_pairs == 1 else jnp.concatenate(rows, axis=0))
    return jnp.concatenate(outs, axis=1) + bsp_ref[...]


def _lane_lo():
    return lax.broadcasted_iota(jnp.int32, (1, LANES), 1) < HEAD_DIM


def _swap_halves(x):
    return pltpu.roll(x, HEAD_DIM, axis=1)


def _head_lhs(q_groups, q_groups_swapped, head, lo):
    j, half = divmod(head, 2)
    kv = head // Q_PER_KV
    src = q_groups[j] if half == kv else q_groups_swapped[j]
    return jnp.where(lo if kv == 0 else jnp.logical_not(lo), src, 0.0)


def _merge_heads(o_heads, lo):
    groups = []
    for j in range(N_HEADS // 2):
        kv = (2 * j) // Q_PER_KV
        if kv == 0:
            groups.append(jnp.where(lo, o_heads[2 * j], _swap_halves(o_heads[2 * j + 1])))
        else:
            groups.append(jnp.where(lo, _swap_halves(o_heads[2 * j]), o_heads[2 * j + 1]))
    return jnp.concatenate(groups, axis=1)


def _swa_prompt(q, k_bf, v_bf, kprev_ref, vprev_ref, bias_ref, sinks_ref, is_first, tile):
    lo = _lane_lo()
    col = lax.broadcasted_iota(jnp.int32, (WINDOW, 2 * WINDOW), 1)
    first_mask = jnp.where(col < WINDOW, jnp.where(is_first, NEG_INF, 0.0).astype(F32), 0.0)
    nblk = tile // WINDOW
    rows_out = []
    for i in range(nblk):
        r0, r1 = i * WINDOW, (i + 1) * WINDOW
        if i == 0:
            kp, vp = kprev_ref[...], vprev_ref[...]
        else:
            kp, vp = k_bf[r0 - WINDOW:r0], v_bf[r0 - WINDOW:r0]
        kb = jnp.concatenate([kp, k_bf[r0:r1]], axis=0)
        vb = jnp.concatenate([vp, v_bf[r0:r1]], axis=0)
        qg = [q[r0:r1, j * LANES:(j + 1) * LANES] for j in range(N_HEADS // 2)]
        qgs = [_swap_halves(x) for x in qg]
        o_heads = []
        for kh in range(N_KV):
            lhs = jnp.concatenate(
                [_head_lhs(qg, qgs, kh * Q_PER_KV + g, lo) for g in range(Q_PER_KV)], axis=0).astype(BF16)
            s = _dot_nt(lhs, kb) + bias_ref[kh]
            ps, linvs = [], []
            for g in range(Q_PER_KV):
                sg = s[g * WINDOW:(g + 1) * WINDOW]
                if i == 0:
                    sg = sg + first_mask
                sink = sinks_ref[kh * Q_PER_KV + g]
                mg = jnp.maximum(jnp.max(sg, axis=-1, keepdims=True), sink)
                pg = jnp.exp(sg - mg)
                lg = jnp.sum(pg, axis=-1, keepdims=True) + jnp.exp(sink - mg)
                ps.append(pg.astype(BF16))
                linvs.append(1.0 / lg)
            o = _dot(jnp.concatenate(ps, axis=0), vb)
            o_heads += [o[g * WINDOW:(g + 1) * WINDOW] * linvs[g] for g in range(Q_PER_KV)]
        rows_out.append(_merge_heads(o_heads, lo))
    kprev_ref[...] = k_bf[tile - WINDOW:tile]
    vprev_ref[...] = v_bf[tile - WINDOW:tile]
    return jnp.concatenate(rows_out, axis=0)


def _shift_window(cache_ref, new_t, out_ref, nb, dec):
    lane = lax.broadcasted_iota(jnp.int32, (1, WINDOW), 1)
    keep = lane < WINDOW - dec
    per_tile = LANES // dec
    for b in range(nb):
        src = new_t[:, (b // per_tile) * LANES:(b // per_tile + 1) * LANES]
        new_cols = pltpu.roll(src, (WINDOW - dec - (b % per_tile) * dec) % LANES, axis=1)
        old = pltpu.roll(cache_ref[b], WINDOW - dec, axis=1)
        out_ref[b] = jnp.where(keep, old, new_cols)


def _swa_sample(q, k, v, ck_ref, cv_ref, wk_ref, wv_ref, bias_c_ref, bias_n_ref, sinks_ref, nb, dec):
    lo = _lane_lo()
    qg = [q[:, j * LANES:(j + 1) * LANES] for j in range(N_HEADS // 2)]
    qgs = [_swap_halves(x) for x in qg]
    qb = jnp.concatenate(
        [_head_lhs(qg, qgs, h, lo).reshape(nb, dec, LANES) for h in range(N_HEADS)], axis=1)
    rows = N_HEADS * dec
    qb_bf = qb.astype(BF16)
    k_bf = k.astype(BF16)
    v_bf = v.astype(BF16)
    ck = ck_ref[...].astype(BF16)
    cv = cv_ref[...].astype(BF16)
    sc = jnp.einsum('bqc,bcp->bqp', qb_bf, ck, preferred_element_type=F32) + bias_c_ref[...][None]
    sn = _dot_nt(qb_bf.reshape(nb * rows, LANES), k_bf).reshape(nb, rows, nb * dec) + bias_n_ref[...]
    sink = jnp.concatenate(
        [jnp.full((1, dec, 1), sinks_ref[h], F32) for h in range(N_HEADS)], axis=1)
    m = jnp.maximum(jnp.max(sc, axis=-1, keepdims=True), jnp.max(sn, axis=-1, keepdims=True))
    m = jnp.maximum(m, sink)
    pc = jnp.exp(sc - m)
    pn = jnp.exp(sn - m)
    l = jnp.sum(pc, axis=-1, keepdims=True) + jnp.sum(pn, axis=-1, keepdims=True) + jnp.exp(sink - m)
    oc = jnp.einsum('bqp,bcp->bqc', pc.astype(BF16), cv, preferred_element_type=F32)
    on = _dot(pn.reshape(nb * rows, nb * dec).astype(BF16), v_bf).reshape(nb, rows, LANES)
    o = (oc + on) * (1.0 / l)
    o_heads = [o[:, h * dec:(h + 1) * dec, :].reshape(nb * dec, LANES) for h in range(N_HEADS)]
    _shift_window(ck_ref, k.T, wk_ref, nb, dec)
    _shift_window(cv_ref, v.T, wv_ref, nb, dec)
    return _merge_heads(o_heads, lo)


def _route(hn, hn_bf, wr_hi_ref, wr_lo_ref, br_ref):
    lo_part = (hn - hn_bf.astype(F32)).astype(BF16)
    wr_hi = wr_hi_ref[...]
    logits = _dot(hn_bf, wr_hi) + _dot(lo_part, wr_hi) + _dot(hn_bf, wr_lo_ref[...]) + br_ref[...]
    lane = lax.broadcasted_iota(jnp.int32, logits.shape, 1)
    big = jnp.int32(ROUTER_LANES)
    lc = jnp.where(lane < N_EXPERT_GROUPS, logits, NEG_INF)
    mx = jnp.max(lc, axis=-1, keepdims=True)
    g_idx = jnp.min(jnp.where(lc == mx, lane, big), axis=-1, keepdims=True)
    p_g = 1.0 / jnp.sum(jnp.exp(lc - mx), axis=-1, keepdims=True)
    e_lane = lane - N_EXPERT_GROUPS
    in_group = (e_lane >= 0) & (e_lane < N_EXPERTS) & ((e_lane >> 2) == g_idx)
    lf = jnp.where(in_group, logits, NEG_INF)
    v1 = jnp.max(lf, axis=-1, keepdims=True)
    i1 = jnp.min(jnp.where(lf == v1, lane, big), axis=-1, keepdims=True)
    lf2 = jnp.where(lane == i1, NEG_INF, lf)
    v2 = jnp.max(lf2, axis=-1, keepdims=True)
    i2 = jnp.min(jnp.where(lf2 == v2, lane, big), axis=-1, keepdims=True)
    e = jnp.exp(v2 - v1)
    w1 = p_g / (1.0 + e)
    w2 = w1 * e
    return jnp.where(lane == i1, w1, jnp.where(lane == i2, w2, 0.0))


def _moe(h, g_ffn_ref, wr_hi_ref, wr_lo_ref, br_ref, wg_ref, wu_ref, wd_ref, hcat_ref):
    hn = _rms(h, g_ffn_ref[...])
    hn_bf = hn.astype(BF16)
    combine = _route(hn, hn_bf, wr_hi_ref, wr_lo_ref, br_ref)
    for e in range(N_EXPERTS):
        gate = _dot(hn_bf, wg_ref[e])
        up = _dot(hn_bf, wu_ref[e])
        c = combine[:, N_EXPERT_GROUPS + e:N_EXPERT_GROUPS + e + 1]
        he = (gate * (1.0 / (1.0 + jnp.exp(-gate)))) * up * c
        hcat_ref[:, e * D_EXPERT:(e + 1) * D_EXPERT] = he.astype(BF16)
    return _dot(hcat_ref[...], wd_ref[...])


def _merge(x, a_out, b_out, g_out_ref, w_out_ref):
    g_out = g_out_ref[...]
    mix = jnp.concatenate(
        [_rms(a_out, g_out[:, :A_WIDTH]), _rms(b_out, g_out[:, A_WIDTH:])], axis=1).astype(BF16)
    return x + _dot(mix, w_out_ref[...])


def _prompt_kernel(x_ref, g_attn_ref, w_in_ref, g_va_ref, g_q_ref, g_k_ref, segq_ref, segk_ref,
                   wsp_ref, bsp_ref, bias_ref, sinks_ref, g_out_ref, w_out_ref, g_ffn_ref,
                   wr_hi_ref, wr_lo_ref, br_ref, wg_ref, wu_ref, wd_ref,
                   y_ref, kwin_ref, vwin_ref,
                   kprev_ref, vprev_ref, hcat_ref, *, tile, tiles_per_seq):
    is_first = (pl.program_id(0) % tiles_per_seq) == 0

    @pl.when(is_first)
    def _():
        kprev_ref[...] = jnp.zeros_like(kprev_ref)
        vprev_ref[...] = jnp.zeros_like(vprev_ref)

    x = x_ref[...]
    u, va, q, k, v = _project(x, g_attn_ref, w_in_ref, g_va_ref, g_q_ref, g_k_ref, segq_ref, segk_ref)
    kwin_ref[...] = k[tile - WINDOW:tile]
    vwin_ref[...] = v[tile - WINDOW:tile]
    a_out = u * _chunk_mlp(va.astype(BF16), wsp_ref, bsp_ref, tile)
    b_out = _swa_prompt(q * (HEAD_DIM ** -0.5), k.astype(BF16), v.astype(BF16), kprev_ref, vprev_ref,
                        bias_ref, sinks_ref, is_first, tile)
    h = _merge(x, a_out, b_out, g_out_ref, w_out_ref)
    y_ref[...] = h + _moe(h, g_ffn_ref, wr_hi_ref, wr_lo_ref, br_ref, wg_ref, wu_ref, wd_ref, hcat_ref)


def _sample_kernel(x_ref, ck_ref, cv_ref, g_attn_ref, w_in_ref, g_va_ref, g_q_ref, g_k_ref, segq_ref,
                   segk_ref, wsp_ref, bsp_ref, bias_c_ref, bias_n_ref, sinks_ref, g_out_ref, w_out_ref,
                   g_ffn_ref, wr_hi_ref, wr_lo_ref, br_ref, wg_ref, wu_ref, wd_ref,
                   y_ref, wk_ref, wv_ref, va_ref,
                   hcat_ref, *, tile, nb, dec):
    x = x_ref[...]
    u, va, q, k, v = _project(x, g_attn_ref, w_in_ref, g_va_ref, g_q_ref, g_k_ref, segq_ref, segk_ref)
    va_ref[...] = va
    a_out = u * _chunk_mlp(va.astype(BF16), wsp_ref, bsp_ref, tile)
    b_out = _swa_sample(q * (HEAD_DIM ** -0.5), k, v, ck_ref, cv_ref, wk_ref, wv_ref,
                        bias_c_ref, bias_n_ref, sinks_ref, nb, dec)
    h = _merge(x, a_out, b_out, g_out_ref, w_out_ref)
    y_ref[...] = h + _moe(h, g_ffn_ref, wr_hi_ref, wr_lo_ref, br_ref, wg_ref, wu_ref, wd_ref, hcat_ref)


def _const_spec(shape):
    nd = len(shape)
    return pl.BlockSpec(shape, lambda i: (0,) * nd, pipeline_mode=pl.Buffered(1))


def _row_spec(tile, width):
    return pl.BlockSpec((tile, width), lambda i: (i, 0))


def _smem_spec():
    return pl.BlockSpec(memory_space=pltpu.SMEM)


def _layer_weights(l, g_attn_norm, w_in, g_v_a, g_q, g_k, g_out_a, g_out_b, w_out, g_ffn_norm,
                   w_coarse, b_coarse, w_fine, b_fine, w_gate, w_up, w_down):
    pad = ROUTER_LANES - N_EXPERT_GROUPS - N_EXPERTS
    wr = jnp.concatenate([w_coarse[l], w_fine[l], jnp.zeros((D_MODEL, pad), F32)], axis=1)
    wr_hi = wr.astype(BF16)
    br = jnp.concatenate([b_coarse[l], b_fine[l], jnp.zeros((pad,), F32)]).reshape(1, ROUTER_LANES)
    return dict(
        g_attn=g_attn_norm[l].reshape(1, D_MODEL),
        w_in=w_in[l].astype(BF16),
        g_va=g_v_a[l].reshape(1, A_WIDTH),
        g_q=jnp.tile(g_q[l], N_HEADS).reshape(1, B_WIDTH),
        g_k=jnp.tile(g_k[l], N_KV).reshape(1, KV_WIDTH),
        segq=jnp.asarray(_seg_ones(B_WIDTH, HEAD_DIM), BF16),
        segk=jnp.asarray(_seg_ones(KV_WIDTH, HEAD_DIM), BF16),
        g_out=jnp.concatenate([g_out_a[l], g_out_b[l]]).reshape(1, D_MODEL),
        w_out=w_out[l].astype(BF16),
        g_ffn=g_ffn_norm[l].reshape(1, D_MODEL),
        wr_hi=wr_hi,
        wr_lo=(wr - wr_hi.astype(F32)).astype(BF16),
        br=br,
        wg=w_gate[l].astype(BF16),
        wu=w_up[l].astype(BF16),
        wd=w_down[l].astype(BF16).reshape(N_EXPERTS * D_EXPERT, D_MODEL),
    )


def _spatial_tables(ws, bs, period, tile):
    pair = min(PAIR, tile)
    reps = pair // period
    wbd = (jnp.tile(ws[:, :period, :period], (1, reps, reps)) * _causal_block_mask(period, pair)).astype(BF16)
    b = jnp.tile(bs[:, :period].T, (tile // period, 1))
    bsp = jnp.repeat(b, A_CH, axis=1)
    return wbd, bsp


_TAIL_NAMES = ("g_out", "w_out", "g_ffn", "wr_hi", "wr_lo", "br", "wg", "wu", "wd")
_HEAD_NAMES = ("g_attn", "w_in", "g_va", "g_q", "g_k", "segq", "segk")


def _run_prompt(x2d, lw, wbd, bsp, sinks, seq_len):
    n_tok = x2d.shape[0]
    tile = PROMPT_TILE
    tiles_per_seq = seq_len // tile
    n_seq = n_tok // seq_len
    bias = jnp.asarray(_prompt_bias())
    head = [lw[n] for n in _HEAD_NAMES]
    tail = [lw[n] for n in _TAIL_NAMES]
    consts = head + [wbd, bsp, bias]
    in_specs = ([_row_spec(tile, D_MODEL)] + [_const_spec(a.shape) for a in consts] + [_smem_spec()]
                + [_const_spec(a.shape) for a in tail])
    win_spec = pl.BlockSpec((WINDOW, KV_WIDTH), lambda i: (i // tiles_per_seq, 0))
    out_shape = (jax.ShapeDtypeStruct((n_tok, D_MODEL), F32),
                 jax.ShapeDtypeStruct((n_seq * WINDOW, KV_WIDTH), F32),
                 jax.ShapeDtypeStruct((n_seq * WINDOW, KV_WIDTH), F32))
    out_specs = (_row_spec(tile, D_MODEL), win_spec, win_spec)
    kern = functools.partial(_prompt_kernel, tile=tile, tiles_per_seq=tiles_per_seq)
    return pl.pallas_call(
        kern,
        out_shape=out_shape,
        grid=(n_tok // tile,),
        in_specs=in_specs,
        out_specs=out_specs,
        scratch_shapes=[pltpu.VMEM((WINDOW, KV_WIDTH), BF16), pltpu.VMEM((WINDOW, KV_WIDTH), BF16),
                        pltpu.VMEM((tile, N_EXPERTS * D_EXPERT), BF16)],
        compiler_params=pltpu.CompilerParams(dimension_semantics=("arbitrary",),
                                             vmem_limit_bytes=VMEM_LIMIT_BYTES),
        name="layer_prompt",
    )(x2d, *consts, sinks, *tail)


def _run_sample(x2d, ck_t, cv_t, lw, wbd, bsp, sinks, dec):
    n_tok = x2d.shape[0]
    n_seq = ck_t.shape[0]
    tile = SAMPLE_TILE
    nb = tile // dec
    bc, bn = _sample_bias(nb, dec)
    head = [lw[n] for n in _HEAD_NAMES]
    tail = [lw[n] for n in _TAIL_NAMES]
    consts = head + [wbd, bsp, jnp.asarray(bc), jnp.asarray(bn)]
    cache_spec = pl.BlockSpec((nb, KV_WIDTH, WINDOW), lambda i: (i, 0, 0))
    in_specs = ([_row_spec(tile, D_MODEL), cache_spec, cache_spec] + [_const_spec(a.shape) for a in consts]
                + [_smem_spec()] + [_const_spec(a.shape) for a in tail])
    out_shape = (jax.ShapeDtypeStruct((n_tok, D_MODEL), F32),
                 jax.ShapeDtypeStruct((n_seq, KV_WIDTH, WINDOW), F32),
                 jax.ShapeDtypeStruct((n_seq, KV_WIDTH, WINDOW), F32),
                 jax.ShapeDtypeStruct((n_tok, A_WIDTH), F32))
    out_specs = (_row_spec(tile, D_MODEL), cache_spec, cache_spec, _row_spec(tile, A_WIDTH))
    kern = functools.partial(_sample_kernel, tile=tile, nb=nb, dec=dec)
    return pl.pallas_call(
        kern,
        out_shape=out_shape,
        grid=(n_tok // tile,),
        in_specs=in_specs,
        out_specs=out_specs,
        scratch_shapes=[pltpu.VMEM((tile, N_EXPERTS * D_EXPERT), BF16)],
        compiler_params=pltpu.CompilerParams(dimension_semantics=("arbitrary",),
                                             vmem_limit_bytes=VMEM_LIMIT_BYTES),
        name="layer_sample",
    )(x2d, ck_t, cv_t, *consts, sinks, *tail)


def _positions_last(c):
    b, w = c.shape[:2]
    return jnp.transpose(c, (0, 2, 3, 1)).reshape(b, KV_WIDTH, w)


def _positions_first(c_t):
    b, _, w = c_t.shape
    return jnp.transpose(c_t.reshape(b, N_KV, HEAD_DIM, w), (0, 3, 1, 2))


def kernel(x_prompt, x_sample, cache_k, cache_v, g_attn_norm, w_in, g_v_a, w_spatial, b_spatial, g_q, g_k, attn_sinks, g_out_a, g_out_b, w_out, g_ffn_norm, w_coarse, b_coarse, w_fine, b_fine, w_gate, w_up, w_down):
    depth = w_in.shape[0]
    batch, seq, _ = x_prompt.shape
    dbatch, dec, _ = x_sample.shape
    win = cache_k.shape[2]
    assert win == WINDOW and seq % PROMPT_TILE == 0 and (dbatch * dec) % SAMPLE_TILE == 0
    assert PAIR % dec == 0 and SAMPLE_TILE % dec == 0 and LANES % dec == 0

    hp = x_prompt.reshape(batch * seq, D_MODEL)
    hs = x_sample.reshape(dbatch * dec, D_MODEL)
    kp_l, vp_l, ks_l, vs_l, cv_l = [], [], [], [], []
    for l in range(depth):
        lw = _layer_weights(l, g_attn_norm, w_in, g_v_a, g_q, g_k, g_out_a, g_out_b, w_out, g_ffn_norm,
                            w_coarse, b_coarse, w_fine, b_fine, w_gate, w_up, w_down)
        sinks = attn_sinks[l].astype(F32)
        wbd_p, bsp_p = _spatial_tables(w_spatial[l], b_spatial[l], CHUNK, PROMPT_TILE)
        wbd_s, bsp_s = _spatial_tables(w_spatial[l], b_spatial[l], dec, SAMPLE_TILE)

        hp, kp, vp = _run_prompt(hp, lw, wbd_p, bsp_p, sinks, seq)
        kp_l.append(kp.reshape(batch, WINDOW, N_KV, HEAD_DIM))
        vp_l.append(vp.reshape(batch, WINDOW, N_KV, HEAD_DIM))

        hs, wk, wv, va = _run_sample(hs, _positions_last(cache_k[l]), _positions_last(cache_v[l]),
                                     lw, wbd_s, bsp_s, sinks, dec)
        ks_l.append(_positions_first(wk))
        vs_l.append(_positions_first(wv))
        cv_l.append(va.reshape(dbatch, dec, A_GROUPS, A_CH))

    return (hp.reshape(batch, seq, D_MODEL), hs.reshape(dbatch, dec, D_MODEL),
            jnp.stack(kp_l, axis=0), jnp.stack(vp_l, axis=0),
            jnp.stack(ks_l, axis=0), jnp.stack(vs_l, axis=0), jnp.stack(cv_l, axis=0))
```

```python
import functools

import numpy as np
import jax
import jax.numpy as jnp
from jax import lax
from jax.experimental import pallas as pl
from jax.experimental.pallas import tpu as pltpu

D_MODEL = 1024
CHUNK = 128
A_GROUPS = 4
A_WIDTH = 512
A_CH = 128
N_HEADS = 8
N_KV = 2
Q_PER_KV = 4
HEAD_DIM = 64
B_WIDTH = 512
KV_WIDTH = 128
WINDOW = 128
IN_COLS = 2 * A_WIDTH + B_WIDTH + 2 * KV_WIDTH
N_EXPERT_GROUPS = 4
EXPERTS_PER_GROUP = 4
N_EXPERTS = 16
D_EXPERT = 256
EPS = 1e-6

LANES = 128
PAIR = 2 * CHUNK
ROUTER_LANES = 128
PROMPT_TILE = 512
SAMPLE_TILE = 128
VMEM_LIMIT_BYTES = 60 * 1024 * 1024

F32 = jnp.float32
BF16 = jnp.bfloat16
NEG_INF = float("-inf")


def _slopes():
    return np.array([2.0 ** (-8.0 * (h + 1) / N_HEADS) for h in range(N_HEADS)], np.float64)


def _prompt_bias():
    t = np.arange(WINDOW)[:, None]
    s = np.arange(2 * WINDOW)[None, :]
    dist = t + WINDOW - s
    valid = (dist >= 0) & (dist < WINDOW)
    sl = _slopes()
    out = np.full((N_KV, Q_PER_KV * WINDOW, 2 * WINDOW), -np.inf, np.float32)
    for kh in range(N_KV):
        for g in range(Q_PER_KV):
            b = np.where(valid, -sl[kh * Q_PER_KV + g] * dist, -np.inf)
            out[kh, g * WINDOW:(g + 1) * WINDOW] = b
    return out


def _sample_bias(nb, dec):
    sl = _slopes()
    t = np.arange(dec)[:, None]
    j = np.arange(WINDOW)[None, :]
    dist_c = t + WINDOW - j
    valid_c = (dist_c >= 0) & (dist_c < WINDOW)
    bc = np.full((N_HEADS * dec, WINDOW), -np.inf, np.float32)
    tp = np.arange(dec)[None, :]
    dist_n = t - tp
    valid_n = dist_n >= 0
    bn = np.full((nb, N_HEADS * dec, nb * dec), -np.inf, np.float32)
    for h in range(N_HEADS):
        bc[h * dec:(h + 1) * dec] = np.where(valid_c, -sl[h] * dist_c, -np.inf)
        blk = np.where(valid_n, -sl[h] * dist_n, -np.inf)
        for b in range(nb):
            bn[b, h * dec:(h + 1) * dec, b * dec:(b + 1) * dec] = blk
    return bc, bn


def _seg_ones(width, seg):
    i = np.arange(width)
    return (i[:, None] // seg == i[None, :] // seg).astype(np.float32)


def _causal_block_mask(period, size):
    i = np.arange(size)
    same = i[:, None] // period == i[None, :] // period
    return (same & (i[None, :] % period <= i[:, None] % period)).astype(np.float32)


def _dot(a, b):
    return jnp.dot(a, b, preferred_element_type=F32)


def _dot_nt(a, b):
    return lax.dot_general(a, b, (((1,), (1,)), ((), ())), preferred_element_type=F32)


def _rms(x, g):
    ms = jnp.mean(x * x, axis=-1, keepdims=True)
    return (x * lax.rsqrt(ms + EPS)) * g


def _seg_rms(x, ones_bf, g, seg):
    ss = _dot((x * x).astype(BF16), ones_bf)
    return (x * lax.rsqrt(ss * (1.0 / seg) + EPS)) * g


def _project(x, g_attn_ref, w_in_ref, g_va_ref, g_q_ref, g_k_ref, segq_ref, segk_ref):
    xn = _rms(x, g_attn_ref[...]).astype(BF16)
    z = _dot(xn, w_in_ref[...])
    i1, i2, i3, i4 = A_WIDTH, 2 * A_WIDTH, 2 * A_WIDTH + B_WIDTH, 2 * A_WIDTH + B_WIDTH + KV_WIDTH
    u = jax.nn.gelu(z[:, :i1])
    va_pre = jax.nn.gelu(z[:, i1:i2])
    g_va = g_va_ref[...]
    va = jnp.concatenate(
        [_rms(va_pre[:, g * A_CH:(g + 1) * A_CH], g_va[:, g * A_CH:(g + 1) * A_CH]) for g in range(A_GROUPS)],
        axis=1)
    q = _seg_rms(z[:, i2:i3], segq_ref[...], g_q_ref[...], HEAD_DIM)
    k = _seg_rms(z[:, i3:i4], segk_ref[...], g_k_ref[...], HEAD_DIM)
    v = z[:, i4:]
    return u, va, q, k, v


def _chunk_mlp(va_bf, wsp_ref, bsp_ref, tile):
    pair = wsp_ref.shape[1]
    n_pairs = tile // pair
    outs = []
    for g in range(A_GROUPS):
        cols = [va_bf[p * pair:(p + 1) * pair, g * A_CH:(g + 1) * A_CH] for p in range(n_pairs)]
        rhs = cols[0] if n_pairs == 1 else jnp.concatenate(cols, axis=1)
        o = _dot(wsp_ref[g], rhs)
        rows = [o[:, p * A_CH:(p + 1) * A_CH] for p in range(n_pairs)]
        outs.append(rows[0] if n_pairs == 1 else jnp.concatenate(rows, axis=0))
    return jnp.concatenate(outs, axis=1) + bsp_ref[...]


def _lane_lo():
    return lax.broadcasted_iota(jnp.int32, (1, LANES), 1) < HEAD_DIM


def _swap_halves(x):
    return pltpu.roll(x, HEAD_DIM, axis=1)


def _head_lhs(q_groups, q_groups_swapped, head, lo):
    j, half = divmod(head, 2)
    kv = head // Q_PER_KV
    src = q_groups[j] if half == kv else q_groups_swapped[j]
    return jnp.where(lo if kv == 0 else jnp.logical_not(lo), src, 0.0)


def _merge_heads(o_heads, lo):
    groups = []
    for j in range(N_HEADS // 2):
        kv = (2 * j) // Q_PER_KV
        if kv == 0:
            groups.append(jnp.where(lo, o_heads[2 * j], _swap_halves(o_heads[2 * j + 1])))
        else:
            groups.append(jnp.where(lo, _swap_halves(o_heads[2 * j]), o_heads[2 * j + 1]))
    return jnp.concatenate(groups, axis=1)


def _dup_halves(x, lo):
    xs = _swap_halves(x)
    return jnp.where(lo, x, xs).astype(BF16), jnp.where(lo, xs, x).astype(BF16)


def _swa_prompt(q, k, v, kprev_ref, vprev_ref, bias_ref, sinks_ref, is_first, tile):
    lo = _lane_lo()
    hi = jnp.logical_not(lo)
    col = lax.broadcasted_iota(jnp.int32, (WINDOW, 2 * WINDOW), 1)
    first_mask = jnp.where(col < WINDOW, jnp.where(is_first, NEG_INF, 0.0).astype(F32), 0.0)
    kd = _dup_halves(k, lo)
    vd = _dup_halves(v, lo)
    nblk = tile // WINDOW
    rows_out = []
    for i in range(nblk):
        r0, r1 = i * WINDOW, (i + 1) * WINDOW
        qg = [q[r0:r1, j * LANES:(j + 1) * LANES] for j in range(N_HEADS // 2)]
        o_heads = []
        for kh in range(N_KV):
            if i == 0:
                kp, vp = kprev_ref[kh], vprev_ref[kh]
            else:
                kp, vp = kd[kh][r0 - WINDOW:r0], vd[kh][r0 - WINDOW:r0]
            kb = jnp.concatenate([kp, kd[kh][r0:r1]], axis=0)
            vb = jnp.concatenate([vp, vd[kh][r0:r1]], axis=0)
            heads = [kh * Q_PER_KV + g for g in range(Q_PER_KV)]
            lhs = jnp.concatenate(
                [jnp.where(lo if h % 2 == 0 else hi, qg[h // 2], 0.0) for h in heads], axis=0).astype(BF16)
            s = _dot_nt(lhs, kb) + bias_ref[kh]
            ps, linvs = [], []
            for g in range(Q_PER_KV):
                sg = s[g * WINDOW:(g + 1) * WINDOW]
                if i == 0:
                    sg = sg + first_mask
                sink = sinks_ref[heads[g]]
                mg = jnp.maximum(jnp.max(sg, axis=-1, keepdims=True), sink)
                pg = jnp.exp(sg - mg)
                lg = jnp.sum(pg, axis=-1, keepdims=True) + jnp.exp(sink - mg)
                ps.append(pg.astype(BF16))
                linvs.append(1.0 / lg)
            o = _dot(jnp.concatenate(ps, axis=0), vb)
            o_heads += [o[g * WINDOW:(g + 1) * WINDOW] * linvs[g] for g in range(Q_PER_KV)]
        rows_out.append(jnp.concatenate(
            [jnp.where(lo, o_heads[2 * j], o_heads[2 * j + 1]) for j in range(N_HEADS // 2)], axis=1))
    for kh in range(N_KV):
        kprev_ref[kh] = kd[kh][tile - WINDOW:tile]
        vprev_ref[kh] = vd[kh][tile - WINDOW:tile]
    return jnp.concatenate(rows_out, axis=0)


def _shift_window(cache_ref, new_t, out_ref, nb, dec):
    lane = lax.broadcasted_iota(jnp.int32, (1, WINDOW), 1)
    keep = lane < WINDOW - dec
    per_tile = LANES // dec
    for b in range(nb):
        src = new_t[:, (b // per_tile) * LANES:(b // per_tile + 1) * LANES]
        new_cols = pltpu.roll(src, (WINDOW - dec - (b % per_tile) * dec) % LANES, axis=1)
        old = pltpu.roll(cache_ref[b], WINDOW - dec, axis=1)
        out_ref[b] = jnp.where(keep, old, new_cols)


def _swa_sample(q, k, v, ck_ref, cv_ref, wk_ref, wv_ref, bias_c_ref, bias_n_ref, sinks_ref, nb, dec):
    lo = _lane_lo()
    qg = [q[:, j * LANES:(j + 1) * LANES] for j in range(N_HEADS // 2)]
    qgs = [_swap_halves(x) for x in qg]
    qb = jnp.concatenate(
        [_head_lhs(qg, qgs, h, lo).reshape(nb, dec, LANES) for h in range(N_HEADS)], axis=1)
    rows = N_HEADS * dec
    qb_bf = qb.astype(BF16)
    k_bf = k.astype(BF16)
    v_bf = v.astype(BF16)
    ck = ck_ref[...].astype(BF16)
    cv = cv_ref[...].astype(BF16)
    sc = jnp.einsum('bqc,bcp->bqp', qb_bf, ck, preferred_element_type=F32) + bias_c_ref[...][None]
    sn = _dot_nt(qb_bf.reshape(nb * rows, LANES), k_bf).reshape(nb, rows, nb * dec) + bias_n_ref[...]
    sink = jnp.concatenate(
        [jnp.full((1, dec, 1), sinks_ref[h], F32) for h in range(N_HEADS)], axis=1)
    m = jnp.maximum(jnp.max(sc, axis=-1, keepdims=True), jnp.max(sn, axis=-1, keepdims=True))
    m = jnp.maximum(m, sink)
    pc = jnp.exp(sc - m)
    pn = jnp.exp(sn - m)
    l = jnp.sum(pc, axis=-1, keepdims=True) + jnp.sum(pn, axis=-1, keepdims=True) + jnp.exp(sink - m)
    oc = jnp.einsum('bqp,bcp->bqc', pc.astype(BF16), cv, preferred_element_type=F32)
    on = _dot(pn.reshape(nb * rows, nb * dec).astype(BF16), v_bf).reshape(nb, rows, LANES)
    o = (oc + on) * (1.0 / l)
    o_heads = [o[:, h * dec:(h + 1) * dec, :].reshape(nb * dec, LANES) for h in range(N_HEADS)]
    _shift_window(ck_ref, k.T, wk_ref, nb, dec)
    _shift_window(cv_ref, v.T, wv_ref, nb, dec)
    return _merge_heads(o_heads, lo)


def _route(hn, hn_bf, wr_hi_ref, wr_lo_ref, br_ref):
    lo_part = (hn - hn_bf.astype(F32)).astype(BF16)
    wr_hi = wr_hi_ref[...]
    logits = _dot(hn_bf, wr_hi) + _dot(lo_part, wr_hi) + _dot(hn_bf, wr_lo_ref[...]) + br_ref[...]
    lane = lax.broadcasted_iota(jnp.int32, logits.shape, 1)
    big = jnp.int32(ROUTER_LANES)
    lc = jnp.where(lane < N_EXPERT_GROUPS, logits, NEG_INF)
    mx = jnp.max(lc, axis=-1, keepdims=True)
    g_idx = jnp.min(jnp.where(lc == mx, lane, big), axis=-1, keepdims=True)
    p_g = 1.0 / jnp.sum(jnp.exp(lc - mx), axis=-1, keepdims=True)
    e_lane = lane - N_EXPERT_GROUPS
    in_group = (e_lane >= 0) & (e_lane < N_EXPERTS) & ((e_lane >> 2) == g_idx)
    lf = jnp.where(in_group, logits, NEG_INF)
    v1 = jnp.max(lf, axis=-1, keepdims=True)
    i1 = jnp.min(jnp.where(lf == v1, lane, big), axis=-1, keepdims=True)
    lf2 = jnp.where(lane == i1, NEG_INF, lf)
    v2 = jnp.max(lf2, axis=-1, keepdims=True)
    i2 = jnp.min(jnp.where(lf2 == v2, lane, big), axis=-1, keepdims=True)
    e = jnp.exp(v2 - v1)
    w1 = p_g / (1.0 + e)
    w2 = w1 * e
    return jnp.where(lane == i1, w1, jnp.where(lane == i2, w2, 0.0))


def _moe(h, g_ffn_ref, wr_hi_ref, wr_lo_ref, br_ref, wg_ref, wu_ref, wd_ref, hcat_ref):
    hn = _rms(h, g_ffn_ref[...])
    hn_bf = hn.astype(BF16)
    combine = _route(hn, hn_bf, wr_hi_ref, wr_lo_ref, br_ref)
    for e in range(N_EXPERTS):
        gate = _dot(hn_bf, wg_ref[e])
        up = _dot(hn_bf, wu_ref[e])
        c = combine[:, N_EXPERT_GROUPS + e:N_EXPERT_GROUPS + e + 1]
        he = (gate * (1.0 / (1.0 + jnp.exp(-gate)))) * up * c
        hcat_ref[:, e * D_EXPERT:(e + 1) * D_EXPERT] = he.astype(BF16)
    return _dot(hcat_ref[...], wd_ref[...])


def _merge(x, a_out, b_out, g_out_ref, w_out_ref):
    g_out = g_out_ref[...]
    mix = jnp.concatenate(
        [_rms(a_out, g_out[:, :A_WIDTH]), _rms(b_out, g_out[:, A_WIDTH:])], axis=1).astype(BF16)
    return x + _dot(mix, w_out_ref[...])


def _prompt_kernel(x_ref, g_attn_ref, w_in_ref, g_va_ref, g_q_ref, g_k_ref, segq_ref, segk_ref,
                   wsp_ref, bsp_ref, bias_ref, sinks_ref, g_out_ref, w_out_ref, g_ffn_ref,
                   wr_hi_ref, wr_lo_ref, br_ref, wg_ref, wu_ref, wd_ref,
                   y_ref, kwin_ref, vwin_ref,
                   kprev_ref, vprev_ref, hcat_ref, *, tile, tiles_per_seq):
    is_first = (pl.program_id(0) % tiles_per_seq) == 0

    @pl.when(is_first)
    def _():
        kprev_ref[...] = jnp.zeros_like(kprev_ref)
        vprev_ref[...] = jnp.zeros_like(vprev_ref)

    x = x_ref[...]
    u, va, q, k, v = _project(x, g_attn_ref, w_in_ref, g_va_ref, g_q_ref, g_k_ref, segq_ref, segk_ref)
    kwin_ref[...] = k[tile - WINDOW:tile]
    vwin_ref[...] = v[tile - WINDOW:tile]
    a_out = u * _chunk_mlp(va.astype(BF16), wsp_ref, bsp_ref, tile)
    b_out = _swa_prompt(q * (HEAD_DIM ** -0.5), k, v, kprev_ref, vprev_ref, bias_ref, sinks_ref, is_first, tile)
    h = _merge(x, a_out, b_out, g_out_ref, w_out_ref)
    y_ref[...] = h + _moe(h, g_ffn_ref, wr_hi_ref, wr_lo_ref, br_ref, wg_ref, wu_ref, wd_ref, hcat_ref)


def _sample_kernel(x_ref, ck_ref, cv_ref, g_attn_ref, w_in_ref, g_va_ref, g_q_ref, g_k_ref, segq_ref,
                   segk_ref, wsp_ref, bsp_ref, bias_c_ref, bias_n_ref, sinks_ref, g_out_ref, w_out_ref,
                   g_ffn_ref, wr_hi_ref, wr_lo_ref, br_ref, wg_ref, wu_ref, wd_ref,
                   y_ref, wk_ref, wv_ref, va_ref,
                   hcat_ref, *, tile, nb, dec):
    x = x_ref[...]
    u, va, q, k, v = _project(x, g_attn_ref, w_in_ref, g_va_ref, g_q_ref, g_k_ref, segq_ref, segk_ref)
    va_ref[...] = va
    a_out = u * _chunk_mlp(va.astype(BF16), wsp_ref, bsp_ref, tile)
    b_out = _swa_sample(q * (HEAD_DIM ** -0.5), k, v, ck_ref, cv_ref, wk_ref, wv_ref,
                        bias_c_ref, bias_n_ref, sinks_ref, nb, dec)
    h = _merge(x, a_out, b_out, g_out_ref, w_out_ref)
    y_ref[...] = h + _moe(h, g_ffn_ref, wr_hi_ref, wr_lo_ref, br_ref, wg_ref, wu_ref, wd_ref, hcat_ref)


def _const_spec(shape):
    nd = len(shape)
    return pl.BlockSpec(shape, lambda i: (0,) * nd, pipeline_mode=pl.Buffered(1))


def _row_spec(tile, width):
    return pl.BlockSpec((tile, width), lambda i: (i, 0))


def _smem_spec():
    return pl.BlockSpec(memory_space=pltpu.SMEM)


def _layer_weights(l, g_attn_norm, w_in, g_v_a, g_q, g_k, g_out_a, g_out_b, w_out, g_ffn_norm,
                   w_coarse, b_coarse, w_fine, b_fine, w_gate, w_up, w_down):
    pad = ROUTER_LANES - N_EXPERT_GROUPS - N_EXPERTS
    wr = jnp.concatenate([w_coarse[l], w_fine[l], jnp.zeros((D_MODEL, pad), F32)], axis=1)
    wr_hi = wr.astype(BF16)
    br = jnp.concatenate([b_coarse[l], b_fine[l], jnp.zeros((pad,), F32)]).reshape(1, ROUTER_LANES)
    return dict(
        g_attn=g_attn_norm[l].reshape(1, D_MODEL),
        w_in=w_in[l].astype(BF16),
        g_va=g_v_a[l].reshape(1, A_WIDTH),
        g_q=jnp.tile(g_q[l], N_HEADS).reshape(1, B_WIDTH),
        g_k=jnp.tile(g_k[l], N_KV).reshape(1, KV_WIDTH),
        segq=jnp.asarray(_seg_ones(B_WIDTH, HEAD_DIM), BF16),
        segk=jnp.asarray(_seg_ones(KV_WIDTH, HEAD_DIM), BF16),
        g_out=jnp.concatenate([g_out_a[l], g_out_b[l]]).reshape(1, D_MODEL),
        w_out=w_out[l].astype(BF16),
        g_ffn=g_ffn_norm[l].reshape(1, D_MODEL),
        wr_hi=wr_hi,
        wr_lo=(wr - wr_hi.astype(F32)).astype(BF16),
        br=br,
        wg=w_gate[l].astype(BF16),
        wu=w_up[l].astype(BF16),
        wd=w_down[l].astype(BF16).reshape(N_EXPERTS * D_EXPERT, D_MODEL),
    )


def _spatial_tables(ws, bs, period, tile):
    pair = min(PAIR, tile)
    reps = pair // period
    wbd = (jnp.tile(ws[:, :period, :period], (1, reps, reps)) * _causal_block_mask(period, pair)).astype(BF16)
    b = jnp.tile(bs[:, :period].T, (tile // period, 1))
    bsp = jnp.repeat(b, A_CH, axis=1)
    return wbd, bsp


_TAIL_NAMES = ("g_out", "w_out", "g_ffn", "wr_hi", "wr_lo", "br", "wg", "wu", "wd")
_HEAD_NAMES = ("g_attn", "w_in", "g_va", "g_q", "g_k", "segq", "segk")


def _run_prompt(x2d, lw, wbd, bsp, sinks, seq_len):
    n_tok = x2d.shape[0]
    tile = PROMPT_TILE
    tiles_per_seq = seq_len // tile
    n_seq = n_tok // seq_len
    bias = jnp.asarray(_prompt_bias())
    head = [lw[n] for n in _HEAD_NAMES]
    tail = [lw[n] for n in _TAIL_NAMES]
    consts = head + [wbd, bsp, bias]
    in_specs = ([_row_spec(tile, D_MODEL)] + [_const_spec(a.shape) for a in consts] + [_smem_spec()]
                + [_const_spec(a.shape) for a in tail])
    win_spec = pl.BlockSpec((WINDOW, KV_WIDTH), lambda i: (i // tiles_per_seq, 0))
    out_shape = (jax.ShapeDtypeStruct((n_tok, D_MODEL), F32),
                 jax.ShapeDtypeStruct((n_seq * WINDOW, KV_WIDTH), F32),
                 jax.ShapeDtypeStruct((n_seq * WINDOW, KV_WIDTH), F32))
    out_specs = (_row_spec(tile, D_MODEL), win_spec, win_spec)
    kern = functools.partial(_prompt_kernel, tile=tile, tiles_per_seq=tiles_per_seq)
    return pl.pallas_call(
        kern,
        out_shape=out_shape,
        grid=(n_tok // tile,),
        in_specs=in_specs,
        out_specs=out_specs,
        scratch_shapes=[pltpu.VMEM((N_KV, WINDOW, KV_WIDTH), BF16), pltpu.VMEM((N_KV, WINDOW, KV_WIDTH), BF16),
                        pltpu.VMEM((tile, N_EXPERTS * D_EXPERT), BF16)],
        compiler_params=pltpu.CompilerParams(dimension_semantics=("arbitrary",),
                                             vmem_limit_bytes=VMEM_LIMIT_BYTES),
        name="layer_prompt",
    )(x2d, *consts, sinks, *tail)


def _run_sample(x2d, ck_t, cv_t, lw, wbd, bsp, sinks, dec):
    n_tok = x2d.shape[0]
    n_seq = ck_t.shape[0]
    tile = SAMPLE_TILE
    nb = tile // dec
    bc, bn = _sample_bias(nb, dec)
    head = [lw[n] for n in _HEAD_NAMES]
    tail = [lw[n] for n in _TAIL_NAMES]
    consts = head + [wbd, bsp, jnp.asarray(bc), jnp.asarray(bn)]
    cache_spec = pl.BlockSpec((nb, KV_WIDTH, WINDOW), lambda i: (i, 0, 0))
    in_specs = ([_row_spec(tile, D_MODEL), cache_spec, cache_spec] + [_const_spec(a.shape) for a in consts]
                + [_smem_spec()] + [_const_spec(a.shape) for a in tail])
    out_shape = (jax.ShapeDtypeStruct((n_tok, D_MODEL), F32),
                 jax.ShapeDtypeStruct((n_seq, KV_WIDTH, WINDOW), F32),
                 jax.ShapeDtypeStruct((n_seq, KV_WIDTH, WINDOW), F32),
                 jax.ShapeDtypeStruct((n_tok, A_WIDTH), F32))
    out_specs = (_row_spec(tile, D_MODEL), cache_spec, cache_spec, _row_spec(tile, A_WIDTH))
    kern = functools.partial(_sample_kernel, tile=tile, nb=nb, dec=dec)
    return pl.pallas_call(
        kern,
        out_shape=out_shape,
        grid=(n_tok // tile,),
        in_specs=in_specs,
        out_specs=out_specs,
        scratch_shapes=[pltpu.VMEM((tile, N_EXPERTS * D_EXPERT), BF16)],
        compiler_params=pltpu.CompilerParams(dimension_semantics=("arbitrary",),
                                             vmem_limit_bytes=VMEM_LIMIT_BYTES),
        name="layer_sample",
    )(x2d, ck_t, cv_t, *consts, sinks, *tail)


def _positions_last(c):
    b, w = c.shape[:2]
    return jnp.transpose(c, (0, 2, 3, 1)).reshape(b, KV_WIDTH, w)


def _positions_first(c_t):
    b, _, w = c_t.shape
    return jnp.transpose(c_t.reshape(b, N_KV, HEAD_DIM, w), (0, 3, 1, 2))


def kernel(x_prompt, x_sample, cache_k, cache_v, g_attn_norm, w_in, g_v_a, w_spatial, b_spatial, g_q, g_k, attn_sinks, g_out_a, g_out_b, w_out, g_ffn_norm, w_coarse, b_coarse, w_fine, b_fine, w_gate, w_up, w_down):
    depth = w_in.shape[0]
    batch, seq, _ = x_prompt.shape
    dbatch, dec, _ = x_sample.shape
    win = cache_k.shape[2]
    assert win == WINDOW and seq % PROMPT_TILE == 0 and (dbatch * dec) % SAMPLE_TILE == 0
    assert PAIR % dec == 0 and SAMPLE_TILE % dec == 0 and LANES % dec == 0

    hp = x_prompt.reshape(batch * seq, D_MODEL)
    hs = x_sample.reshape(dbatch * dec, D_MODEL)
    kp_l, vp_l, ks_l, vs_l, cv_l = [], [], [], [], []
    for l in range(depth):
        lw = _layer_weights(l, g_attn_norm, w_in, g_v_a, g_q, g_k, g_out_a, g_out_b, w_out, g_ffn_norm,
                            w_coarse, b_coarse, w_fine, b_fine, w_gate, w_up, w_down)
        sinks = attn_sinks[l].astype(F32)
        wbd_p, bsp_p = _spatial_tables(w_spatial[l], b_spatial[l], CHUNK, PROMPT_TILE)
        wbd_s, bsp_s = _spatial_tables(w_spatial[l], b_spatial[l], dec, SAMPLE_TILE)

        hp, kp, vp = _run_prompt(hp, lw, wbd_p, bsp_p, sinks, seq)
        kp_l.append(kp.reshape(batch, WINDOW, N_KV, HEAD_DIM))
        vp_l.append(vp.reshape(batch, WINDOW, N_KV, HEAD_DIM))

        hs, wk, wv, va = _run_sample(hs, _positions_last(cache_k[l]), _positions_last(cache_v[l]),
                                     lw, wbd_s, bsp_s, sinks, dec)
        ks_l.append(_positions_first(wk))
        vs_l.append(_positions_first(wv))
        cv_l.append(va.reshape(dbatch, dec, A_GROUPS, A_CH))

    return (hp.reshape(batch, seq, D_MODEL), hs.reshape(dbatch, dec, D_MODEL),
            jnp.stack(kp_l, axis=0), jnp.stack(vp_l, axis=0),
            jnp.stack(ks_l, axis=0), jnp.stack(vs_l, axis=0), jnp.stack(cv_l, axis=0))
```

```python
import functools

import numpy as np
import jax
import jax.numpy as jnp
from jax import lax
from jax.experimental import pallas as pl
from jax.experimental.pallas import tpu as pltpu

D_MODEL = 1024
CHUNK = 128
A_GROUPS = 4
A_WIDTH = 512
A_CH = 128
N_HEADS = 8
N_KV = 2
Q_PER_KV = 4
HEAD_DIM = 64
B_WIDTH = 512
KV_WIDTH = 128
WINDOW = 128
IN_COLS = 2 * A_WIDTH + B_WIDTH + 2 * KV_WIDTH
N_EXPERT_GROUPS = 4
EXPERTS_PER_GROUP = 4
N_EXPERTS = 16
D_EXPERT = 256
EPS = 1e-6

LANES = 128
PAIR = 2 * CHUNK
ROUTER_LANES = 128
PROMPT_TILE = 512
SAMPLE_TILE = 128
VMEM_LIMIT_BYTES = 60 * 1024 * 1024

F32 = jnp.float32
BF16 = jnp.bfloat16
NEG_INF = float("-inf")


def _slopes():
    return np.array([2.0 ** (-8.0 * (h + 1) / N_HEADS) for h in range(N_HEADS)], np.float64)


def _prompt_bias():
    t = np.arange(WINDOW)[:, None]
    s = np.arange(2 * WINDOW)[None, :]
    dist = t + WINDOW - s
    valid = (dist >= 0) & (dist < WINDOW)
    sl = _slopes()
    out = np.full((N_KV, Q_PER_KV * WINDOW, 2 * WINDOW), -np.inf, np.float32)
    for kh in range(N_KV):
        for g in range(Q_PER_KV):
            b = np.where(valid, -sl[kh * Q_PER_KV + g] * dist, -np.inf)
            out[kh, g * WINDOW:(g + 1) * WINDOW] = b
    return out


def _sample_bias(nb, dec):
    sl = _slopes()
    t = np.arange(dec)[:, None]
    j = np.arange(WINDOW)[None, :]
    dist_c = t + WINDOW - j
    valid_c = (dist_c >= 0) & (dist_c < WINDOW)
    bc = np.full((N_HEADS * dec, WINDOW), -np.inf, np.float32)
    tp = np.arange(dec)[None, :]
    dist_n = t - tp
    valid_n = dist_n >= 0
    bn = np.full((nb, N_HEADS * dec, nb * dec), -np.inf, np.float32)
    for h in range(N_HEADS):
        bc[h * dec:(h + 1) * dec] = np.where(valid_c, -sl[h] * dist_c, -np.inf)
        blk = np.where(valid_n, -sl[h] * dist_n, -np.inf)
        for b in range(nb):
            bn[b, h * dec:(h + 1) * dec, b * dec:(b + 1) * dec] = blk
    return bc, bn


def _seg_ones(width, seg):
    i = np.arange(width)
    return (i[:, None] // seg == i[None, :] // seg).astype(np.float32)


def _causal_block_mask(period, size):
    i = np.arange(size)
    same = i[:, None] // period == i[None, :] // period
    return (same & (i[None, :] % period <= i[:, None] % period)).astype(np.float32)


def _dot(a, b):
    return jnp.dot(a, b, preferred_element_type=F32)


def _dot_nt(a, b):
    return lax.dot_general(a, b, (((1,), (1,)), ((), ())), preferred_element_type=F32)


def _rms(x, g):
    ms = jnp.mean(x * x, axis=-1, keepdims=True)
    return (x * lax.rsqrt(ms + EPS)) * g


def _seg_rms(x, ones_bf, g, seg):
    ss = _dot((x * x).astype(BF16), ones_bf)
    return (x * lax.rsqrt(ss * (1.0 / seg) + EPS)) * g


def _project(x, g_attn_ref, w_in_ref, g_va_ref, g_q_ref, g_k_ref, segq_ref, segk_ref):
    xn = _rms(x, g_attn_ref[...]).astype(BF16)
    z = _dot(xn, w_in_ref[...])
    i1, i2, i3, i4 = A_WIDTH, 2 * A_WIDTH, 2 * A_WIDTH + B_WIDTH, 2 * A_WIDTH + B_WIDTH + KV_WIDTH
    u = jax.nn.gelu(z[:, :i1])
    va_pre = jax.nn.gelu(z[:, i1:i2])
    g_va = g_va_ref[...]
    va = jnp.concatenate(
        [_rms(va_pre[:, g * A_CH:(g + 1) * A_CH], g_va[:, g * A_CH:(g + 1) * A_CH]) for g in range(A_GROUPS)],
        axis=1)
    q = _seg_rms(z[:, i2:i3], segq_ref[...], g_q_ref[...], HEAD_DIM)
    k = _seg_rms(z[:, i3:i4], segk_ref[...], g_k_ref[...], HEAD_DIM)
    v = z[:, i4:]
    return u, va, q, k, v


def _chunk_mlp(va_bf, wsp_ref, bsp_ref, tile):
    pair = wsp_ref.shape[1]
    n_pairs = tile // pair
    outs = []
    for g in range(A_GROUPS):
        cols = [va_bf[p * pair:(p + 1) * pair, g * A_CH:(g + 1) * A_CH] for p in range(n_pairs)]
        rhs = cols[0] if n_pairs == 1 else jnp.concatenate(cols, axis=1)
        o = _dot(wsp_ref[g], rhs)
        rows = [o[:, p * A_CH:(p + 1) * A_CH] for p in range(n_pairs)]
        outs.append(rows[0] if n_pairs == 1 else jnp.concatenate(rows, axis=0))
    return jnp.concatenate(outs, axis=1) + bsp_ref[...]


def _lane_lo():
    return lax.broadcasted_iota(jnp.int32, (1, LANES), 1) < HEAD_DIM


def _swap_halves(x):
    return pltpu.roll(x, HEAD_DIM, axis=1)


def _head_lhs(q_groups, q_groups_swapped, head, lo):
    j, half = divmod(head, 2)
    kv = head // Q_PER_KV
    src = q_groups[j] if half == kv else q_groups_swapped[j]
    return jnp.where(lo if kv == 0 else jnp.logical_not(lo), src, 0.0)


def _merge_heads(o_heads, lo):
    groups = []
    for j in range(N_HEADS // 2):
        kv = (2 * j) // Q_PER_KV
        if kv == 0:
            groups.append(jnp.where(lo, o_heads[2 * j], _swap_halves(o_heads[2 * j + 1])))
        else:
            groups.append(jnp.where(lo, _swap_halves(o_heads[2 * j]), o_heads[2 * j + 1]))
    return jnp.concatenate(groups, axis=1)


def _dup_halves(x, lo):
    xs = _swap_halves(x)
    return jnp.where(lo, x, xs).astype(BF16), jnp.where(lo, xs, x).astype(BF16)


def _swa_prompt(q, k, v, kprev_ref, vprev_ref, bias_ref, sinks_ref, is_first, tile):
    lo = _lane_lo()
    hi = jnp.logical_not(lo)
    col = lax.broadcasted_iota(jnp.int32, (WINDOW, 2 * WINDOW), 1)
    first_mask = jnp.where(col < WINDOW, jnp.where(is_first, NEG_INF, 0.0).astype(F32), 0.0)
    kd = _dup_halves(k, lo)
    vd = _dup_halves(v, lo)
    nblk = tile // WINDOW
    rows_out = []
    for i in range(nblk):
        r0, r1 = i * WINDOW, (i + 1) * WINDOW
        qg = [q[r0:r1, j * LANES:(j + 1) * LANES] for j in range(N_HEADS // 2)]
        o_heads = []
        for kh in range(N_KV):
            if i == 0:
                kp, vp = kprev_ref[kh], vprev_ref[kh]
            else:
                kp, vp = kd[kh][r0 - WINDOW:r0], vd[kh][r0 - WINDOW:r0]
            kb = jnp.concatenate([kp, kd[kh][r0:r1]], axis=0)
            vb = jnp.concatenate([vp, vd[kh][r0:r1]], axis=0)
            heads = [kh * Q_PER_KV + g for g in range(Q_PER_KV)]
            lhs = jnp.concatenate(
                [jnp.where(lo if h % 2 == 0 else hi, qg[h // 2], 0.0) for h in heads], axis=0).astype(BF16)
            s = _dot_nt(lhs, kb) + bias_ref[kh]
            ps, linvs = [], []
            for g in range(Q_PER_KV):
                sg = s[g * WINDOW:(g + 1) * WINDOW]
                if i == 0:
                    sg = sg + first_mask
                sink = sinks_ref[heads[g]]
                mg = jnp.maximum(jnp.max(sg, axis=-1, keepdims=True), sink)
                pg = jnp.exp(sg - mg)
                lg = jnp.sum(pg, axis=-1, keepdims=True) + jnp.exp(sink - mg)
                ps.append(pg.astype(BF16))
                linvs.append(1.0 / lg)
            o = _dot(jnp.concatenate(ps, axis=0), vb)
            o_heads += [o[g * WINDOW:(g + 1) * WINDOW] * linvs[g] for g in range(Q_PER_KV)]
        rows_out.append(jnp.concatenate(
            [jnp.where(lo, o_heads[2 * j], o_heads[2 * j + 1]) for j in range(N_HEADS // 2)], axis=1))
    for kh in range(N_KV):
        kprev_ref[kh] = kd[kh][tile - WINDOW:tile]
        vprev_ref[kh] = vd[kh][tile - WINDOW:tile]
    return jnp.concatenate(rows_out, axis=0)


def _shift_window(cache_ref, new_t, out_ref, nb, dec):
    lane = lax.broadcasted_iota(jnp.int32, (1, WINDOW), 1)
    keep = lane < WINDOW - dec
    per_tile = LANES // dec
    for b in range(nb):
        src = new_t[:, (b // per_tile) * LANES:(b // per_tile + 1) * LANES]
        new_cols = pltpu.roll(src, (WINDOW - dec - (b % per_tile) * dec) % LANES, axis=1)
        old = pltpu.roll(cache_ref[b], WINDOW - dec, axis=1)
        out_ref[b] = jnp.where(keep, old, new_cols)


def _swa_sample(q, k, v, ck_ref, cv_ref, wk_ref, wv_ref, bias_c_ref, bias_n_ref, sinks_ref, nb, dec):
    lo = _lane_lo()
    qg = [q[:, j * LANES:(j + 1) * LANES] for j in range(N_HEADS // 2)]
    qgs = [_swap_halves(x) for x in qg]
    qb = jnp.concatenate(
        [_head_lhs(qg, qgs, h, lo).reshape(nb, dec, LANES) for h in range(N_HEADS)], axis=1)
    rows = N_HEADS * dec
    qb_bf = qb.astype(BF16)
    k_bf = k.astype(BF16)
    v_bf = v.astype(BF16)
    ck = ck_ref[...].astype(BF16)
    cv = cv_ref[...].astype(BF16)
    sc = jnp.einsum('bqc,bcp->bqp', qb_bf, ck, preferred_element_type=F32) + bias_c_ref[...][None]
    sn = _dot_nt(qb_bf.reshape(nb * rows, LANES), k_bf).reshape(nb, rows, nb * dec) + bias_n_ref[...]
    sink = jnp.concatenate(
        [jnp.full((1, dec, 1), sinks_ref[h], F32) for h in range(N_HEADS)], axis=1)
    m = jnp.maximum(jnp.max(sc, axis=-1, keepdims=True), jnp.max(sn, axis=-1, keepdims=True))
    m = jnp.maximum(m, sink)
    pc = jnp.exp(sc - m)
    pn = jnp.exp(sn - m)
    l = jnp.sum(pc, axis=-1, keepdims=True) + jnp.sum(pn, axis=-1, keepdims=True) + jnp.exp(sink - m)
    oc = jnp.einsum('bqp,bcp->bqc', pc.astype(BF16), cv, preferred_element_type=F32)
    on = _dot(pn.reshape(nb * rows, nb * dec).astype(BF16), v_bf).reshape(nb, rows, LANES)
    o = (oc + on) * (1.0 / l)
    o_heads = [o[:, h * dec:(h + 1) * dec, :].reshape(nb * dec, LANES) for h in range(N_HEADS)]
    _shift_window(ck_ref, k.T, wk_ref, nb, dec)
    _shift_window(cv_ref, v.T, wv_ref, nb, dec)
    return _merge_heads(o_heads, lo)


def _route(hn, hn_bf, wr_hi_ref, wr_lo_ref, br_ref):
    lo_part = (hn - hn_bf.astype(F32)).astype(BF16)
    wr_hi = wr_hi_ref[...]
    logits = _dot(hn_bf, wr_hi) + _dot(lo_part, wr_hi) + _dot(hn_bf, wr_lo_ref[...]) + br_ref[...]
    lane = lax.broadcasted_iota(jnp.int32, logits.shape, 1)
    big = jnp.int32(ROUTER_LANES)
    lc = jnp.where(lane < N_EXPERT_GROUPS, logits, NEG_INF)
    mx = jnp.max(lc, axis=-1, keepdims=True)
    g_idx = jnp.min(jnp.where(lc == mx, lane, big), axis=-1, keepdims=True)
    p_g = 1.0 / jnp.sum(jnp.exp(lc - mx), axis=-1, keepdims=True)
    e_lane = lane - N_EXPERT_GROUPS
    in_group = (e_lane >= 0) & (e_lane < N_EXPERTS) & ((e_lane >> 2) == g_idx)
    lf = jnp.where(in_group, logits, NEG_INF)
    v1 = jnp.max(lf, axis=-1, keepdims=True)
    i1 = jnp.min(jnp.where(lf == v1, lane, big), axis=-1, keepdims=True)
    lf2 = jnp.where(lane == i1, NEG_INF, lf)
    v2 = jnp.max(lf2, axis=-1, keepdims=True)
    i2 = jnp.min(jnp.where(lf2 == v2, lane, big), axis=-1, keepdims=True)
    e = jnp.exp(v2 - v1)
    w1 = p_g / (1.0 + e)
    w2 = w1 * e
    return lane, g_idx, i1, i2, w1, w2


def _silu(x):
    return x * (1.0 / (1.0 + jnp.exp(-x)))


def _moe_dense(h, g_ffn_ref, wr_hi_ref, wr_lo_ref, br_ref, wg_ref, wu_ref, wd_ref, hcat_ref):
    hn = _rms(h, g_ffn_ref[...])
    hn_bf = hn.astype(BF16)
    lane, _, i1, i2, w1, w2 = _route(hn, hn_bf, wr_hi_ref, wr_lo_ref, br_ref)
    combine = jnp.where(lane == i1, w1, jnp.where(lane == i2, w2, 0.0))
    for e in range(N_EXPERTS):
        gate = _dot(hn_bf, wg_ref[e])
        up = _dot(hn_bf, wu_ref[e])
        c = combine[:, N_EXPERT_GROUPS + e:N_EXPERT_GROUPS + e + 1]
        hcat_ref[:, e * D_EXPERT:(e + 1) * D_EXPERT] = (_silu(gate) * up * c).astype(BF16)
    return _dot(hcat_ref[...], wd_ref[...])


GATE_LANES = EXPERTS_PER_GROUP
GROUP_LANE = EXPERTS_PER_GROUP
LOW_SHIFT = 8
SORT_BLOCK = 128


def _moe_grouped(h, g_ffn_ref, wr_hi_ref, wr_lo_ref, br_ref, wg_ref, wu_ref, wd_ref, lstrict_ref,
                 xs_ref, es_ref, ys_ref, tile):
    hn = _rms(h, g_ffn_ref[...])
    hn_bf = hn.astype(BF16)
    lane, g_idx, i1, i2, w1, w2 = _route(hn, hn_bf, wr_hi_ref, wr_lo_ref, br_ref)

    onehot = lane == g_idx
    gmat = jnp.where(onehot, 1.0, 0.0)
    before = _dot(lstrict_ref[...], gmat.astype(BF16))
    rank = jnp.sum(jnp.where(onehot, before, 0.0), axis=-1, keepdims=True)
    counts = jnp.sum(gmat, axis=0, keepdims=True)
    lane1 = lax.broadcasted_iota(jnp.int32, counts.shape, 1)
    offs = [jnp.float32(0.0)]
    for g in range(N_EXPERT_GROUPS - 1):
        offs.append(offs[-1] + jnp.sum(jnp.where(lane1 == g, counts, 0.0)))
    offv = jnp.zeros_like(rank)
    for g in range(1, N_EXPERT_GROUPS):
        offv = offv + jnp.where(g_idx == g, offs[g], 0.0)
    pos = (rank + offv).astype(jnp.int32)
    dest = lax.broadcasted_iota(jnp.int32, (tile, tile), 1)
    pt = jnp.where(dest == pos, 1.0, 0.0).astype(BF16)

    j1 = (i1 - N_EXPERT_GROUPS) & (EXPERTS_PER_GROUP - 1)
    j2 = (i2 - N_EXPERT_GROUPS) & (EXPERTS_PER_GROUP - 1)
    gates = jnp.where(lane == j1, w1, jnp.where(lane == j2, w2, 0.0))
    gates_hi = gates.astype(BF16).astype(F32)
    gates_lo = (gates - gates_hi).astype(BF16).astype(F32)
    side = gates_hi + pltpu.roll(gates_lo, LOW_SHIFT, axis=1) + jnp.where(lane == GROUP_LANE, g_idx.astype(F32), 0.0)
    x_ext = jnp.concatenate([hn_bf, side.astype(BF16)], axis=1)
    srt = lax.dot_general(pt, x_ext, (((0,), (0,)), ((), ())), preferred_element_type=F32)
    xs_ref[...] = srt[:, :D_MODEL].astype(BF16)
    es = srt[:, D_MODEL:]
    es_ref[...] = es + pltpu.roll(es, LANES - LOW_SHIFT, axis=1)

    def group_of(row):
        g = jnp.int32(0)
        for k in range(1, N_EXPERT_GROUPS):
            g = g + (jnp.float32(row) >= offs[k]).astype(jnp.int32)
        return g

    for b in range(tile // SORT_BLOCK):
        rows = pl.ds(b * SORT_BLOCK, SORT_BLOCK)
        ys_ref[rows, :] = jnp.zeros((SORT_BLOCK, D_MODEL), F32)

        def body(g, carry, rows=rows):
            xb = xs_ref[rows, :]
            eb = es_ref[rows, :]
            mine = jnp.where(eb[:, GROUP_LANE:GROUP_LANE + 1] == g.astype(F32), 1.0, 0.0)
            parts = []
            for j in range(EXPERTS_PER_GROUP):
                e = g * EXPERTS_PER_GROUP + j
                gate = _dot(xb, wg_ref[e])
                up = _dot(xb, wu_ref[e])
                parts.append((_silu(gate) * up * (eb[:, j:j + 1] * mine)).astype(BF16))
            ys_ref[rows, :] += _dot(jnp.concatenate(parts, axis=1), wd_ref[g])
            return carry

        lax.fori_loop(group_of(b * SORT_BLOCK), group_of((b + 1) * SORT_BLOCK - 1) + 1, body, 0)
    return _dot(pt, ys_ref[...].astype(BF16))


def _merge(x, a_out, b_out, g_out_ref, w_out_ref):
    g_out = g_out_ref[...]
    mix = jnp.concatenate(
        [_rms(a_out, g_out[:, :A_WIDTH]), _rms(b_out, g_out[:, A_WIDTH:])], axis=1).astype(BF16)
    return x + _dot(mix, w_out_ref[...])


def _prompt_kernel(x_ref, g_attn_ref, w_in_ref, g_va_ref, g_q_ref, g_k_ref, segq_ref, segk_ref,
                   wsp_ref, bsp_ref, bias_ref, sinks_ref, g_out_ref, w_out_ref, g_ffn_ref,
                   wr_hi_ref, wr_lo_ref, br_ref, wg_ref, wu_ref, wd_ref, lstrict_ref,
                   y_ref, kwin_ref, vwin_ref,
                   kprev_ref, vprev_ref, xs_ref, es_ref, ys_ref, *, tile, tiles_per_seq):
    is_first = (pl.program_id(0) % tiles_per_seq) == 0

    @pl.when(is_first)
    def _():
        kprev_ref[...] = jnp.zeros_like(kprev_ref)
        vprev_ref[...] = jnp.zeros_like(vprev_ref)

    x = x_ref[...]
    u, va, q, k, v = _project(x, g_attn_ref, w_in_ref, g_va_ref, g_q_ref, g_k_ref, segq_ref, segk_ref)
    kwin_ref[...] = k[tile - WINDOW:tile]
    vwin_ref[...] = v[tile - WINDOW:tile]
    a_out = u * _chunk_mlp(va.astype(BF16), wsp_ref, bsp_ref, tile)
    b_out = _swa_prompt(q * (HEAD_DIM ** -0.5), k, v, kprev_ref, vprev_ref, bias_ref, sinks_ref, is_first, tile)
    h = _merge(x, a_out, b_out, g_out_ref, w_out_ref)
    y_ref[...] = h + _moe_grouped(h, g_ffn_ref, wr_hi_ref, wr_lo_ref, br_ref, wg_ref, wu_ref, wd_ref,
                                  lstrict_ref, xs_ref, es_ref, ys_ref, tile)


def _sample_kernel(x_ref, ck_ref, cv_ref, g_attn_ref, w_in_ref, g_va_ref, g_q_ref, g_k_ref, segq_ref,
                   segk_ref, wsp_ref, bsp_ref, bias_c_ref, bias_n_ref, sinks_ref, g_out_ref, w_out_ref,
                   g_ffn_ref, wr_hi_ref, wr_lo_ref, br_ref, wg_ref, wu_ref, wd_ref,
                   y_ref, wk_ref, wv_ref, va_ref,
                   hcat_ref, *, tile, nb, dec):
    x = x_ref[...]
    u, va, q, k, v = _project(x, g_attn_ref, w_in_ref, g_va_ref, g_q_ref, g_k_ref, segq_ref, segk_ref)
    va_ref[...] = va
    a_out = u * _chunk_mlp(va.astype(BF16), wsp_ref, bsp_ref, tile)
    b_out = _swa_sample(q * (HEAD_DIM ** -0.5), k, v, ck_ref, cv_ref, wk_ref, wv_ref,
                        bias_c_ref, bias_n_ref, sinks_ref, nb, dec)
    h = _merge(x, a_out, b_out, g_out_ref, w_out_ref)
    y_ref[...] = h + _moe_dense(h, g_ffn_ref, wr_hi_ref, wr_lo_ref, br_ref, wg_ref, wu_ref, wd_ref, hcat_ref)


def _const_spec(shape):
    nd = len(shape)
    return pl.BlockSpec(shape, lambda i: (0,) * nd, pipeline_mode=pl.Buffered(1))


def _row_spec(tile, width):
    return pl.BlockSpec((tile, width), lambda i: (i, 0))


def _smem_spec():
    return pl.BlockSpec(memory_space=pltpu.SMEM)


def _layer_weights(l, g_attn_norm, w_in, g_v_a, g_q, g_k, g_out_a, g_out_b, w_out, g_ffn_norm,
                   w_coarse, b_coarse, w_fine, b_fine, w_gate, w_up, w_down):
    pad = ROUTER_LANES - N_EXPERT_GROUPS - N_EXPERTS
    wr = jnp.concatenate([w_coarse[l], w_fine[l], jnp.zeros((D_MODEL, pad), F32)], axis=1)
    wr_hi = wr.astype(BF16)
    br = jnp.concatenate([b_coarse[l], b_fine[l], jnp.zeros((pad,), F32)]).reshape(1, ROUTER_LANES)
    return dict(
        g_attn=g_attn_norm[l].reshape(1, D_MODEL),
        w_in=w_in[l].astype(BF16),
        g_va=g_v_a[l].reshape(1, A_WIDTH),
        g_q=jnp.tile(g_q[l], N_HEADS).reshape(1, B_WIDTH),
        g_k=jnp.tile(g_k[l], N_KV).reshape(1, KV_WIDTH),
        segq=jnp.asarray(_seg_ones(B_WIDTH, HEAD_DIM), BF16),
        segk=jnp.asarray(_seg_ones(KV_WIDTH, HEAD_DIM), BF16),
        g_out=jnp.concatenate([g_out_a[l], g_out_b[l]]).reshape(1, D_MODEL),
        w_out=w_out[l].astype(BF16),
        g_ffn=g_ffn_norm[l].reshape(1, D_MODEL),
        wr_hi=wr_hi,
        wr_lo=(wr - wr_hi.astype(F32)).astype(BF16),
        br=br,
        wg=w_gate[l].astype(BF16),
        wu=w_up[l].astype(BF16),
        wd=w_down[l].astype(BF16).reshape(N_EXPERTS * D_EXPERT, D_MODEL),
    )


def _spatial_tables(ws, bs, period, tile):
    pair = min(PAIR, tile)
    reps = pair // period
    wbd = (jnp.tile(ws[:, :period, :period], (1, reps, reps)) * _causal_block_mask(period, pair)).astype(BF16)
    b = jnp.tile(bs[:, :period].T, (tile // period, 1))
    bsp = jnp.repeat(b, A_CH, axis=1)
    return wbd, bsp


_TAIL_NAMES = ("g_out", "w_out", "g_ffn", "wr_hi", "wr_lo", "br", "wg", "wu", "wd")
_HEAD_NAMES = ("g_attn", "w_in", "g_va", "g_q", "g_k", "segq", "segk")


def _run_prompt(x2d, lw, wbd, bsp, sinks, seq_len):
    n_tok = x2d.shape[0]
    tile = PROMPT_TILE
    tiles_per_seq = seq_len // tile
    n_seq = n_tok // seq_len
    bias = jnp.asarray(_prompt_bias())
    head = [lw[n] for n in _HEAD_NAMES]
    tail = [lw[n] for n in _TAIL_NAMES]
    tail[-1] = tail[-1].reshape(N_EXPERT_GROUPS, EXPERTS_PER_GROUP * D_EXPERT, D_MODEL)
    tail.append(jnp.asarray(np.tril(np.ones((tile, tile), np.float32), -1), BF16))
    consts = head + [wbd, bsp, bias]
    in_specs = ([_row_spec(tile, D_MODEL)] + [_const_spec(a.shape) for a in consts] + [_smem_spec()]
                + [_const_spec(a.shape) for a in tail])
    win_spec = pl.BlockSpec((WINDOW, KV_WIDTH), lambda i: (i // tiles_per_seq, 0))
    out_shape = (jax.ShapeDtypeStruct((n_tok, D_MODEL), F32),
                 jax.ShapeDtypeStruct((n_seq * WINDOW, KV_WIDTH), F32),
                 jax.ShapeDtypeStruct((n_seq * WINDOW, KV_WIDTH), F32))
    out_specs = (_row_spec(tile, D_MODEL), win_spec, win_spec)
    kern = functools.partial(_prompt_kernel, tile=tile, tiles_per_seq=tiles_per_seq)
    return pl.pallas_call(
        kern,
        out_shape=out_shape,
        grid=(n_tok // tile,),
        in_specs=in_specs,
        out_specs=out_specs,
        scratch_shapes=[pltpu.VMEM((N_KV, WINDOW, KV_WIDTH), BF16), pltpu.VMEM((N_KV, WINDOW, KV_WIDTH), BF16),
                        pltpu.VMEM((tile, D_MODEL), BF16), pltpu.VMEM((tile, LANES), F32),
                        pltpu.VMEM((tile, D_MODEL), F32)],
        compiler_params=pltpu.CompilerParams(dimension_semantics=("arbitrary",),
                                             vmem_limit_bytes=VMEM_LIMIT_BYTES),
        name="layer_prompt",
    )(x2d, *consts, sinks, *tail)


def _run_sample(x2d, ck_t, cv_t, lw, wbd, bsp, sinks, dec):
    n_tok = x2d.shape[0]
    n_seq = ck_t.shape[0]
    tile = SAMPLE_TILE
    nb = tile // dec
    bc, bn = _sample_bias(nb, dec)
    head = [lw[n] for n in _HEAD_NAMES]
    tail = [lw[n] for n in _TAIL_NAMES]
    consts = head + [wbd, bsp, jnp.asarray(bc), jnp.asarray(bn)]
    cache_spec = pl.BlockSpec((nb, KV_WIDTH, WINDOW), lambda i: (i, 0, 0))
    in_specs = ([_row_spec(tile, D_MODEL), cache_spec, cache_spec] + [_const_spec(a.shape) for a in consts]
                + [_smem_spec()] + [_const_spec(a.shape) for a in tail])
    out_shape = (jax.ShapeDtypeStruct((n_tok, D_MODEL), F32),
                 jax.ShapeDtypeStruct((n_seq, KV_WIDTH, WINDOW), F32),
                 jax.ShapeDtypeStruct((n_seq, KV_WIDTH, WINDOW), F32),
                 jax.ShapeDtypeStruct((n_tok, A_WIDTH), F32))
    out_specs = (_row_spec(tile, D_MODEL), cache_spec, cache_spec, _row_spec(tile, A_WIDTH))
    kern = functools.partial(_sample_kernel, tile=tile, nb=nb, dec=dec)
    return pl.pallas_call(
        kern,
        out_shape=out_shape,
        grid=(n_tok // tile,),
        in_specs=in_specs,
        out_specs=out_specs,
        scratch_shapes=[pltpu.VMEM((tile, N_EXPERTS * D_EXPERT), BF16)],
        compiler_params=pltpu.CompilerParams(dimension_semantics=("arbitrary",),
                                             vmem_limit_bytes=VMEM_LIMIT_BYTES),
        name="layer_sample",
    )(x2d, ck_t, cv_t, *consts, sinks, *tail)


def _positions_last(c):
    b, w = c.shape[:2]
    return jnp.transpose(c, (0, 2, 3, 1)).reshape(b, KV_WIDTH, w)


def _positions_first(c_t):
    b, _, w = c_t.shape
    return jnp.transpose(c_t.reshape(b, N_KV, HEAD_DIM, w), (0, 3, 1, 2))


def kernel(x_prompt, x_sample, cache_k, cache_v, g_attn_norm, w_in, g_v_a, w_spatial, b_spatial, g_q, g_k, attn_sinks, g_out_a, g_out_b, w_out, g_ffn_norm, w_coarse, b_coarse, w_fine, b_fine, w_gate, w_up, w_down):
    depth = w_in.shape[0]
    batch, seq, _ = x_prompt.shape
    dbatch, dec, _ = x_sample.shape
    win = cache_k.shape[2]
    assert win == WINDOW and seq % PROMPT_TILE == 0 and (dbatch * dec) % SAMPLE_TILE == 0
    assert PAIR % dec == 0 and SAMPLE_TILE % dec == 0 and LANES % dec == 0

    hp = x_prompt.reshape(batch * seq, D_MODEL)
    hs = x_sample.reshape(dbatch * dec, D_MODEL)
    kp_l, vp_l, ks_l, vs_l, cv_l = [], [], [], [], []
    for l in range(depth):
        lw = _layer_weights(l, g_attn_norm, w_in, g_v_a, g_q, g_k, g_out_a, g_out_b, w_out, g_ffn_norm,
                            w_coarse, b_coarse, w_fine, b_fine, w_gate, w_up, w_down)
        sinks = attn_sinks[l].astype(F32)
        wbd_p, bsp_p = _spatial_tables(w_spatial[l], b_spatial[l], CHUNK, PROMPT_TILE)
        wbd_s, bsp_s = _spatial_tables(w_spatial[l], b_spatial[l], dec, SAMPLE_TILE)

        hp, kp, vp = _run_prompt(hp, lw, wbd_p, bsp_p, sinks, seq)
        kp_l.append(kp.reshape(batch, WINDOW, N_KV, HEAD_DIM))
        vp_l.append(vp.reshape(batch, WINDOW, N_KV, HEAD_DIM))

        hs, wk, wv, va = _run_sample(hs, _positions_last(cache_k[l]), _positions_last(cache_v[l]),
                                     lw, wbd_s, bsp_s, sinks, dec)
        ks_l.append(_positions_first(wk))
        vs_l.append(_positions_first(wv))
        cv_l.append(va.reshape(dbatch, dec, A_GROUPS, A_CH))

    return (hp.reshape(batch, seq, D_MODEL), hs.reshape(dbatch, dec, D_MODEL),
            jnp.stack(kp_l, axis=0), jnp.stack(vp_l, axis=0),
            jnp.stack(ks_l, axis=0), jnp.stack(vs_l, axis=0), jnp.stack(cv_l, axis=0))
```

```python
import functools

import numpy as np
import jax
import jax.numpy as jnp
from jax import lax
from jax.experimental import pallas as pl
from jax.experimental.pallas import tpu as pltpu

D_MODEL = 1024
CHUNK = 128
A_GROUPS = 4
A_WIDTH = 512
A_CH = 128
N_HEADS = 8
N_KV = 2
Q_PER_KV = 4
HEAD_DIM = 64
B_WIDTH = 512
KV_WIDTH = 128
WINDOW = 128
IN_COLS = 2 * A_WIDTH + B_WIDTH + 2 * KV_WIDTH
N_EXPERT_GROUPS = 4
EXPERTS_PER_GROUP = 4
N_EXPERTS = 16
D_EXPERT = 256
EPS = 1e-6

LANES = 128
PAIR = 2 * CHUNK
ROUTER_LANES = 128
PROMPT_TILE = 512
SAMPLE_TILE = 128
VMEM_LIMIT_BYTES = 60 * 1024 * 1024

F32 = jnp.float32
BF16 = jnp.bfloat16
NEG_INF = float("-inf")
LOG2E = 1.4426950408889634
Q_SCALE = (HEAD_DIM ** -0.5) * LOG2E


def _slopes():
    return np.array([2.0 ** (-8.0 * (h + 1) / N_HEADS) for h in range(N_HEADS)], np.float64)


def _prompt_bias():
    t = np.arange(WINDOW)[:, None]
    s = np.arange(2 * WINDOW)[None, :]
    dist = t + WINDOW - s
    valid = (dist >= 0) & (dist < WINDOW)
    sl = _slopes()
    out = np.full((N_KV, Q_PER_KV * WINDOW, 2 * WINDOW), -np.inf, np.float32)
    for kh in range(N_KV):
        for g in range(Q_PER_KV):
            b = np.where(valid, -sl[kh * Q_PER_KV + g] * LOG2E * dist, -np.inf)
            out[kh, g * WINDOW:(g + 1) * WINDOW] = b
    return out


def _sample_bias(nb, dec):
    sl = _slopes()
    t = np.arange(dec)[:, None]
    j = np.arange(WINDOW)[None, :]
    dist_c = t + WINDOW - j
    valid_c = (dist_c >= 0) & (dist_c < WINDOW)
    bc = np.full((N_HEADS * dec, WINDOW), -np.inf, np.float32)
    tp = np.arange(dec)[None, :]
    dist_n = t - tp
    valid_n = dist_n >= 0
    bn = np.full((nb, N_HEADS * dec, nb * dec), -np.inf, np.float32)
    for h in range(N_HEADS):
        bc[h * dec:(h + 1) * dec] = np.where(valid_c, -sl[h] * LOG2E * dist_c, -np.inf)
        blk = np.where(valid_n, -sl[h] * LOG2E * dist_n, -np.inf)
        for b in range(nb):
            bn[b, h * dec:(h + 1) * dec, b * dec:(b + 1) * dec] = blk
    return bc, bn


def _seg_ones(width, seg):
    i = np.arange(width)
    return (i[:, None] // seg == i[None, :] // seg).astype(np.float32)


def _causal_block_mask(period, size):
    i = np.arange(size)
    same = i[:, None] // period == i[None, :] // period
    return (same & (i[None, :] % period <= i[:, None] % period)).astype(np.float32)


def _dot(a, b):
    return jnp.dot(a, b, preferred_element_type=F32)


def _dot_nt(a, b):
    return lax.dot_general(a, b, (((1,), (1,)), ((), ())), preferred_element_type=F32)


def _rms(x, g):
    ms = jnp.mean(x * x, axis=-1, keepdims=True)
    return (x * lax.rsqrt(ms + EPS)) * g


def _seg_rms(x, ones_bf, g, seg):
    ss = _dot((x * x).astype(BF16), ones_bf)
    return (x * lax.rsqrt(ss * (1.0 / seg) + EPS)) * g


GELU_K0 = -2.0 * 0.7978845608028654 * LOG2E
GELU_K1 = GELU_K0 * 0.044715


def _gelu_tanh(x):
    t = (x * x) * GELU_K1 + GELU_K0
    return x * (1.0 / (1.0 + jnp.exp2(x * t)))


def _project(x, g_attn_ref, w_in_ref, g_va_ref, g_q_ref, g_k_ref, segq_ref, segk_ref):
    xn = _rms(x, g_attn_ref[...]).astype(BF16)
    z = _dot(xn, w_in_ref[...])
    i1, i2, i3, i4 = A_WIDTH, 2 * A_WIDTH, 2 * A_WIDTH + B_WIDTH, 2 * A_WIDTH + B_WIDTH + KV_WIDTH
    u = _gelu_tanh(z[:, :i1])
    va_pre = _gelu_tanh(z[:, i1:i2])
    g_va = g_va_ref[...]
    va = jnp.concatenate(
        [_rms(va_pre[:, g * A_CH:(g + 1) * A_CH], g_va[:, g * A_CH:(g + 1) * A_CH]) for g in range(A_GROUPS)],
        axis=1)
    q = _seg_rms(z[:, i2:i3], segq_ref[...], g_q_ref[...], HEAD_DIM)
    k = _seg_rms(z[:, i3:i4], segk_ref[...], g_k_ref[...], HEAD_DIM)
    v = z[:, i4:]
    return u, va, q, k, v


def _chunk_mlp(va_bf, wsp_ref, bsp_ref, tile):
    pair = wsp_ref.shape[1]
    n_pairs = tile // pair
    outs = []
    for g in range(A_GROUPS):
        cols = [va_bf[p * pair:(p + 1) * pair, g * A_CH:(g + 1) * A_CH] for p in range(n_pairs)]
        rhs = cols[0] if n_pairs == 1 else jnp.concatenate(cols, axis=1)
        o = _dot(wsp_ref[g], rhs)
        rows = [o[:, p * A_CH:(p + 1) * A_CH] for p in range(n_pairs)]
        outs.append(rows[0] if n_pairs == 1 else jnp.concatenate(rows, axis=0))
    return jnp.concatenate(outs, axis=1) + bsp_ref[...]


def _lane_lo():
    return lax.broadcasted_iota(jnp.int32, (1, LANES), 1) < HEAD_DIM


def _swap_halves(x):
    return pltpu.roll(x, HEAD_DIM, axis=1)


def _head_lhs(q_groups, q_groups_swapped, head, lo):
    j, half = divmod(head, 2)
    kv = head // Q_PER_KV
    src = q_groups[j] if half == kv else q_groups_swapped[j]
    return jnp.where(lo if kv == 0 else jnp.logical_not(lo), src, 0.0)


def _merge_heads(o_heads, lo):
    groups = []
    for j in range(N_HEADS // 2):
        kv = (2 * j) // Q_PER_KV
        if kv == 0:
            groups.append(jnp.where(lo, o_heads[2 * j], _swap_halves(o_heads[2 * j + 1])))
        else:
            groups.append(jnp.where(lo, _swap_halves(o_heads[2 * j]), o_heads[2 * j + 1]))
    return jnp.concatenate(groups, axis=1)


def _dup_halves(x, lo):
    xs = _swap_halves(x)
    return jnp.where(lo, x, xs).astype(BF16), jnp.where(lo, xs, x).astype(BF16)


def _swa_prompt(q, k, v, kprev_ref, vprev_ref, bias_ref, sinks_ref, is_first, tile):
    lo = _lane_lo()
    hi = jnp.logical_not(lo)
    col = lax.broadcasted_iota(jnp.int32, (WINDOW, 2 * WINDOW), 1)
    first_mask = jnp.where(col < WINDOW, jnp.where(is_first, NEG_INF, 0.0).astype(F32), 0.0)
    kd = _dup_halves(k, lo)
    vd = _dup_halves(v, lo)
    nblk = tile // WINDOW
    rows_out = []
    for i in range(nblk):
        r0, r1 = i * WINDOW, (i + 1) * WINDOW
        qg = [q[r0:r1, j * LANES:(j + 1) * LANES] for j in range(N_HEADS // 2)]
        o_heads = []
        for kh in range(N_KV):
            if i == 0:
                kp, vp = kprev_ref[kh], vprev_ref[kh]
            else:
                kp, vp = kd[kh][r0 - WINDOW:r0], vd[kh][r0 - WINDOW:r0]
            kb = jnp.concatenate([kp, kd[kh][r0:r1]], axis=0)
            vb = jnp.concatenate([vp, vd[kh][r0:r1]], axis=0)
            heads = [kh * Q_PER_KV + g for g in range(Q_PER_KV)]
            lhs = jnp.concatenate(
                [jnp.where(lo if h % 2 == 0 else hi, qg[h // 2], 0.0) for h in heads], axis=0).astype(BF16)
            s = _dot_nt(lhs, kb) + bias_ref[kh]
            ps, linvs = [], []
            for g in range(Q_PER_KV):
                sg = s[g * WINDOW:(g + 1) * WINDOW]
                if i == 0:
                    sg = sg + first_mask
                sink = sinks_ref[heads[g]] * LOG2E
                mg = jnp.maximum(jnp.max(sg, axis=-1, keepdims=True), sink)
                pg = jnp.exp2(sg - mg)
                lg = jnp.sum(pg, axis=-1, keepdims=True) + jnp.exp2(sink - mg)
                ps.append(pg.astype(BF16))
                linvs.append(1.0 / lg)
            o = _dot(jnp.concatenate(ps, axis=0), vb)
            o_heads += [o[g * WINDOW:(g + 1) * WINDOW] * linvs[g] for g in range(Q_PER_KV)]
        rows_out.append(jnp.concatenate(
            [jnp.where(lo, o_heads[2 * j], o_heads[2 * j + 1]) for j in range(N_HEADS // 2)], axis=1))
    for kh in range(N_KV):
        kprev_ref[kh] = kd[kh][tile - WINDOW:tile]
        vprev_ref[kh] = vd[kh][tile - WINDOW:tile]
    return jnp.concatenate(rows_out, axis=0)


def _shift_window(cache_ref, new_t, out_ref, nb, dec):
    lane = lax.broadcasted_iota(jnp.int32, (1, WINDOW), 1)
    keep = lane < WINDOW - dec
    per_tile = LANES // dec
    for b in range(nb):
        src = new_t[:, (b // per_tile) * LANES:(b // per_tile + 1) * LANES]
        new_cols = pltpu.roll(src, (WINDOW - dec - (b % per_tile) * dec) % LANES, axis=1)
        old = pltpu.roll(cache_ref[b], WINDOW - dec, axis=1)
        out_ref[b] = jnp.where(keep, old, new_cols)


def _swa_sample(q, k, v, ck_ref, cv_ref, wk_ref, wv_ref, bias_c_ref, bias_n_ref, sinks_ref, nb, dec):
    lo = _lane_lo()
    qg = [q[:, j * LANES:(j + 1) * LANES] for j in range(N_HEADS // 2)]
    qgs = [_swap_halves(x) for x in qg]
    qb = jnp.concatenate(
        [_head_lhs(qg, qgs, h, lo).reshape(nb, dec, LANES) for h in range(N_HEADS)], axis=1)
    rows = N_HEADS * dec
    qb_bf = qb.astype(BF16)
    k_bf = k.astype(BF16)
    v_bf = v.astype(BF16)
    ck = ck_ref[...].astype(BF16)
    cv = cv_ref[...].astype(BF16)
    sc = jnp.einsum('bqc,bcp->bqp', qb_bf, ck, preferred_element_type=F32) + bias_c_ref[...][None]
    sn = _dot_nt(qb_bf.reshape(nb * rows, LANES), k_bf).reshape(nb, rows, nb * dec) + bias_n_ref[...]
    sink = jnp.concatenate(
        [jnp.full((1, dec, 1), sinks_ref[h] * LOG2E, F32) for h in range(N_HEADS)], axis=1)
    m = jnp.maximum(jnp.max(sc, axis=-1, keepdims=True), jnp.max(sn, axis=-1, keepdims=True))
    m = jnp.maximum(m, sink)
    pc = jnp.exp2(sc - m)
    pn = jnp.exp2(sn - m)
    l = jnp.sum(pc, axis=-1, keepdims=True) + jnp.sum(pn, axis=-1, keepdims=True) + jnp.exp2(sink - m)
    oc = jnp.einsum('bqp,bcp->bqc', pc.astype(BF16), cv, preferred_element_type=F32)
    on = _dot(pn.reshape(nb * rows, nb * dec).astype(BF16), v_bf).reshape(nb, rows, LANES)
    o = (oc + on) * (1.0 / l)
    o_heads = [o[:, h * dec:(h + 1) * dec, :].reshape(nb * dec, LANES) for h in range(N_HEADS)]
    _shift_window(ck_ref, k.T, wk_ref, nb, dec)
    _shift_window(cv_ref, v.T, wv_ref, nb, dec)
    return _merge_heads(o_heads, lo)


def _route(hn_bf, wr_ref, br_ref):
    logits = _dot(hn_bf, wr_ref[...]) + br_ref[...]
    lane = lax.broadcasted_iota(jnp.int32, logits.shape, 1)
    big = jnp.int32(ROUTER_LANES)
    lc = jnp.where(lane < N_EXPERT_GROUPS, logits, NEG_INF)
    mx = jnp.max(lc, axis=-1, keepdims=True)
    g_idx = jnp.min(jnp.where(lc == mx, lane, big), axis=-1, keepdims=True)
    p_g = 1.0 / jnp.sum(jnp.exp(lc - mx), axis=-1, keepdims=True)
    e_lane = lane - N_EXPERT_GROUPS
    in_group = (e_lane >= 0) & (e_lane < N_EXPERTS) & ((e_lane >> 2) == g_idx)
    lf = jnp.where(in_group, logits, NEG_INF)
    v1 = jnp.max(lf, axis=-1, keepdims=True)
    i1 = jnp.min(jnp.where(lf == v1, lane, big), axis=-1, keepdims=True)
    lf2 = jnp.where(lane == i1, NEG_INF, lf)
    v2 = jnp.max(lf2, axis=-1, keepdims=True)
    i2 = jnp.min(jnp.where(lf2 == v2, lane, big), axis=-1, keepdims=True)
    e = jnp.exp(v2 - v1)
    w1 = p_g / (1.0 + e)
    w2 = w1 * e
    return lane, g_idx, i1, i2, w1, w2


def _silu(x):
    return x * (1.0 / (1.0 + jnp.exp2(x * (-LOG2E))))


def _moe_dense(h, g_ffn_ref, wr_ref, br_ref, wg_ref, wu_ref, wd_ref, hcat_ref):
    hn = _rms(h, g_ffn_ref[...])
    hn_bf = hn.astype(BF16)
    lane, _, i1, i2, w1, w2 = _route(hn_bf, wr_ref, br_ref)
    combine = jnp.where(lane == i1, w1, jnp.where(lane == i2, w2, 0.0))
    for e in range(N_EXPERTS):
        gate = _dot(hn_bf, wg_ref[e])
        up = _dot(hn_bf, wu_ref[e])
        c = combine[:, N_EXPERT_GROUPS + e:N_EXPERT_GROUPS + e + 1]
        hcat_ref[:, e * D_EXPERT:(e + 1) * D_EXPERT] = (_silu(gate) * up * c).astype(BF16)
    return _dot(hcat_ref[...], wd_ref[...])


GATE_LANES = EXPERTS_PER_GROUP
GROUP_LANE = EXPERTS_PER_GROUP
LOW_SHIFT = 8
SORT_BLOCK = 128


def _moe_grouped(h, g_ffn_ref, wr_ref, br_ref, wg_ref, wu_ref, wd_ref, lstrict_ref,
                 xs_ref, es_ref, ys_ref, tile):
    hn = _rms(h, g_ffn_ref[...])
    hn_bf = hn.astype(BF16)
    lane, g_idx, i1, i2, w1, w2 = _route(hn_bf, wr_ref, br_ref)

    onehot = lane == g_idx
    gmat = jnp.where(onehot, 1.0, 0.0)
    before = _dot(lstrict_ref[...], gmat.astype(BF16))
    rank = jnp.sum(jnp.where(onehot, before, 0.0), axis=-1, keepdims=True)
    counts = jnp.sum(gmat, axis=0, keepdims=True)
    lane1 = lax.broadcasted_iota(jnp.int32, counts.shape, 1)
    offs = [jnp.float32(0.0)]
    for g in range(N_EXPERT_GROUPS - 1):
        offs.append(offs[-1] + jnp.sum(jnp.where(lane1 == g, counts, 0.0)))
    offv = jnp.zeros_like(rank)
    for g in range(1, N_EXPERT_GROUPS):
        offv = offv + jnp.where(g_idx == g, offs[g], 0.0)
    pos = (rank + offv).astype(jnp.int32)
    dest = lax.broadcasted_iota(jnp.int32, (tile, tile), 1)
    pt = jnp.where(dest == pos, 1.0, 0.0).astype(BF16)

    j1 = (i1 - N_EXPERT_GROUPS) & (EXPERTS_PER_GROUP - 1)
    j2 = (i2 - N_EXPERT_GROUPS) & (EXPERTS_PER_GROUP - 1)
    gates = jnp.where(lane == j1, w1, jnp.where(lane == j2, w2, 0.0))
    gates_hi = gates.astype(BF16).astype(F32)
    gates_lo = (gates - gates_hi).astype(BF16).astype(F32)
    side = gates_hi + pltpu.roll(gates_lo, LOW_SHIFT, axis=1) + jnp.where(lane == GROUP_LANE, g_idx.astype(F32), 0.0)
    x_ext = jnp.concatenate([hn_bf, side.astype(BF16)], axis=1)
    srt = lax.dot_general(pt, x_ext, (((0,), (0,)), ((), ())), preferred_element_type=F32)
    xs_ref[...] = srt[:, :D_MODEL].astype(BF16)
    es = srt[:, D_MODEL:]
    es_ref[...] = es + pltpu.roll(es, LANES - LOW_SHIFT, axis=1)

    def group_of(row):
        g = jnp.int32(0)
        for k in range(1, N_EXPERT_GROUPS):
            g = g + (jnp.float32(row) >= offs[k]).astype(jnp.int32)
        return g

    for b in range(tile // SORT_BLOCK):
        rows = pl.ds(b * SORT_BLOCK, SORT_BLOCK)
        ys_ref[rows, :] = jnp.zeros((SORT_BLOCK, D_MODEL), F32)

        def body(g, carry, rows=rows):
            xb = xs_ref[rows, :]
            eb = es_ref[rows, :]
            mine = jnp.where(eb[:, GROUP_LANE:GROUP_LANE + 1] == g.astype(F32), 1.0, 0.0)
            parts = []
            for j in range(EXPERTS_PER_GROUP):
                e = g * EXPERTS_PER_GROUP + j
                gate = _dot(xb, wg_ref[e])
                up = _dot(xb, wu_ref[e])
                parts.append((_silu(gate) * up * (eb[:, j:j + 1] * mine)).astype(BF16))
            ys_ref[rows, :] += _dot(jnp.concatenate(parts, axis=1), wd_ref[g])
            return carry

        lax.fori_loop(group_of(b * SORT_BLOCK), group_of((b + 1) * SORT_BLOCK - 1) + 1, body, 0)
    return _dot(pt, ys_ref[...].astype(BF16))


def _merge(x, a_out, b_out, g_out_ref, w_out_ref):
    g_out = g_out_ref[...]
    mix = jnp.concatenate(
        [_rms(a_out, g_out[:, :A_WIDTH]), _rms(b_out, g_out[:, A_WIDTH:])], axis=1).astype(BF16)
    return x + _dot(mix, w_out_ref[...])


def _prompt_kernel(x_ref, g_attn_ref, w_in_ref, g_va_ref, g_q_ref, g_k_ref, segq_ref, segk_ref,
                   wsp_ref, bsp_ref, bias_ref, sinks_ref, g_out_ref, w_out_ref, g_ffn_ref,
                   wr_ref, br_ref, wg_ref, wu_ref, wd_ref, lstrict_ref,
                   y_ref, kwin_ref, vwin_ref,
                   kprev_ref, vprev_ref, xs_ref, es_ref, ys_ref, *, tile, tiles_per_seq):
    is_first = (pl.program_id(0) % tiles_per_seq) == 0

    @pl.when(is_first)
    def _():
        kprev_ref[...] = jnp.zeros_like(kprev_ref)
        vprev_ref[...] = jnp.zeros_like(vprev_ref)

    x = x_ref[...]
    u, va, q, k, v = _project(x, g_attn_ref, w_in_ref, g_va_ref, g_q_ref, g_k_ref, segq_ref, segk_ref)
    kwin_ref[...] = k[tile - WINDOW:tile]
    vwin_ref[...] = v[tile - WINDOW:tile]
    a_out = u * _chunk_mlp(va.astype(BF16), wsp_ref, bsp_ref, tile)
    b_out = _swa_prompt(q * Q_SCALE, k, v, kprev_ref, vprev_ref, bias_ref, sinks_ref, is_first, tile)
    h = _merge(x, a_out, b_out, g_out_ref, w_out_ref)
    y_ref[...] = h + _moe_grouped(h, g_ffn_ref, wr_ref, br_ref, wg_ref, wu_ref, wd_ref,
                                  lstrict_ref, xs_ref, es_ref, ys_ref, tile)


def _sample_kernel(x_ref, ck_ref, cv_ref, g_attn_ref, w_in_ref, g_va_ref, g_q_ref, g_k_ref, segq_ref,
                   segk_ref, wsp_ref, bsp_ref, bias_c_ref, bias_n_ref, sinks_ref, g_out_ref, w_out_ref,
                   g_ffn_ref, wr_ref, br_ref, wg_ref, wu_ref, wd_ref,
                   y_ref, wk_ref, wv_ref, va_ref,
                   hcat_ref, *, tile, nb, dec):
    x = x_ref[...]
    u, va, q, k, v = _project(x, g_attn_ref, w_in_ref, g_va_ref, g_q_ref, g_k_ref, segq_ref, segk_ref)
    va_ref[...] = va
    a_out = u * _chunk_mlp(va.astype(BF16), wsp_ref, bsp_ref, tile)
    b_out = _swa_sample(q * Q_SCALE, k, v, ck_ref, cv_ref, wk_ref, wv_ref,
                        bias_c_ref, bias_n_ref, sinks_ref, nb, dec)
    h = _merge(x, a_out, b_out, g_out_ref, w_out_ref)
    y_ref[...] = h + _moe_dense(h, g_ffn_ref, wr_ref, br_ref, wg_ref, wu_ref, wd_ref, hcat_ref)


def _const_spec(shape):
    nd = len(shape)
    return pl.BlockSpec(shape, lambda i: (0,) * nd, pipeline_mode=pl.Buffered(1))


def _row_spec(tile, width):
    return pl.BlockSpec((tile, width), lambda i: (i, 0))


def _smem_spec():
    return pl.BlockSpec(memory_space=pltpu.SMEM)


def _layer_weights(l, g_attn_norm, w_in, g_v_a, g_q, g_k, g_out_a, g_out_b, w_out, g_ffn_norm,
                   w_coarse, b_coarse, w_fine, b_fine, w_gate, w_up, w_down):
    pad = ROUTER_LANES - N_EXPERT_GROUPS - N_EXPERTS
    wr = jnp.concatenate([w_coarse[l], w_fine[l], jnp.zeros((D_MODEL, pad), F32)], axis=1)
    br = jnp.concatenate([b_coarse[l], b_fine[l], jnp.zeros((pad,), F32)]).reshape(1, ROUTER_LANES)
    return dict(
        g_attn=g_attn_norm[l].reshape(1, D_MODEL),
        w_in=w_in[l].astype(BF16),
        g_va=g_v_a[l].reshape(1, A_WIDTH),
        g_q=jnp.tile(g_q[l], N_HEADS).reshape(1, B_WIDTH),
        g_k=jnp.tile(g_k[l], N_KV).reshape(1, KV_WIDTH),
        segq=jnp.asarray(_seg_ones(B_WIDTH, HEAD_DIM), BF16),
        segk=jnp.asarray(_seg_ones(KV_WIDTH, HEAD_DIM), BF16),
        g_out=jnp.concatenate([g_out_a[l], g_out_b[l]]).reshape(1, D_MODEL),
        w_out=w_out[l].astype(BF16),
        g_ffn=g_ffn_norm[l].reshape(1, D_MODEL),
        wr=wr.astype(BF16),
        br=br,
        wg=w_gate[l].astype(BF16),
        wu=w_up[l].astype(BF16),
        wd=w_down[l].astype(BF16).reshape(N_EXPERTS * D_EXPERT, D_MODEL),
    )


def _spatial_tables(ws, bs, period, tile):
    pair = min(PAIR, tile)
    reps = pair // period
    wbd = (jnp.tile(ws[:, :period, :period], (1, reps, reps)) * _causal_block_mask(period, pair)).astype(BF16)
    b = jnp.tile(bs[:, :period].T, (tile // period, 1))
    bsp = jnp.repeat(b, A_CH, axis=1)
    return wbd, bsp


_TAIL_NAMES = ("g_out", "w_out", "g_ffn", "wr", "br", "wg", "wu", "wd")
_HEAD_NAMES = ("g_attn", "w_in", "g_va", "g_q", "g_k", "segq", "segk")


def _run_prompt(x2d, lw, wbd, bsp, sinks, seq_len):
    n_tok = x2d.shape[0]
    tile = PROMPT_TILE
    tiles_per_seq = seq_len // tile
    n_seq = n_tok // seq_len
    bias = jnp.asarray(_prompt_bias())
    head = [lw[n] for n in _HEAD_NAMES]
    tail = [lw[n] for n in _TAIL_NAMES]
    tail[-1] = tail[-1].reshape(N_EXPERT_GROUPS, EXPERTS_PER_GROUP * D_EXPERT, D_MODEL)
    tail.append(jnp.asarray(np.tril(np.ones((tile, tile), np.float32), -1), BF16))
    consts = head + [wbd, bsp, bias]
    in_specs = ([_row_spec(tile, D_MODEL)] + [_const_spec(a.shape) for a in consts] + [_smem_spec()]
                + [_const_spec(a.shape) for a in tail])
    win_spec = pl.BlockSpec((WINDOW, KV_WIDTH), lambda i: (i // tiles_per_seq, 0))
    out_shape = (jax.ShapeDtypeStruct((n_tok, D_MODEL), F32),
                 jax.ShapeDtypeStruct((n_seq * WINDOW, KV_WIDTH), F32),
                 jax.ShapeDtypeStruct((n_seq * WINDOW, KV_WIDTH), F32))
    out_specs = (_row_spec(tile, D_MODEL), win_spec, win_spec)
    kern = functools.partial(_prompt_kernel, tile=tile, tiles_per_seq=tiles_per_seq)
    return pl.pallas_call(
        kern,
        out_shape=out_shape,
        grid=(n_tok // tile,),
        in_specs=in_specs,
        out_specs=out_specs,
        scratch_shapes=[pltpu.VMEM((N_KV, WINDOW, KV_WIDTH), BF16), pltpu.VMEM((N_KV, WINDOW, KV_WIDTH), BF16),
                        pltpu.VMEM((tile, D_MODEL), BF16), pltpu.VMEM((tile, LANES), F32),
                        pltpu.VMEM((tile, D_MODEL), F32)],
        compiler_params=pltpu.CompilerParams(dimension_semantics=("arbitrary",),
                                             vmem_limit_bytes=VMEM_LIMIT_BYTES),
        name="layer_prompt",
    )(x2d, *consts, sinks, *tail)


def _run_sample(x2d, ck_t, cv_t, lw, wbd, bsp, sinks, dec):
    n_tok = x2d.shape[0]
    n_seq = ck_t.shape[0]
    tile = SAMPLE_TILE
    nb = tile // dec
    bc, bn = _sample_bias(nb, dec)
    head = [lw[n] for n in _HEAD_NAMES]
    tail = [lw[n] for n in _TAIL_NAMES]
    consts = head + [wbd, bsp, jnp.asarray(bc), jnp.asarray(bn)]
    cache_spec = pl.BlockSpec((nb, KV_WIDTH, WINDOW), lambda i: (i, 0, 0))
    in_specs = ([_row_spec(tile, D_MODEL), cache_spec, cache_spec] + [_const_spec(a.shape) for a in consts]
                + [_smem_spec()] + [_const_spec(a.shape) for a in tail])
    out_shape = (jax.ShapeDtypeStruct((n_tok, D_MODEL), F32),
                 jax.ShapeDtypeStruct((n_seq, KV_WIDTH, WINDOW), F32),
                 jax.ShapeDtypeStruct((n_seq, KV_WIDTH, WINDOW), F32),
                 jax.ShapeDtypeStruct((n_tok, A_WIDTH), F32))
    out_specs = (_row_spec(tile, D_MODEL), cache_spec, cache_spec, _row_spec(tile, A_WIDTH))
    kern = functools.partial(_sample_kernel, tile=tile, nb=nb, dec=dec)
    return pl.pallas_call(
        kern,
        out_shape=out_shape,
        grid=(n_tok // tile,),
        in_specs=in_specs,
        out_specs=out_specs,
        scratch_shapes=[pltpu.VMEM((tile, N_EXPERTS * D_EXPERT), BF16)],
        compiler_params=pltpu.CompilerParams(dimension_semantics=("arbitrary",),
                                             vmem_limit_bytes=VMEM_LIMIT_BYTES),
        name="layer_sample",
    )(x2d, ck_t, cv_t, *consts, sinks, *tail)


def _positions_last(c):
    b, w = c.shape[:2]
    return jnp.transpose(c, (0, 2, 3, 1)).reshape(b, KV_WIDTH, w)


def _positions_first(c_t):
    b, _, w = c_t.shape
    return jnp.transpose(c_t.reshape(b, N_KV, HEAD_DIM, w), (0, 3, 1, 2))


def kernel(x_prompt, x_sample, cache_k, cache_v, g_attn_norm, w_in, g_v_a, w_spatial, b_spatial, g_q, g_k, attn_sinks, g_out_a, g_out_b, w_out, g_ffn_norm, w_coarse, b_coarse, w_fine, b_fine, w_gate, w_up, w_down):
    depth = w_in.shape[0]
    batch, seq, _ = x_prompt.shape
    dbatch, dec, _ = x_sample.shape
    win = cache_k.shape[2]
    assert win == WINDOW and seq % PROMPT_TILE == 0 and (dbatch * dec) % SAMPLE_TILE == 0
    assert PAIR % dec == 0 and SAMPLE_TILE % dec == 0 and LANES % dec == 0

    hp = x_prompt.reshape(batch * seq, D_MODEL)
    hs = x_sample.reshape(dbatch * dec, D_MODEL)
    kp_l, vp_l, ks_l, vs_l, cv_l = [], [], [], [], []
    for l in range(depth):
        lw = _layer_weights(l, g_attn_norm, w_in, g_v_a, g_q, g_k, g_out_a, g_out_b, w_out, g_ffn_norm,
                            w_coarse, b_coarse, w_fine, b_fine, w_gate, w_up, w_down)
        sinks = attn_sinks[l].astype(F32)
        wbd_p, bsp_p = _spatial_tables(w_spatial[l], b_spatial[l], CHUNK, PROMPT_TILE)
        wbd_s, bsp_s = _spatial_tables(w_spatial[l], b_spatial[l], dec, SAMPLE_TILE)

        hp, kp, vp = _run_prompt(hp, lw, wbd_p, bsp_p, sinks, seq)
        kp_l.append(kp.reshape(batch, WINDOW, N_KV, HEAD_DIM))
        vp_l.append(vp.reshape(batch, WINDOW, N_KV, HEAD_DIM))

        hs, wk, wv, va = _run_sample(hs, _positions_last(cache_k[l]), _positions_last(cache_v[l]),
                                     lw, wbd_s, bsp_s, sinks, dec)
        ks_l.append(_positions_first(wk))
        vs_l.append(_positions_first(wv))
        cv_l.append(va.reshape(dbatch, dec, A_GROUPS, A_CH))

    return (hp.reshape(batch, seq, D_MODEL), hs.reshape(dbatch, dec, D_MODEL),
            jnp.stack(kp_l, axis=0), jnp.stack(vp_l, axis=0),
            jnp.stack(ks_l, axis=0), jnp.stack(vs_l, axis=0), jnp.stack(cv_l, axis=0))
```

```python
import functools

import numpy as np
import jax
import jax.numpy as jnp
from jax import lax
from jax.experimental import pallas as pl
from jax.experimental.pallas import tpu as pltpu

D_MODEL = 1024
CHUNK = 128
A_GROUPS = 4
A_WIDTH = 512
A_CH = 128
N_HEADS = 8
N_KV = 2
Q_PER_KV = 4
HEAD_DIM = 64
B_WIDTH = 512
KV_WIDTH = 128
WINDOW = 128
IN_COLS = 2 * A_WIDTH + B_WIDTH + 2 * KV_WIDTH
N_EXPERT_GROUPS = 4
EXPERTS_PER_GROUP = 4
N_EXPERTS = 16
D_EXPERT = 256
EPS = 1e-6

LANES = 128
PAIR = 2 * CHUNK
ROUTER_LANES = 128
PROMPT_TILE = 512
SAMPLE_TILE = 128
VMEM_LIMIT_BYTES = 60 * 1024 * 1024

F32 = jnp.float32
BF16 = jnp.bfloat16
NEG_INF = float("-inf")
LOG2E = 1.4426950408889634
Q_SCALE = (HEAD_DIM ** -0.5) * LOG2E


def _slopes():
    return np.array([2.0 ** (-8.0 * (h + 1) / N_HEADS) for h in range(N_HEADS)], np.float64)


def _prompt_bias():
    t = np.arange(WINDOW)[:, None]
    s = np.arange(2 * WINDOW)[None, :]
    dist = t + WINDOW - s
    valid = (dist >= 0) & (dist < WINDOW)
    sl = _slopes()
    out = np.full((N_KV, Q_PER_KV * WINDOW, 2 * WINDOW), -np.inf, np.float32)
    for kh in range(N_KV):
        for g in range(Q_PER_KV):
            b = np.where(valid, -sl[kh * Q_PER_KV + g] * LOG2E * dist, -np.inf)
            out[kh, g * WINDOW:(g + 1) * WINDOW] = b
    return out


def _sample_bias(nb, dec):
    sl = _slopes()
    t = np.arange(dec)[:, None]
    j = np.arange(WINDOW)[None, :]
    dist_c = t + WINDOW - j
    valid_c = (dist_c >= 0) & (dist_c < WINDOW)
    bc = np.full((N_HEADS * dec, WINDOW), -np.inf, np.float32)
    tp = np.arange(dec)[None, :]
    dist_n = t - tp
    valid_n = dist_n >= 0
    bn = np.full((nb, N_HEADS * dec, nb * dec), -np.inf, np.float32)
    for h in range(N_HEADS):
        bc[h * dec:(h + 1) * dec] = np.where(valid_c, -sl[h] * LOG2E * dist_c, -np.inf)
        blk = np.where(valid_n, -sl[h] * LOG2E * dist_n, -np.inf)
        for b in range(nb):
            bn[b, h * dec:(h + 1) * dec, b * dec:(b + 1) * dec] = blk
    return bc, bn


def _seg_ones(width, seg):
    i = np.arange(width)
    return (i[:, None] // seg == i[None, :] // seg).astype(np.float32)


def _causal_block_mask(period, size):
    i = np.arange(size)
    same = i[:, None] // period == i[None, :] // period
    return (same & (i[None, :] % period <= i[:, None] % period)).astype(np.float32)


def _dot(a, b):
    return jnp.dot(a, b, preferred_element_type=F32)


def _dot_nt(a, b):
    return lax.dot_general(a, b, (((1,), (1,)), ((), ())), preferred_element_type=F32)


def _rms(x, g):
    ms = jnp.mean(x * x, axis=-1, keepdims=True)
    return (x * lax.rsqrt(ms + EPS)) * g


def _seg_rms(x, ones_bf, g, seg):
    ss = _dot((x * x).astype(BF16), ones_bf)
    return (x * lax.rsqrt(ss * (1.0 / seg) + EPS)) * g


GELU_K0 = -2.0 * 0.7978845608028654 * LOG2E
GELU_K1 = GELU_K0 * 0.044715


def _gelu_tanh(x):
    t = (x * x) * GELU_K1 + GELU_K0
    return x * (1.0 / (1.0 + jnp.exp2(x * t)))


def _project(x, g_attn_ref, w_in_ref, g_va_ref, g_q_ref, g_k_ref, segq_ref, segk_ref):
    xn = _rms(x, g_attn_ref[...]).astype(BF16)
    z = _dot(xn, w_in_ref[...])
    i1, i2, i3, i4 = A_WIDTH, 2 * A_WIDTH, 2 * A_WIDTH + B_WIDTH, 2 * A_WIDTH + B_WIDTH + KV_WIDTH
    u = _gelu_tanh(z[:, :i1])
    va_pre = _gelu_tanh(z[:, i1:i2])
    g_va = g_va_ref[...]
    va = jnp.concatenate(
        [_rms(va_pre[:, g * A_CH:(g + 1) * A_CH], g_va[:, g * A_CH:(g + 1) * A_CH]) for g in range(A_GROUPS)],
        axis=1)
    q = _seg_rms(z[:, i2:i3], segq_ref[...], g_q_ref[...], HEAD_DIM)
    k = _seg_rms(z[:, i3:i4], segk_ref[...], g_k_ref[...], HEAD_DIM)
    v = z[:, i4:]
    return u, va, q, k, v


def _chunk_mlp(va_bf, wsp_ref, bsp_ref, tile):
    pair = wsp_ref.shape[1]
    n_pairs = tile // pair
    outs = []
    for g in range(A_GROUPS):
        cols = [va_bf[p * pair:(p + 1) * pair, g * A_CH:(g + 1) * A_CH] for p in range(n_pairs)]
        rhs = cols[0] if n_pairs == 1 else jnp.concatenate(cols, axis=1)
        o = _dot(wsp_ref[g], rhs)
        rows = [o[:, p * A_CH:(p + 1) * A_CH] for p in range(n_pairs)]
        outs.append(rows[0] if n_pairs == 1 else jnp.concatenate(rows, axis=0))
    return jnp.concatenate(outs, axis=1) + bsp_ref[...]


def _lane_lo():
    return lax.broadcasted_iota(jnp.int32, (1, LANES), 1) < HEAD_DIM


def _swap_halves(x):
    return pltpu.roll(x, HEAD_DIM, axis=1)


def _head_lhs(q_groups, q_groups_swapped, head, lo):
    j, half = divmod(head, 2)
    kv = head // Q_PER_KV
    src = q_groups[j] if half == kv else q_groups_swapped[j]
    return jnp.where(lo if kv == 0 else jnp.logical_not(lo), src, 0.0)


def _merge_heads(o_heads, lo):
    groups = []
    for j in range(N_HEADS // 2):
        kv = (2 * j) // Q_PER_KV
        if kv == 0:
            groups.append(jnp.where(lo, o_heads[2 * j], _swap_halves(o_heads[2 * j + 1])))
        else:
            groups.append(jnp.where(lo, _swap_halves(o_heads[2 * j]), o_heads[2 * j + 1]))
    return jnp.concatenate(groups, axis=1)


def _dup_halves(x, lo):
    xs = _swap_halves(x)
    return jnp.where(lo, x, xs).astype(BF16), jnp.where(lo, xs, x).astype(BF16)


def _swa_prompt(q, k, v, kprev_ref, vprev_ref, bias_ref, sinks_ref, is_first, tile):
    lo = _lane_lo()
    hi = jnp.logical_not(lo)
    col = lax.broadcasted_iota(jnp.int32, (WINDOW, 2 * WINDOW), 1)
    first_mask = jnp.where(col < WINDOW, jnp.where(is_first, NEG_INF, 0.0).astype(F32), 0.0)
    kd = _dup_halves(k, lo)
    vd = _dup_halves(v, lo)
    nblk = tile // WINDOW
    rows_out = []
    for i in range(nblk):
        r0, r1 = i * WINDOW, (i + 1) * WINDOW
        qg = [q[r0:r1, j * LANES:(j + 1) * LANES] for j in range(N_HEADS // 2)]
        o_heads = []
        for kh in range(N_KV):
            if i == 0:
                kp, vp = kprev_ref[kh], vprev_ref[kh]
            else:
                kp, vp = kd[kh][r0 - WINDOW:r0], vd[kh][r0 - WINDOW:r0]
            kb = jnp.concatenate([kp, kd[kh][r0:r1]], axis=0)
            vb = jnp.concatenate([vp, vd[kh][r0:r1]], axis=0)
            heads = [kh * Q_PER_KV + g for g in range(Q_PER_KV)]
            lhs = jnp.concatenate(
                [jnp.where(lo if h % 2 == 0 else hi, qg[h // 2], 0.0) for h in heads], axis=0).astype(BF16)
            s = _dot_nt(lhs, kb) + bias_ref[kh]
            ps, linvs = [], []
            for g in range(Q_PER_KV):
                sg = s[g * WINDOW:(g + 1) * WINDOW]
                if i == 0:
                    sg = sg + first_mask
                sink = sinks_ref[heads[g]] * LOG2E
                mg = jnp.maximum(jnp.max(sg, axis=-1, keepdims=True), sink)
                pg = jnp.exp2(sg - mg)
                lg = jnp.sum(pg, axis=-1, keepdims=True) + jnp.exp2(sink - mg)
                ps.append(pg.astype(BF16))
                linvs.append(1.0 / lg)
            o = _dot(jnp.concatenate(ps, axis=0), vb)
            o_heads += [o[g * WINDOW:(g + 1) * WINDOW] * linvs[g] for g in range(Q_PER_KV)]
        rows_out.append(jnp.concatenate(
            [jnp.where(lo, o_heads[2 * j], o_heads[2 * j + 1]) for j in range(N_HEADS // 2)], axis=1))
    for kh in range(N_KV):
        kprev_ref[kh] = kd[kh][tile - WINDOW:tile]
        vprev_ref[kh] = vd[kh][tile - WINDOW:tile]
    return jnp.concatenate(rows_out, axis=0)


def _shift_window(cache_ref, new_t, out_ref, nb, dec):
    lane = lax.broadcasted_iota(jnp.int32, (1, WINDOW), 1)
    keep = lane < WINDOW - dec
    per_tile = LANES // dec
    for b in range(nb):
        src = new_t[:, (b // per_tile) * LANES:(b // per_tile + 1) * LANES]
        new_cols = pltpu.roll(src, (WINDOW - dec - (b % per_tile) * dec) % LANES, axis=1)
        old = pltpu.roll(cache_ref[b], WINDOW - dec, axis=1)
        out_ref[b] = jnp.where(keep, old, new_cols)


def _swa_sample(q, k, v, ck_ref, cv_ref, wk_ref, wv_ref, bias_c_ref, bias_n_ref, sinks_ref, nb, dec):
    lo = _lane_lo()
    qg = [q[:, j * LANES:(j + 1) * LANES] for j in range(N_HEADS // 2)]
    qgs = [_swap_halves(x) for x in qg]
    qb = jnp.concatenate(
        [_head_lhs(qg, qgs, h, lo).reshape(nb, dec, LANES) for h in range(N_HEADS)], axis=1)
    rows = N_HEADS * dec
    qb_bf = qb.astype(BF16)
    k_bf = k.astype(BF16)
    v_bf = v.astype(BF16)
    ck = ck_ref[...].astype(BF16)
    cv = cv_ref[...].astype(BF16)
    sc = jnp.einsum('bqc,bcp->bqp', qb_bf, ck, preferred_element_type=F32) + bias_c_ref[...][None]
    sn = _dot_nt(qb_bf.reshape(nb * rows, LANES), k_bf).reshape(nb, rows, nb * dec) + bias_n_ref[...]
    sink = jnp.concatenate(
        [jnp.full((1, dec, 1), sinks_ref[h] * LOG2E, F32) for h in range(N_HEADS)], axis=1)
    m = jnp.maximum(jnp.max(sc, axis=-1, keepdims=True), jnp.max(sn, axis=-1, keepdims=True))
    m = jnp.maximum(m, sink)
    pc = jnp.exp2(sc - m)
    pn = jnp.exp2(sn - m)
    l = jnp.sum(pc, axis=-1, keepdims=True) + jnp.sum(pn, axis=-1, keepdims=True) + jnp.exp2(sink - m)
    oc = jnp.einsum('bqp,bcp->bqc', pc.astype(BF16), cv, preferred_element_type=F32)
    on = _dot(pn.reshape(nb * rows, nb * dec).astype(BF16), v_bf).reshape(nb, rows, LANES)
    o = (oc + on) * (1.0 / l)
    o_heads = [o[:, h * dec:(h + 1) * dec, :].reshape(nb * dec, LANES) for h in range(N_HEADS)]
    _shift_window(ck_ref, k.T, wk_ref, nb, dec)
    _shift_window(cv_ref, v.T, wv_ref, nb, dec)
    return _merge_heads(o_heads, lo)


def _route(hn_bf, wr_ref, br_ref):
    logits = _dot(hn_bf, wr_ref[...]) + br_ref[...]
    lane = lax.broadcasted_iota(jnp.int32, logits.shape, 1)
    big = jnp.int32(ROUTER_LANES)
    lc = jnp.where(lane < N_EXPERT_GROUPS, logits, NEG_INF)
    mx = jnp.max(lc, axis=-1, keepdims=True)
    g_idx = jnp.min(jnp.where(lc == mx, lane, big), axis=-1, keepdims=True)
    p_g = 1.0 / jnp.sum(jnp.exp(lc - mx), axis=-1, keepdims=True)
    e_lane = lane - N_EXPERT_GROUPS
    in_group = (e_lane >= 0) & (e_lane < N_EXPERTS) & ((e_lane >> 2) == g_idx)
    lf = jnp.where(in_group, logits, NEG_INF)
    v1 = jnp.max(lf, axis=-1, keepdims=True)
    i1 = jnp.min(jnp.where(lf == v1, lane, big), axis=-1, keepdims=True)
    lf2 = jnp.where(lane == i1, NEG_INF, lf)
    v2 = jnp.max(lf2, axis=-1, keepdims=True)
    i2 = jnp.min(jnp.where(lf2 == v2, lane, big), axis=-1, keepdims=True)
    e = jnp.exp(v2 - v1)
    w1 = p_g / (1.0 + e)
    w2 = w1 * e
    return lane, g_idx, i1, i2, w1, w2


def _silu(x):
    return x * (1.0 / (1.0 + jnp.exp2(x * (-LOG2E))))


def _moe_dense(h, g_ffn_ref, wr_ref, br_ref, wg_ref, wu_ref, wd_ref, hcat_ref):
    hn = _rms(h, g_ffn_ref[...])
    hn_bf = hn.astype(BF16)
    lane, _, i1, i2, w1, w2 = _route(hn_bf, wr_ref, br_ref)
    combine = jnp.where(lane == i1, w1, jnp.where(lane == i2, w2, 0.0))
    for e in range(N_EXPERTS):
        gate = _dot(hn_bf, wg_ref[e])
        up = _dot(hn_bf, wu_ref[e])
        c = combine[:, N_EXPERT_GROUPS + e:N_EXPERT_GROUPS + e + 1]
        hcat_ref[:, e * D_EXPERT:(e + 1) * D_EXPERT] = (_silu(gate) * up * c).astype(BF16)
    return _dot(hcat_ref[...], wd_ref[...])


GATE_LANES = EXPERTS_PER_GROUP
GROUP_LANE = EXPERTS_PER_GROUP
LOW_SHIFT = 8
SORT_BLOCK = 128


def _moe_grouped(h, g_ffn_ref, wr_ref, br_ref, wg_ref, wu_ref, wd_ref, lstrict_ref,
                 xs_ref, es_ref, ys_ref, tile):
    hn = _rms(h, g_ffn_ref[...])
    hn_bf = hn.astype(BF16)
    lane, g_idx, i1, i2, w1, w2 = _route(hn_bf, wr_ref, br_ref)

    onehot = lane == g_idx
    gmat = jnp.where(onehot, 1.0, 0.0)
    before = _dot(lstrict_ref[...], gmat.astype(BF16))
    rank = jnp.sum(jnp.where(onehot, before, 0.0), axis=-1, keepdims=True)
    counts = jnp.sum(gmat, axis=0, keepdims=True)
    lane1 = lax.broadcasted_iota(jnp.int32, counts.shape, 1)
    offs = [jnp.float32(0.0)]
    for g in range(N_EXPERT_GROUPS - 1):
        offs.append(offs[-1] + jnp.sum(jnp.where(lane1 == g, counts, 0.0)))
    offv = jnp.zeros_like(rank)
    for g in range(1, N_EXPERT_GROUPS):
        offv = offv + jnp.where(g_idx == g, offs[g], 0.0)
    pos = (rank + offv).astype(jnp.int32)
    dest = lax.broadcasted_iota(jnp.int32, (tile, tile), 1)
    pt = jnp.where(dest == pos, 1.0, 0.0).astype(BF16)

    j1 = (i1 - N_EXPERT_GROUPS) & (EXPERTS_PER_GROUP - 1)
    j2 = (i2 - N_EXPERT_GROUPS) & (EXPERTS_PER_GROUP - 1)
    gates = jnp.where(lane == j1, w1, jnp.where(lane == j2, w2, 0.0))
    gates_hi = gates.astype(BF16).astype(F32)
    gates_lo = (gates - gates_hi).astype(BF16).astype(F32)
    side = gates_hi + pltpu.roll(gates_lo, LOW_SHIFT, axis=1) + jnp.where(lane == GROUP_LANE, g_idx.astype(F32), 0.0)
    x_ext = jnp.concatenate([hn_bf, side.astype(BF16)], axis=1)
    srt = lax.dot_general(pt, x_ext, (((0,), (0,)), ((), ())), preferred_element_type=F32)
    xs_ref[...] = srt[:, :D_MODEL].astype(BF16)
    es = srt[:, D_MODEL:]
    es_ref[...] = es + pltpu.roll(es, LANES - LOW_SHIFT, axis=1)

    def group_of(row):
        g = jnp.int32(0)
        for k in range(1, N_EXPERT_GROUPS):
            g = g + (jnp.float32(row) >= offs[k]).astype(jnp.int32)
        return g

    def group_pass(rows, g, keep, accumulate):
        xb = xs_ref[rows, :]
        eb = es_ref[rows, :]
        mine = jnp.where(eb[:, GROUP_LANE:GROUP_LANE + 1] == jnp.asarray(g).astype(F32), keep, 0.0)
        parts = []
        for j in range(EXPERTS_PER_GROUP):
            e = g * EXPERTS_PER_GROUP + j
            gate = _dot(xb, wg_ref[e])
            up = _dot(xb, wu_ref[e])
            parts.append((_silu(gate) * up * (eb[:, j:j + 1] * mine)).astype(BF16))
        out = _dot(jnp.concatenate(parts, axis=1), wd_ref[g])
        if accumulate:
            ys_ref[rows, :] += out
        else:
            ys_ref[rows, :] = out

    nblk = tile // SORT_BLOCK
    for b in range(nblk):
        group_pass(pl.ds(b * SORT_BLOCK, SORT_BLOCK), group_of(b * SORT_BLOCK), 1.0, False)
    for g in range(1, N_EXPERT_GROUPS):
        start = offs[g].astype(jnp.int32)
        blk = jnp.minimum(lax.div(start, jnp.int32(SORT_BLOCK)), nblk - 1)
        inside = jnp.where(lax.rem(start, jnp.int32(SORT_BLOCK)) != 0, 1.0, 0.0)
        group_pass(pl.ds(pl.multiple_of(blk * SORT_BLOCK, SORT_BLOCK), SORT_BLOCK), g, inside, True)
    return _dot(pt, ys_ref[...].astype(BF16))


def _merge(x, a_out, b_out, g_out_ref, w_out_ref):
    g_out = g_out_ref[...]
    mix = jnp.concatenate(
        [_rms(a_out, g_out[:, :A_WIDTH]), _rms(b_out, g_out[:, A_WIDTH:])], axis=1).astype(BF16)
    return x + _dot(mix, w_out_ref[...])


def _prompt_kernel(x_ref, g_attn_ref, w_in_ref, g_va_ref, g_q_ref, g_k_ref, segq_ref, segk_ref,
                   wsp_ref, bsp_ref, bias_ref, sinks_ref, g_out_ref, w_out_ref, g_ffn_ref,
                   wr_ref, br_ref, wg_ref, wu_ref, wd_ref, lstrict_ref,
                   y_ref, kwin_ref, vwin_ref,
                   kprev_ref, vprev_ref, xs_ref, es_ref, ys_ref, *, tile, tiles_per_seq):
    is_first = (pl.program_id(0) % tiles_per_seq) == 0

    @pl.when(is_first)
    def _():
        kprev_ref[...] = jnp.zeros_like(kprev_ref)
        vprev_ref[...] = jnp.zeros_like(vprev_ref)

    x = x_ref[...]
    u, va, q, k, v = _project(x, g_attn_ref, w_in_ref, g_va_ref, g_q_ref, g_k_ref, segq_ref, segk_ref)
    kwin_ref[...] = k[tile - WINDOW:tile]
    vwin_ref[...] = v[tile - WINDOW:tile]
    a_out = u * _chunk_mlp(va.astype(BF16), wsp_ref, bsp_ref, tile)
    b_out = _swa_prompt(q * Q_SCALE, k, v, kprev_ref, vprev_ref, bias_ref, sinks_ref, is_first, tile)
    h = _merge(x, a_out, b_out, g_out_ref, w_out_ref)
    y_ref[...] = h + _moe_grouped(h, g_ffn_ref, wr_ref, br_ref, wg_ref, wu_ref, wd_ref,
                                  lstrict_ref, xs_ref, es_ref, ys_ref, tile)


def _sample_kernel(x_ref, ck_ref, cv_ref, g_attn_ref, w_in_ref, g_va_ref, g_q_ref, g_k_ref, segq_ref,
                   segk_ref, wsp_ref, bsp_ref, bias_c_ref, bias_n_ref, sinks_ref, g_out_ref, w_out_ref,
                   g_ffn_ref, wr_ref, br_ref, wg_ref, wu_ref, wd_ref,
                   y_ref, wk_ref, wv_ref, va_ref,
                   hcat_ref, *, tile, nb, dec):
    x = x_ref[...]
    u, va, q, k, v = _project(x, g_attn_ref, w_in_ref, g_va_ref, g_q_ref, g_k_ref, segq_ref, segk_ref)
    va_ref[...] = va
    a_out = u * _chunk_mlp(va.astype(BF16), wsp_ref, bsp_ref, tile)
    b_out = _swa_sample(q * Q_SCALE, k, v, ck_ref, cv_ref, wk_ref, wv_ref,
                        bias_c_ref, bias_n_ref, sinks_ref, nb, dec)
    h = _merge(x, a_out, b_out, g_out_ref, w_out_ref)
    y_ref[...] = h + _moe_dense(h, g_ffn_ref, wr_ref, br_ref, wg_ref, wu_ref, wd_ref, hcat_ref)


def _const_spec(shape):
    nd = len(shape)
    return pl.BlockSpec(shape, lambda i: (0,) * nd, pipeline_mode=pl.Buffered(1))


def _row_spec(tile, width):
    return pl.BlockSpec((tile, width), lambda i: (i, 0))


def _smem_spec():
    return pl.BlockSpec(memory_space=pltpu.SMEM)


def _layer_weights(l, g_attn_norm, w_in, g_v_a, g_q, g_k, g_out_a, g_out_b, w_out, g_ffn_norm,
                   w_coarse, b_coarse, w_fine, b_fine, w_gate, w_up, w_down):
    pad = ROUTER_LANES - N_EXPERT_GROUPS - N_EXPERTS
    wr = jnp.concatenate([w_coarse[l], w_fine[l], jnp.zeros((D_MODEL, pad), F32)], axis=1)
    br = jnp.concatenate([b_coarse[l], b_fine[l], jnp.zeros((pad,), F32)]).reshape(1, ROUTER_LANES)
    return dict(
        g_attn=g_attn_norm[l].reshape(1, D_MODEL),
        w_in=w_in[l].astype(BF16),
        g_va=g_v_a[l].reshape(1, A_WIDTH),
        g_q=jnp.tile(g_q[l], N_HEADS).reshape(1, B_WIDTH),
        g_k=jnp.tile(g_k[l], N_KV).reshape(1, KV_WIDTH),
        segq=jnp.asarray(_seg_ones(B_WIDTH, HEAD_DIM), BF16),
        segk=jnp.asarray(_seg_ones(KV_WIDTH, HEAD_DIM), BF16),
        g_out=jnp.concatenate([g_out_a[l], g_out_b[l]]).reshape(1, D_MODEL),
        w_out=w_out[l].astype(BF16),
        g_ffn=g_ffn_norm[l].reshape(1, D_MODEL),
        wr=wr.astype(BF16),
        br=br,
        wg=w_gate[l].astype(BF16),
        wu=w_up[l].astype(BF16),
        wd=w_down[l].astype(BF16).reshape(N_EXPERTS * D_EXPERT, D_MODEL),
    )


def _spatial_tables(ws, bs, period, tile):
    pair = min(PAIR, tile)
    reps = pair // period
    wbd = (jnp.tile(ws[:, :period, :period], (1, reps, reps)) * _causal_block_mask(period, pair)).astype(BF16)
    b = jnp.tile(bs[:, :period].T, (tile // period, 1))
    bsp = jnp.repeat(b, A_CH, axis=1)
    return wbd, bsp


_TAIL_NAMES = ("g_out", "w_out", "g_ffn", "wr", "br", "wg", "wu", "wd")
_HEAD_NAMES = ("g_attn", "w_in", "g_va", "g_q", "g_k", "segq", "segk")


def _run_prompt(x2d, lw, wbd, bsp, sinks, seq_len):
    n_tok = x2d.shape[0]
    tile = PROMPT_TILE
    tiles_per_seq = seq_len // tile
    n_seq = n_tok // seq_len
    bias = jnp.asarray(_prompt_bias())
    head = [lw[n] for n in _HEAD_NAMES]
    tail = [lw[n] for n in _TAIL_NAMES]
    tail[-1] = tail[-1].reshape(N_EXPERT_GROUPS, EXPERTS_PER_GROUP * D_EXPERT, D_MODEL)
    tail.append(jnp.asarray(np.tril(np.ones((tile, tile), np.float32), -1), BF16))
    consts = head + [wbd, bsp, bias]
    in_specs = ([_row_spec(tile, D_MODEL)] + [_const_spec(a.shape) for a in consts] + [_smem_spec()]
                + [_const_spec(a.shape) for a in tail])
    win_spec = pl.BlockSpec((WINDOW, KV_WIDTH), lambda i: (i // tiles_per_seq, 0))
    out_shape = (jax.ShapeDtypeStruct((n_tok, D_MODEL), F32),
                 jax.ShapeDtypeStruct((n_seq * WINDOW, KV_WIDTH), F32),
                 jax.ShapeDtypeStruct((n_seq * WINDOW, KV_WIDTH), F32))
    out_specs = (_row_spec(tile, D_MODEL), win_spec, win_spec)
    kern = functools.partial(_prompt_kernel, tile=tile, tiles_per_seq=tiles_per_seq)
    return pl.pallas_call(
        kern,
        out_shape=out_shape,
        grid=(n_tok // tile,),
        in_specs=in_specs,
        out_specs=out_specs,
        scratch_shapes=[pltpu.VMEM((N_KV, WINDOW, KV_WIDTH), BF16), pltpu.VMEM((N_KV, WINDOW, KV_WIDTH), BF16),
                        pltpu.VMEM((tile, D_MODEL), BF16), pltpu.VMEM((tile, LANES), F32),
                        pltpu.VMEM((tile, D_MODEL), F32)],
        compiler_params=pltpu.CompilerParams(dimension_semantics=("arbitrary",),
                                             vmem_limit_bytes=VMEM_LIMIT_BYTES),
        name="layer_prompt",
    )(x2d, *consts, sinks, *tail)


def _run_sample(x2d, ck_t, cv_t, lw, wbd, bsp, sinks, dec):
    n_tok = x2d.shape[0]
    n_seq = ck_t.shape[0]
    tile = SAMPLE_TILE
    nb = tile // dec
    bc, bn = _sample_bias(nb, dec)
    head = [lw[n] for n in _HEAD_NAMES]
    tail = [lw[n] for n in _TAIL_NAMES]
    consts = head + [wbd, bsp, jnp.asarray(bc), jnp.asarray(bn)]
    cache_spec = pl.BlockSpec((nb, KV_WIDTH, WINDOW), lambda i: (i, 0, 0))
    in_specs = ([_row_spec(tile, D_MODEL), cache_spec, cache_spec] + [_const_spec(a.shape) for a in consts]
                + [_smem_spec()] + [_const_spec(a.shape) for a in tail])
    out_shape = (jax.ShapeDtypeStruct((n_tok, D_MODEL), F32),
                 jax.ShapeDtypeStruct((n_seq, KV_WIDTH, WINDOW), F32),
                 jax.ShapeDtypeStruct((n_seq, KV_WIDTH, WINDOW), F32),
                 jax.ShapeDtypeStruct((n_tok, A_WIDTH), F32))
    out_specs = (_row_spec(tile, D_MODEL), cache_spec, cache_spec, _row_spec(tile, A_WIDTH))
    kern = functools.partial(_sample_kernel, tile=tile, nb=nb, dec=dec)
    return pl.pallas_call(
        kern,
        out_shape=out_shape,
        grid=(n_tok // tile,),
        in_specs=in_specs,
        out_specs=out_specs,
        scratch_shapes=[pltpu.VMEM((tile, N_EXPERTS * D_EXPERT), BF16)],
        compiler_params=pltpu.CompilerParams(dimension_semantics=("arbitrary",),
                                             vmem_limit_bytes=VMEM_LIMIT_BYTES),
        name="layer_sample",
    )(x2d, ck_t, cv_t, *consts, sinks, *tail)


def _positions_last(c):
    b, w = c.shape[:2]
    return jnp.transpose(c, (0, 2, 3, 1)).reshape(b, KV_WIDTH, w)


def _positions_first(c_t):
    b, _, w = c_t.shape
    return jnp.transpose(c_t.reshape(b, N_KV, HEAD_DIM, w), (0, 3, 1, 2))


def kernel(x_prompt, x_sample, cache_k, cache_v, g_attn_norm, w_in, g_v_a, w_spatial, b_spatial, g_q, g_k, attn_sinks, g_out_a, g_out_b, w_out, g_ffn_norm, w_coarse, b_coarse, w_fine, b_fine, w_gate, w_up, w_down):
    depth = w_in.shape[0]
    batch, seq, _ = x_prompt.shape
    dbatch, dec, _ = x_sample.shape
    win = cache_k.shape[2]
    assert win == WINDOW and seq % PROMPT_TILE == 0 and (dbatch * dec) % SAMPLE_TILE == 0
    assert PAIR % dec == 0 and SAMPLE_TILE % dec == 0 and LANES % dec == 0

    hp = x_prompt.reshape(batch * seq, D_MODEL)
    hs = x_sample.reshape(dbatch * dec, D_MODEL)
    kp_l, vp_l, ks_l, vs_l, cv_l = [], [], [], [], []
    for l in range(depth):
        lw = _layer_weights(l, g_attn_norm, w_in, g_v_a, g_q, g_k, g_out_a, g_out_b, w_out, g_ffn_norm,
                            w_coarse, b_coarse, w_fine, b_fine, w_gate, w_up, w_down)
        sinks = attn_sinks[l].astype(F32)
        wbd_p, bsp_p = _spatial_tables(w_spatial[l], b_spatial[l], CHUNK, PROMPT_TILE)
        wbd_s, bsp_s = _spatial_tables(w_spatial[l], b_spatial[l], dec, SAMPLE_TILE)

        hp, kp, vp = _run_prompt(hp, lw, wbd_p, bsp_p, sinks, seq)
        kp_l.append(kp.reshape(batch, WINDOW, N_KV, HEAD_DIM))
        vp_l.append(vp.reshape(batch, WINDOW, N_KV, HEAD_DIM))

        hs, wk, wv, va = _run_sample(hs, _positions_last(cache_k[l]), _positions_last(cache_v[l]),
                                     lw, wbd_s, bsp_s, sinks, dec)
        ks_l.append(_positions_first(wk))
        vs_l.append(_positions_first(wv))
        cv_l.append(va.reshape(dbatch, dec, A_GROUPS, A_CH))

    return (hp.reshape(batch, seq, D_MODEL), hs.reshape(dbatch, dec, D_MODEL),
            jnp.stack(kp_l, axis=0), jnp.stack(vp_l, axis=0),
            jnp.stack(ks_l, axis=0), jnp.stack(vs_l, axis=0), jnp.stack(cv_l, axis=0))
```

```python
import functools

import numpy as np
import jax
import jax.numpy as jnp
from jax import lax
from jax.experimental import pallas as pl
from jax.experimental.pallas import tpu as pltpu

D_MODEL = 1024
CHUNK = 128
A_GROUPS = 4
A_WIDTH = 512
A_CH = 128
N_HEADS = 8
N_KV = 2
Q_PER_KV = 4
HEAD_DIM = 64
B_WIDTH = 512
KV_WIDTH = 128
WINDOW = 128
IN_COLS = 2 * A_WIDTH + B_WIDTH + 2 * KV_WIDTH
N_EXPERT_GROUPS = 4
EXPERTS_PER_GROUP = 4
N_EXPERTS = 16
D_EXPERT = 256
EPS = 1e-6

LANES = 128
PAIR = 2 * CHUNK
ROUTER_LANES = 128
PROMPT_TILE = 512
SAMPLE_TILE = 128
VMEM_LIMIT_BYTES = 60 * 1024 * 1024

F32 = jnp.float32
BF16 = jnp.bfloat16
NEG_INF = float("-inf")
LOG2E = 1.4426950408889634
Q_SCALE = (HEAD_DIM ** -0.5) * LOG2E


def _slopes():
    return np.array([2.0 ** (-8.0 * (h + 1) / N_HEADS) for h in range(N_HEADS)], np.float64)


def _prompt_bias():
    t = np.arange(WINDOW)[:, None]
    s = np.arange(2 * WINDOW)[None, :]
    dist = t + WINDOW - s
    valid = (dist >= 0) & (dist < WINDOW)
    sl = _slopes()
    out = np.full((N_KV, Q_PER_KV * WINDOW, 2 * WINDOW), -np.inf, np.float32)
    for kh in range(N_KV):
        for g in range(Q_PER_KV):
            b = np.where(valid, -sl[kh * Q_PER_KV + g] * LOG2E * dist, -np.inf)
            out[kh, g * WINDOW:(g + 1) * WINDOW] = b
    return out


def _sample_bias(nb, dec):
    sl = _slopes()
    t = np.arange(dec)[:, None]
    j = np.arange(WINDOW)[None, :]
    dist_c = t + WINDOW - j
    valid_c = (dist_c >= 0) & (dist_c < WINDOW)
    bc = np.full((N_HEADS * dec, WINDOW), -np.inf, np.float32)
    tp = np.arange(dec)[None, :]
    dist_n = t - tp
    valid_n = dist_n >= 0
    bn = np.full((nb, N_HEADS * dec, nb * dec), -np.inf, np.float32)
    for h in range(N_HEADS):
        bc[h * dec:(h + 1) * dec] = np.where(valid_c, -sl[h] * LOG2E * dist_c, -np.inf)
        blk = np.where(valid_n, -sl[h] * LOG2E * dist_n, -np.inf)
        for b in range(nb):
            bn[b, h * dec:(h + 1) * dec, b * dec:(b + 1) * dec] = blk
    return bc, bn


def _seg_ones(width, seg):
    i = np.arange(width)
    return (i[:, None] // seg == i[None, :] // seg).astype(np.float32)


def _causal_block_mask(period, size):
    i = np.arange(size)
    same = i[:, None] // period == i[None, :] // period
    return (same & (i[None, :] % period <= i[:, None] % period)).astype(np.float32)


def _dot(a, b):
    return jnp.dot(a, b, preferred_element_type=F32)


def _dot_nt(a, b):
    return lax.dot_general(a, b, (((1,), (1,)), ((), ())), preferred_element_type=F32)


def _rms(x, g):
    ms = jnp.mean(x * x, axis=-1, keepdims=True)
    return (x * lax.rsqrt(ms + EPS)) * g


def _seg_rms(x, ones_bf, g, seg):
    ss = _dot((x * x).astype(BF16), ones_bf)
    return (x * lax.rsqrt(ss * (1.0 / seg) + EPS)) * g


GELU_K0 = -2.0 * 0.7978845608028654 * LOG2E
GELU_K1 = GELU_K0 * 0.044715


def _gelu_tanh(x):
    t = (x * x) * GELU_K1 + GELU_K0
    return x * (1.0 / (1.0 + jnp.exp2(x * t)))


def _project(x, g_attn_ref, w_in_ref, g_va_ref, g_q_ref, g_k_ref, segq_ref, segk_ref):
    xn = _rms(x, g_attn_ref[...]).astype(BF16)
    z = _dot(xn, w_in_ref[...])
    i1, i2, i3, i4 = A_WIDTH, 2 * A_WIDTH, 2 * A_WIDTH + B_WIDTH, 2 * A_WIDTH + B_WIDTH + KV_WIDTH
    u = _gelu_tanh(z[:, :i1])
    va_pre = _gelu_tanh(z[:, i1:i2])
    g_va = g_va_ref[...]
    va = jnp.concatenate(
        [_rms(va_pre[:, g * A_CH:(g + 1) * A_CH], g_va[:, g * A_CH:(g + 1) * A_CH]) for g in range(A_GROUPS)],
        axis=1)
    q = _seg_rms(z[:, i2:i3], segq_ref[...], g_q_ref[...], HEAD_DIM)
    k = _seg_rms(z[:, i3:i4], segk_ref[...], g_k_ref[...], HEAD_DIM)
    v = z[:, i4:]
    return u, va, q, k, v


def _chunk_mlp(va_bf, wsp_ref, bsp_ref, tile):
    pair = wsp_ref.shape[1]
    n_pairs = tile // pair
    outs = []
    for g in range(A_GROUPS):
        cols = [va_bf[p * pair:(p + 1) * pair, g * A_CH:(g + 1) * A_CH] for p in range(n_pairs)]
        rhs = cols[0] if n_pairs == 1 else jnp.concatenate(cols, axis=1)
        o = _dot(wsp_ref[g], rhs)
        rows = [o[:, p * A_CH:(p + 1) * A_CH] for p in range(n_pairs)]
        outs.append(rows[0] if n_pairs == 1 else jnp.concatenate(rows, axis=0))
    return jnp.concatenate(outs, axis=1) + bsp_ref[...]


def _lane_lo():
    return lax.broadcasted_iota(jnp.int32, (1, LANES), 1) < HEAD_DIM


def _swap_halves(x):
    return pltpu.roll(x, HEAD_DIM, axis=1)


def _head_lhs(q_groups, q_groups_swapped, head, lo):
    j, half = divmod(head, 2)
    kv = head // Q_PER_KV
    src = q_groups[j] if half == kv else q_groups_swapped[j]
    return jnp.where(lo if kv == 0 else jnp.logical_not(lo), src, 0.0)


def _merge_heads(o_heads, lo):
    groups = []
    for j in range(N_HEADS // 2):
        kv = (2 * j) // Q_PER_KV
        if kv == 0:
            groups.append(jnp.where(lo, o_heads[2 * j], _swap_halves(o_heads[2 * j + 1])))
        else:
            groups.append(jnp.where(lo, _swap_halves(o_heads[2 * j]), o_heads[2 * j + 1]))
    return jnp.concatenate(groups, axis=1)


def _dup_halves(x, lo):
    xs = _swap_halves(x)
    return jnp.where(lo, x, xs).astype(BF16), jnp.where(lo, xs, x).astype(BF16)


def _swa_prompt(q, k, v, kprev_ref, vprev_ref, bias_ref, sinks_ref, is_first, tile):
    lo = _lane_lo()
    hi = jnp.logical_not(lo)
    col = lax.broadcasted_iota(jnp.int32, (WINDOW, 2 * WINDOW), 1)
    first_mask = jnp.where(col < WINDOW, jnp.where(is_first, NEG_INF, 0.0).astype(F32), 0.0)
    kd = _dup_halves(k, lo)
    vd = _dup_halves(v, lo)
    nblk = tile // WINDOW
    rows_out = []
    for i in range(nblk):
        r0, r1 = i * WINDOW, (i + 1) * WINDOW
        qg = [q[r0:r1, j * LANES:(j + 1) * LANES] for j in range(N_HEADS // 2)]
        o_heads = []
        for kh in range(N_KV):
            if i == 0:
                kp, vp = kprev_ref[kh], vprev_ref[kh]
            else:
                kp, vp = kd[kh][r0 - WINDOW:r0], vd[kh][r0 - WINDOW:r0]
            kb = jnp.concatenate([kp, kd[kh][r0:r1]], axis=0)
            vb = jnp.concatenate([vp, vd[kh][r0:r1]], axis=0)
            heads = [kh * Q_PER_KV + g for g in range(Q_PER_KV)]
            lhs = jnp.concatenate(
                [jnp.where(lo if h % 2 == 0 else hi, qg[h // 2], 0.0) for h in heads], axis=0).astype(BF16)
            s = _dot_nt(lhs, kb) + bias_ref[kh]
            ps, linvs = [], []
            for g in range(Q_PER_KV):
                sg = s[g * WINDOW:(g + 1) * WINDOW]
                if i == 0:
                    sg = sg + first_mask
                sink = sinks_ref[heads[g]] * LOG2E
                mg = jnp.maximum(jnp.max(sg, axis=-1, keepdims=True), sink)
                pg = jnp.exp2(sg - mg)
                lg = jnp.sum(pg, axis=-1, keepdims=True) + jnp.exp2(sink - mg)
                ps.append(pg.astype(BF16))
                linvs.append(1.0 / lg)
            o = _dot(jnp.concatenate(ps, axis=0), vb)
            o_heads += [o[g * WINDOW:(g + 1) * WINDOW] * linvs[g] for g in range(Q_PER_KV)]
        rows_out.append(jnp.concatenate(
            [jnp.where(lo, o_heads[2 * j], o_heads[2 * j + 1]) for j in range(N_HEADS // 2)], axis=1))
    for kh in range(N_KV):
        kprev_ref[kh] = kd[kh][tile - WINDOW:tile]
        vprev_ref[kh] = vd[kh][tile - WINDOW:tile]
    return jnp.concatenate(rows_out, axis=0)


def _shift_window(cache_ref, new_t, out_ref, nb, dec):
    lane = lax.broadcasted_iota(jnp.int32, (1, WINDOW), 1)
    keep = lane < WINDOW - dec
    per_tile = LANES // dec
    for b in range(nb):
        src = new_t[:, (b // per_tile) * LANES:(b // per_tile + 1) * LANES]
        new_cols = pltpu.roll(src, (WINDOW - dec - (b % per_tile) * dec) % LANES, axis=1)
        old = pltpu.roll(cache_ref[b], WINDOW - dec, axis=1)
        out_ref[b] = jnp.where(keep, old, new_cols)


def _swa_sample(q, k, v, ck_ref, cv_ref, wk_ref, wv_ref, bias_c_ref, bias_n_ref, sinks_ref, nb, dec):
    lo = _lane_lo()
    qg = [q[:, j * LANES:(j + 1) * LANES] for j in range(N_HEADS // 2)]
    qgs = [_swap_halves(x) for x in qg]
    qb = jnp.concatenate(
        [_head_lhs(qg, qgs, h, lo).reshape(nb, dec, LANES) for h in range(N_HEADS)], axis=1)
    rows = N_HEADS * dec
    qb_bf = qb.astype(BF16)
    k_bf = k.astype(BF16)
    v_bf = v.astype(BF16)
    ck = ck_ref[...].astype(BF16)
    cv = cv_ref[...].astype(BF16)
    sc = jnp.einsum('bqc,bcp->bqp', qb_bf, ck, preferred_element_type=F32) + bias_c_ref[...][None]
    sn = _dot_nt(qb_bf.reshape(nb * rows, LANES), k_bf).reshape(nb, rows, nb * dec) + bias_n_ref[...]
    sink = jnp.concatenate(
        [jnp.full((1, dec, 1), sinks_ref[h] * LOG2E, F32) for h in range(N_HEADS)], axis=1)
    m = jnp.maximum(jnp.max(sc, axis=-1, keepdims=True), jnp.max(sn, axis=-1, keepdims=True))
    m = jnp.maximum(m, sink)
    pc = jnp.exp2(sc - m)
    pn = jnp.exp2(sn - m)
    l = jnp.sum(pc, axis=-1, keepdims=True) + jnp.sum(pn, axis=-1, keepdims=True) + jnp.exp2(sink - m)
    oc = jnp.einsum('bqp,bcp->bqc', pc.astype(BF16), cv, preferred_element_type=F32)
    on = _dot(pn.reshape(nb * rows, nb * dec).astype(BF16), v_bf).reshape(nb, rows, LANES)
    o = (oc + on) * (1.0 / l)
    o_heads = [o[:, h * dec:(h + 1) * dec, :].reshape(nb * dec, LANES) for h in range(N_HEADS)]
    _shift_window(ck_ref, k.T, wk_ref, nb, dec)
    _shift_window(cv_ref, v.T, wv_ref, nb, dec)
    return _merge_heads(o_heads, lo)


def _route(hn_bf, wr_ref, br_ref):
    logits = _dot(hn_bf, wr_ref[...]) + br_ref[...]
    lane = lax.broadcasted_iota(jnp.int32, logits.shape, 1)
    big = jnp.int32(ROUTER_LANES)
    lc = jnp.where(lane < N_EXPERT_GROUPS, logits, NEG_INF)
    mx = jnp.max(lc, axis=-1, keepdims=True)
    g_idx = jnp.min(jnp.where(lc == mx, lane, big), axis=-1, keepdims=True)
    p_g = 1.0 / jnp.sum(jnp.exp(lc - mx), axis=-1, keepdims=True)
    e_lane = lane - N_EXPERT_GROUPS
    in_group = (e_lane >= 0) & (e_lane < N_EXPERTS) & ((e_lane >> 2) == g_idx)
    lf = jnp.where(in_group, logits, NEG_INF)
    v1 = jnp.max(lf, axis=-1, keepdims=True)
    i1 = jnp.min(jnp.where(lf == v1, lane, big), axis=-1, keepdims=True)
    lf2 = jnp.where(lane == i1, NEG_INF, lf)
    v2 = jnp.max(lf2, axis=-1, keepdims=True)
    i2 = jnp.min(jnp.where(lf2 == v2, lane, big), axis=-1, keepdims=True)
    e = jnp.exp(v2 - v1)
    w1 = p_g / (1.0 + e)
    w2 = w1 * e
    return lane, g_idx, i1, i2, w1, w2


def _silu(x):
    return x * (1.0 / (1.0 + jnp.exp2(x * (-LOG2E))))


def _moe_dense(h, g_ffn_ref, wr_ref, br_ref, wg_ref, wu_ref, wd_ref, hcat_ref):
    hn = _rms(h, g_ffn_ref[...])
    hn_bf = hn.astype(BF16)
    lane, _, i1, i2, w1, w2 = _route(hn_bf, wr_ref, br_ref)
    combine = jnp.where(lane == i1, w1, jnp.where(lane == i2, w2, 0.0))
    for e in range(N_EXPERTS):
        gate = _dot(hn_bf, wg_ref[e])
        up = _dot(hn_bf, wu_ref[e])
        c = combine[:, N_EXPERT_GROUPS + e:N_EXPERT_GROUPS + e + 1]
        hcat_ref[:, e * D_EXPERT:(e + 1) * D_EXPERT] = (_silu(gate) * up * c).astype(BF16)
    return _dot(hcat_ref[...], wd_ref[...])


GATE_LANES = EXPERTS_PER_GROUP
GROUP_LANE = EXPERTS_PER_GROUP
LOW_SHIFT = 8
SORT_BLOCK = 128


def _moe_grouped(h, g_ffn_ref, wr_ref, br_ref, wg_ref, wu_ref, wd_ref, lstrict_ref,
                 xs_ref, es_ref, ys_ref, tile):
    hn = _rms(h, g_ffn_ref[...])
    hn_bf = hn.astype(BF16)
    lane, g_idx, i1, i2, w1, w2 = _route(hn_bf, wr_ref, br_ref)

    onehot = lane == g_idx
    gmat = jnp.where(onehot, 1.0, 0.0)
    before = _dot(lstrict_ref[...], gmat.astype(BF16))
    rank = jnp.sum(jnp.where(onehot, before, 0.0), axis=-1, keepdims=True)
    counts = jnp.sum(gmat, axis=0, keepdims=True)
    lane1 = lax.broadcasted_iota(jnp.int32, counts.shape, 1)
    offs = [jnp.float32(0.0)]
    for g in range(N_EXPERT_GROUPS - 1):
        offs.append(offs[-1] + jnp.sum(jnp.where(lane1 == g, counts, 0.0)))
    offv = jnp.zeros_like(rank)
    for g in range(1, N_EXPERT_GROUPS):
        offv = offv + jnp.where(g_idx == g, offs[g], 0.0)
    pos = (rank + offv).astype(jnp.int32)
    dest = lax.broadcasted_iota(jnp.int32, (tile, tile), 1)
    pt = jnp.where(dest == pos, 1.0, 0.0).astype(BF16)

    j1 = (i1 - N_EXPERT_GROUPS) & (EXPERTS_PER_GROUP - 1)
    j2 = (i2 - N_EXPERT_GROUPS) & (EXPERTS_PER_GROUP - 1)
    gates = jnp.where(lane == j1, w1, jnp.where(lane == j2, w2, 0.0))
    gates_hi = gates.astype(BF16).astype(F32)
    gates_lo = (gates - gates_hi).astype(BF16).astype(F32)
    side = gates_hi + pltpu.roll(gates_lo, LOW_SHIFT, axis=1) + jnp.where(lane == GROUP_LANE, g_idx.astype(F32), 0.0)
    x_ext = jnp.concatenate([hn_bf, side.astype(BF16)], axis=1)
    srt = lax.dot_general(pt, x_ext, (((0,), (0,)), ((), ())), preferred_element_type=F32)
    xs_ref[...] = srt[:, :D_MODEL].astype(BF16)
    es = srt[:, D_MODEL:]
    es_ref[...] = es + pltpu.roll(es, LANES - LOW_SHIFT, axis=1)

    def group_of(row):
        g = jnp.int32(0)
        for k in range(1, N_EXPERT_GROUPS):
            g = g + (jnp.float32(row) >= offs[k]).astype(jnp.int32)
        return g

    def group_pass(rows, g, keep, accumulate):
        xb = xs_ref[rows, :]
        eb = es_ref[rows, :]
        mine = jnp.where(eb[:, GROUP_LANE:GROUP_LANE + 1] == jnp.asarray(g).astype(F32), keep, 0.0)
        parts = []
        for j in range(EXPERTS_PER_GROUP):
            e = g * EXPERTS_PER_GROUP + j
            gate = _dot(xb, wg_ref[e])
            up = _dot(xb, wu_ref[e])
            parts.append((_silu(gate) * up * (eb[:, j:j + 1] * mine)).astype(BF16))
        wd = wd_ref[pl.ds(g * EXPERTS_PER_GROUP, EXPERTS_PER_GROUP)]
        out = _dot(jnp.concatenate(parts, axis=1), wd.reshape(EXPERTS_PER_GROUP * D_EXPERT, D_MODEL))
        if accumulate:
            ys_ref[rows, :] += out
        else:
            ys_ref[rows, :] = out

    nblk = tile // SORT_BLOCK
    for b in range(nblk):
        group_pass(pl.ds(b * SORT_BLOCK, SORT_BLOCK), group_of(b * SORT_BLOCK), 1.0, False)
    for g in range(1, N_EXPERT_GROUPS):
        start = offs[g].astype(jnp.int32)
        blk = jnp.minimum(lax.div(start, jnp.int32(SORT_BLOCK)), nblk - 1)
        inside = jnp.where(lax.rem(start, jnp.int32(SORT_BLOCK)) != 0, 1.0, 0.0)
        group_pass(pl.ds(pl.multiple_of(blk * SORT_BLOCK, SORT_BLOCK), SORT_BLOCK), g, inside, True)
    return _dot(pt, ys_ref[...].astype(BF16))


def _merge(x, a_out, b_out, g_out_ref, w_out_ref):
    g_out = g_out_ref[...]
    mix = jnp.concatenate(
        [_rms(a_out, g_out[:, :A_WIDTH]), _rms(b_out, g_out[:, A_WIDTH:])], axis=1).astype(BF16)
    return x + _dot(mix, w_out_ref[...])


def _stage_expert_weights(wg_hbm, wu_hbm, wd_hbm, wg_s, wu_s, wd_s, st_a, st_d, sems):
    def gate_cp(e):
        return pltpu.make_async_copy(wg_hbm.at[e], st_a.at[0], sems.at[0])

    def up_cp(e):
        return pltpu.make_async_copy(wu_hbm.at[e], st_a.at[1], sems.at[1])

    def down_cp(e, slot):
        return pltpu.make_async_copy(wd_hbm.at[e], st_d.at[slot], sems.at[2 + slot])

    gate_cp(0).start()
    up_cp(0).start()
    down_cp(0, 0).start()

    def pair(p, carry):
        for slot in (0, 1):
            e = 2 * p + slot
            nxt = e + 1
            more = nxt < N_EXPERTS

            @pl.when(more)
            def _():
                down_cp(nxt, 1 - slot).start()

            gate_cp(e).wait()
            wg_s[e] = st_a[0].astype(BF16)

            @pl.when(more)
            def _():
                gate_cp(nxt).start()

            up_cp(e).wait()
            wu_s[e] = st_a[1].astype(BF16)

            @pl.when(more)
            def _():
                up_cp(nxt).start()

            down_cp(e, slot).wait()
            wd_s[e] = st_d[slot].astype(BF16)
        return carry

    lax.fori_loop(0, N_EXPERTS // 2, pair, 0)


def _bf16_weight_copies(wg_s, wu_s, wd_s, wg_o, wu_o, wd_o, sems):
    return (pltpu.make_async_copy(wg_s, wg_o, sems.at[4]),
            pltpu.make_async_copy(wu_s, wu_o, sems.at[5]),
            pltpu.make_async_copy(wd_s, wd_o, sems.at[6]))


def _prompt_kernel(x_ref, g_attn_ref, w_in_ref, g_va_ref, g_q_ref, g_k_ref, segq_ref, segk_ref,
                   wsp_ref, bsp_ref, bias_ref, sinks_ref, g_out_ref, w_out_ref, g_ffn_ref,
                   wr_ref, br_ref, lstrict_ref, wg_hbm, wu_hbm, wd_hbm,
                   y_ref, kwin_ref, vwin_ref, wg_o, wu_o, wd_o,
                   kprev_ref, vprev_ref, xs_ref, es_ref, ys_ref, wg_ref, wu_ref, wd_ref, st_a, st_d, sems,
                   *, tile, tiles_per_seq):
    step = pl.program_id(0)
    is_first = (step % tiles_per_seq) == 0

    @pl.when(step == 0)
    def _():
        _stage_expert_weights(wg_hbm, wu_hbm, wd_hbm, wg_ref, wu_ref, wd_ref, st_a, st_d, sems)
        for cp in _bf16_weight_copies(wg_ref, wu_ref, wd_ref, wg_o, wu_o, wd_o, sems):
            cp.start()

    @pl.when(step == pl.num_programs(0) - 1)
    def _():
        for cp in _bf16_weight_copies(wg_ref, wu_ref, wd_ref, wg_o, wu_o, wd_o, sems):
            cp.wait()

    @pl.when(is_first)
    def _():
        kprev_ref[...] = jnp.zeros_like(kprev_ref)
        vprev_ref[...] = jnp.zeros_like(vprev_ref)

    x = x_ref[...]
    u, va, q, k, v = _project(x, g_attn_ref, w_in_ref, g_va_ref, g_q_ref, g_k_ref, segq_ref, segk_ref)
    kwin_ref[...] = k[tile - WINDOW:tile]
    vwin_ref[...] = v[tile - WINDOW:tile]
    a_out = u * _chunk_mlp(va.astype(BF16), wsp_ref, bsp_ref, tile)
    b_out = _swa_prompt(q * Q_SCALE, k, v, kprev_ref, vprev_ref, bias_ref, sinks_ref, is_first, tile)
    h = _merge(x, a_out, b_out, g_out_ref, w_out_ref)
    y_ref[...] = h + _moe_grouped(h, g_ffn_ref, wr_ref, br_ref, wg_ref, wu_ref, wd_ref,
                                  lstrict_ref, xs_ref, es_ref, ys_ref, tile)


def _sample_kernel(x_ref, ck_ref, cv_ref, g_attn_ref, w_in_ref, g_va_ref, g_q_ref, g_k_ref, segq_ref,
                   segk_ref, wsp_ref, bsp_ref, bias_c_ref, bias_n_ref, sinks_ref, g_out_ref, w_out_ref,
                   g_ffn_ref, wr_ref, br_ref, wg_ref, wu_ref, wd_ref,
                   y_ref, wk_ref, wv_ref, va_ref,
                   hcat_ref, *, tile, nb, dec):
    x = x_ref[...]
    u, va, q, k, v = _project(x, g_attn_ref, w_in_ref, g_va_ref, g_q_ref, g_k_ref, segq_ref, segk_ref)
    va_ref[...] = va
    a_out = u * _chunk_mlp(va.astype(BF16), wsp_ref, bsp_ref, tile)
    b_out = _swa_sample(q * Q_SCALE, k, v, ck_ref, cv_ref, wk_ref, wv_ref,
                        bias_c_ref, bias_n_ref, sinks_ref, nb, dec)
    h = _merge(x, a_out, b_out, g_out_ref, w_out_ref)
    y_ref[...] = h + _moe_dense(h, g_ffn_ref, wr_ref, br_ref, wg_ref, wu_ref, wd_ref, hcat_ref)


def _const_spec(shape):
    nd = len(shape)
    return pl.BlockSpec(shape, lambda i: (0,) * nd, pipeline_mode=pl.Buffered(1))


def _row_spec(tile, width):
    return pl.BlockSpec((tile, width), lambda i: (i, 0))


def _smem_spec():
    return pl.BlockSpec(memory_space=pltpu.SMEM)


def _layer_weights(l, g_attn_norm, w_in, g_v_a, g_q, g_k, g_out_a, g_out_b, w_out, g_ffn_norm,
                   w_coarse, b_coarse, w_fine, b_fine, w_gate, w_up, w_down):
    pad = ROUTER_LANES - N_EXPERT_GROUPS - N_EXPERTS
    wr = jnp.concatenate([w_coarse[l], w_fine[l], jnp.zeros((D_MODEL, pad), F32)], axis=1)
    br = jnp.concatenate([b_coarse[l], b_fine[l], jnp.zeros((pad,), F32)]).reshape(1, ROUTER_LANES)
    return dict(
        g_attn=g_attn_norm[l].reshape(1, D_MODEL),
        w_in=w_in[l].astype(BF16),
        g_va=g_v_a[l].reshape(1, A_WIDTH),
        g_q=jnp.tile(g_q[l], N_HEADS).reshape(1, B_WIDTH),
        g_k=jnp.tile(g_k[l], N_KV).reshape(1, KV_WIDTH),
        segq=jnp.asarray(_seg_ones(B_WIDTH, HEAD_DIM), BF16),
        segk=jnp.asarray(_seg_ones(KV_WIDTH, HEAD_DIM), BF16),
        g_out=jnp.concatenate([g_out_a[l], g_out_b[l]]).reshape(1, D_MODEL),
        w_out=w_out[l].astype(BF16),
        g_ffn=g_ffn_norm[l].reshape(1, D_MODEL),
        wr=wr.astype(BF16),
        br=br,
    )


def _spatial_tables(ws, bs, period, tile):
    pair = min(PAIR, tile)
    reps = pair // period
    wbd = (jnp.tile(ws[:, :period, :period], (1, reps, reps)) * _causal_block_mask(period, pair)).astype(BF16)
    b = jnp.tile(bs[:, :period].T, (tile // period, 1))
    bsp = jnp.repeat(b, A_CH, axis=1)
    return wbd, bsp


_TAIL_NAMES = ("g_out", "w_out", "g_ffn", "wr", "br")
_HEAD_NAMES = ("g_attn", "w_in", "g_va", "g_q", "g_k", "segq", "segk")


def _run_prompt(x2d, lw, wbd, bsp, sinks, seq_len, w_gate, w_up, w_down):
    n_tok = x2d.shape[0]
    tile = PROMPT_TILE
    tiles_per_seq = seq_len // tile
    n_seq = n_tok // seq_len
    bias = jnp.asarray(_prompt_bias())
    head = [lw[n] for n in _HEAD_NAMES]
    tail = [lw[n] for n in _TAIL_NAMES]
    tail.append(jnp.asarray(np.tril(np.ones((tile, tile), np.float32), -1), BF16))
    experts = [w_gate, w_up, w_down]
    consts = head + [wbd, bsp, bias]
    any_spec = pl.BlockSpec(memory_space=pl.ANY)
    in_specs = ([_row_spec(tile, D_MODEL)] + [_const_spec(a.shape) for a in consts] + [_smem_spec()]
                + [_const_spec(a.shape) for a in tail] + [any_spec] * len(experts))
    win_spec = pl.BlockSpec((WINDOW, KV_WIDTH), lambda i: (i // tiles_per_seq, 0))
    out_shape = (jax.ShapeDtypeStruct((n_tok, D_MODEL), F32),
                 jax.ShapeDtypeStruct((n_seq * WINDOW, KV_WIDTH), F32),
                 jax.ShapeDtypeStruct((n_seq * WINDOW, KV_WIDTH), F32),
                 *[jax.ShapeDtypeStruct(w.shape, BF16) for w in experts])
    out_specs = (_row_spec(tile, D_MODEL), win_spec, win_spec, any_spec, any_spec, any_spec)
    kern = functools.partial(_prompt_kernel, tile=tile, tiles_per_seq=tiles_per_seq)
    return pl.pallas_call(
        kern,
        out_shape=out_shape,
        grid=(n_tok // tile,),
        in_specs=in_specs,
        out_specs=out_specs,
        scratch_shapes=[pltpu.VMEM((N_KV, WINDOW, KV_WIDTH), BF16), pltpu.VMEM((N_KV, WINDOW, KV_WIDTH), BF16),
                        pltpu.VMEM((tile, D_MODEL), BF16), pltpu.VMEM((tile, LANES), F32),
                        pltpu.VMEM((tile, D_MODEL), F32),
                        *[pltpu.VMEM(w.shape, BF16) for w in experts],
                        pltpu.VMEM((2,) + w_gate.shape[1:], F32), pltpu.VMEM((2,) + w_down.shape[1:], F32),
                        pltpu.SemaphoreType.DMA((7,))],
        compiler_params=pltpu.CompilerParams(dimension_semantics=("arbitrary",),
                                             vmem_limit_bytes=VMEM_LIMIT_BYTES),
        name="layer_prompt",
    )(x2d, *consts, sinks, *tail, *experts)


def _run_sample(x2d, ck_t, cv_t, lw, wbd, bsp, sinks, dec, wg_bf, wu_bf, wd_bf):
    n_tok = x2d.shape[0]
    n_seq = ck_t.shape[0]
    tile = SAMPLE_TILE
    nb = tile // dec
    bc, bn = _sample_bias(nb, dec)
    head = [lw[n] for n in _HEAD_NAMES]
    tail = [lw[n] for n in _TAIL_NAMES] + [wg_bf, wu_bf, wd_bf.reshape(N_EXPERTS * D_EXPERT, D_MODEL)]
    consts = head + [wbd, bsp, jnp.asarray(bc), jnp.asarray(bn)]
    cache_spec = pl.BlockSpec((nb, KV_WIDTH, WINDOW), lambda i: (i, 0, 0))
    in_specs = ([_row_spec(tile, D_MODEL), cache_spec, cache_spec] + [_const_spec(a.shape) for a in consts]
                + [_smem_spec()] + [_const_spec(a.shape) for a in tail])
    out_shape = (jax.ShapeDtypeStruct((n_tok, D_MODEL), F32),
                 jax.ShapeDtypeStruct((n_seq, KV_WIDTH, WINDOW), F32),
                 jax.ShapeDtypeStruct((n_seq, KV_WIDTH, WINDOW), F32),
                 jax.ShapeDtypeStruct((n_tok, A_WIDTH), F32))
    out_specs = (_row_spec(tile, D_MODEL), cache_spec, cache_spec, _row_spec(tile, A_WIDTH))
    kern = functools.partial(_sample_kernel, tile=tile, nb=nb, dec=dec)
    return pl.pallas_call(
        kern,
        out_shape=out_shape,
        grid=(n_tok // tile,),
        in_specs=in_specs,
        out_specs=out_specs,
        scratch_shapes=[pltpu.VMEM((tile, N_EXPERTS * D_EXPERT), BF16)],
        compiler_params=pltpu.CompilerParams(dimension_semantics=("arbitrary",),
                                             vmem_limit_bytes=VMEM_LIMIT_BYTES),
        name="layer_sample",
    )(x2d, ck_t, cv_t, *consts, sinks, *tail)


def _positions_last(c):
    b, w = c.shape[:2]
    return jnp.transpose(c, (0, 2, 3, 1)).reshape(b, KV_WIDTH, w)


def _positions_first(c_t):
    b, _, w = c_t.shape
    return jnp.transpose(c_t.reshape(b, N_KV, HEAD_DIM, w), (0, 3, 1, 2))


def kernel(x_prompt, x_sample, cache_k, cache_v, g_attn_norm, w_in, g_v_a, w_spatial, b_spatial, g_q, g_k, attn_sinks, g_out_a, g_out_b, w_out, g_ffn_norm, w_coarse, b_coarse, w_fine, b_fine, w_gate, w_up, w_down):
    depth = w_in.shape[0]
    batch, seq, _ = x_prompt.shape
    dbatch, dec, _ = x_sample.shape
    win = cache_k.shape[2]
    assert win == WINDOW and seq % PROMPT_TILE == 0 and (dbatch * dec) % SAMPLE_TILE == 0
    assert PAIR % dec == 0 and SAMPLE_TILE % dec == 0 and LANES % dec == 0

    hp = x_prompt.reshape(batch * seq, D_MODEL)
    hs = x_sample.reshape(dbatch * dec, D_MODEL)
    kp_l, vp_l, ks_l, vs_l, cv_l = [], [], [], [], []
    for l in range(depth):
        lw = _layer_weights(l, g_attn_norm, w_in, g_v_a, g_q, g_k, g_out_a, g_out_b, w_out, g_ffn_norm,
                            w_coarse, b_coarse, w_fine, b_fine, w_gate, w_up, w_down)
        sinks = attn_sinks[l].astype(F32)
        wbd_p, bsp_p = _spatial_tables(w_spatial[l], b_spatial[l], CHUNK, PROMPT_TILE)
        wbd_s, bsp_s = _spatial_tables(w_spatial[l], b_spatial[l], dec, SAMPLE_TILE)

        hp, kp, vp, wg_bf, wu_bf, wd_bf = _run_prompt(hp, lw, wbd_p, bsp_p, sinks, seq,
                                                      w_gate[l], w_up[l], w_down[l])
        kp_l.append(kp.reshape(batch, WINDOW, N_KV, HEAD_DIM))
        vp_l.append(vp.reshape(batch, WINDOW, N_KV, HEAD_DIM))

        hs, wk, wv, va = _run_sample(hs, _positions_last(cache_k[l]), _positions_last(cache_v[l]),
                                     lw, wbd_s, bsp_s, sinks, dec, wg_bf, wu_bf, wd_bf)
        ks_l.append(_positions_first(wk))
        vs_l.append(_positions_first(wv))
        cv_l.append(va.reshape(dbatch, dec, A_GROUPS, A_CH))

    return (hp.reshape(batch, seq, D_MODEL), hs.reshape(dbatch, dec, D_MODEL),
            jnp.stack(kp_l, axis=0), jnp.stack(vp_l, axis=0),
            jnp.stack(ks_l, axis=0), jnp.stack(vs_l, axis=0), jnp.stack(cv_l, axis=0))
```

```python
import functools

import numpy as np
import jax
import jax.numpy as jnp
from jax import lax
from jax.experimental import pallas as pl
from jax.experimental.pallas import tpu as pltpu

D_MODEL = 1024
CHUNK = 128
A_GROUPS = 4
A_WIDTH = 512
A_CH = 128
N_HEADS = 8
N_KV = 2
Q_PER_KV = 4
HEAD_DIM = 64
B_WIDTH = 512
KV_WIDTH = 128
WINDOW = 128
IN_COLS = 2 * A_WIDTH + B_WIDTH + 2 * KV_WIDTH
N_EXPERT_GROUPS = 4
EXPERTS_PER_GROUP = 4
N_EXPERTS = 16
D_EXPERT = 256
EPS = 1e-6

LANES = 128
PAIR = 2 * CHUNK
ROUTER_LANES = 128
PROMPT_TILE = 512
SAMPLE_TILE = 128
VMEM_LIMIT_BYTES = 60 * 1024 * 1024

F32 = jnp.float32
BF16 = jnp.bfloat16
NEG_INF = float("-inf")
LOG2E = 1.4426950408889634
Q_SCALE = (HEAD_DIM ** -0.5) * LOG2E


def _slopes():
    return np.array([2.0 ** (-8.0 * (h + 1) / N_HEADS) for h in range(N_HEADS)], np.float64)


def _prompt_bias():
    t = np.arange(WINDOW)[:, None]
    s = np.arange(2 * WINDOW)[None, :]
    dist = t + WINDOW - s
    valid = (dist >= 0) & (dist < WINDOW)
    sl = _slopes()
    out = np.full((N_KV, Q_PER_KV * WINDOW, 2 * WINDOW), -np.inf, np.float32)
    for kh in range(N_KV):
        for g in range(Q_PER_KV):
            b = np.where(valid, -sl[kh * Q_PER_KV + g] * LOG2E * dist, -np.inf)
            out[kh, g * WINDOW:(g + 1) * WINDOW] = b
    return out


def _sample_bias(nb, dec):
    sl = _slopes()
    t = np.arange(dec)[:, None]
    j = np.arange(WINDOW)[None, :]
    dist_c = t + WINDOW - j
    valid_c = (dist_c >= 0) & (dist_c < WINDOW)
    bc = np.full((N_HEADS * dec, WINDOW), -np.inf, np.float32)
    tp = np.arange(dec)[None, :]
    dist_n = t - tp
    valid_n = dist_n >= 0
    bn = np.full((nb, N_HEADS * dec, nb * dec), -np.inf, np.float32)
    for h in range(N_HEADS):
        bc[h * dec:(h + 1) * dec] = np.where(valid_c, -sl[h] * LOG2E * dist_c, -np.inf)
        blk = np.where(valid_n, -sl[h] * LOG2E * dist_n, -np.inf)
        for b in range(nb):
            bn[b, h * dec:(h + 1) * dec, b * dec:(b + 1) * dec] = blk
    return bc, bn


def _seg_ones(width, seg):
    i = np.arange(width)
    return (i[:, None] // seg == i[None, :] // seg).astype(np.float32)


def _causal_block_mask(period, size):
    i = np.arange(size)
    same = i[:, None] // period == i[None, :] // period
    return (same & (i[None, :] % period <= i[:, None] % period)).astype(np.float32)


def _dot(a, b):
    return jnp.dot(a, b, preferred_element_type=F32)


def _dot_nt(a, b):
    return lax.dot_general(a, b, (((1,), (1,)), ((), ())), preferred_element_type=F32)


def _rms(x, g):
    ms = jnp.mean(x * x, axis=-1, keepdims=True)
    return (x * lax.rsqrt(ms + EPS)) * g


def _seg_rms(x, ones_bf, g, seg):
    ss = _dot((x * x).astype(BF16), ones_bf)
    return (x * lax.rsqrt(ss * (1.0 / seg) + EPS)) * g


GELU_K0 = -2.0 * 0.7978845608028654 * LOG2E
GELU_K1 = GELU_K0 * 0.044715


def _gelu_tanh(x):
    t = (x * x) * GELU_K1 + GELU_K0
    return x * (1.0 / (1.0 + jnp.exp2(x * t)))


VEC_ATTN = 0
VEC_VA = VEC_ATTN + D_MODEL
VEC_Q = VEC_VA + A_WIDTH
VEC_K = VEC_Q + B_WIDTH
VEC_OUT = VEC_K + KV_WIDTH
VEC_FFN = VEC_OUT + D_MODEL
VEC_ROUTER_BIAS = VEC_FFN + D_MODEL
VEC_LANES = VEC_ROUTER_BIAS + ROUTER_LANES


def _vec(vec_ref, start, width):
    return vec_ref[:, start:start + width]


def _project(x, vec_ref, w_in_ref, segq_ref, segk_ref):
    xn = _rms(x, _vec(vec_ref, VEC_ATTN, D_MODEL)).astype(BF16)
    z = _dot(xn, w_in_ref[...])
    i1, i2, i3, i4 = A_WIDTH, 2 * A_WIDTH, 2 * A_WIDTH + B_WIDTH, 2 * A_WIDTH + B_WIDTH + KV_WIDTH
    u = _gelu_tanh(z[:, :i1])
    va_pre = _gelu_tanh(z[:, i1:i2])
    g_va = _vec(vec_ref, VEC_VA, A_WIDTH)
    va = jnp.concatenate(
        [_rms(va_pre[:, g * A_CH:(g + 1) * A_CH], g_va[:, g * A_CH:(g + 1) * A_CH]) for g in range(A_GROUPS)],
        axis=1)
    q = _seg_rms(z[:, i2:i3], segq_ref[...], _vec(vec_ref, VEC_Q, B_WIDTH), HEAD_DIM)
    k = _seg_rms(z[:, i3:i4], segk_ref[...], _vec(vec_ref, VEC_K, KV_WIDTH), HEAD_DIM)
    v = z[:, i4:]
    return u, va, q, k, v


def _chunk_mlp(va_bf, wsp_ref, bsp_ref, tile):
    pair = wsp_ref.shape[1]
    n_pairs = tile // pair
    outs = []
    for g in range(A_GROUPS):
        cols = [va_bf[p * pair:(p + 1) * pair, g * A_CH:(g + 1) * A_CH] for p in range(n_pairs)]
        rhs = cols[0] if n_pairs == 1 else jnp.concatenate(cols, axis=1)
        o = _dot(wsp_ref[g], rhs)
        rows = [o[:, p * A_CH:(p + 1) * A_CH] for p in range(n_pairs)]
        outs.append(rows[0] if n_pairs == 1 else jnp.concatenate(rows, axis=0))
    mixed = jnp.concatenate(outs, axis=1)
    bias = bsp_ref[...]
    reps = tile // bias.shape[0]
    return mixed + (bias if reps == 1 else jnp.concatenate([bias] * reps, axis=0))


def _lane_lo():
    return lax.broadcasted_iota(jnp.int32, (1, LANES), 1) < HEAD_DIM


def _swap_halves(x):
    return pltpu.roll(x, HEAD_DIM, axis=1)


def _head_lhs(q_groups, q_groups_swapped, head, lo):
    j, half = divmod(head, 2)
    kv = head // Q_PER_KV
    src = q_groups[j] if half == kv else q_groups_swapped[j]
    return jnp.where(lo if kv == 0 else jnp.logical_not(lo), src, 0.0)


def _merge_heads(o_heads, lo):
    groups = []
    for j in range(N_HEADS // 2):
        kv = (2 * j) // Q_PER_KV
        if kv == 0:
            groups.append(jnp.where(lo, o_heads[2 * j], _swap_halves(o_heads[2 * j + 1])))
        else:
            groups.append(jnp.where(lo, _swap_halves(o_heads[2 * j]), o_heads[2 * j + 1]))
    return jnp.concatenate(groups, axis=1)


def _dup_halves(x, lo):
    xs = _swap_halves(x)
    return jnp.where(lo, x, xs).astype(BF16), jnp.where(lo, xs, x).astype(BF16)


def _swa_prompt(q, k, v, kprev_ref, vprev_ref, bias_ref, sinks_ref, is_first, tile):
    lo = _lane_lo()
    hi = jnp.logical_not(lo)
    col = lax.broadcasted_iota(jnp.int32, (WINDOW, 2 * WINDOW), 1)
    first_mask = jnp.where(col < WINDOW, jnp.where(is_first, NEG_INF, 0.0).astype(F32), 0.0)
    kd = _dup_halves(k, lo)
    vd = _dup_halves(v, lo)
    nblk = tile // WINDOW
    rows_out = []
    for i in range(nblk):
        r0, r1 = i * WINDOW, (i + 1) * WINDOW
        qg = [q[r0:r1, j * LANES:(j + 1) * LANES] for j in range(N_HEADS // 2)]
        o_heads = []
        for kh in range(N_KV):
            if i == 0:
                kp, vp = kprev_ref[kh], vprev_ref[kh]
            else:
                kp, vp = kd[kh][r0 - WINDOW:r0], vd[kh][r0 - WINDOW:r0]
            kb = jnp.concatenate([kp, kd[kh][r0:r1]], axis=0)
            vb = jnp.concatenate([vp, vd[kh][r0:r1]], axis=0)
            heads = [kh * Q_PER_KV + g for g in range(Q_PER_KV)]
            lhs = jnp.concatenate(
                [jnp.where(lo if h % 2 == 0 else hi, qg[h // 2], 0.0) for h in heads], axis=0).astype(BF16)
            s = _dot_nt(lhs, kb) + bias_ref[kh]
            ps, linvs = [], []
            for g in range(Q_PER_KV):
                sg = s[g * WINDOW:(g + 1) * WINDOW]
                if i == 0:
                    sg = sg + first_mask
                sink = sinks_ref[heads[g]] * LOG2E
                mg = jnp.maximum(jnp.max(sg, axis=-1, keepdims=True), sink)
                pg = jnp.exp2(sg - mg)
                lg = jnp.sum(pg, axis=-1, keepdims=True) + jnp.exp2(sink - mg)
                ps.append(pg.astype(BF16))
                linvs.append(1.0 / lg)
            o = _dot(jnp.concatenate(ps, axis=0), vb)
            o_heads += [o[g * WINDOW:(g + 1) * WINDOW] * linvs[g] for g in range(Q_PER_KV)]
        rows_out.append(jnp.concatenate(
            [jnp.where(lo, o_heads[2 * j], o_heads[2 * j + 1]) for j in range(N_HEADS // 2)], axis=1))
    for kh in range(N_KV):
        kprev_ref[kh] = kd[kh][tile - WINDOW:tile]
        vprev_ref[kh] = vd[kh][tile - WINDOW:tile]
    return jnp.concatenate(rows_out, axis=0)


def _shift_window(cache_ref, new_t, out_ref, nb, dec):
    lane = lax.broadcasted_iota(jnp.int32, (1, WINDOW), 1)
    keep = lane < WINDOW - dec
    per_tile = LANES // dec
    for b in range(nb):
        src = new_t[:, (b // per_tile) * LANES:(b // per_tile + 1) * LANES]
        new_cols = pltpu.roll(src, (WINDOW - dec - (b % per_tile) * dec) % LANES, axis=1)
        old = pltpu.roll(cache_ref[b], WINDOW - dec, axis=1)
        out_ref[b] = jnp.where(keep, old, new_cols)


def _swa_sample(q, k, v, ck_ref, cv_ref, wk_ref, wv_ref, bias_c_ref, bias_n_ref, sinks_ref, nb, dec):
    lo = _lane_lo()
    qg = [q[:, j * LANES:(j + 1) * LANES] for j in range(N_HEADS // 2)]
    qgs = [_swap_halves(x) for x in qg]
    qb = jnp.concatenate(
        [_head_lhs(qg, qgs, h, lo).reshape(nb, dec, LANES) for h in range(N_HEADS)], axis=1)
    rows = N_HEADS * dec
    qb_bf = qb.astype(BF16)
    k_bf = k.astype(BF16)
    v_bf = v.astype(BF16)
    ck = ck_ref[...].astype(BF16)
    cv = cv_ref[...].astype(BF16)
    sc = jnp.einsum('bqc,bcp->bqp', qb_bf, ck, preferred_element_type=F32) + bias_c_ref[...][None]
    sn = _dot_nt(qb_bf.reshape(nb * rows, LANES), k_bf).reshape(nb, rows, nb * dec) + bias_n_ref[...]
    sink = jnp.concatenate(
        [jnp.full((1, dec, 1), sinks_ref[h] * LOG2E, F32) for h in range(N_HEADS)], axis=1)
    m = jnp.maximum(jnp.max(sc, axis=-1, keepdims=True), jnp.max(sn, axis=-1, keepdims=True))
    m = jnp.maximum(m, sink)
    pc = jnp.exp2(sc - m)
    pn = jnp.exp2(sn - m)
    l = jnp.sum(pc, axis=-1, keepdims=True) + jnp.sum(pn, axis=-1, keepdims=True) + jnp.exp2(sink - m)
    oc = jnp.einsum('bqp,bcp->bqc', pc.astype(BF16), cv, preferred_element_type=F32)
    on = _dot(pn.reshape(nb * rows, nb * dec).astype(BF16), v_bf).reshape(nb, rows, LANES)
    o = (oc + on) * (1.0 / l)
    o_heads = [o[:, h * dec:(h + 1) * dec, :].reshape(nb * dec, LANES) for h in range(N_HEADS)]
    _shift_window(ck_ref, k.T, wk_ref, nb, dec)
    _shift_window(cv_ref, v.T, wv_ref, nb, dec)
    return _merge_heads(o_heads, lo)


def _route(hn_bf, wr_ref, vec_ref):
    logits = _dot(hn_bf, wr_ref[...]) + _vec(vec_ref, VEC_ROUTER_BIAS, ROUTER_LANES)
    lane = lax.broadcasted_iota(jnp.int32, logits.shape, 1)
    big = jnp.int32(ROUTER_LANES)
    lc = jnp.where(lane < N_EXPERT_GROUPS, logits, NEG_INF)
    mx = jnp.max(lc, axis=-1, keepdims=True)
    g_idx = jnp.min(jnp.where(lc == mx, lane, big), axis=-1, keepdims=True)
    p_g = 1.0 / jnp.sum(jnp.exp(lc - mx), axis=-1, keepdims=True)
    e_lane = lane - N_EXPERT_GROUPS
    in_group = (e_lane >= 0) & (e_lane < N_EXPERTS) & ((e_lane >> 2) == g_idx)
    lf = jnp.where(in_group, logits, NEG_INF)
    v1 = jnp.max(lf, axis=-1, keepdims=True)
    i1 = jnp.min(jnp.where(lf == v1, lane, big), axis=-1, keepdims=True)
    lf2 = jnp.where(lane == i1, NEG_INF, lf)
    v2 = jnp.max(lf2, axis=-1, keepdims=True)
    i2 = jnp.min(jnp.where(lf2 == v2, lane, big), axis=-1, keepdims=True)
    e = jnp.exp(v2 - v1)
    w1 = p_g / (1.0 + e)
    w2 = w1 * e
    return lane, g_idx, i1, i2, w1, w2


def _silu(x):
    return x * (1.0 / (1.0 + jnp.exp2(x * (-LOG2E))))


GATE_LANES = EXPERTS_PER_GROUP
GROUP_LANE = EXPERTS_PER_GROUP
LOW_SHIFT = 8
SORT_BLOCK = 128


def _moe_grouped(h, vec_ref, wr_ref, wg_ref, wu_ref, wd_ref, lstrict_ref, xs_ref, es_ref, ys_ref, tile):
    hn_bf = _rms(h, _vec(vec_ref, VEC_FFN, D_MODEL)).astype(BF16)
    lane, g_idx, i1, i2, w1, w2 = _route(hn_bf, wr_ref, vec_ref)

    onehot = lane == g_idx
    gmat = jnp.where(onehot, 1.0, 0.0)
    before = _dot(lstrict_ref[...], gmat.astype(BF16))
    rank = jnp.sum(jnp.where(onehot, before, 0.0), axis=-1, keepdims=True)
    counts = jnp.sum(gmat, axis=0, keepdims=True)
    lane1 = lax.broadcasted_iota(jnp.int32, counts.shape, 1)
    offs = [jnp.float32(0.0)]
    for g in range(N_EXPERT_GROUPS - 1):
        offs.append(offs[-1] + jnp.sum(jnp.where(lane1 == g, counts, 0.0)))
    offv = jnp.zeros_like(rank)
    for g in range(1, N_EXPERT_GROUPS):
        offv = offv + jnp.where(g_idx == g, offs[g], 0.0)
    pos = (rank + offv).astype(jnp.int32)
    dest = lax.broadcasted_iota(jnp.int32, (tile, tile), 1)
    pt = jnp.where(dest == pos, 1.0, 0.0).astype(BF16)

    j1 = (i1 - N_EXPERT_GROUPS) & (EXPERTS_PER_GROUP - 1)
    j2 = (i2 - N_EXPERT_GROUPS) & (EXPERTS_PER_GROUP - 1)
    gates = jnp.where(lane == j1, w1, jnp.where(lane == j2, w2, 0.0))
    gates_hi = gates.astype(BF16).astype(F32)
    gates_lo = (gates - gates_hi).astype(BF16).astype(F32)
    side = gates_hi + pltpu.roll(gates_lo, LOW_SHIFT, axis=1) + jnp.where(lane == GROUP_LANE, g_idx.astype(F32), 0.0)
    x_ext = jnp.concatenate([hn_bf, side.astype(BF16)], axis=1)
    srt = lax.dot_general(pt, x_ext, (((0,), (0,)), ((), ())), preferred_element_type=F32)
    xs_ref[...] = srt[:, :D_MODEL].astype(BF16)
    es = srt[:, D_MODEL:]
    es_ref[...] = es + pltpu.roll(es, LANES - LOW_SHIFT, axis=1)

    def group_of(row):
        g = jnp.int32(0)
        for k in range(1, N_EXPERT_GROUPS):
            g = g + (jnp.float32(row) >= offs[k]).astype(jnp.int32)
        return g

    def group_pass(rows, g, keep, accumulate):
        xb = xs_ref[rows, :]
        eb = es_ref[rows, :]
        mine = jnp.where(eb[:, GROUP_LANE:GROUP_LANE + 1] == jnp.asarray(g).astype(F32), keep, 0.0)
        parts = []
        for j in range(EXPERTS_PER_GROUP):
            e = g * EXPERTS_PER_GROUP + j
            gate = _dot(xb, wg_ref[e])
            up = _dot(xb, wu_ref[e])
            parts.append((_silu(gate) * up * (eb[:, j:j + 1] * mine)).astype(BF16))
        wd = wd_ref[pl.ds(g * EXPERTS_PER_GROUP, EXPERTS_PER_GROUP)]
        out = _dot(jnp.concatenate(parts, axis=1), wd.reshape(EXPERTS_PER_GROUP * D_EXPERT, D_MODEL))
        if accumulate:
            ys_ref[rows, :] += out
        else:
            ys_ref[rows, :] = out

    nblk = tile // SORT_BLOCK
    for b in range(nblk):
        group_pass(pl.ds(b * SORT_BLOCK, SORT_BLOCK), group_of(b * SORT_BLOCK), 1.0, False)
    for g in range(1, N_EXPERT_GROUPS):
        start = offs[g].astype(jnp.int32)
        blk = jnp.minimum(lax.div(start, jnp.int32(SORT_BLOCK)), nblk - 1)
        inside = jnp.where(lax.rem(start, jnp.int32(SORT_BLOCK)) != 0, 1.0, 0.0)
        group_pass(pl.ds(pl.multiple_of(blk * SORT_BLOCK, SORT_BLOCK), SORT_BLOCK), g, inside, True)
    return _dot(pt, ys_ref[...].astype(BF16))


def _merge(x, a_out, b_out, vec_ref, w_out_ref):
    g_out = _vec(vec_ref, VEC_OUT, D_MODEL)
    mix = jnp.concatenate(
        [_rms(a_out, g_out[:, :A_WIDTH]), _rms(b_out, g_out[:, A_WIDTH:])], axis=1).astype(BF16)
    return x + _dot(mix, w_out_ref[...])


def _stage_expert_weights(wg_hbm, wu_hbm, wd_hbm, wg_s, wu_s, wd_s, st_a, st_d, sems):
    def gate_cp(e):
        return pltpu.make_async_copy(wg_hbm.at[e], st_a.at[0], sems.at[0])

    def up_cp(e):
        return pltpu.make_async_copy(wu_hbm.at[e], st_a.at[1], sems.at[1])

    def down_cp(e, slot):
        return pltpu.make_async_copy(wd_hbm.at[e], st_d.at[slot], sems.at[2 + slot])

    gate_cp(0).start()
    up_cp(0).start()
    down_cp(0, 0).start()

    def pair(p, carry):
        for slot in (0, 1):
            e = 2 * p + slot
            nxt = e + 1
            more = nxt < N_EXPERTS

            @pl.when(more)
            def _():
                down_cp(nxt, 1 - slot).start()

            gate_cp(e).wait()
            wg_s[e] = st_a[0].astype(BF16)

            @pl.when(more)
            def _():
                gate_cp(nxt).start()

            up_cp(e).wait()
            wu_s[e] = st_a[1].astype(BF16)

            @pl.when(more)
            def _():
                up_cp(nxt).start()

            down_cp(e, slot).wait()
            wd_s[e] = st_d[slot].astype(BF16)
        return carry

    lax.fori_loop(0, N_EXPERTS // 2, pair, 0)


def _bf16_weight_copies(wg_s, wu_s, wd_s, wg_o, wu_o, wd_o, sems):
    return (pltpu.make_async_copy(wg_s, wg_o, sems.at[4]),
            pltpu.make_async_copy(wu_s, wu_o, sems.at[5]),
            pltpu.make_async_copy(wd_s, wd_o, sems.at[6]))


def _prompt_kernel(x_ref, vec_ref, w_in_ref, segq_ref, segk_ref, wsp_ref, bsp_ref, bias_ref, sinks_ref,
                   w_out_ref, wr_ref, lstrict_ref, wg_hbm, wu_hbm, wd_hbm,
                   y_ref, kwin_ref, vwin_ref, wg_o, wu_o, wd_o,
                   kprev_ref, vprev_ref, xs_ref, es_ref, ys_ref, wg_ref, wu_ref, wd_ref, st_a, st_d, sems,
                   *, tile, tiles_per_seq):
    step = pl.program_id(0)
    is_first = (step % tiles_per_seq) == 0

    @pl.when(step == 0)
    def _():
        _stage_expert_weights(wg_hbm, wu_hbm, wd_hbm, wg_ref, wu_ref, wd_ref, st_a, st_d, sems)
        for cp in _bf16_weight_copies(wg_ref, wu_ref, wd_ref, wg_o, wu_o, wd_o, sems):
            cp.start()

    @pl.when(step == pl.num_programs(0) - 1)
    def _():
        for cp in _bf16_weight_copies(wg_ref, wu_ref, wd_ref, wg_o, wu_o, wd_o, sems):
            cp.wait()

    @pl.when(is_first)
    def _():
        kprev_ref[...] = jnp.zeros_like(kprev_ref)
        vprev_ref[...] = jnp.zeros_like(vprev_ref)

    x = x_ref[...]
    u, va, q, k, v = _project(x, vec_ref, w_in_ref, segq_ref, segk_ref)
    kwin_ref[...] = k[tile - WINDOW:tile]
    vwin_ref[...] = v[tile - WINDOW:tile]
    a_out = u * _chunk_mlp(va.astype(BF16), wsp_ref, bsp_ref, tile)
    b_out = _swa_prompt(q * Q_SCALE, k, v, kprev_ref, vprev_ref, bias_ref, sinks_ref, is_first, tile)
    h = _merge(x, a_out, b_out, vec_ref, w_out_ref)
    y_ref[...] = h + _moe_grouped(h, vec_ref, wr_ref, wg_ref, wu_ref, wd_ref, lstrict_ref,
                                  xs_ref, es_ref, ys_ref, tile)


def _sample_kernel(x_ref, ck_ref, cv_ref, vec_ref, w_in_ref, segq_ref, segk_ref, wsp_ref, bsp_ref,
                   bias_c_ref, bias_n_ref, sinks_ref, w_out_ref, wr_ref, lstrict_ref, wg_ref, wu_ref, wd_ref,
                   y_ref, wk_ref, wv_ref, va_ref,
                   hacc_ref, xs_ref, es_ref, ys_ref, *, tile, moe_tile, nb, dec):
    step = pl.program_id(0)
    x = x_ref[...]
    u, va, q, k, v = _project(x, vec_ref, w_in_ref, segq_ref, segk_ref)
    for g in range(A_GROUPS):
        va_ref[pl.ds(g, tile, stride=A_GROUPS), :] = va[:, g * A_CH:(g + 1) * A_CH]
    a_out = u * _chunk_mlp(va.astype(BF16), wsp_ref, bsp_ref, tile)
    b_out = _swa_sample(q * Q_SCALE, k, v, ck_ref, cv_ref, wk_ref, wv_ref,
                        bias_c_ref, bias_n_ref, sinks_ref, nb, dec)
    per_moe = moe_tile // tile
    slot = step % per_moe
    hacc_ref[pl.ds(pl.multiple_of(slot * tile, tile), tile), :] = _merge(x, a_out, b_out, vec_ref, w_out_ref)

    @pl.when(slot == per_moe - 1)
    def _():
        h = hacc_ref[...]
        y_ref[...] = h + _moe_grouped(h, vec_ref, wr_ref, wg_ref, wu_ref, wd_ref, lstrict_ref,
                                      xs_ref, es_ref, ys_ref, moe_tile)


def _const_spec(shape):
    nd = len(shape)
    return pl.BlockSpec(shape, lambda i: (0,) * nd, pipeline_mode=pl.Buffered(1))


def _row_spec(tile, width):
    return pl.BlockSpec((tile, width), lambda i: (i, 0))


def _smem_spec():
    return pl.BlockSpec(memory_space=pltpu.SMEM)


def _layer_weights(l, g_attn_norm, w_in, g_v_a, g_q, g_k, g_out_a, g_out_b, w_out, g_ffn_norm,
                   w_coarse, b_coarse, w_fine, b_fine):
    pad = ROUTER_LANES - N_EXPERT_GROUPS - N_EXPERTS
    wr = jnp.concatenate([w_coarse[l], w_fine[l], jnp.zeros((D_MODEL, pad), F32)], axis=1)
    vec = jnp.concatenate([
        g_attn_norm[l], g_v_a[l].reshape(A_WIDTH), jnp.tile(g_q[l], N_HEADS), jnp.tile(g_k[l], N_KV),
        g_out_a[l], g_out_b[l], g_ffn_norm[l], b_coarse[l], b_fine[l], jnp.zeros((pad,), F32)])
    return dict(
        vec=vec.reshape(1, VEC_LANES),
        w_in=w_in[l].astype(BF16),
        segq=jnp.asarray(_seg_ones(B_WIDTH, HEAD_DIM), BF16),
        segk=jnp.asarray(_seg_ones(KV_WIDTH, HEAD_DIM), BF16),
        w_out=w_out[l].astype(BF16),
        wr=wr.astype(BF16),
    )


def _spatial_tables(ws, bs, period, tile):
    pair = min(PAIR, tile)
    reps = pair // period
    wbd = (jnp.tile(ws[:, :period, :period], (1, reps, reps)) * _causal_block_mask(period, pair)).astype(BF16)
    bsp = jnp.repeat(bs[:, :period].T, A_CH, axis=1)
    return wbd, bsp


_HEAD_NAMES = ("vec", "w_in", "segq", "segk")
_TAIL_NAMES = ("w_out", "wr")


def _lower_triangle(n):
    return jnp.asarray(np.tril(np.ones((n, n), np.float32), -1), BF16)


def _run_prompt(x2d, lw, wbd, bsp, sinks, seq_len, w_gate, w_up, w_down):
    n_tok = x2d.shape[0]
    tile = PROMPT_TILE
    tiles_per_seq = seq_len // tile
    n_seq = n_tok // seq_len
    bias = jnp.asarray(_prompt_bias())
    head = [lw[n] for n in _HEAD_NAMES]
    tail = [lw[n] for n in _TAIL_NAMES]
    tail.append(_lower_triangle(tile))
    experts = [w_gate, w_up, w_down]
    consts = head + [wbd, bsp, bias]
    any_spec = pl.BlockSpec(memory_space=pl.ANY)
    in_specs = ([_row_spec(tile, D_MODEL)] + [_const_spec(a.shape) for a in consts] + [_smem_spec()]
                + [_const_spec(a.shape) for a in tail] + [any_spec] * len(experts))
    win_spec = pl.BlockSpec((WINDOW, KV_WIDTH), lambda i: (i // tiles_per_seq, 0))
    out_shape = (jax.ShapeDtypeStruct((n_tok, D_MODEL), F32),
                 jax.ShapeDtypeStruct((n_seq * WINDOW, KV_WIDTH), F32),
                 jax.ShapeDtypeStruct((n_seq * WINDOW, KV_WIDTH), F32),
                 *[jax.ShapeDtypeStruct(w.shape, BF16) for w in experts])
    out_specs = (_row_spec(tile, D_MODEL), win_spec, win_spec, any_spec, any_spec, any_spec)
    kern = functools.partial(_prompt_kernel, tile=tile, tiles_per_seq=tiles_per_seq)
    return pl.pallas_call(
        kern,
        out_shape=out_shape,
        grid=(n_tok // tile,),
        in_specs=in_specs,
        out_specs=out_specs,
        scratch_shapes=[pltpu.VMEM((N_KV, WINDOW, KV_WIDTH), BF16), pltpu.VMEM((N_KV, WINDOW, KV_WIDTH), BF16),
                        pltpu.VMEM((tile, D_MODEL), BF16), pltpu.VMEM((tile, LANES), F32),
                        pltpu.VMEM((tile, D_MODEL), F32),
                        *[pltpu.VMEM(w.shape, BF16) for w in experts],
                        pltpu.VMEM((2,) + w_gate.shape[1:], F32), pltpu.VMEM((2,) + w_down.shape[1:], F32),
                        pltpu.SemaphoreType.DMA((7,))],
        compiler_params=pltpu.CompilerParams(dimension_semantics=("arbitrary",),
                                             vmem_limit_bytes=VMEM_LIMIT_BYTES),
        name="layer_prompt",
    )(x2d, *consts, sinks, *tail, *experts)


def _run_sample(x2d, ck_t, cv_t, lw, wbd, bsp, sinks, dec, wg_bf, wu_bf, wd_bf):
    n_tok = x2d.shape[0]
    n_seq = ck_t.shape[0]
    tile = SAMPLE_TILE
    moe_tile = PROMPT_TILE
    nb = tile // dec
    bc, bn = _sample_bias(nb, dec)
    head = [lw[n] for n in _HEAD_NAMES]
    tail = [lw[n] for n in _TAIL_NAMES] + [_lower_triangle(moe_tile), wg_bf, wu_bf, wd_bf]
    consts = head + [wbd, bsp, jnp.asarray(bc), jnp.asarray(bn)]
    cache_spec = pl.BlockSpec((nb, KV_WIDTH, WINDOW), lambda i: (i, 0, 0))
    in_specs = ([_row_spec(tile, D_MODEL), cache_spec, cache_spec] + [_const_spec(a.shape) for a in consts]
                + [_smem_spec()] + [_const_spec(a.shape) for a in tail])
    out_shape = (jax.ShapeDtypeStruct((n_tok, D_MODEL), F32),
                 jax.ShapeDtypeStruct((n_seq, KV_WIDTH, WINDOW), F32),
                 jax.ShapeDtypeStruct((n_seq, KV_WIDTH, WINDOW), F32),
                 jax.ShapeDtypeStruct((n_tok * A_GROUPS, A_CH), F32))
    per_moe = moe_tile // tile
    out_specs = (pl.BlockSpec((moe_tile, D_MODEL), lambda i: (i // per_moe, 0)), cache_spec, cache_spec,
                 _row_spec(tile * A_GROUPS, A_CH))
    kern = functools.partial(_sample_kernel, tile=tile, moe_tile=moe_tile, nb=nb, dec=dec)
    return pl.pallas_call(
        kern,
        out_shape=out_shape,
        grid=(n_tok // tile,),
        in_specs=in_specs,
        out_specs=out_specs,
        scratch_shapes=[pltpu.VMEM((moe_tile, D_MODEL), F32), pltpu.VMEM((moe_tile, D_MODEL), BF16),
                        pltpu.VMEM((moe_tile, LANES), F32), pltpu.VMEM((moe_tile, D_MODEL), F32)],
        compiler_params=pltpu.CompilerParams(dimension_semantics=("arbitrary",),
                                             vmem_limit_bytes=VMEM_LIMIT_BYTES),
        name="layer_sample",
    )(x2d, ck_t, cv_t, *consts, sinks, *tail)


def _positions_last(c):
    b, w = c.shape[:2]
    return jnp.transpose(c, (0, 2, 3, 1)).reshape(b, KV_WIDTH, w)


def _positions_first(c_t):
    b, _, w = c_t.shape
    return jnp.transpose(c_t.reshape(b, N_KV, HEAD_DIM, w), (0, 3, 1, 2))


def kernel(x_prompt, x_sample, cache_k, cache_v, g_attn_norm, w_in, g_v_a, w_spatial, b_spatial, g_q, g_k, attn_sinks, g_out_a, g_out_b, w_out, g_ffn_norm, w_coarse, b_coarse, w_fine, b_fine, w_gate, w_up, w_down):
    depth = w_in.shape[0]
    batch, seq, _ = x_prompt.shape
    dbatch, dec, _ = x_sample.shape
    win = cache_k.shape[2]
    assert win == WINDOW and seq % PROMPT_TILE == 0 and (dbatch * dec) % PROMPT_TILE == 0
    assert PAIR % dec == 0 and SAMPLE_TILE % dec == 0 and LANES % dec == 0

    hp = x_prompt.reshape(batch * seq, D_MODEL)
    hs = x_sample.reshape(dbatch * dec, D_MODEL)
    kp_l, vp_l, ks_l, vs_l, cv_l = [], [], [], [], []
    for l in range(depth):
        lw = _layer_weights(l, g_attn_norm, w_in, g_v_a, g_q, g_k, g_out_a, g_out_b, w_out, g_ffn_norm,
                            w_coarse, b_coarse, w_fine, b_fine)
        sinks = attn_sinks[l].astype(F32)
        wbd_p, bsp_p = _spatial_tables(w_spatial[l], b_spatial[l], CHUNK, PROMPT_TILE)
        wbd_s, bsp_s = _spatial_tables(w_spatial[l], b_spatial[l], dec, SAMPLE_TILE)

        hp, kp, vp, wg_bf, wu_bf, wd_bf = _run_prompt(hp, lw, wbd_p, bsp_p, sinks, seq,
                                                      w_gate[l], w_up[l], w_down[l])
        kp_l.append(kp.reshape(batch, WINDOW, N_KV, HEAD_DIM))
        vp_l.append(vp.reshape(batch, WINDOW, N_KV, HEAD_DIM))

        hs, wk, wv, va = _run_sample(hs, _positions_last(cache_k[l]), _positions_last(cache_v[l]),
                                     lw, wbd_s, bsp_s, sinks, dec, wg_bf, wu_bf, wd_bf)
        ks_l.append(_positions_first(wk))
        vs_l.append(_positions_first(wv))
        cv_l.append(va.reshape(dbatch, dec, A_GROUPS, A_CH))

    return (hp.reshape(batch, seq, D_MODEL), hs.reshape(dbatch, dec, D_MODEL),
            jnp.stack(kp_l, axis=0), jnp.stack(vp_l, axis=0),
            jnp.stack(ks_l, axis=0), jnp.stack(vs_l, axis=0), jnp.stack(cv_l, axis=0))
```

```python
import functools

import numpy as np
import jax
import jax.numpy as jnp
from jax import lax
from jax.experimental import pallas as pl
from jax.experimental.pallas import tpu as pltpu

D_MODEL = 1024
CHUNK = 128
A_GROUPS = 4
A_WIDTH = 512
A_CH = 128
N_HEADS = 8
N_KV = 2
Q_PER_KV = 4
HEAD_DIM = 64
B_WIDTH = 512
KV_WIDTH = 128
WINDOW = 128
IN_COLS = 2 * A_WIDTH + B_WIDTH + 2 * KV_WIDTH
N_EXPERT_GROUPS = 4
EXPERTS_PER_GROUP = 4
N_EXPERTS = 16
D_EXPERT = 256
EPS = 1e-6

LANES = 128
PAIR = 2 * CHUNK
ROUTER_LANES = 128
PROMPT_TILE = 512
SAMPLE_TILE = 128
VMEM_LIMIT_BYTES = 60 * 1024 * 1024

F32 = jnp.float32
BF16 = jnp.bfloat16
NEG_INF = float("-inf")
LOG2E = 1.4426950408889634
Q_SCALE = (HEAD_DIM ** -0.5) * LOG2E


def _slopes():
    return np.array([2.0 ** (-8.0 * (h + 1) / N_HEADS) for h in range(N_HEADS)], np.float64)


def _prompt_bias():
    t = np.arange(WINDOW)[:, None]
    s = np.arange(2 * WINDOW)[None, :]
    dist = t + WINDOW - s
    valid = (dist >= 0) & (dist < WINDOW)
    sl = _slopes()
    out = np.full((N_KV, Q_PER_KV * WINDOW, 2 * WINDOW), -np.inf, np.float32)
    for kh in range(N_KV):
        for g in range(Q_PER_KV):
            b = np.where(valid, -sl[kh * Q_PER_KV + g] * LOG2E * dist, -np.inf)
            out[kh, g * WINDOW:(g + 1) * WINDOW] = b
    return out


def _sample_bias(nb, dec):
    sl = _slopes()
    t = np.arange(dec)[:, None]
    j = np.arange(WINDOW)[None, :]
    dist_c = t + WINDOW - j
    valid_c = (dist_c >= 0) & (dist_c < WINDOW)
    bc = np.full((N_HEADS * dec, WINDOW), -np.inf, np.float32)
    tp = np.arange(dec)[None, :]
    dist_n = t - tp
    valid_n = dist_n >= 0
    bn = np.full((nb, N_HEADS * dec, nb * dec), -np.inf, np.float32)
    for h in range(N_HEADS):
        bc[h * dec:(h + 1) * dec] = np.where(valid_c, -sl[h] * LOG2E * dist_c, -np.inf)
        blk = np.where(valid_n, -sl[h] * LOG2E * dist_n, -np.inf)
        for b in range(nb):
            bn[b, h * dec:(h + 1) * dec, b * dec:(b + 1) * dec] = blk
    return bc, bn


def _seg_ones(width, seg):
    i = np.arange(width)
    return (i[:, None] // seg == i[None, :] // seg).astype(np.float32)


def _causal_block_mask(period, size):
    i = np.arange(size)
    same = i[:, None] // period == i[None, :] // period
    return (same & (i[None, :] % period <= i[:, None] % period)).astype(np.float32)


def _dot(a, b):
    return jnp.dot(a, b, preferred_element_type=F32)


def _dot_nt(a, b):
    return lax.dot_general(a, b, (((1,), (1,)), ((), ())), preferred_element_type=F32)


def _rms(x, g):
    ms = jnp.mean(x * x, axis=-1, keepdims=True)
    return (x * lax.rsqrt(ms + EPS)) * g


def _seg_rms(x, ones_bf, g, seg):
    ss = _dot((x * x).astype(BF16), ones_bf)
    return (x * lax.rsqrt(ss * (1.0 / seg) + EPS)) * g


GELU_K0 = -2.0 * 0.7978845608028654 * LOG2E
GELU_K1 = GELU_K0 * 0.044715


def _gelu_tanh(x):
    t = (x * x) * GELU_K1 + GELU_K0
    return x * (1.0 / (1.0 + jnp.exp2(x * t)))


VEC_ATTN = 0
VEC_VA = VEC_ATTN + D_MODEL
VEC_Q = VEC_VA + A_WIDTH
VEC_K = VEC_Q + B_WIDTH
VEC_OUT = VEC_K + KV_WIDTH
VEC_FFN = VEC_OUT + D_MODEL
VEC_ROUTER_BIAS = VEC_FFN + D_MODEL
VEC_LANES = VEC_ROUTER_BIAS + ROUTER_LANES


def _vec(vec_ref, start, width):
    return vec_ref[:, start:start + width]


def _project(x, vec_ref, w_in_ref, segq_ref, segk_ref):
    xn = _rms(x, _vec(vec_ref, VEC_ATTN, D_MODEL)).astype(BF16)
    z = _dot(xn, w_in_ref[...])
    i1, i2, i3, i4 = A_WIDTH, 2 * A_WIDTH, 2 * A_WIDTH + B_WIDTH, 2 * A_WIDTH + B_WIDTH + KV_WIDTH
    u = _gelu_tanh(z[:, :i1])
    va_pre = _gelu_tanh(z[:, i1:i2])
    g_va = _vec(vec_ref, VEC_VA, A_WIDTH)
    va = jnp.concatenate(
        [_rms(va_pre[:, g * A_CH:(g + 1) * A_CH], g_va[:, g * A_CH:(g + 1) * A_CH]) for g in range(A_GROUPS)],
        axis=1)
    q = _seg_rms(z[:, i2:i3], segq_ref[...], _vec(vec_ref, VEC_Q, B_WIDTH), HEAD_DIM)
    k = _seg_rms(z[:, i3:i4], segk_ref[...], _vec(vec_ref, VEC_K, KV_WIDTH), HEAD_DIM)
    v = z[:, i4:]
    return u, va, q, k, v


def _chunk_mlp(va_bf, wsp_ref, bsp_ref, tile):
    pair = wsp_ref.shape[1]
    n_pairs = tile // pair
    outs = []
    for g in range(A_GROUPS):
        cols = [va_bf[p * pair:(p + 1) * pair, g * A_CH:(g + 1) * A_CH] for p in range(n_pairs)]
        rhs = cols[0] if n_pairs == 1 else jnp.concatenate(cols, axis=1)
        o = _dot(wsp_ref[g], rhs)
        rows = [o[:, p * A_CH:(p + 1) * A_CH] for p in range(n_pairs)]
        outs.append(rows[0] if n_pairs == 1 else jnp.concatenate(rows, axis=0))
    mixed = jnp.concatenate(outs, axis=1)
    bias = bsp_ref[...]
    reps = tile // bias.shape[0]
    return mixed + (bias if reps == 1 else jnp.concatenate([bias] * reps, axis=0))


def _lane_lo():
    return lax.broadcasted_iota(jnp.int32, (1, LANES), 1) < HEAD_DIM


def _swap_halves(x):
    return pltpu.roll(x, HEAD_DIM, axis=1)


def _head_lhs(q_groups, q_groups_swapped, head, lo):
    j, half = divmod(head, 2)
    kv = head // Q_PER_KV
    src = q_groups[j] if half == kv else q_groups_swapped[j]
    return jnp.where(lo if kv == 0 else jnp.logical_not(lo), src, 0.0)


def _merge_heads(o_heads, lo):
    groups = []
    for j in range(N_HEADS // 2):
        kv = (2 * j) // Q_PER_KV
        if kv == 0:
            groups.append(jnp.where(lo, o_heads[2 * j], _swap_halves(o_heads[2 * j + 1])))
        else:
            groups.append(jnp.where(lo, _swap_halves(o_heads[2 * j]), o_heads[2 * j + 1]))
    return jnp.concatenate(groups, axis=1)


def _dup_halves(x, lo):
    xs = _swap_halves(x)
    return jnp.where(lo, x, xs).astype(BF16), jnp.where(lo, xs, x).astype(BF16)


def _swa_prompt(q, k, v, kprev_ref, vprev_ref, bias_ref, sinks_ref, is_first, tile):
    lo = _lane_lo()
    hi = jnp.logical_not(lo)
    col = lax.broadcasted_iota(jnp.int32, (WINDOW, 2 * WINDOW), 1)
    first_mask = jnp.where(col < WINDOW, jnp.where(is_first, NEG_INF, 0.0).astype(F32), 0.0)
    kd = _dup_halves(k, lo)
    vd = _dup_halves(v, lo)
    nblk = tile // WINDOW
    rows_out = []
    for i in range(nblk):
        r0, r1 = i * WINDOW, (i + 1) * WINDOW
        qg = [q[r0:r1, j * LANES:(j + 1) * LANES] for j in range(N_HEADS // 2)]
        o_heads = []
        for kh in range(N_KV):
            if i == 0:
                kp, vp = kprev_ref[kh], vprev_ref[kh]
            else:
                kp, vp = kd[kh][r0 - WINDOW:r0], vd[kh][r0 - WINDOW:r0]
            kb = jnp.concatenate([kp, kd[kh][r0:r1]], axis=0)
            vb = jnp.concatenate([vp, vd[kh][r0:r1]], axis=0)
            heads = [kh * Q_PER_KV + g for g in range(Q_PER_KV)]
            lhs = jnp.concatenate(
                [jnp.where(lo if h % 2 == 0 else hi, qg[h // 2], 0.0) for h in heads], axis=0).astype(BF16)
            s = _dot_nt(lhs, kb) + bias_ref[kh]
            ps, linvs = [], []
            for g in range(Q_PER_KV):
                sg = s[g * WINDOW:(g + 1) * WINDOW]
                if i == 0:
                    sg = sg + first_mask
                sink = sinks_ref[heads[g]] * LOG2E
                mg = jnp.maximum(jnp.max(sg, axis=-1, keepdims=True), sink)
                pg = jnp.exp2(sg - mg)
                lg = jnp.sum(pg, axis=-1, keepdims=True) + jnp.exp2(sink - mg)
                ps.append(pg.astype(BF16))
                linvs.append(1.0 / lg)
            o = _dot(jnp.concatenate(ps, axis=0), vb)
            o_heads += [o[g * WINDOW:(g + 1) * WINDOW] * linvs[g] for g in range(Q_PER_KV)]
        rows_out.append(jnp.concatenate(
            [jnp.where(lo, o_heads[2 * j], o_heads[2 * j + 1]) for j in range(N_HEADS // 2)], axis=1))
    for kh in range(N_KV):
        kprev_ref[kh] = kd[kh][tile - WINDOW:tile]
        vprev_ref[kh] = vd[kh][tile - WINDOW:tile]
    return jnp.concatenate(rows_out, axis=0)


def _shift_window(cache_ref, new_t, out_ref, nb, dec):
    lane = lax.broadcasted_iota(jnp.int32, (1, WINDOW), 1)
    keep = lane < WINDOW - dec
    per_tile = LANES // dec
    for b in range(nb):
        src = new_t[:, (b // per_tile) * LANES:(b // per_tile + 1) * LANES]
        new_cols = pltpu.roll(src, (WINDOW - dec - (b % per_tile) * dec) % LANES, axis=1)
        old = pltpu.roll(cache_ref[b], WINDOW - dec, axis=1)
        out_ref[b] = jnp.where(keep, old, new_cols)


def _swa_sample(q, k, v, ck_ref, cv_ref, wk_ref, wv_ref, bias_c_ref, bias_n_ref, sinks_ref, nb, dec):
    lo = _lane_lo()
    qg = [q[:, j * LANES:(j + 1) * LANES] for j in range(N_HEADS // 2)]
    qgs = [_swap_halves(x) for x in qg]
    qb = jnp.concatenate(
        [_head_lhs(qg, qgs, h, lo).reshape(nb, dec, LANES) for h in range(N_HEADS)], axis=1)
    rows = N_HEADS * dec
    qb_bf = qb.astype(BF16)
    k_bf = k.astype(BF16)
    v_bf = v.astype(BF16)
    ck = ck_ref[...].astype(BF16)
    cv = cv_ref[...].astype(BF16)
    sc = jnp.einsum('bqc,bcp->bqp', qb_bf, ck, preferred_element_type=F32) + bias_c_ref[...][None]
    sn = _dot_nt(qb_bf.reshape(nb * rows, LANES), k_bf).reshape(nb, rows, nb * dec) + bias_n_ref[...]
    sink = jnp.concatenate(
        [jnp.full((1, dec, 1), sinks_ref[h] * LOG2E, F32) for h in range(N_HEADS)], axis=1)
    m = jnp.maximum(jnp.max(sc, axis=-1, keepdims=True), jnp.max(sn, axis=-1, keepdims=True))
    m = jnp.maximum(m, sink)
    pc = jnp.exp2(sc - m)
    pn = jnp.exp2(sn - m)
    l = jnp.sum(pc, axis=-1, keepdims=True) + jnp.sum(pn, axis=-1, keepdims=True) + jnp.exp2(sink - m)
    oc = jnp.einsum('bqp,bcp->bqc', pc.astype(BF16), cv, preferred_element_type=F32)
    on = _dot(pn.reshape(nb * rows, nb * dec).astype(BF16), v_bf).reshape(nb, rows, LANES)
    o = (oc + on) * (1.0 / l)
    o_heads = [o[:, h * dec:(h + 1) * dec, :].reshape(nb * dec, LANES) for h in range(N_HEADS)]
    _shift_window(ck_ref, k.T, wk_ref, nb, dec)
    _shift_window(cv_ref, v.T, wv_ref, nb, dec)
    return _merge_heads(o_heads, lo)


def _route(hn_bf, wr_ref, vec_ref):
    logits = _dot(hn_bf, wr_ref[...]) + _vec(vec_ref, VEC_ROUTER_BIAS, ROUTER_LANES)
    lane = lax.broadcasted_iota(jnp.int32, logits.shape, 1)
    big = jnp.int32(ROUTER_LANES)
    lc = jnp.where(lane < N_EXPERT_GROUPS, logits, NEG_INF)
    mx = jnp.max(lc, axis=-1, keepdims=True)
    g_idx = jnp.min(jnp.where(lc == mx, lane, big), axis=-1, keepdims=True)
    p_g = 1.0 / jnp.sum(jnp.exp(lc - mx), axis=-1, keepdims=True)
    e_lane = lane - N_EXPERT_GROUPS
    in_group = (e_lane >= 0) & (e_lane < N_EXPERTS) & ((e_lane >> 2) == g_idx)
    lf = jnp.where(in_group, logits, NEG_INF)
    v1 = jnp.max(lf, axis=-1, keepdims=True)
    i1 = jnp.min(jnp.where(lf == v1, lane, big), axis=-1, keepdims=True)
    lf2 = jnp.where(lane == i1, NEG_INF, lf)
    v2 = jnp.max(lf2, axis=-1, keepdims=True)
    i2 = jnp.min(jnp.where(lf2 == v2, lane, big), axis=-1, keepdims=True)
    e = jnp.exp(v2 - v1)
    w1 = p_g / (1.0 + e)
    w2 = w1 * e
    return lane, g_idx, i1, i2, w1, w2


def _silu(x):
    return x * (1.0 / (1.0 + jnp.exp2(x * (-LOG2E))))


GROUP_LANE = EXPERTS_PER_GROUP
LOW_SHIFT = 8
SORT_BLOCK = 128
GROUP_ROOM = 160
SORTED_ROWS = N_EXPERT_GROUPS * GROUP_ROOM


def _moe_grouped(h, out_ref, vec_ref, wr_ref, wg_ref, wu_ref, wd_ref, lstrict_ref, xs_ref, es_ref, ys_ref, tile):
    hn_bf = _rms(h, _vec(vec_ref, VEC_FFN, D_MODEL)).astype(BF16)
    lane, g_idx, i1, i2, w1, w2 = _route(hn_bf, wr_ref, vec_ref)

    onehot = lane == g_idx
    gmat = jnp.where(onehot, 1.0, 0.0)
    before = _dot(lstrict_ref[...], gmat.astype(BF16))
    rank = jnp.sum(jnp.where(onehot, before, 0.0), axis=-1, keepdims=True)
    counts = jnp.sum(gmat, axis=0, keepdims=True)
    lane1 = lax.broadcasted_iota(jnp.int32, counts.shape, 1)
    sizes = [jnp.sum(jnp.where(lane1 == g, counts, 0.0)) for g in range(N_EXPERT_GROUPS)]
    offs = [jnp.float32(0.0)]
    for g in range(N_EXPERT_GROUPS - 1):
        offs.append(offs[-1] + sizes[g])
    roomy = functools.reduce(jnp.maximum, sizes) <= GROUP_ROOM

    j1 = (i1 - N_EXPERT_GROUPS) & (EXPERTS_PER_GROUP - 1)
    j2 = (i2 - N_EXPERT_GROUPS) & (EXPERTS_PER_GROUP - 1)
    gates = jnp.where(lane == j1, w1, jnp.where(lane == j2, w2, 0.0))
    gates_hi = gates.astype(BF16).astype(F32)
    gates_lo = (gates - gates_hi).astype(BF16).astype(F32)
    side = gates_hi + pltpu.roll(gates_lo, LOW_SHIFT, axis=1) + jnp.where(lane == GROUP_LANE, g_idx.astype(F32), 0.0)
    x_ext = jnp.concatenate([hn_bf, side.astype(BF16)], axis=1)

    def sort_rows(pos, n_rows):
        dest = lax.broadcasted_iota(jnp.int32, (tile, n_rows), 1)
        pt = jnp.where(dest == pos, 1.0, 0.0).astype(BF16)
        srt = lax.dot_general(pt, x_ext, (((0,), (0,)), ((), ())), preferred_element_type=F32)
        xs_ref[pl.ds(0, n_rows), :] = srt[:, :D_MODEL].astype(BF16)
        es = srt[:, D_MODEL:]
        es_ref[pl.ds(0, n_rows), :] = es + pltpu.roll(es, LANES - LOW_SHIFT, axis=1)
        return pt

    def group_pass(rows, g, keep, accumulate):
        xb = xs_ref[rows, :]
        eb = es_ref[rows, :]
        parts = []
        for j in range(EXPERTS_PER_GROUP):
            e = g * EXPERTS_PER_GROUP + j
            gate = _dot(xb, wg_ref[e])
            up = _dot(xb, wu_ref[e])
            c = eb[:, j:j + 1]
            if keep is not None:
                c = c * jnp.where(eb[:, GROUP_LANE:GROUP_LANE + 1] == jnp.asarray(g).astype(F32), keep, 0.0)
            parts.append((_silu(gate) * up * c).astype(BF16))
        wd = wd_ref[pl.ds(g * EXPERTS_PER_GROUP, EXPERTS_PER_GROUP)]
        out = _dot(jnp.concatenate(parts, axis=1), wd.reshape(EXPERTS_PER_GROUP * D_EXPERT, D_MODEL))
        if accumulate:
            ys_ref[rows, :] += out
        else:
            ys_ref[rows, :] = out

    def unsort(pt, n_rows):
        out_ref[...] = h + _dot(pt, ys_ref[pl.ds(0, n_rows), :].astype(BF16))

    @pl.when(roomy)
    def _():
        pt = sort_rows(g_idx * GROUP_ROOM + rank.astype(jnp.int32), SORTED_ROWS)
        for g in range(N_EXPERT_GROUPS):
            group_pass(pl.ds(g * GROUP_ROOM, GROUP_ROOM), g, None, False)
        unsort(pt, SORTED_ROWS)

    @pl.when(jnp.logical_not(roomy))
    def _():
        offv = jnp.zeros_like(rank)
        for g in range(1, N_EXPERT_GROUPS):
            offv = offv + jnp.where(g_idx == g, offs[g], 0.0)
        pt = sort_rows((rank + offv).astype(jnp.int32), tile)

        def group_of(row):
            g = jnp.int32(0)
            for k in range(1, N_EXPERT_GROUPS):
                g = g + (jnp.float32(row) >= offs[k]).astype(jnp.int32)
            return g

        nblk = tile // SORT_BLOCK
        for b in range(nblk):
            group_pass(pl.ds(b * SORT_BLOCK, SORT_BLOCK), group_of(b * SORT_BLOCK), 1.0, False)
        for g in range(1, N_EXPERT_GROUPS):
            start = offs[g].astype(jnp.int32)
            blk = jnp.minimum(lax.div(start, jnp.int32(SORT_BLOCK)), nblk - 1)
            inside = jnp.where(lax.rem(start, jnp.int32(SORT_BLOCK)) != 0, 1.0, 0.0)
            group_pass(pl.ds(pl.multiple_of(blk * SORT_BLOCK, SORT_BLOCK), SORT_BLOCK), g, inside, True)
        unsort(pt, tile)


def _merge(x, a_out, b_out, vec_ref, w_out_ref):
    g_out = _vec(vec_ref, VEC_OUT, D_MODEL)
    mix = jnp.concatenate(
        [_rms(a_out, g_out[:, :A_WIDTH]), _rms(b_out, g_out[:, A_WIDTH:])], axis=1).astype(BF16)
    return x + _dot(mix, w_out_ref[...])


def _stage_expert_weights(wg_hbm, wu_hbm, wd_hbm, wg_s, wu_s, wd_s, st_a, st_d, sems):
    def gate_cp(e):
        return pltpu.make_async_copy(wg_hbm.at[e], st_a.at[0], sems.at[0])

    def up_cp(e):
        return pltpu.make_async_copy(wu_hbm.at[e], st_a.at[1], sems.at[1])

    def down_cp(e, slot):
        return pltpu.make_async_copy(wd_hbm.at[e], st_d.at[slot], sems.at[2 + slot])

    gate_cp(0).start()
    up_cp(0).start()
    down_cp(0, 0).start()

    def pair(p, carry):
        for slot in (0, 1):
            e = 2 * p + slot
            nxt = e + 1
            more = nxt < N_EXPERTS

            @pl.when(more)
            def _():
                down_cp(nxt, 1 - slot).start()

            gate_cp(e).wait()
            wg_s[e] = st_a[0].astype(BF16)

            @pl.when(more)
            def _():
                gate_cp(nxt).start()

            up_cp(e).wait()
            wu_s[e] = st_a[1].astype(BF16)

            @pl.when(more)
            def _():
                up_cp(nxt).start()

            down_cp(e, slot).wait()
            wd_s[e] = st_d[slot].astype(BF16)
        return carry

    lax.fori_loop(0, N_EXPERTS // 2, pair, 0)


def _bf16_weight_copies(wg_s, wu_s, wd_s, wg_o, wu_o, wd_o, sems):
    return (pltpu.make_async_copy(wg_s, wg_o, sems.at[4]),
            pltpu.make_async_copy(wu_s, wu_o, sems.at[5]),
            pltpu.make_async_copy(wd_s, wd_o, sems.at[6]))


def _prompt_kernel(x_ref, vec_ref, w_in_ref, segq_ref, segk_ref, wsp_ref, bsp_ref, bias_ref, sinks_ref,
                   w_out_ref, wr_ref, lstrict_ref, wg_hbm, wu_hbm, wd_hbm,
                   y_ref, kwin_ref, vwin_ref, wg_o, wu_o, wd_o,
                   kprev_ref, vprev_ref, xs_ref, es_ref, ys_ref, wg_ref, wu_ref, wd_ref, st_a, st_d, sems,
                   *, tile, tiles_per_seq):
    step = pl.program_id(0)
    is_first = (step % tiles_per_seq) == 0

    @pl.when(step == 0)
    def _():
        _stage_expert_weights(wg_hbm, wu_hbm, wd_hbm, wg_ref, wu_ref, wd_ref, st_a, st_d, sems)
        for cp in _bf16_weight_copies(wg_ref, wu_ref, wd_ref, wg_o, wu_o, wd_o, sems):
            cp.start()

    @pl.when(step == pl.num_programs(0) - 1)
    def _():
        for cp in _bf16_weight_copies(wg_ref, wu_ref, wd_ref, wg_o, wu_o, wd_o, sems):
            cp.wait()

    @pl.when(is_first)
    def _():
        kprev_ref[...] = jnp.zeros_like(kprev_ref)
        vprev_ref[...] = jnp.zeros_like(vprev_ref)

    x = x_ref[...]
    u, va, q, k, v = _project(x, vec_ref, w_in_ref, segq_ref, segk_ref)
    kwin_ref[...] = k[tile - WINDOW:tile]
    vwin_ref[...] = v[tile - WINDOW:tile]
    a_out = u * _chunk_mlp(va.astype(BF16), wsp_ref, bsp_ref, tile)
    b_out = _swa_prompt(q * Q_SCALE, k, v, kprev_ref, vprev_ref, bias_ref, sinks_ref, is_first, tile)
    h = _merge(x, a_out, b_out, vec_ref, w_out_ref)
    _moe_grouped(h, y_ref, vec_ref, wr_ref, wg_ref, wu_ref, wd_ref, lstrict_ref, xs_ref, es_ref, ys_ref, tile)


def _sample_kernel(x_ref, ck_ref, cv_ref, vec_ref, w_in_ref, segq_ref, segk_ref, wsp_ref, bsp_ref,
                   bias_c_ref, bias_n_ref, sinks_ref, w_out_ref, wr_ref, lstrict_ref, wg_ref, wu_ref, wd_ref,
                   y_ref, wk_ref, wv_ref, va_ref,
                   hacc_ref, xs_ref, es_ref, ys_ref, *, tile, moe_tile, nb, dec):
    step = pl.program_id(0)
    x = x_ref[...]
    u, va, q, k, v = _project(x, vec_ref, w_in_ref, segq_ref, segk_ref)
    for g in range(A_GROUPS):
        va_ref[pl.ds(g, tile, stride=A_GROUPS), :] = va[:, g * A_CH:(g + 1) * A_CH]
    a_out = u * _chunk_mlp(va.astype(BF16), wsp_ref, bsp_ref, tile)
    b_out = _swa_sample(q * Q_SCALE, k, v, ck_ref, cv_ref, wk_ref, wv_ref,
                        bias_c_ref, bias_n_ref, sinks_ref, nb, dec)
    per_moe = moe_tile // tile
    slot = step % per_moe
    hacc_ref[pl.ds(pl.multiple_of(slot * tile, tile), tile), :] = _merge(x, a_out, b_out, vec_ref, w_out_ref)

    @pl.when(slot == per_moe - 1)
    def _():
        _moe_grouped(hacc_ref[...], y_ref, vec_ref, wr_ref, wg_ref, wu_ref, wd_ref, lstrict_ref,
                     xs_ref, es_ref, ys_ref, moe_tile)


def _const_spec(shape):
    nd = len(shape)
    return pl.BlockSpec(shape, lambda i: (0,) * nd, pipeline_mode=pl.Buffered(1))


def _row_spec(tile, width):
    return pl.BlockSpec((tile, width), lambda i: (i, 0))


def _smem_spec():
    return pl.BlockSpec(memory_space=pltpu.SMEM)


def _layer_weights(l, g_attn_norm, w_in, g_v_a, g_q, g_k, g_out_a, g_out_b, w_out, g_ffn_norm,
                   w_coarse, b_coarse, w_fine, b_fine):
    pad = ROUTER_LANES - N_EXPERT_GROUPS - N_EXPERTS
    wr = jnp.concatenate([w_coarse[l], w_fine[l], jnp.zeros((D_MODEL, pad), F32)], axis=1)
    vec = jnp.concatenate([
        g_attn_norm[l], g_v_a[l].reshape(A_WIDTH), jnp.tile(g_q[l], N_HEADS), jnp.tile(g_k[l], N_KV),
        g_out_a[l], g_out_b[l], g_ffn_norm[l], b_coarse[l], b_fine[l], jnp.zeros((pad,), F32)])
    return dict(
        vec=vec.reshape(1, VEC_LANES),
        w_in=w_in[l].astype(BF16),
        segq=jnp.asarray(_seg_ones(B_WIDTH, HEAD_DIM), BF16),
        segk=jnp.asarray(_seg_ones(KV_WIDTH, HEAD_DIM), BF16),
        w_out=w_out[l].astype(BF16),
        wr=wr.astype(BF16),
    )


def _spatial_tables(ws, bs, period, tile):
    pair = min(PAIR, tile)
    reps = pair // period
    wbd = (jnp.tile(ws[:, :period, :period], (1, reps, reps)) * _causal_block_mask(period, pair)).astype(BF16)
    bsp = jnp.repeat(bs[:, :period].T, A_CH, axis=1)
    return wbd, bsp


_HEAD_NAMES = ("vec", "w_in", "segq", "segk")
_TAIL_NAMES = ("w_out", "wr")


def _sorted_scratch(tile):
    assert tile <= SORTED_ROWS and tile % SORT_BLOCK == 0
    return [pltpu.VMEM((SORTED_ROWS, D_MODEL), BF16), pltpu.VMEM((SORTED_ROWS, LANES), F32),
            pltpu.VMEM((SORTED_ROWS, D_MODEL), F32)]


def _lower_triangle(n):
    return jnp.asarray(np.tril(np.ones((n, n), np.float32), -1), BF16)


def _run_prompt(x2d, lw, wbd, bsp, sinks, seq_len, w_gate, w_up, w_down):
    n_tok = x2d.shape[0]
    tile = PROMPT_TILE
    tiles_per_seq = seq_len // tile
    n_seq = n_tok // seq_len
    bias = jnp.asarray(_prompt_bias())
    head = [lw[n] for n in _HEAD_NAMES]
    tail = [lw[n] for n in _TAIL_NAMES]
    tail.append(_lower_triangle(tile))
    experts = [w_gate, w_up, w_down]
    consts = head + [wbd, bsp, bias]
    any_spec = pl.BlockSpec(memory_space=pl.ANY)
    in_specs = ([_row_spec(tile, D_MODEL)] + [_const_spec(a.shape) for a in consts] + [_smem_spec()]
                + [_const_spec(a.shape) for a in tail] + [any_spec] * len(experts))
    win_spec = pl.BlockSpec((WINDOW, KV_WIDTH), lambda i: (i // tiles_per_seq, 0))
    out_shape = (jax.ShapeDtypeStruct((n_tok, D_MODEL), F32),
                 jax.ShapeDtypeStruct((n_seq * WINDOW, KV_WIDTH), F32),
                 jax.ShapeDtypeStruct((n_seq * WINDOW, KV_WIDTH), F32),
                 *[jax.ShapeDtypeStruct(w.shape, BF16) for w in experts])
    out_specs = (_row_spec(tile, D_MODEL), win_spec, win_spec, any_spec, any_spec, any_spec)
    kern = functools.partial(_prompt_kernel, tile=tile, tiles_per_seq=tiles_per_seq)
    return pl.pallas_call(
        kern,
        out_shape=out_shape,
        grid=(n_tok // tile,),
        in_specs=in_specs,
        out_specs=out_specs,
        scratch_shapes=[pltpu.VMEM((N_KV, WINDOW, KV_WIDTH), BF16), pltpu.VMEM((N_KV, WINDOW, KV_WIDTH), BF16),
                        *_sorted_scratch(tile),
                        *[pltpu.VMEM(w.shape, BF16) for w in experts],
                        pltpu.VMEM((2,) + w_gate.shape[1:], F32), pltpu.VMEM((2,) + w_down.shape[1:], F32),
                        pltpu.SemaphoreType.DMA((7,))],
        compiler_params=pltpu.CompilerParams(dimension_semantics=("arbitrary",),
                                             vmem_limit_bytes=VMEM_LIMIT_BYTES),
        name="layer_prompt",
    )(x2d, *consts, sinks, *tail, *experts)


def _run_sample(x2d, ck_t, cv_t, lw, wbd, bsp, sinks, dec, wg_bf, wu_bf, wd_bf):
    n_tok = x2d.shape[0]
    n_seq = ck_t.shape[0]
    tile = SAMPLE_TILE
    moe_tile = PROMPT_TILE
    nb = tile // dec
    bc, bn = _sample_bias(nb, dec)
    head = [lw[n] for n in _HEAD_NAMES]
    tail = [lw[n] for n in _TAIL_NAMES] + [_lower_triangle(moe_tile), wg_bf, wu_bf, wd_bf]
    consts = head + [wbd, bsp, jnp.asarray(bc), jnp.asarray(bn)]
    cache_spec = pl.BlockSpec((nb, KV_WIDTH, WINDOW), lambda i: (i, 0, 0))
    in_specs = ([_row_spec(tile, D_MODEL), cache_spec, cache_spec] + [_const_spec(a.shape) for a in consts]
                + [_smem_spec()] + [_const_spec(a.shape) for a in tail])
    out_shape = (jax.ShapeDtypeStruct((n_tok, D_MODEL), F32),
                 jax.ShapeDtypeStruct((n_seq, KV_WIDTH, WINDOW), F32),
                 jax.ShapeDtypeStruct((n_seq, KV_WIDTH, WINDOW), F32),
                 jax.ShapeDtypeStruct((n_tok * A_GROUPS, A_CH), F32))
    per_moe = moe_tile // tile
    out_specs = (pl.BlockSpec((moe_tile, D_MODEL), lambda i: (i // per_moe, 0)), cache_spec, cache_spec,
                 _row_spec(tile * A_GROUPS, A_CH))
    kern = functools.partial(_sample_kernel, tile=tile, moe_tile=moe_tile, nb=nb, dec=dec)
    return pl.pallas_call(
        kern,
        out_shape=out_shape,
        grid=(n_tok // tile,),
        in_specs=in_specs,
        out_specs=out_specs,
        scratch_shapes=[pltpu.VMEM((moe_tile, D_MODEL), F32), *_sorted_scratch(moe_tile)],
        compiler_params=pltpu.CompilerParams(dimension_semantics=("arbitrary",),
                                             vmem_limit_bytes=VMEM_LIMIT_BYTES),
        name="layer_sample",
    )(x2d, ck_t, cv_t, *consts, sinks, *tail)


def _positions_last(c):
    b, w = c.shape[:2]
    return jnp.transpose(c, (0, 2, 3, 1)).reshape(b, KV_WIDTH, w)


def _positions_first(c_t):
    b, _, w = c_t.shape
    return jnp.transpose(c_t.reshape(b, N_KV, HEAD_DIM, w), (0, 3, 1, 2))


def kernel(x_prompt, x_sample, cache_k, cache_v, g_attn_norm, w_in, g_v_a, w_spatial, b_spatial, g_q, g_k, attn_sinks, g_out_a, g_out_b, w_out, g_ffn_norm, w_coarse, b_coarse, w_fine, b_fine, w_gate, w_up, w_down):
    depth = w_in.shape[0]
    batch, seq, _ = x_prompt.shape
    dbatch, dec, _ = x_sample.shape
    win = cache_k.shape[2]
    assert win == WINDOW and seq % PROMPT_TILE == 0 and (dbatch * dec) % PROMPT_TILE == 0
    assert PAIR % dec == 0 and SAMPLE_TILE % dec == 0 and LANES % dec == 0

    hp = x_prompt.reshape(batch * seq, D_MODEL)
    hs = x_sample.reshape(dbatch * dec, D_MODEL)
    kp_l, vp_l, ks_l, vs_l, cv_l = [], [], [], [], []
    for l in range(depth):
        lw = _layer_weights(l, g_attn_norm, w_in, g_v_a, g_q, g_k, g_out_a, g_out_b, w_out, g_ffn_norm,
                            w_coarse, b_coarse, w_fine, b_fine)
        sinks = attn_sinks[l].astype(F32)
        wbd_p, bsp_p = _spatial_tables(w_spatial[l], b_spatial[l], CHUNK, PROMPT_TILE)
        wbd_s, bsp_s = _spatial_tables(w_spatial[l], b_spatial[l], dec, SAMPLE_TILE)

        hp, kp, vp, wg_bf, wu_bf, wd_bf = _run_prompt(hp, lw, wbd_p, bsp_p, sinks, seq,
                                                      w_gate[l], w_up[l], w_down[l])
        kp_l.append(kp.reshape(batch, WINDOW, N_KV, HEAD_DIM))
        vp_l.append(vp.reshape(batch, WINDOW, N_KV, HEAD_DIM))

        hs, wk, wv, va = _run_sample(hs, _positions_last(cache_k[l]), _positions_last(cache_v[l]),
                                     lw, wbd_s, bsp_s, sinks, dec, wg_bf, wu_bf, wd_bf)
        ks_l.append(_positions_first(wk))
        vs_l.append(_positions_first(wv))
        cv_l.append(va.reshape(dbatch, dec, A_GROUPS, A_CH))

    return (hp.reshape(batch, seq, D_MODEL), hs.reshape(dbatch, dec, D_MODEL),
            jnp.stack(kp_l, axis=0), jnp.stack(vp_l, axis=0),
            jnp.stack(ks_l, axis=0), jnp.stack(vs_l, axis=0), jnp.stack(cv_l, axis=0))
```

```python
import functools

import numpy as np
import jax
import jax.numpy as jnp
from jax import lax
from jax.experimental import pallas as pl
from jax.experimental.pallas import tpu as pltpu

D_MODEL = 1024
CHUNK = 128
A_GROUPS = 4
A_WIDTH = 512
A_CH = 128
N_HEADS = 8
N_KV = 2
Q_PER_KV = 4
HEAD_DIM = 64
B_WIDTH = 512
KV_WIDTH = 128
WINDOW = 128
IN_COLS = 2 * A_WIDTH + B_WIDTH + 2 * KV_WIDTH
N_EXPERT_GROUPS = 4
EXPERTS_PER_GROUP = 4
N_EXPERTS = 16
D_EXPERT = 256
EPS = 1e-6

LANES = 128
PAIR = 2 * CHUNK
ROUTER_LANES = 128
PROMPT_TILE = 512
SAMPLE_TILE = 256
VMEM_LIMIT_BYTES = 60 * 1024 * 1024

F32 = jnp.float32
BF16 = jnp.bfloat16
NEG_INF = float("-inf")
LOG2E = 1.4426950408889634
Q_SCALE = (HEAD_DIM ** -0.5) * LOG2E


def _slopes():
    return np.array([2.0 ** (-8.0 * (h + 1) / N_HEADS) for h in range(N_HEADS)], np.float64)


def _prompt_bias():
    t = np.arange(WINDOW)[:, None]
    s = np.arange(2 * WINDOW)[None, :]
    dist = t + WINDOW - s
    valid = (dist >= 0) & (dist < WINDOW)
    sl = _slopes()
    out = np.full((N_KV, Q_PER_KV * WINDOW, 2 * WINDOW), -np.inf, np.float32)
    for kh in range(N_KV):
        for g in range(Q_PER_KV):
            b = np.where(valid, -sl[kh * Q_PER_KV + g] * LOG2E * dist, -np.inf)
            out[kh, g * WINDOW:(g + 1) * WINDOW] = b
    return out


def _sample_bias(nb, dec):
    sl = _slopes()
    t = np.arange(dec)[:, None]
    j = np.arange(WINDOW)[None, :]
    dist_c = t + WINDOW - j
    valid_c = (dist_c >= 0) & (dist_c < WINDOW)
    bc = np.full((N_HEADS * dec, WINDOW), -np.inf, np.float32)
    tp = np.arange(dec)[None, :]
    dist_n = t - tp
    valid_n = dist_n >= 0
    bn = np.full((nb, N_HEADS * dec, nb * dec), -np.inf, np.float32)
    for h in range(N_HEADS):
        bc[h * dec:(h + 1) * dec] = np.where(valid_c, -sl[h] * LOG2E * dist_c, -np.inf)
        blk = np.where(valid_n, -sl[h] * LOG2E * dist_n, -np.inf)
        for b in range(nb):
            bn[b, h * dec:(h + 1) * dec, b * dec:(b + 1) * dec] = blk
    return bc, bn


def _seg_ones(width, seg):
    i = np.arange(width)
    return (i[:, None] // seg == i[None, :] // seg).astype(np.float32)


def _causal_block_mask(period, size):
    i = np.arange(size)
    same = i[:, None] // period == i[None, :] // period
    return (same & (i[None, :] % period <= i[:, None] % period)).astype(np.float32)


def _dot(a, b):
    return jnp.dot(a, b, preferred_element_type=F32)


def _dot_nt(a, b):
    return lax.dot_general(a, b, (((1,), (1,)), ((), ())), preferred_element_type=F32)


def _rms(x, g):
    ms = jnp.mean(x * x, axis=-1, keepdims=True)
    return (x * lax.rsqrt(ms + EPS)) * g


def _seg_rms(x, ones_bf, g, seg):
    ss = _dot((x * x).astype(BF16), ones_bf)
    return (x * lax.rsqrt(ss * (1.0 / seg) + EPS)) * g


GELU_K0 = -2.0 * 0.7978845608028654 * LOG2E
GELU_K1 = GELU_K0 * 0.044715


def _gelu_tanh(x):
    t = (x * x) * GELU_K1 + GELU_K0
    return x * (1.0 / (1.0 + jnp.exp2(x * t)))


VEC_ATTN = 0
VEC_VA = VEC_ATTN + D_MODEL
VEC_Q = VEC_VA + A_WIDTH
VEC_K = VEC_Q + B_WIDTH
VEC_OUT = VEC_K + KV_WIDTH
VEC_FFN = VEC_OUT + D_MODEL
VEC_ROUTER_BIAS = VEC_FFN + D_MODEL
VEC_LANES = VEC_ROUTER_BIAS + ROUTER_LANES


def _vec(vec_ref, start, width):
    return vec_ref[:, start:start + width]


def _project(x, vec_ref, w_in_ref, segq_ref, segk_ref):
    xn = _rms(x, _vec(vec_ref, VEC_ATTN, D_MODEL)).astype(BF16)
    z = _dot(xn, w_in_ref[...])
    i1, i2, i3, i4 = A_WIDTH, 2 * A_WIDTH, 2 * A_WIDTH + B_WIDTH, 2 * A_WIDTH + B_WIDTH + KV_WIDTH
    u = _gelu_tanh(z[:, :i1])
    va_pre = _gelu_tanh(z[:, i1:i2])
    g_va = _vec(vec_ref, VEC_VA, A_WIDTH)
    va = jnp.concatenate(
        [_rms(va_pre[:, g * A_CH:(g + 1) * A_CH], g_va[:, g * A_CH:(g + 1) * A_CH]) for g in range(A_GROUPS)],
        axis=1)
    q = _seg_rms(z[:, i2:i3], segq_ref[...], _vec(vec_ref, VEC_Q, B_WIDTH), HEAD_DIM)
    k = _seg_rms(z[:, i3:i4], segk_ref[...], _vec(vec_ref, VEC_K, KV_WIDTH), HEAD_DIM)
    v = z[:, i4:]
    return u, va, q, k, v


def _chunk_mlp(va_bf, wsp_ref, bsp_ref, tile):
    pair = wsp_ref.shape[1]
    n_pairs = tile // pair
    outs = []
    for g in range(A_GROUPS):
        cols = [va_bf[p * pair:(p + 1) * pair, g * A_CH:(g + 1) * A_CH] for p in range(n_pairs)]
        rhs = cols[0] if n_pairs == 1 else jnp.concatenate(cols, axis=1)
        o = _dot(wsp_ref[g], rhs)
        rows = [o[:, p * A_CH:(p + 1) * A_CH] for p in range(n_pairs)]
        outs.append(rows[0] if n_pairs == 1 else jnp.concatenate(rows, axis=0))
    mixed = jnp.concatenate(outs, axis=1)
    bias = bsp_ref[...]
    reps = tile // bias.shape[0]
    return mixed + (bias if reps == 1 else jnp.concatenate([bias] * reps, axis=0))


def _lane_lo():
    return lax.broadcasted_iota(jnp.int32, (1, LANES), 1) < HEAD_DIM


def _swap_halves(x):
    return pltpu.roll(x, HEAD_DIM, axis=1)


def _head_lhs(q_groups, q_groups_swapped, head, lo):
    j, half = divmod(head, 2)
    kv = head // Q_PER_KV
    src = q_groups[j] if half == kv else q_groups_swapped[j]
    return jnp.where(lo if kv == 0 else jnp.logical_not(lo), src, 0.0)


def _merge_heads(o_heads, lo):
    groups = []
    for j in range(N_HEADS // 2):
        kv = (2 * j) // Q_PER_KV
        if kv == 0:
            groups.append(jnp.where(lo, o_heads[2 * j], _swap_halves(o_heads[2 * j + 1])))
        else:
            groups.append(jnp.where(lo, _swap_halves(o_heads[2 * j]), o_heads[2 * j + 1]))
    return jnp.concatenate(groups, axis=1)


def _dup_halves(x, lo):
    xs = _swap_halves(x)
    return jnp.where(lo, x, xs).astype(BF16), jnp.where(lo, xs, x).astype(BF16)


def _swa_prompt(q, k, v, kprev_ref, vprev_ref, bias_ref, sinks_ref, is_first, tile):
    lo = _lane_lo()
    hi = jnp.logical_not(lo)
    col = lax.broadcasted_iota(jnp.int32, (WINDOW, 2 * WINDOW), 1)
    first_mask = jnp.where(col < WINDOW, jnp.where(is_first, NEG_INF, 0.0).astype(F32), 0.0)
    kd = _dup_halves(k, lo)
    vd = _dup_halves(v, lo)
    nblk = tile // WINDOW
    rows_out = []
    for i in range(nblk):
        r0, r1 = i * WINDOW, (i + 1) * WINDOW
        qg = [q[r0:r1, j * LANES:(j + 1) * LANES] for j in range(N_HEADS // 2)]
        o_heads = []
        for kh in range(N_KV):
            if i == 0:
                kp, vp = kprev_ref[kh], vprev_ref[kh]
            else:
                kp, vp = kd[kh][r0 - WINDOW:r0], vd[kh][r0 - WINDOW:r0]
            kb = jnp.concatenate([kp, kd[kh][r0:r1]], axis=0)
            vb = jnp.concatenate([vp, vd[kh][r0:r1]], axis=0)
            heads = [kh * Q_PER_KV + g for g in range(Q_PER_KV)]
            lhs = jnp.concatenate(
                [jnp.where(lo if h % 2 == 0 else hi, qg[h // 2], 0.0) for h in heads], axis=0).astype(BF16)
            s = _dot_nt(lhs, kb) + bias_ref[kh]
            ps, linvs = [], []
            for g in range(Q_PER_KV):
                sg = s[g * WINDOW:(g + 1) * WINDOW]
                if i == 0:
                    sg = sg + first_mask
                sink = sinks_ref[heads[g]] * LOG2E
                mg = jnp.maximum(jnp.max(sg, axis=-1, keepdims=True), sink)
                pg = jnp.exp2(sg - mg)
                lg = jnp.sum(pg, axis=-1, keepdims=True) + jnp.exp2(sink - mg)
                ps.append(pg.astype(BF16))
                linvs.append(1.0 / lg)
            o = _dot(jnp.concatenate(ps, axis=0), vb)
            o_heads += [o[g * WINDOW:(g + 1) * WINDOW] * linvs[g] for g in range(Q_PER_KV)]
        rows_out.append(jnp.concatenate(
            [jnp.where(lo, o_heads[2 * j], o_heads[2 * j + 1]) for j in range(N_HEADS // 2)], axis=1))
    for kh in range(N_KV):
        kprev_ref[kh] = kd[kh][tile - WINDOW:tile]
        vprev_ref[kh] = vd[kh][tile - WINDOW:tile]
    return jnp.concatenate(rows_out, axis=0)


def _shift_window(cache_ref, new_t, out_ref, nb, dec):
    lane = lax.broadcasted_iota(jnp.int32, (1, WINDOW), 1)
    keep = lane < WINDOW - dec
    per_tile = LANES // dec
    for b in range(nb):
        src = new_t[:, (b // per_tile) * LANES:(b // per_tile + 1) * LANES]
        new_cols = pltpu.roll(src, (WINDOW - dec - (b % per_tile) * dec) % LANES, axis=1)
        old = pltpu.roll(cache_ref[b], WINDOW - dec, axis=1)
        out_ref[b] = jnp.where(keep, old, new_cols)


def _swa_sample(q, k, v, ck_ref, cv_ref, wk_ref, wv_ref, bias_c_ref, bias_n_ref, sinks_ref, nb, dec):
    lo = _lane_lo()
    qg = [q[:, j * LANES:(j + 1) * LANES] for j in range(N_HEADS // 2)]
    qgs = [_swap_halves(x) for x in qg]
    qb = jnp.concatenate(
        [_head_lhs(qg, qgs, h, lo).reshape(nb, dec, LANES) for h in range(N_HEADS)], axis=1)
    rows = N_HEADS * dec
    qb_bf = qb.astype(BF16)
    k_bf = k.astype(BF16)
    v_bf = v.astype(BF16)
    ck = ck_ref[...].astype(BF16)
    cv = cv_ref[...].astype(BF16)
    sc = jnp.einsum('bqc,bcp->bqp', qb_bf, ck, preferred_element_type=F32) + bias_c_ref[...][None]
    sn = _dot_nt(qb_bf.reshape(nb * rows, LANES), k_bf).reshape(nb, rows, nb * dec) + bias_n_ref[...]
    sink = jnp.concatenate(
        [jnp.full((1, dec, 1), sinks_ref[h] * LOG2E, F32) for h in range(N_HEADS)], axis=1)
    m = jnp.maximum(jnp.max(sc, axis=-1, keepdims=True), jnp.max(sn, axis=-1, keepdims=True))
    m = jnp.maximum(m, sink)
    pc = jnp.exp2(sc - m)
    pn = jnp.exp2(sn - m)
    l = jnp.sum(pc, axis=-1, keepdims=True) + jnp.sum(pn, axis=-1, keepdims=True) + jnp.exp2(sink - m)
    oc = jnp.einsum('bqp,bcp->bqc', pc.astype(BF16), cv, preferred_element_type=F32)
    on = _dot(pn.reshape(nb * rows, nb * dec).astype(BF16), v_bf).reshape(nb, rows, LANES)
    o = (oc + on) * (1.0 / l)
    o_heads = [o[:, h * dec:(h + 1) * dec, :].reshape(nb * dec, LANES) for h in range(N_HEADS)]
    _shift_window(ck_ref, k.T, wk_ref, nb, dec)
    _shift_window(cv_ref, v.T, wv_ref, nb, dec)
    return _merge_heads(o_heads, lo)


def _route(hn_bf, wr_ref, vec_ref):
    logits = _dot(hn_bf, wr_ref[...]) + _vec(vec_ref, VEC_ROUTER_BIAS, ROUTER_LANES)
    lane = lax.broadcasted_iota(jnp.int32, logits.shape, 1)
    big = jnp.int32(ROUTER_LANES)
    lc = jnp.where(lane < N_EXPERT_GROUPS, logits, NEG_INF)
    mx = jnp.max(lc, axis=-1, keepdims=True)
    g_idx = jnp.min(jnp.where(lc == mx, lane, big), axis=-1, keepdims=True)
    p_g = 1.0 / jnp.sum(jnp.exp(lc - mx), axis=-1, keepdims=True)
    e_lane = lane - N_EXPERT_GROUPS
    in_group = (e_lane >= 0) & (e_lane < N_EXPERTS) & ((e_lane >> 2) == g_idx)
    lf = jnp.where(in_group, logits, NEG_INF)
    v1 = jnp.max(lf, axis=-1, keepdims=True)
    i1 = jnp.min(jnp.where(lf == v1, lane, big), axis=-1, keepdims=True)
    lf2 = jnp.where(lane == i1, NEG_INF, lf)
    v2 = jnp.max(lf2, axis=-1, keepdims=True)
    i2 = jnp.min(jnp.where(lf2 == v2, lane, big), axis=-1, keepdims=True)
    e = jnp.exp(v2 - v1)
    w1 = p_g / (1.0 + e)
    w2 = w1 * e
    return lane, g_idx, i1, i2, w1, w2


def _silu(x):
    return x * (1.0 / (1.0 + jnp.exp2(x * (-LOG2E))))


GATE_LANES = EXPERTS_PER_GROUP
GROUP_LANE = EXPERTS_PER_GROUP
LOW_SHIFT = 8
SORT_BLOCK = 128


def _moe_grouped(h, vec_ref, wr_ref, wg_ref, wu_ref, wd_ref, lstrict_ref, xs_ref, es_ref, ys_ref, tile):
    hn_bf = _rms(h, _vec(vec_ref, VEC_FFN, D_MODEL)).astype(BF16)
    lane, g_idx, i1, i2, w1, w2 = _route(hn_bf, wr_ref, vec_ref)

    onehot = lane == g_idx
    gmat = jnp.where(onehot, 1.0, 0.0)
    before = _dot(lstrict_ref[...], gmat.astype(BF16))
    rank = jnp.sum(jnp.where(onehot, before, 0.0), axis=-1, keepdims=True)
    counts = jnp.sum(gmat, axis=0, keepdims=True)
    lane1 = lax.broadcasted_iota(jnp.int32, counts.shape, 1)
    offs = [jnp.float32(0.0)]
    for g in range(N_EXPERT_GROUPS - 1):
        offs.append(offs[-1] + jnp.sum(jnp.where(lane1 == g, counts, 0.0)))
    offv = jnp.zeros_like(rank)
    for g in range(1, N_EXPERT_GROUPS):
        offv = offv + jnp.where(g_idx == g, offs[g], 0.0)
    pos = (rank + offv).astype(jnp.int32)
    dest = lax.broadcasted_iota(jnp.int32, (tile, tile), 1)
    pt = jnp.where(dest == pos, 1.0, 0.0).astype(BF16)

    j1 = (i1 - N_EXPERT_GROUPS) & (EXPERTS_PER_GROUP - 1)
    j2 = (i2 - N_EXPERT_GROUPS) & (EXPERTS_PER_GROUP - 1)
    gates = jnp.where(lane == j1, w1, jnp.where(lane == j2, w2, 0.0))
    gates_hi = gates.astype(BF16).astype(F32)
    gates_lo = (gates - gates_hi).astype(BF16).astype(F32)
    side = gates_hi + pltpu.roll(gates_lo, LOW_SHIFT, axis=1) + jnp.where(lane == GROUP_LANE, g_idx.astype(F32), 0.0)
    x_ext = jnp.concatenate([hn_bf, side.astype(BF16)], axis=1)
    srt = lax.dot_general(pt, x_ext, (((0,), (0,)), ((), ())), preferred_element_type=F32)
    xs_ref[...] = srt[:, :D_MODEL].astype(BF16)
    es = srt[:, D_MODEL:]
    es_ref[...] = es + pltpu.roll(es, LANES - LOW_SHIFT, axis=1)

    def group_of(row):
        g = jnp.int32(0)
        for k in range(1, N_EXPERT_GROUPS):
            g = g + (jnp.float32(row) >= offs[k]).astype(jnp.int32)
        return g

    def group_pass(rows, g, keep, accumulate):
        xb = xs_ref[rows, :]
        eb = es_ref[rows, :]
        mine = jnp.where(eb[:, GROUP_LANE:GROUP_LANE + 1] == jnp.asarray(g).astype(F32), keep, 0.0)
        parts = []
        for j in range(EXPERTS_PER_GROUP):
            e = g * EXPERTS_PER_GROUP + j
            gate = _dot(xb, wg_ref[e])
            up = _dot(xb, wu_ref[e])
            parts.append((_silu(gate) * up * (eb[:, j:j + 1] * mine)).astype(BF16))
        wd = wd_ref[pl.ds(g * EXPERTS_PER_GROUP, EXPERTS_PER_GROUP)]
        out = _dot(jnp.concatenate(parts, axis=1), wd.reshape(EXPERTS_PER_GROUP * D_EXPERT, D_MODEL))
        if accumulate:
            ys_ref[rows, :] += out
        else:
            ys_ref[rows, :] = out

    nblk = tile // SORT_BLOCK
    for b in range(nblk):
        group_pass(pl.ds(b * SORT_BLOCK, SORT_BLOCK), group_of(b * SORT_BLOCK), 1.0, False)
    for g in range(1, N_EXPERT_GROUPS):
        start = offs[g].astype(jnp.int32)
        blk = jnp.minimum(lax.div(start, jnp.int32(SORT_BLOCK)), nblk - 1)
        inside = jnp.where(lax.rem(start, jnp.int32(SORT_BLOCK)) != 0, 1.0, 0.0)
        group_pass(pl.ds(pl.multiple_of(blk * SORT_BLOCK, SORT_BLOCK), SORT_BLOCK), g, inside, True)
    return _dot(pt, ys_ref[...].astype(BF16))


def _merge(x, a_out, b_out, vec_ref, w_out_ref):
    g_out = _vec(vec_ref, VEC_OUT, D_MODEL)
    mix = jnp.concatenate(
        [_rms(a_out, g_out[:, :A_WIDTH]), _rms(b_out, g_out[:, A_WIDTH:])], axis=1).astype(BF16)
    return x + _dot(mix, w_out_ref[...])


def _stage_expert_weights(wg_hbm, wu_hbm, wd_hbm, wg_s, wu_s, wd_s, st_a, st_d, sems):
    def gate_cp(e):
        return pltpu.make_async_copy(wg_hbm.at[e], st_a.at[0], sems.at[0])

    def up_cp(e):
        return pltpu.make_async_copy(wu_hbm.at[e], st_a.at[1], sems.at[1])

    def down_cp(e, slot):
        return pltpu.make_async_copy(wd_hbm.at[e], st_d.at[slot], sems.at[2 + slot])

    gate_cp(0).start()
    up_cp(0).start()
    down_cp(0, 0).start()

    def pair(p, carry):
        for slot in (0, 1):
            e = 2 * p + slot
            nxt = e + 1
            more = nxt < N_EXPERTS

            @pl.when(more)
            def _():
                down_cp(nxt, 1 - slot).start()

            gate_cp(e).wait()
            wg_s[e] = st_a[0].astype(BF16)

            @pl.when(more)
            def _():
                gate_cp(nxt).start()

            up_cp(e).wait()
            wu_s[e] = st_a[1].astype(BF16)

            @pl.when(more)
            def _():
                up_cp(nxt).start()

            down_cp(e, slot).wait()
            wd_s[e] = st_d[slot].astype(BF16)
        return carry

    lax.fori_loop(0, N_EXPERTS // 2, pair, 0)


def _prompt_kernel(x_ref, hs_ref, vec_ref, w_in_ref, segq_ref, segk_ref, wsp_ref, bsp_ref, bias_ref, sinks_ref,
                   w_out_ref, wr_ref, lstrict_ref, wg_hbm, wu_hbm, wd_hbm,
                   y_ref, ys_out_ref, kwin_ref, vwin_ref,
                   kprev_ref, vprev_ref, xs_ref, es_ref, ys_ref, wg_ref, wu_ref, wd_ref, st_a, st_d, sems,
                   *, tile, tiles_per_seq, n_tiles):
    step = pl.program_id(0)

    @pl.when(step == 0)
    def _():
        _stage_expert_weights(wg_hbm, wu_hbm, wd_hbm, wg_ref, wu_ref, wd_ref, st_a, st_d, sems)

    @pl.when(step < n_tiles)
    def _():
        is_first = (step % tiles_per_seq) == 0

        @pl.when(is_first)
        def _():
            kprev_ref[...] = jnp.zeros_like(kprev_ref)
            vprev_ref[...] = jnp.zeros_like(vprev_ref)

        x = x_ref[...]
        u, va, q, k, v = _project(x, vec_ref, w_in_ref, segq_ref, segk_ref)
        kwin_ref[...] = k[tile - WINDOW:tile]
        vwin_ref[...] = v[tile - WINDOW:tile]
        a_out = u * _chunk_mlp(va.astype(BF16), wsp_ref, bsp_ref, tile)
        b_out = _swa_prompt(q * Q_SCALE, k, v, kprev_ref, vprev_ref, bias_ref, sinks_ref, is_first, tile)
        h = _merge(x, a_out, b_out, vec_ref, w_out_ref)
        y_ref[...] = h + _moe_grouped(h, vec_ref, wr_ref, wg_ref, wu_ref, wd_ref, lstrict_ref,
                                      xs_ref, es_ref, ys_ref, tile)

    @pl.when(step >= n_tiles)
    def _():
        h = hs_ref[...]
        ys_out_ref[...] = h + _moe_grouped(h, vec_ref, wr_ref, wg_ref, wu_ref, wd_ref, lstrict_ref,
                                           xs_ref, es_ref, ys_ref, tile)


def _sample_kernel(x_ref, ck_ref, cv_ref, vec_ref, w_in_ref, segq_ref, segk_ref, wsp_ref, bsp_ref,
                   bias_c_ref, bias_n_ref, sinks_ref, w_out_ref,
                   h_ref, wk_ref, wv_ref, va_ref, *, tile, nb, dec):
    x = x_ref[...]
    u, va, q, k, v = _project(x, vec_ref, w_in_ref, segq_ref, segk_ref)
    for g in range(A_GROUPS):
        va_ref[pl.ds(g, tile, stride=A_GROUPS), :] = va[:, g * A_CH:(g + 1) * A_CH]
    a_out = u * _chunk_mlp(va.astype(BF16), wsp_ref, bsp_ref, tile)
    b_out = _swa_sample(q * Q_SCALE, k, v, ck_ref, cv_ref, wk_ref, wv_ref,
                        bias_c_ref, bias_n_ref, sinks_ref, nb, dec)
    h_ref[...] = _merge(x, a_out, b_out, vec_ref, w_out_ref)


def _const_spec(shape):
    nd = len(shape)
    return pl.BlockSpec(shape, lambda i: (0,) * nd, pipeline_mode=pl.Buffered(1))


def _held_row_spec(tile, width, index_map):
    return pl.BlockSpec((tile, width), index_map, pipeline_mode=pl.Buffered(1))


def _row_spec(tile, width):
    return pl.BlockSpec((tile, width), lambda i: (i, 0))


def _smem_spec():
    return pl.BlockSpec(memory_space=pltpu.SMEM)


def _layer_weights(l, g_attn_norm, w_in, g_v_a, g_q, g_k, g_out_a, g_out_b, w_out, g_ffn_norm,
                   w_coarse, b_coarse, w_fine, b_fine):
    pad = ROUTER_LANES - N_EXPERT_GROUPS - N_EXPERTS
    wr = jnp.concatenate([w_coarse[l], w_fine[l], jnp.zeros((D_MODEL, pad), F32)], axis=1)
    vec = jnp.concatenate([
        g_attn_norm[l], g_v_a[l].reshape(A_WIDTH), jnp.tile(g_q[l], N_HEADS), jnp.tile(g_k[l], N_KV),
        g_out_a[l], g_out_b[l], g_ffn_norm[l], b_coarse[l], b_fine[l], jnp.zeros((pad,), F32)])
    return dict(
        vec=vec.reshape(1, VEC_LANES),
        w_in=w_in[l].astype(BF16),
        segq=jnp.asarray(_seg_ones(B_WIDTH, HEAD_DIM), BF16),
        segk=jnp.asarray(_seg_ones(KV_WIDTH, HEAD_DIM), BF16),
        w_out=w_out[l].astype(BF16),
        wr=wr.astype(BF16),
    )


def _spatial_tables(ws, bs, period, tile):
    pair = min(PAIR, tile)
    reps = pair // period
    wbd = (jnp.tile(ws[:, :period, :period], (1, reps, reps)) * _causal_block_mask(period, pair)).astype(BF16)
    bsp = jnp.repeat(bs[:, :period].T, A_CH, axis=1)
    return wbd, bsp


_HEAD_NAMES = ("vec", "w_in", "segq", "segk")
_TAIL_NAMES = ("w_out", "wr")


def _lower_triangle(n):
    return jnp.asarray(np.tril(np.ones((n, n), np.float32), -1), BF16)


def _run_prompt(x2d, hs2d, lw, wbd, bsp, sinks, seq_len, w_gate, w_up, w_down):
    n_tok = x2d.shape[0]
    tile = PROMPT_TILE
    tiles_per_seq = seq_len // tile
    n_seq = n_tok // seq_len
    n_tiles = n_tok // tile
    n_extra = hs2d.shape[0] // tile
    bias = jnp.asarray(_prompt_bias())
    head = [lw[n] for n in _HEAD_NAMES]
    tail = [lw[n] for n in _TAIL_NAMES]
    tail.append(_lower_triangle(tile))
    experts = [w_gate, w_up, w_down]
    consts = head + [wbd, bsp, bias]
    any_spec = pl.BlockSpec(memory_space=pl.ANY)
    last = n_tiles - 1
    x_spec = pl.BlockSpec((tile, D_MODEL), lambda i: (jnp.minimum(i, last), 0))
    extra_map = lambda i: (jnp.maximum(i - n_tiles, 0), 0)
    in_specs = ([x_spec, _held_row_spec(tile, D_MODEL, extra_map)]
                + [_const_spec(a.shape) for a in consts] + [_smem_spec()]
                + [_const_spec(a.shape) for a in tail] + [any_spec] * len(experts))
    win_spec = pl.BlockSpec((WINDOW, KV_WIDTH), lambda i: (jnp.minimum(i, last) // tiles_per_seq, 0))
    out_shape = (jax.ShapeDtypeStruct((n_tok, D_MODEL), F32),
                 jax.ShapeDtypeStruct(hs2d.shape, F32),
                 jax.ShapeDtypeStruct((n_seq * WINDOW, KV_WIDTH), F32),
                 jax.ShapeDtypeStruct((n_seq * WINDOW, KV_WIDTH), F32))
    out_specs = (x_spec, _held_row_spec(tile, D_MODEL, extra_map), win_spec, win_spec)
    kern = functools.partial(_prompt_kernel, tile=tile, tiles_per_seq=tiles_per_seq, n_tiles=n_tiles)
    return pl.pallas_call(
        kern,
        out_shape=out_shape,
        grid=(n_tiles + n_extra,),
        in_specs=in_specs,
        out_specs=out_specs,
        scratch_shapes=[pltpu.VMEM((N_KV, WINDOW, KV_WIDTH), BF16), pltpu.VMEM((N_KV, WINDOW, KV_WIDTH), BF16),
                        pltpu.VMEM((tile, D_MODEL), BF16), pltpu.VMEM((tile, LANES), F32),
                        pltpu.VMEM((tile, D_MODEL), F32),
                        *[pltpu.VMEM(w.shape, BF16) for w in experts],
                        pltpu.VMEM((2,) + w_gate.shape[1:], F32), pltpu.VMEM((2,) + w_down.shape[1:], F32),
                        pltpu.SemaphoreType.DMA((4,))],
        compiler_params=pltpu.CompilerParams(dimension_semantics=("arbitrary",),
                                             vmem_limit_bytes=VMEM_LIMIT_BYTES),
        name="layer_prompt",
    )(x2d, hs2d, *consts, sinks, *tail, *experts)


def _run_sample(x2d, ck_t, cv_t, lw, wbd, bsp, sinks, dec):
    n_tok = x2d.shape[0]
    n_seq = ck_t.shape[0]
    tile = SAMPLE_TILE
    nb = tile // dec
    bc, bn = _sample_bias(nb, dec)
    consts = [lw[n] for n in _HEAD_NAMES] + [wbd, bsp, jnp.asarray(bc), jnp.asarray(bn)]
    tail = [lw["w_out"]]
    cache_spec = pl.BlockSpec((nb, KV_WIDTH, WINDOW), lambda i: (i, 0, 0))
    in_specs = ([_row_spec(tile, D_MODEL), cache_spec, cache_spec] + [_const_spec(a.shape) for a in consts]
                + [_smem_spec()] + [_const_spec(a.shape) for a in tail])
    out_shape = (jax.ShapeDtypeStruct((n_tok, D_MODEL), F32),
                 jax.ShapeDtypeStruct((n_seq, KV_WIDTH, WINDOW), F32),
                 jax.ShapeDtypeStruct((n_seq, KV_WIDTH, WINDOW), F32),
                 jax.ShapeDtypeStruct((n_tok * A_GROUPS, A_CH), F32))
    out_specs = (_row_spec(tile, D_MODEL), cache_spec, cache_spec, _row_spec(tile * A_GROUPS, A_CH))
    kern = functools.partial(_sample_kernel, tile=tile, nb=nb, dec=dec)
    return pl.pallas_call(
        kern,
        out_shape=out_shape,
        grid=(n_tok // tile,),
        in_specs=in_specs,
        out_specs=out_specs,
        compiler_params=pltpu.CompilerParams(dimension_semantics=("arbitrary",),
                                             vmem_limit_bytes=VMEM_LIMIT_BYTES),
        name="layer_sample",
    )(x2d, ck_t, cv_t, *consts, sinks, *tail)


def _positions_last(c):
    b, w = c.shape[:2]
    return jnp.transpose(c, (0, 2, 3, 1)).reshape(b, KV_WIDTH, w)


def _positions_first(c_t):
    b, _, w = c_t.shape
    return jnp.transpose(c_t.reshape(b, N_KV, HEAD_DIM, w), (0, 3, 1, 2))


def kernel(x_prompt, x_sample, cache_k, cache_v, g_attn_norm, w_in, g_v_a, w_spatial, b_spatial, g_q, g_k, attn_sinks, g_out_a, g_out_b, w_out, g_ffn_norm, w_coarse, b_coarse, w_fine, b_fine, w_gate, w_up, w_down):
    depth = w_in.shape[0]
    batch, seq, _ = x_prompt.shape
    dbatch, dec, _ = x_sample.shape
    win = cache_k.shape[2]
    assert win == WINDOW and seq % PROMPT_TILE == 0 and (dbatch * dec) % PROMPT_TILE == 0
    assert PAIR % dec == 0 and SAMPLE_TILE % dec == 0 and LANES % dec == 0

    hp = x_prompt.reshape(batch * seq, D_MODEL)
    hs = x_sample.reshape(dbatch * dec, D_MODEL)
    kp_l, vp_l, ks_l, vs_l, cv_l = [], [], [], [], []
    for l in range(depth):
        lw = _layer_weights(l, g_attn_norm, w_in, g_v_a, g_q, g_k, g_out_a, g_out_b, w_out, g_ffn_norm,
                            w_coarse, b_coarse, w_fine, b_fine)
        sinks = attn_sinks[l].astype(F32)
        wbd_p, bsp_p = _spatial_tables(w_spatial[l], b_spatial[l], CHUNK, PROMPT_TILE)
        wbd_s, bsp_s = _spatial_tables(w_spatial[l], b_spatial[l], dec, SAMPLE_TILE)

        hs, wk, wv, va = _run_sample(hs, _positions_last(cache_k[l]), _positions_last(cache_v[l]),
                                     lw, wbd_s, bsp_s, sinks, dec)
        hp, hs, kp, vp = _run_prompt(hp, hs, lw, wbd_p, bsp_p, sinks, seq, w_gate[l], w_up[l], w_down[l])
        kp_l.append(kp.reshape(batch, WINDOW, N_KV, HEAD_DIM))
        vp_l.append(vp.reshape(batch, WINDOW, N_KV, HEAD_DIM))
        ks_l.append(_positions_first(wk))
        vs_l.append(_positions_first(wv))
        cv_l.append(va.reshape(dbatch, dec, A_GROUPS, A_CH))

    return (hp.reshape(batch, seq, D_MODEL), hs.reshape(dbatch, dec, D_MODEL),
            jnp.stack(kp_l, axis=0), jnp.stack(vp_l, axis=0),
            jnp.stack(ks_l, axis=0), jnp.stack(vs_l, axis=0), jnp.stack(cv_l, axis=0))
```

```python
import functools

import numpy as np
import jax
import jax.numpy as jnp
from jax import lax
from jax.experimental import pallas as pl
from jax.experimental.pallas import tpu as pltpu

D_MODEL = 1024
CHUNK = 128
A_GROUPS = 4
A_WIDTH = 512
A_CH = 128
N_HEADS = 8
N_KV = 2
Q_PER_KV = 4
HEAD_DIM = 64
B_WIDTH = 512
KV_WIDTH = 128
WINDOW = 128
IN_COLS = 2 * A_WIDTH + B_WIDTH + 2 * KV_WIDTH
N_EXPERT_GROUPS = 4
EXPERTS_PER_GROUP = 4
N_EXPERTS = 16
D_EXPERT = 256
EPS = 1e-6

LANES = 128
PAIR = 2 * CHUNK
ROUTER_LANES = 128
PROMPT_TILE = 512
SAMPLE_TILE = 128
VMEM_LIMIT_BYTES = 60 * 1024 * 1024

F32 = jnp.float32
BF16 = jnp.bfloat16
NEG_INF = float("-inf")
LOG2E = 1.4426950408889634
Q_SCALE = (HEAD_DIM ** -0.5) * LOG2E


def _slopes():
    return np.array([2.0 ** (-8.0 * (h + 1) / N_HEADS) for h in range(N_HEADS)], np.float64)


def _prompt_bias():
    t = np.arange(WINDOW)[:, None]
    s = np.arange(2 * WINDOW)[None, :]
    dist = t + WINDOW - s
    valid = (dist >= 0) & (dist < WINDOW)
    sl = _slopes()
    out = np.full((N_KV, Q_PER_KV * WINDOW, 2 * WINDOW), -np.inf, np.float32)
    for kh in range(N_KV):
        for g in range(Q_PER_KV):
            b = np.where(valid, -sl[kh * Q_PER_KV + g] * LOG2E * dist, -np.inf)
            out[kh, g * WINDOW:(g + 1) * WINDOW] = b
    return out


def _sample_bias(nb, dec):
    sl = _slopes()
    t = np.arange(dec)[:, None]
    j = np.arange(WINDOW)[None, :]
    dist_c = t + WINDOW - j
    valid_c = (dist_c >= 0) & (dist_c < WINDOW)
    bc = np.full((N_HEADS * dec, WINDOW), -np.inf, np.float32)
    tp = np.arange(dec)[None, :]
    dist_n = t - tp
    valid_n = dist_n >= 0
    bn = np.full((nb, N_HEADS * dec, nb * dec), -np.inf, np.float32)
    for h in range(N_HEADS):
        bc[h * dec:(h + 1) * dec] = np.where(valid_c, -sl[h] * LOG2E * dist_c, -np.inf)
        blk = np.where(valid_n, -sl[h] * LOG2E * dist_n, -np.inf)
        for b in range(nb):
            bn[b, h * dec:(h + 1) * dec, b * dec:(b + 1) * dec] = blk
    return bc, bn


def _seg_ones(width, seg):
    i = np.arange(width)
    return (i[:, None] // seg == i[None, :] // seg).astype(np.float32)


def _causal_block_mask(period, size):
    i = np.arange(size)
    same = i[:, None] // period == i[None, :] // period
    return (same & (i[None, :] % period <= i[:, None] % period)).astype(np.float32)


def _dot(a, b):
    return jnp.dot(a, b, preferred_element_type=F32)


def _dot_nt(a, b):
    return lax.dot_general(a, b, (((1,), (1,)), ((), ())), preferred_element_type=F32)


def _rms(x, g):
    ms = jnp.mean(x * x, axis=-1, keepdims=True)
    return (x * lax.rsqrt(ms + EPS)) * g


def _seg_rms(x, ones_bf, g, seg):
    ss = _dot((x * x).astype(BF16), ones_bf)
    return (x * lax.rsqrt(ss * (1.0 / seg) + EPS)) * g


GELU_K0 = -2.0 * 0.7978845608028654 * LOG2E
GELU_K1 = GELU_K0 * 0.044715


def _gelu_tanh(x):
    t = (x * x) * GELU_K1 + GELU_K0
    return x * (1.0 / (1.0 + jnp.exp2(x * t)))


VEC_ATTN = 0
VEC_VA = VEC_ATTN + D_MODEL
VEC_Q = VEC_VA + A_WIDTH
VEC_K = VEC_Q + B_WIDTH
VEC_OUT = VEC_K + KV_WIDTH
VEC_FFN = VEC_OUT + D_MODEL
VEC_ROUTER_BIAS = VEC_FFN + D_MODEL
VEC_LANES = VEC_ROUTER_BIAS + ROUTER_LANES


def _vec(vec_ref, start, width):
    return vec_ref[:, start:start + width]


def _project(x, vec_ref, w_in_ref, segq_ref, segk_ref):
    xn = _rms(x, _vec(vec_ref, VEC_ATTN, D_MODEL)).astype(BF16)
    z = _dot(xn, w_in_ref[...])
    i1, i2, i3, i4 = A_WIDTH, 2 * A_WIDTH, 2 * A_WIDTH + B_WIDTH, 2 * A_WIDTH + B_WIDTH + KV_WIDTH
    u = _gelu_tanh(z[:, :i1])
    va_pre = _gelu_tanh(z[:, i1:i2])
    g_va = _vec(vec_ref, VEC_VA, A_WIDTH)
    va = jnp.concatenate(
        [_rms(va_pre[:, g * A_CH:(g + 1) * A_CH], g_va[:, g * A_CH:(g + 1) * A_CH]) for g in range(A_GROUPS)],
        axis=1)
    q = _seg_rms(z[:, i2:i3], segq_ref[...], _vec(vec_ref, VEC_Q, B_WIDTH), HEAD_DIM)
    k = _seg_rms(z[:, i3:i4], segk_ref[...], _vec(vec_ref, VEC_K, KV_WIDTH), HEAD_DIM)
    v = z[:, i4:]
    return u, va, q, k, v


def _chunk_mlp(va_bf, wsp_ref, bsp_ref, tile):
    pair = wsp_ref.shape[1]
    n_pairs = tile // pair
    outs = []
    for g in range(A_GROUPS):
        cols = [va_bf[p * pair:(p + 1) * pair, g * A_CH:(g + 1) * A_CH] for p in range(n_pairs)]
        rhs = cols[0] if n_pairs == 1 else jnp.concatenate(cols, axis=1)
        o = _dot(wsp_ref[g], rhs)
        rows = [o[:, p * A_CH:(p + 1) * A_CH] for p in range(n_pairs)]
        outs.append(rows[0] if n_pairs == 1 else jnp.concatenate(rows, axis=0))
    mixed = jnp.concatenate(outs, axis=1)
    bias = bsp_ref[...]
    reps = tile // bias.shape[0]
    return mixed + (bias if reps == 1 else jnp.concatenate([bias] * reps, axis=0))


def _lane_lo():
    return lax.broadcasted_iota(jnp.int32, (1, LANES), 1) < HEAD_DIM


def _swap_halves(x):
    return pltpu.roll(x, HEAD_DIM, axis=1)


def _head_lhs(q_groups, q_groups_swapped, head, lo):
    j, half = divmod(head, 2)
    kv = head // Q_PER_KV
    src = q_groups[j] if half == kv else q_groups_swapped[j]
    return jnp.where(lo if kv == 0 else jnp.logical_not(lo), src, 0.0)


def _merge_heads(o_heads, lo):
    groups = []
    for j in range(N_HEADS // 2):
        kv = (2 * j) // Q_PER_KV
        if kv == 0:
            groups.append(jnp.where(lo, o_heads[2 * j], _swap_halves(o_heads[2 * j + 1])))
        else:
            groups.append(jnp.where(lo, _swap_halves(o_heads[2 * j]), o_heads[2 * j + 1]))
    return jnp.concatenate(groups, axis=1)


def _dup_halves(x, lo):
    xs = _swap_halves(x)
    return jnp.where(lo, x, xs).astype(BF16), jnp.where(lo, xs, x).astype(BF16)


def _swa_prompt(q, k, v, kprev_ref, vprev_ref, bias_ref, sinks_ref, is_first, tile):
    lo = _lane_lo()
    hi = jnp.logical_not(lo)
    col = lax.broadcasted_iota(jnp.int32, (WINDOW, 2 * WINDOW), 1)
    first_mask = jnp.where(col < WINDOW, jnp.where(is_first, NEG_INF, 0.0).astype(F32), 0.0)
    kd = _dup_halves(k, lo)
    vd = _dup_halves(v, lo)
    nblk = tile // WINDOW
    rows_out = []
    for i in range(nblk):
        r0, r1 = i * WINDOW, (i + 1) * WINDOW
        qg = [q[r0:r1, j * LANES:(j + 1) * LANES] for j in range(N_HEADS // 2)]
        o_heads = []
        for kh in range(N_KV):
            if i == 0:
                kp, vp = kprev_ref[kh], vprev_ref[kh]
            else:
                kp, vp = kd[kh][r0 - WINDOW:r0], vd[kh][r0 - WINDOW:r0]
            kb = jnp.concatenate([kp, kd[kh][r0:r1]], axis=0)
            vb = jnp.concatenate([vp, vd[kh][r0:r1]], axis=0)
            heads = [kh * Q_PER_KV + g for g in range(Q_PER_KV)]
            lhs = jnp.concatenate(
                [jnp.where(lo if h % 2 == 0 else hi, qg[h // 2], 0.0) for h in heads], axis=0).astype(BF16)
            s = _dot_nt(lhs, kb) + bias_ref[kh]
            ps, linvs = [], []
            for g in range(Q_PER_KV):
                sg = s[g * WINDOW:(g + 1) * WINDOW]
                if i == 0:
                    sg = sg + first_mask
                sink = sinks_ref[heads[g]] * LOG2E
                mg = jnp.maximum(jnp.max(sg, axis=-1, keepdims=True), sink)
                pg = jnp.exp2(sg - mg)
                lg = jnp.sum(pg, axis=-1, keepdims=True) + jnp.exp2(sink - mg)
                ps.append(pg.astype(BF16))
                linvs.append(1.0 / lg)
            o = _dot(jnp.concatenate(ps, axis=0), vb)
            o_heads += [o[g * WINDOW:(g + 1) * WINDOW] * linvs[g] for g in range(Q_PER_KV)]
        rows_out.append(jnp.concatenate(
            [jnp.where(lo, o_heads[2 * j], o_heads[2 * j + 1]) for j in range(N_HEADS // 2)], axis=1))
    for kh in range(N_KV):
        kprev_ref[kh] = kd[kh][tile - WINDOW:tile]
        vprev_ref[kh] = vd[kh][tile - WINDOW:tile]
    return jnp.concatenate(rows_out, axis=0)


def _shift_window(cache_ref, new_t, out_ref, nb, dec):
    lane = lax.broadcasted_iota(jnp.int32, (1, WINDOW), 1)
    keep = lane < WINDOW - dec
    per_tile = LANES // dec
    for b in range(nb):
        src = new_t[:, (b // per_tile) * LANES:(b // per_tile + 1) * LANES]
        new_cols = pltpu.roll(src, (WINDOW - dec - (b % per_tile) * dec) % LANES, axis=1)
        old = pltpu.roll(cache_ref[b], WINDOW - dec, axis=1)
        out_ref[b] = jnp.where(keep, old, new_cols)


def _swa_sample(q, k, v, ck_ref, cv_ref, wk_ref, wv_ref, bias_c_ref, bias_n_ref, sinks_ref, nb, dec):
    lo = _lane_lo()
    qg = [q[:, j * LANES:(j + 1) * LANES] for j in range(N_HEADS // 2)]
    qgs = [_swap_halves(x) for x in qg]
    qb = jnp.concatenate(
        [_head_lhs(qg, qgs, h, lo).reshape(nb, dec, LANES) for h in range(N_HEADS)], axis=1)
    rows = N_HEADS * dec
    qb_bf = qb.astype(BF16)
    k_bf = k.astype(BF16)
    v_bf = v.astype(BF16)
    ck = ck_ref[...].astype(BF16)
    cv = cv_ref[...].astype(BF16)
    sc = jnp.einsum('bqc,bcp->bqp', qb_bf, ck, preferred_element_type=F32) + bias_c_ref[...][None]
    sn = _dot_nt(qb_bf.reshape(nb * rows, LANES), k_bf).reshape(nb, rows, nb * dec) + bias_n_ref[...]
    sink = jnp.concatenate(
        [jnp.full((1, dec, 1), sinks_ref[h] * LOG2E, F32) for h in range(N_HEADS)], axis=1)
    m = jnp.maximum(jnp.max(sc, axis=-1, keepdims=True), jnp.max(sn, axis=-1, keepdims=True))
    m = jnp.maximum(m, sink)
    pc = jnp.exp2(sc - m)
    pn = jnp.exp2(sn - m)
    l = jnp.sum(pc, axis=-1, keepdims=True) + jnp.sum(pn, axis=-1, keepdims=True) + jnp.exp2(sink - m)
    oc = jnp.einsum('bqp,bcp->bqc', pc.astype(BF16), cv, preferred_element_type=F32)
    on = _dot(pn.reshape(nb * rows, nb * dec).astype(BF16), v_bf).reshape(nb, rows, LANES)
    o = (oc + on) * (1.0 / l)
    o_heads = [o[:, h * dec:(h + 1) * dec, :].reshape(nb * dec, LANES) for h in range(N_HEADS)]
    _shift_window(ck_ref, k.T, wk_ref, nb, dec)
    _shift_window(cv_ref, v.T, wv_ref, nb, dec)
    return _merge_heads(o_heads, lo)


def _route(hn_bf, wr_ref, vec_ref):
    logits = _dot(hn_bf, wr_ref[...]) + _vec(vec_ref, VEC_ROUTER_BIAS, ROUTER_LANES)
    lane = lax.broadcasted_iota(jnp.int32, logits.shape, 1)
    big = jnp.int32(ROUTER_LANES)
    lc = jnp.where(lane < N_EXPERT_GROUPS, logits, NEG_INF)
    mx = jnp.max(lc, axis=-1, keepdims=True)
    g_idx = jnp.min(jnp.where(lc == mx, lane, big), axis=-1, keepdims=True)
    p_g = 1.0 / jnp.sum(jnp.exp(lc - mx), axis=-1, keepdims=True)
    e_lane = lane - N_EXPERT_GROUPS
    in_group = (e_lane >= 0) & (e_lane < N_EXPERTS) & ((e_lane >> 2) == g_idx)
    lf = jnp.where(in_group, logits, NEG_INF)
    v1 = jnp.max(lf, axis=-1, keepdims=True)
    i1 = jnp.min(jnp.where(lf == v1, lane, big), axis=-1, keepdims=True)
    lf2 = jnp.where(lane == i1, NEG_INF, lf)
    v2 = jnp.max(lf2, axis=-1, keepdims=True)
    i2 = jnp.min(jnp.where(lf2 == v2, lane, big), axis=-1, keepdims=True)
    e = jnp.exp(v2 - v1)
    w1 = p_g / (1.0 + e)
    w2 = w1 * e
    return lane, g_idx, i1, i2, w1, w2


def _silu(x):
    return x * (1.0 / (1.0 + jnp.exp2(x * (-LOG2E))))


GROUP_LANE = EXPERTS_PER_GROUP
LOW_SHIFT = 8
SORT_BLOCK = 128


def _moe_grouped(h, vec_ref, wr_ref, wg_ref, wu_ref, wd_ref, lstrict_ref, xs_ref, es_ref, ys_ref, tile):
    hn_bf = _rms(h, _vec(vec_ref, VEC_FFN, D_MODEL)).astype(BF16)
    lane, g_idx, i1, i2, w1, w2 = _route(hn_bf, wr_ref, vec_ref)

    onehot = lane == g_idx
    gmat = jnp.where(onehot, 1.0, 0.0)
    before = _dot(lstrict_ref[...], gmat.astype(BF16))
    rank = jnp.sum(jnp.where(onehot, before, 0.0), axis=-1, keepdims=True)
    counts = jnp.sum(gmat, axis=0, keepdims=True)
    lane1 = lax.broadcasted_iota(jnp.int32, counts.shape, 1)
    offs = [jnp.float32(0.0)]
    for g in range(N_EXPERT_GROUPS - 1):
        offs.append(offs[-1] + jnp.sum(jnp.where(lane1 == g, counts, 0.0)))
    offv = jnp.zeros_like(rank)
    for g in range(1, N_EXPERT_GROUPS):
        offv = offv + jnp.where(g_idx == g, offs[g], 0.0)
    pos = (rank + offv).astype(jnp.int32)
    dest = lax.broadcasted_iota(jnp.int32, (tile, tile), 1)
    pt = jnp.where(dest == pos, 1.0, 0.0).astype(BF16)

    j1 = (i1 - N_EXPERT_GROUPS) & (EXPERTS_PER_GROUP - 1)
    j2 = (i2 - N_EXPERT_GROUPS) & (EXPERTS_PER_GROUP - 1)
    gates = jnp.where(lane == j1, w1, jnp.where(lane == j2, w2, 0.0))
    gates_hi = gates.astype(BF16).astype(F32)
    gates_lo = (gates - gates_hi).astype(BF16).astype(F32)
    side = gates_hi + pltpu.roll(gates_lo, LOW_SHIFT, axis=1) + jnp.where(lane == GROUP_LANE, g_idx.astype(F32), 0.0)
    x_ext = jnp.concatenate([hn_bf, side.astype(BF16)], axis=1)
    srt = lax.dot_general(pt, x_ext, (((0,), (0,)), ((), ())), preferred_element_type=F32)
    xs_ref[...] = srt[:, :D_MODEL].astype(BF16)
    es = srt[:, D_MODEL:]
    es_ref[...] = es + pltpu.roll(es, LANES - LOW_SHIFT, axis=1)

    def group_of(row):
        g = jnp.int32(0)
        for k in range(1, N_EXPERT_GROUPS):
            g = g + (jnp.float32(row) >= offs[k]).astype(jnp.int32)
        return g

    def group_pass(rows, g, keep, accumulate):
        xb = xs_ref[rows, :]
        eb = es_ref[rows, :]
        mine = jnp.where(eb[:, GROUP_LANE:GROUP_LANE + 1] == jnp.asarray(g).astype(F32), keep, 0.0)
        parts = []
        for j in range(EXPERTS_PER_GROUP):
            e = g * EXPERTS_PER_GROUP + j
            gate = _dot(xb, wg_ref[e])
            up = _dot(xb, wu_ref[e])
            parts.append((_silu(gate) * up * (eb[:, j:j + 1] * mine)).astype(BF16))
        wd = wd_ref[pl.ds(g * EXPERTS_PER_GROUP, EXPERTS_PER_GROUP)]
        out = _dot(jnp.concatenate(parts, axis=1), wd.reshape(EXPERTS_PER_GROUP * D_EXPERT, D_MODEL))
        if accumulate:
            ys_ref[rows, :] += out
        else:
            ys_ref[rows, :] = out

    nblk = tile // SORT_BLOCK
    for b in range(nblk):
        group_pass(pl.ds(b * SORT_BLOCK, SORT_BLOCK), group_of(b * SORT_BLOCK), 1.0, False)
    for g in range(1, N_EXPERT_GROUPS):
        start = offs[g].astype(jnp.int32)
        blk = jnp.minimum(lax.div(start, jnp.int32(SORT_BLOCK)), nblk - 1)
        inside = jnp.where(lax.rem(start, jnp.int32(SORT_BLOCK)) != 0, 1.0, 0.0)
        group_pass(pl.ds(pl.multiple_of(blk * SORT_BLOCK, SORT_BLOCK), SORT_BLOCK), g, inside, True)
    return _dot(pt, ys_ref[...].astype(BF16))


def _merge(x, a_out, b_out, vec_ref, w_out_ref):
    g_out = _vec(vec_ref, VEC_OUT, D_MODEL)
    mix = jnp.concatenate(
        [_rms(a_out, g_out[:, :A_WIDTH]), _rms(b_out, g_out[:, A_WIDTH:])], axis=1).astype(BF16)
    return x + _dot(mix, w_out_ref[...])


def _stage_expert_weights(wg_hbm, wu_hbm, wd_hbm, wg_s, wu_s, wd_s, st_a, st_d, sems):
    def gate_cp(e):
        return pltpu.make_async_copy(wg_hbm.at[e], st_a.at[0], sems.at[0])

    def up_cp(e):
        return pltpu.make_async_copy(wu_hbm.at[e], st_a.at[1], sems.at[1])

    def down_cp(e, slot):
        return pltpu.make_async_copy(wd_hbm.at[e], st_d.at[slot], sems.at[2 + slot])

    gate_cp(0).start()
    up_cp(0).start()
    down_cp(0, 0).start()

    def pair(p, carry):
        for slot in (0, 1):
            e = 2 * p + slot
            nxt = e + 1
            more = nxt < N_EXPERTS

            @pl.when(more)
            def _():
                down_cp(nxt, 1 - slot).start()

            gate_cp(e).wait()
            wg_s[e] = st_a[0].astype(BF16)

            @pl.when(more)
            def _():
                gate_cp(nxt).start()

            up_cp(e).wait()
            wu_s[e] = st_a[1].astype(BF16)

            @pl.when(more)
            def _():
                up_cp(nxt).start()

            down_cp(e, slot).wait()
            wd_s[e] = st_d[slot].astype(BF16)
        return carry

    lax.fori_loop(0, N_EXPERTS // 2, pair, 0)


def _bf16_weight_copies(wg_s, wu_s, wd_s, wg_o, wu_o, wd_o, sems):
    return (pltpu.make_async_copy(wg_s, wg_o, sems.at[4]),
            pltpu.make_async_copy(wu_s, wu_o, sems.at[5]),
            pltpu.make_async_copy(wd_s, wd_o, sems.at[6]))


def _prompt_kernel(x_ref, vec_ref, w_in_ref, segq_ref, segk_ref, wsp_ref, bsp_ref, bias_ref, sinks_ref,
                   w_out_ref, wr_ref, lstrict_ref, wg_hbm, wu_hbm, wd_hbm,
                   y_ref, kwin_ref, vwin_ref, wg_o, wu_o, wd_o,
                   kprev_ref, vprev_ref, xs_ref, es_ref, ys_ref, wg_ref, wu_ref, wd_ref, st_a, st_d, sems,
                   *, tile, tiles_per_seq):
    step = pl.program_id(0)
    is_first = (step % tiles_per_seq) == 0

    @pl.when(step == 0)
    def _():
        _stage_expert_weights(wg_hbm, wu_hbm, wd_hbm, wg_ref, wu_ref, wd_ref, st_a, st_d, sems)
        for cp in _bf16_weight_copies(wg_ref, wu_ref, wd_ref, wg_o, wu_o, wd_o, sems):
            cp.start()

    @pl.when(step == pl.num_programs(0) - 1)
    def _():
        for cp in _bf16_weight_copies(wg_ref, wu_ref, wd_ref, wg_o, wu_o, wd_o, sems):
            cp.wait()

    @pl.when(is_first)
    def _():
        kprev_ref[...] = jnp.zeros_like(kprev_ref)
        vprev_ref[...] = jnp.zeros_like(vprev_ref)

    x = x_ref[...]
    u, va, q, k, v = _project(x, vec_ref, w_in_ref, segq_ref, segk_ref)
    kwin_ref[...] = k[tile - WINDOW:tile]
    vwin_ref[...] = v[tile - WINDOW:tile]
    a_out = u * _chunk_mlp(va.astype(BF16), wsp_ref, bsp_ref, tile)
    b_out = _swa_prompt(q * Q_SCALE, k, v, kprev_ref, vprev_ref, bias_ref, sinks_ref, is_first, tile)
    h = _merge(x, a_out, b_out, vec_ref, w_out_ref)
    y_ref[...] = h + _moe_grouped(h, vec_ref, wr_ref, wg_ref, wu_ref, wd_ref, lstrict_ref,
                                  xs_ref, es_ref, ys_ref, tile)


def _sample_kernel(x_ref, ck_ref, cv_ref, vec_ref, w_in_ref, segq_ref, segk_ref, wsp_ref, bsp_ref,
                   bias_c_ref, bias_n_ref, sinks_ref, w_out_ref, wr_ref, lstrict_ref, wg_ref, wu_ref, wd_ref,
                   y_ref, wk_ref, wv_ref, va_ref,
                   hacc_ref, xs_ref, es_ref, ys_ref, *, tile, moe_tile, nb, dec):
    step = pl.program_id(0)
    x = x_ref[...]
    u, va, q, k, v = _project(x, vec_ref, w_in_ref, segq_ref, segk_ref)
    for g in range(A_GROUPS):
        va_ref[pl.ds(g, tile, stride=A_GROUPS), :] = va[:, g * A_CH:(g + 1) * A_CH]
    a_out = u * _chunk_mlp(va.astype(BF16), wsp_ref, bsp_ref, tile)
    b_out = _swa_sample(q * Q_SCALE, k, v, ck_ref, cv_ref, wk_ref, wv_ref,
                        bias_c_ref, bias_n_ref, sinks_ref, nb, dec)
    per_moe = moe_tile // tile
    slot = step % per_moe
    hacc_ref[pl.ds(pl.multiple_of(slot * tile, tile), tile), :] = _merge(x, a_out, b_out, vec_ref, w_out_ref)

    @pl.when(slot == per_moe - 1)
    def _():
        h = hacc_ref[...]
        y_ref[...] = h + _moe_grouped(h, vec_ref, wr_ref, wg_ref, wu_ref, wd_ref, lstrict_ref,
                                      xs_ref, es_ref, ys_ref, moe_tile)


def _const_spec(shape):
    nd = len(shape)
    return pl.BlockSpec(shape, lambda i: (0,) * nd, pipeline_mode=pl.Buffered(1))


def _row_spec(tile, width):
    return pl.BlockSpec((tile, width), lambda i: (i, 0))


def _smem_spec():
    return pl.BlockSpec(memory_space=pltpu.SMEM)


def _layer_weights(l, g_attn_norm, w_in, g_v_a, g_q, g_k, g_out_a, g_out_b, w_out, g_ffn_norm,
                   w_coarse, b_coarse, w_fine, b_fine):
    pad = ROUTER_LANES - N_EXPERT_GROUPS - N_EXPERTS
    wr = jnp.concatenate([w_coarse[l], w_fine[l], jnp.zeros((D_MODEL, pad), F32)], axis=1)
    vec = jnp.concatenate([
        g_attn_norm[l], g_v_a[l].reshape(A_WIDTH), jnp.tile(g_q[l], N_HEADS), jnp.tile(g_k[l], N_KV),
        g_out_a[l], g_out_b[l], g_ffn_norm[l], b_coarse[l], b_fine[l], jnp.zeros((pad,), F32)])
    return dict(
        vec=vec.reshape(1, VEC_LANES),
        w_in=w_in[l].astype(BF16),
        segq=jnp.asarray(_seg_ones(B_WIDTH, HEAD_DIM), BF16),
        segk=jnp.asarray(_seg_ones(KV_WIDTH, HEAD_DIM), BF16),
        w_out=w_out[l].astype(BF16),
        wr=wr.astype(BF16),
    )


def _spatial_tables(ws, bs, period, tile):
    pair = min(PAIR, tile)
    reps = pair // period
    wbd = (jnp.tile(ws[:, :period, :period], (1, reps, reps)) * _causal_block_mask(period, pair)).astype(BF16)
    bsp = jnp.repeat(bs[:, :period].T, A_CH, axis=1)
    return wbd, bsp


_HEAD_NAMES = ("vec", "w_in", "segq", "segk")
_TAIL_NAMES = ("w_out", "wr")


def _lower_triangle(n):
    return jnp.asarray(np.tril(np.ones((n, n), np.float32), -1), BF16)


def _run_prompt(x2d, lw, wbd, bsp, sinks, seq_len, w_gate, w_up, w_down):
    n_tok = x2d.shape[0]
    tile = PROMPT_TILE
    tiles_per_seq = seq_len // tile
    n_seq = n_tok // seq_len
    bias = jnp.asarray(_prompt_bias())
    head = [lw[n] for n in _HEAD_NAMES]
    tail = [lw[n] for n in _TAIL_NAMES]
    tail.append(_lower_triangle(tile))
    experts = [w_gate, w_up, w_down]
    consts = head + [wbd, bsp, bias]
    any_spec = pl.BlockSpec(memory_space=pl.ANY)
    in_specs = ([_row_spec(tile, D_MODEL)] + [_const_spec(a.shape) for a in consts] + [_smem_spec()]
                + [_const_spec(a.shape) for a in tail] + [any_spec] * len(experts))
    win_spec = pl.BlockSpec((WINDOW, KV_WIDTH), lambda i: (i // tiles_per_seq, 0))
    out_shape = (jax.ShapeDtypeStruct((n_tok, D_MODEL), F32),
                 jax.ShapeDtypeStruct((n_seq * WINDOW, KV_WIDTH), F32),
                 jax.ShapeDtypeStruct((n_seq * WINDOW, KV_WIDTH), F32),
                 *[jax.ShapeDtypeStruct(w.shape, BF16) for w in experts])
    out_specs = (_row_spec(tile, D_MODEL), win_spec, win_spec, any_spec, any_spec, any_spec)
    kern = functools.partial(_prompt_kernel, tile=tile, tiles_per_seq=tiles_per_seq)
    return pl.pallas_call(
        kern,
        out_shape=out_shape,
        grid=(n_tok // tile,),
        in_specs=in_specs,
        out_specs=out_specs,
        scratch_shapes=[pltpu.VMEM((N_KV, WINDOW, KV_WIDTH), BF16), pltpu.VMEM((N_KV, WINDOW, KV_WIDTH), BF16),
                        pltpu.VMEM((tile, D_MODEL), BF16), pltpu.VMEM((tile, LANES), F32),
                        pltpu.VMEM((tile, D_MODEL), F32),
                        *[pltpu.VMEM(w.shape, BF16) for w in experts],
                        pltpu.VMEM((2,) + w_gate.shape[1:], F32), pltpu.VMEM((2,) + w_down.shape[1:], F32),
                        pltpu.SemaphoreType.DMA((7,))],
        compiler_params=pltpu.CompilerParams(dimension_semantics=("arbitrary",),
                                             vmem_limit_bytes=VMEM_LIMIT_BYTES),
        name="layer_prompt",
    )(x2d, *consts, sinks, *tail, *experts)


def _run_sample(x2d, ck_t, cv_t, lw, wbd, bsp, sinks, dec, wg_bf, wu_bf, wd_bf):
    n_tok = x2d.shape[0]
    n_seq = ck_t.shape[0]
    tile = SAMPLE_TILE
    moe_tile = PROMPT_TILE
    nb = tile // dec
    bc, bn = _sample_bias(nb, dec)
    head = [lw[n] for n in _HEAD_NAMES]
    tail = [lw[n] for n in _TAIL_NAMES] + [_lower_triangle(moe_tile), wg_bf, wu_bf, wd_bf]
    consts = head + [wbd, bsp, jnp.asarray(bc), jnp.asarray(bn)]
    cache_spec = pl.BlockSpec((nb, KV_WIDTH, WINDOW), lambda i: (i, 0, 0))
    in_specs = ([_row_spec(tile, D_MODEL), cache_spec, cache_spec] + [_const_spec(a.shape) for a in consts]
                + [_smem_spec()] + [_const_spec(a.shape) for a in tail])
    out_shape = (jax.ShapeDtypeStruct((n_tok, D_MODEL), F32),
                 jax.ShapeDtypeStruct((n_seq, KV_WIDTH, WINDOW), F32),
                 jax.ShapeDtypeStruct((n_seq, KV_WIDTH, WINDOW), F32),
                 jax.ShapeDtypeStruct((n_tok * A_GROUPS, A_CH), F32))
    per_moe = moe_tile // tile
    out_specs = (pl.BlockSpec((moe_tile, D_MODEL), lambda i: (i // per_moe, 0)), cache_spec, cache_spec,
                 _row_spec(tile * A_GROUPS, A_CH))
    kern = functools.partial(_sample_kernel, tile=tile, moe_tile=moe_tile, nb=nb, dec=dec)
    return pl.pallas_call(
        kern,
        out_shape=out_shape,
        grid=(n_tok // tile,),
        in_specs=in_specs,
        out_specs=out_specs,
        scratch_shapes=[pltpu.VMEM((moe_tile, D_MODEL), F32), pltpu.VMEM((moe_tile, D_MODEL), BF16),
                        pltpu.VMEM((moe_tile, LANES), F32), pltpu.VMEM((moe_tile, D_MODEL), F32)],
        compiler_params=pltpu.CompilerParams(dimension_semantics=("arbitrary",),
                                             vmem_limit_bytes=VMEM_LIMIT_BYTES),
        name="layer_sample",
    )(x2d, ck_t, cv_t, *consts, sinks, *tail)


def _positions_last(c):
    b, w = c.shape[:2]
    return jnp.transpose(c, (0, 2, 3, 1)).reshape(b, KV_WIDTH, w)


def _positions_first(c_t):
    b, _, w = c_t.shape
    return jnp.transpose(c_t.reshape(b, N_KV, HEAD_DIM, w), (0, 3, 1, 2))


def kernel(x_prompt, x_sample, cache_k, cache_v, g_attn_norm, w_in, g_v_a, w_spatial, b_spatial, g_q, g_k, attn_sinks, g_out_a, g_out_b, w_out, g_ffn_norm, w_coarse, b_coarse, w_fine, b_fine, w_gate, w_up, w_down):
    depth = w_in.shape[0]
    batch, seq, _ = x_prompt.shape
    dbatch, dec, _ = x_sample.shape
    win = cache_k.shape[2]
    assert win == WINDOW and seq % PROMPT_TILE == 0 and (dbatch * dec) % PROMPT_TILE == 0
    assert PAIR % dec == 0 and SAMPLE_TILE % dec == 0 and LANES % dec == 0

    hp = x_prompt.reshape(batch * seq, D_MODEL)
    hs = x_sample.reshape(dbatch * dec, D_MODEL)
    kp_l, vp_l, ks_l, vs_l, cv_l = [], [], [], [], []
    for l in range(depth):
        lw = _layer_weights(l, g_attn_norm, w_in, g_v_a, g_q, g_k, g_out_a, g_out_b, w_out, g_ffn_norm,
                            w_coarse, b_coarse, w_fine, b_fine)
        sinks = attn_sinks[l].astype(F32)
        wbd_p, bsp_p = _spatial_tables(w_spatial[l], b_spatial[l], CHUNK, PROMPT_TILE)
        wbd_s, bsp_s = _spatial_tables(w_spatial[l], b_spatial[l], dec, SAMPLE_TILE)

        hp, kp, vp, wg_bf, wu_bf, wd_bf = _run_prompt(hp, lw, wbd_p, bsp_p, sinks, seq,
                                                      w_gate[l], w_up[l], w_down[l])
        kp_l.append(kp.reshape(batch, WINDOW, N_KV, HEAD_DIM))
        vp_l.append(vp.reshape(batch, WINDOW, N_KV, HEAD_DIM))

        hs, wk, wv, va = _run_sample(hs, _positions_last(cache_k[l]), _positions_last(cache_v[l]),
                                     lw, wbd_s, bsp_s, sinks, dec, wg_bf, wu_bf, wd_bf)
        ks_l.append(_positions_first(wk))
        vs_l.append(_positions_first(wv))
        cv_l.append(va.reshape(dbatch, dec, A_GROUPS, A_CH))

    return (hp.reshape(batch, seq, D_MODEL), hs.reshape(dbatch, dec, D_MODEL),
            jnp.stack(kp_l, axis=0), jnp.stack(vp_l, axis=0),
            jnp.stack(ks_l, axis=0), jnp.stack(vs_l, axis=0), jnp.stack(cv_l, axis=0))
```

```python
import functools

import numpy as np
import jax
import jax.numpy as jnp
from jax import lax
from jax.experimental import pallas as pl
from jax.experimental.pallas import tpu as pltpu

D_MODEL = 1024
CHUNK = 128
A_GROUPS = 4
A_WIDTH = 512
A_CH = 128
N_HEADS = 8
N_KV = 2
Q_PER_KV = 4
HEAD_DIM = 64
B_WIDTH = 512
KV_WIDTH = 128
WINDOW = 128
IN_COLS = 2 * A_WIDTH + B_WIDTH + 2 * KV_WIDTH
N_EXPERT_GROUPS = 4
EXPERTS_PER_GROUP = 4
N_EXPERTS = 16
D_EXPERT = 256
EPS = 1e-6

LANES = 128
PAIR = CHUNK
ROUTER_LANES = 128
PROMPT_TILE = 512
SAMPLE_TILE = 128
VMEM_LIMIT_BYTES = 60 * 1024 * 1024

F32 = jnp.float32
BF16 = jnp.bfloat16
NEG_INF = float("-inf")
LOG2E = 1.4426950408889634
Q_SCALE = (HEAD_DIM ** -0.5) * LOG2E


def _slopes():
    return np.array([2.0 ** (-8.0 * (h + 1) / N_HEADS) for h in range(N_HEADS)], np.float64)


def _prompt_bias():
    t = np.arange(WINDOW)[:, None]
    s = np.arange(2 * WINDOW)[None, :]
    dist = t + WINDOW - s
    valid = (dist >= 0) & (dist < WINDOW)
    sl = _slopes()
    out = np.full((N_KV, Q_PER_KV * WINDOW, 2 * WINDOW), -np.inf, np.float32)
    for kh in range(N_KV):
        for g in range(Q_PER_KV):
            b = np.where(valid, -sl[kh * Q_PER_KV + g] * LOG2E * dist, -np.inf)
            out[kh, g * WINDOW:(g + 1) * WINDOW] = b
    return out


def _sample_bias(nb, dec):
    sl = _slopes()
    t = np.arange(dec)[:, None]
    j = np.arange(WINDOW)[None, :]
    dist_c = t + WINDOW - j
    valid_c = (dist_c >= 0) & (dist_c < WINDOW)
    bc = np.full((N_HEADS * dec, WINDOW), -np.inf, np.float32)
    tp = np.arange(dec)[None, :]
    dist_n = t - tp
    valid_n = dist_n >= 0
    bn = np.full((nb, N_HEADS * dec, nb * dec), -np.inf, np.float32)
    for h in range(N_HEADS):
        bc[h * dec:(h + 1) * dec] = np.where(valid_c, -sl[h] * LOG2E * dist_c, -np.inf)
        blk = np.where(valid_n, -sl[h] * LOG2E * dist_n, -np.inf)
        for b in range(nb):
            bn[b, h * dec:(h + 1) * dec, b * dec:(b + 1) * dec] = blk
    return bc, bn


def _seg_ones(width, seg):
    i = np.arange(width)
    return (i[:, None] // seg == i[None, :] // seg).astype(np.float32)


def _causal_block_mask(period, size):
    i = np.arange(size)
    same = i[:, None] // period == i[None, :] // period
    return (same & (i[None, :] % period <= i[:, None] % period)).astype(np.float32)


def _dot(a, b):
    return jnp.dot(a, b, preferred_element_type=F32)


def _dot_nt(a, b):
    return lax.dot_general(a, b, (((1,), (1,)), ((), ())), preferred_element_type=F32)


def _rms(x, g):
    ms = jnp.mean(x * x, axis=-1, keepdims=True)
    return (x * lax.rsqrt(ms + EPS)) * g


def _seg_rms(x, ones_bf, g, seg):
    ss = _dot((x * x).astype(BF16), ones_bf)
    return (x * lax.rsqrt(ss * (1.0 / seg) + EPS)) * g


GELU_K0 = -2.0 * 0.7978845608028654 * LOG2E
GELU_K1 = GELU_K0 * 0.044715


def _gelu_tanh(x):
    t = (x * x) * GELU_K1 + GELU_K0
    return x * (1.0 / (1.0 + jnp.exp2(x * t)))


VEC_ATTN = 0
VEC_VA = VEC_ATTN + D_MODEL
VEC_Q = VEC_VA + A_WIDTH
VEC_K = VEC_Q + B_WIDTH
VEC_OUT = VEC_K + KV_WIDTH
VEC_FFN = VEC_OUT + D_MODEL
VEC_ROUTER_BIAS = VEC_FFN + D_MODEL
VEC_LANES = VEC_ROUTER_BIAS + ROUTER_LANES


def _vec(vec_ref, start, width):
    return vec_ref[:, start:start + width]


def _project(x, vec_ref, w_in_ref, segq_ref, segk_ref):
    xn = _rms(x, _vec(vec_ref, VEC_ATTN, D_MODEL)).astype(BF16)
    z = _dot(xn, w_in_ref[...])
    i1, i2, i3, i4 = A_WIDTH, 2 * A_WIDTH, 2 * A_WIDTH + B_WIDTH, 2 * A_WIDTH + B_WIDTH + KV_WIDTH
    u = _gelu_tanh(z[:, :i1])
    va_pre = _gelu_tanh(z[:, i1:i2])
    g_va = _vec(vec_ref, VEC_VA, A_WIDTH)
    va = jnp.concatenate(
        [_rms(va_pre[:, g * A_CH:(g + 1) * A_CH], g_va[:, g * A_CH:(g + 1) * A_CH]) for g in range(A_GROUPS)],
        axis=1)
    q = _seg_rms(z[:, i2:i3], segq_ref[...], _vec(vec_ref, VEC_Q, B_WIDTH), HEAD_DIM)
    k = _seg_rms(z[:, i3:i4], segk_ref[...], _vec(vec_ref, VEC_K, KV_WIDTH), HEAD_DIM)
    v = z[:, i4:]
    return u, va, q, k, v


def _chunk_mlp(va_bf, wsp_ref, bsp_ref, tile):
    pair = wsp_ref.shape[1]
    n_pairs = tile // pair
    outs = []
    for g in range(A_GROUPS):
        cols = [va_bf[p * pair:(p + 1) * pair, g * A_CH:(g + 1) * A_CH] for p in range(n_pairs)]
        rhs = cols[0] if n_pairs == 1 else jnp.concatenate(cols, axis=1)
        o = _dot(wsp_ref[g], rhs)
        rows = [o[:, p * A_CH:(p + 1) * A_CH] for p in range(n_pairs)]
        outs.append(rows[0] if n_pairs == 1 else jnp.concatenate(rows, axis=0))
    mixed = jnp.concatenate(outs, axis=1)
    bias = bsp_ref[...]
    reps = tile // bias.shape[0]
    return mixed + (bias if reps == 1 else jnp.concatenate([bias] * reps, axis=0))


def _lane_lo():
    return lax.broadcasted_iota(jnp.int32, (1, LANES), 1) < HEAD_DIM


def _swap_halves(x):
    return pltpu.roll(x, HEAD_DIM, axis=1)


def _head_lhs(q_groups, q_groups_swapped, head, lo):
    j, half = divmod(head, 2)
    kv = head // Q_PER_KV
    src = q_groups[j] if half == kv else q_groups_swapped[j]
    return jnp.where(lo if kv == 0 else jnp.logical_not(lo), src, 0.0)


def _merge_heads(o_heads, lo):
    groups = []
    for j in range(N_HEADS // 2):
        kv = (2 * j) // Q_PER_KV
        if kv == 0:
            groups.append(jnp.where(lo, o_heads[2 * j], _swap_halves(o_heads[2 * j + 1])))
        else:
            groups.append(jnp.where(lo, _swap_halves(o_heads[2 * j]), o_heads[2 * j + 1]))
    return jnp.concatenate(groups, axis=1)


def _dup_halves(x, lo):
    xs = _swap_halves(x)
    return jnp.where(lo, x, xs).astype(BF16), jnp.where(lo, xs, x).astype(BF16)


def _swa_prompt(q, k, v, kprev_ref, vprev_ref, bias_ref, sinks_ref, is_first, tile):
    lo = _lane_lo()
    hi = jnp.logical_not(lo)
    col = lax.broadcasted_iota(jnp.int32, (WINDOW, 2 * WINDOW), 1)
    first_mask = jnp.where(col < WINDOW, jnp.where(is_first, NEG_INF, 0.0).astype(F32), 0.0)
    kd = _dup_halves(k, lo)
    vd = _dup_halves(v, lo)
    nblk = tile // WINDOW
    rows_out = []
    for i in range(nblk):
        r0, r1 = i * WINDOW, (i + 1) * WINDOW
        qg = [q[r0:r1, j * LANES:(j + 1) * LANES] for j in range(N_HEADS // 2)]
        o_heads = []
        for kh in range(N_KV):
            if i == 0:
                kp, vp = kprev_ref[kh], vprev_ref[kh]
            else:
                kp, vp = kd[kh][r0 - WINDOW:r0], vd[kh][r0 - WINDOW:r0]
            kb = jnp.concatenate([kp, kd[kh][r0:r1]], axis=0)
            vb = jnp.concatenate([vp, vd[kh][r0:r1]], axis=0)
            heads = [kh * Q_PER_KV + g for g in range(Q_PER_KV)]
            lhs = jnp.concatenate(
                [jnp.where(lo if h % 2 == 0 else hi, qg[h // 2], 0.0) for h in heads], axis=0).astype(BF16)
            s = _dot_nt(lhs, kb) + bias_ref[kh]
            ps, linvs = [], []
            for g in range(Q_PER_KV):
                sg = s[g * WINDOW:(g + 1) * WINDOW]
                if i == 0:
                    sg = sg + first_mask
                sink = sinks_ref[heads[g]] * LOG2E
                mg = jnp.maximum(jnp.max(sg, axis=-1, keepdims=True), sink)
                pg = jnp.exp2(sg - mg)
                lg = jnp.sum(pg, axis=-1, keepdims=True) + jnp.exp2(sink - mg)
                ps.append(pg.astype(BF16))
                linvs.append(1.0 / lg)
            o = _dot(jnp.concatenate(ps, axis=0), vb)
            o_heads += [o[g * WINDOW:(g + 1) * WINDOW] * linvs[g] for g in range(Q_PER_KV)]
        rows_out.append(jnp.concatenate(
            [jnp.where(lo, o_heads[2 * j], o_heads[2 * j + 1]) for j in range(N_HEADS // 2)], axis=1))
    for kh in range(N_KV):
        kprev_ref[kh] = kd[kh][tile - WINDOW:tile]
        vprev_ref[kh] = vd[kh][tile - WINDOW:tile]
    return jnp.concatenate(rows_out, axis=0)


def _shift_window(cache_ref, new_t, out_ref, nb, dec):
    lane = lax.broadcasted_iota(jnp.int32, (1, WINDOW), 1)
    keep = lane < WINDOW - dec
    per_tile = LANES // dec
    for b in range(nb):
        src = new_t[:, (b // per_tile) * LANES:(b // per_tile + 1) * LANES]
        new_cols = pltpu.roll(src, (WINDOW - dec - (b % per_tile) * dec) % LANES, axis=1)
        old = pltpu.roll(cache_ref[b], WINDOW - dec, axis=1)
        out_ref[b] = jnp.where(keep, old, new_cols)


def _swa_sample(q, k, v, ck_ref, cv_ref, wk_ref, wv_ref, bias_c_ref, bias_n_ref, sinks_ref, nb, dec):
    lo = _lane_lo()
    qg = [q[:, j * LANES:(j + 1) * LANES] for j in range(N_HEADS // 2)]
    qgs = [_swap_halves(x) for x in qg]
    qb = jnp.concatenate(
        [_head_lhs(qg, qgs, h, lo).reshape(nb, dec, LANES) for h in range(N_HEADS)], axis=1)
    rows = N_HEADS * dec
    qb_bf = qb.astype(BF16)
    k_bf = k.astype(BF16)
    v_bf = v.astype(BF16)
    ck = ck_ref[...].astype(BF16)
    cv = cv_ref[...].astype(BF16)
    sc = jnp.einsum('bqc,bcp->bqp', qb_bf, ck, preferred_element_type=F32) + bias_c_ref[...][None]
    sn = _dot_nt(qb_bf.reshape(nb * rows, LANES), k_bf).reshape(nb, rows, nb * dec) + bias_n_ref[...]
    sink = jnp.concatenate(
        [jnp.full((1, dec, 1), sinks_ref[h] * LOG2E, F32) for h in range(N_HEADS)], axis=1)
    m = jnp.maximum(jnp.max(sc, axis=-1, keepdims=True), jnp.max(sn, axis=-1, keepdims=True))
    m = jnp.maximum(m, sink)
    pc = jnp.exp2(sc - m)
    pn = jnp.exp2(sn - m)
    l = jnp.sum(pc, axis=-1, keepdims=True) + jnp.sum(pn, axis=-1, keepdims=True) + jnp.exp2(sink - m)
    oc = jnp.einsum('bqp,bcp->bqc', pc.astype(BF16), cv, preferred_element_type=F32)
    on = _dot(pn.reshape(nb * rows, nb * dec).astype(BF16), v_bf).reshape(nb, rows, LANES)
    o = (oc + on) * (1.0 / l)
    o_heads = [o[:, h * dec:(h + 1) * dec, :].reshape(nb * dec, LANES) for h in range(N_HEADS)]
    _shift_window(ck_ref, k.T, wk_ref, nb, dec)
    _shift_window(cv_ref, v.T, wv_ref, nb, dec)
    return _merge_heads(o_heads, lo)


def _route(hn_bf, wr_ref, vec_ref):
    logits = _dot(hn_bf, wr_ref[...]) + _vec(vec_ref, VEC_ROUTER_BIAS, ROUTER_LANES)
    lane = lax.broadcasted_iota(jnp.int32, logits.shape, 1)
    big = jnp.int32(ROUTER_LANES)
    lc = jnp.where(lane < N_EXPERT_GROUPS, logits, NEG_INF)
    mx = jnp.max(lc, axis=-1, keepdims=True)
    g_idx = jnp.min(jnp.where(lc == mx, lane, big), axis=-1, keepdims=True)
    p_g = 1.0 / jnp.sum(jnp.exp(lc - mx), axis=-1, keepdims=True)
    e_lane = lane - N_EXPERT_GROUPS
    in_group = (e_lane >= 0) & (e_lane < N_EXPERTS) & ((e_lane >> 2) == g_idx)
    lf = jnp.where(in_group, logits, NEG_INF)
    v1 = jnp.max(lf, axis=-1, keepdims=True)
    i1 = jnp.min(jnp.where(lf == v1, lane, big), axis=-1, keepdims=True)
    lf2 = jnp.where(lane == i1, NEG_INF, lf)
    v2 = jnp.max(lf2, axis=-1, keepdims=True)
    i2 = jnp.min(jnp.where(lf2 == v2, lane, big), axis=-1, keepdims=True)
    e = jnp.exp(v2 - v1)
    w1 = p_g / (1.0 + e)
    w2 = w1 * e
    return lane, g_idx, i1, i2, w1, w2


def _silu(x):
    return x * (1.0 / (1.0 + jnp.exp2(x * (-LOG2E))))


def _moe_dense(h, vec_ref, wr_ref, wg_ref, wu_ref, wd_ref, hcat_ref):
    hn_bf = _rms(h, _vec(vec_ref, VEC_FFN, D_MODEL)).astype(BF16)
    lane, _, i1, i2, w1, w2 = _route(hn_bf, wr_ref, vec_ref)
    combine = jnp.where(lane == i1, w1, jnp.where(lane == i2, w2, 0.0))
    for e in range(N_EXPERTS):
        gate = _dot(hn_bf, wg_ref[e])
        up = _dot(hn_bf, wu_ref[e])
        c = combine[:, N_EXPERT_GROUPS + e:N_EXPERT_GROUPS + e + 1]
        hcat_ref[:, e * D_EXPERT:(e + 1) * D_EXPERT] = (_silu(gate) * up * c).astype(BF16)
    return _dot(hcat_ref[...], wd_ref[...])


GROUP_LANE = EXPERTS_PER_GROUP
LOW_SHIFT = 8
SORT_BLOCK = 128


def _moe_grouped(h, vec_ref, wr_ref, wg_ref, wu_ref, wd_ref, lstrict_ref, xs_ref, es_ref, ys_ref, tile):
    hn_bf = _rms(h, _vec(vec_ref, VEC_FFN, D_MODEL)).astype(BF16)
    lane, g_idx, i1, i2, w1, w2 = _route(hn_bf, wr_ref, vec_ref)

    onehot = lane == g_idx
    gmat = jnp.where(onehot, 1.0, 0.0)
    before = _dot(lstrict_ref[...], gmat.astype(BF16))
    rank = jnp.sum(jnp.where(onehot, before, 0.0), axis=-1, keepdims=True)
    counts = jnp.sum(gmat, axis=0, keepdims=True)
    lane1 = lax.broadcasted_iota(jnp.int32, counts.shape, 1)
    offs = [jnp.float32(0.0)]
    for g in range(N_EXPERT_GROUPS - 1):
        offs.append(offs[-1] + jnp.sum(jnp.where(lane1 == g, counts, 0.0)))
    offv = jnp.zeros_like(rank)
    for g in range(1, N_EXPERT_GROUPS):
        offv = offv + jnp.where(g_idx == g, offs[g], 0.0)
    pos = (rank + offv).astype(jnp.int32)
    dest = lax.broadcasted_iota(jnp.int32, (tile, tile), 1)
    pt = jnp.where(dest == pos, 1.0, 0.0).astype(BF16)

    j1 = (i1 - N_EXPERT_GROUPS) & (EXPERTS_PER_GROUP - 1)
    j2 = (i2 - N_EXPERT_GROUPS) & (EXPERTS_PER_GROUP - 1)
    gates = jnp.where(lane == j1, w1, jnp.where(lane == j2, w2, 0.0))
    gates_hi = gates.astype(BF16).astype(F32)
    gates_lo = (gates - gates_hi).astype(BF16).astype(F32)
    side = gates_hi + pltpu.roll(gates_lo, LOW_SHIFT, axis=1) + jnp.where(lane == GROUP_LANE, g_idx.astype(F32), 0.0)
    x_ext = jnp.concatenate([hn_bf, side.astype(BF16)], axis=1)
    srt = lax.dot_general(pt, x_ext, (((0,), (0,)), ((), ())), preferred_element_type=F32)
    xs_ref[...] = srt[:, :D_MODEL].astype(BF16)
    es = srt[:, D_MODEL:]
    es_ref[...] = es + pltpu.roll(es, LANES - LOW_SHIFT, axis=1)

    def group_of(row):
        g = jnp.int32(0)
        for k in range(1, N_EXPERT_GROUPS):
            g = g + (jnp.float32(row) >= offs[k]).astype(jnp.int32)
        return g

    def group_pass(rows, g, keep, accumulate):
        xb = xs_ref[rows, :]
        eb = es_ref[rows, :]
        mine = jnp.where(eb[:, GROUP_LANE:GROUP_LANE + 1] == jnp.asarray(g).astype(F32), keep, 0.0)
        parts = []
        for j in range(EXPERTS_PER_GROUP):
            e = g * EXPERTS_PER_GROUP + j
            gate = _dot(xb, wg_ref[e])
            up = _dot(xb, wu_ref[e])
            parts.append((_silu(gate) * up * (eb[:, j:j + 1] * mine)).astype(BF16))
        wd = wd_ref[pl.ds(g * EXPERTS_PER_GROUP, EXPERTS_PER_GROUP)]
        out = _dot(jnp.concatenate(parts, axis=1), wd.reshape(EXPERTS_PER_GROUP * D_EXPERT, D_MODEL))
        if accumulate:
            ys_ref[rows, :] += out
        else:
            ys_ref[rows, :] = out

    nblk = tile // SORT_BLOCK
    for b in range(nblk):
        group_pass(pl.ds(b * SORT_BLOCK, SORT_BLOCK), group_of(b * SORT_BLOCK), 1.0, False)
    for g in range(1, N_EXPERT_GROUPS):
        start = offs[g].astype(jnp.int32)
        blk = jnp.minimum(lax.div(start, jnp.int32(SORT_BLOCK)), nblk - 1)
        inside = jnp.where(lax.rem(start, jnp.int32(SORT_BLOCK)) != 0, 1.0, 0.0)
        group_pass(pl.ds(pl.multiple_of(blk * SORT_BLOCK, SORT_BLOCK), SORT_BLOCK), g, inside, True)
    return _dot(pt, ys_ref[...].astype(BF16))


def _merge(x, a_out, b_out, vec_ref, w_out_ref):
    g_out = _vec(vec_ref, VEC_OUT, D_MODEL)
    mix = jnp.concatenate(
        [_rms(a_out, g_out[:, :A_WIDTH]), _rms(b_out, g_out[:, A_WIDTH:])], axis=1).astype(BF16)
    return x + _dot(mix, w_out_ref[...])


def _stage_expert_weights(wg_hbm, wu_hbm, wd_hbm, wg_s, wu_s, wd_s, st_a, st_d, sems):
    def gate_cp(e):
        return pltpu.make_async_copy(wg_hbm.at[e], st_a.at[0], sems.at[0])

    def up_cp(e):
        return pltpu.make_async_copy(wu_hbm.at[e], st_a.at[1], sems.at[1])

    def down_cp(e, slot):
        return pltpu.make_async_copy(wd_hbm.at[e], st_d.at[slot], sems.at[2 + slot])

    gate_cp(0).start()
    up_cp(0).start()
    down_cp(0, 0).start()

    def pair(p, carry):
        for slot in (0, 1):
            e = 2 * p + slot
            nxt = e + 1
            more = nxt < N_EXPERTS

            @pl.when(more)
            def _():
                down_cp(nxt, 1 - slot).start()

            gate_cp(e).wait()
            wg_s[e] = st_a[0].astype(BF16)

            @pl.when(more)
            def _():
                gate_cp(nxt).start()

            up_cp(e).wait()
            wu_s[e] = st_a[1].astype(BF16)

            @pl.when(more)
            def _():
                up_cp(nxt).start()

            down_cp(e, slot).wait()
            wd_s[e] = st_d[slot].astype(BF16)
        return carry

    lax.fori_loop(0, N_EXPERTS // 2, pair, 0)


def _bf16_weight_copies(wg_s, wu_s, wd_s, wg_o, wu_o, wd_o, sems):
    return (pltpu.make_async_copy(wg_s, wg_o, sems.at[4]),
            pltpu.make_async_copy(wu_s, wu_o, sems.at[5]),
            pltpu.make_async_copy(wd_s, wd_o, sems.at[6]))


def _prompt_kernel(x_ref, vec_ref, w_in_ref, segq_ref, segk_ref, wsp_ref, bsp_ref, bias_ref, sinks_ref,
                   w_out_ref, wr_ref, lstrict_ref, wg_hbm, wu_hbm, wd_hbm,
                   y_ref, kwin_ref, vwin_ref, wg_o, wu_o, wd_o,
                   kprev_ref, vprev_ref, xs_ref, es_ref, ys_ref, wg_ref, wu_ref, wd_ref, st_a, st_d, sems,
                   *, tile, tiles_per_seq):
    step = pl.program_id(0)
    is_first = (step % tiles_per_seq) == 0

    @pl.when(step == 0)
    def _():
        _stage_expert_weights(wg_hbm, wu_hbm, wd_hbm, wg_ref, wu_ref, wd_ref, st_a, st_d, sems)
        for cp in _bf16_weight_copies(wg_ref, wu_ref, wd_ref, wg_o, wu_o, wd_o, sems):
            cp.start()

    @pl.when(step == pl.num_programs(0) - 1)
    def _():
        for cp in _bf16_weight_copies(wg_ref, wu_ref, wd_ref, wg_o, wu_o, wd_o, sems):
            cp.wait()

    @pl.when(is_first)
    def _():
        kprev_ref[...] = jnp.zeros_like(kprev_ref)
        vprev_ref[...] = jnp.zeros_like(vprev_ref)

    x = x_ref[...]
    u, va, q, k, v = _project(x, vec_ref, w_in_ref, segq_ref, segk_ref)
    kwin_ref[...] = k[tile - WINDOW:tile].T
    vwin_ref[...] = v[tile - WINDOW:tile].T
    a_out = u * _chunk_mlp(va.astype(BF16), wsp_ref, bsp_ref, tile)
    b_out = _swa_prompt(q * Q_SCALE, k, v, kprev_ref, vprev_ref, bias_ref, sinks_ref, is_first, tile)
    h = _merge(x, a_out, b_out, vec_ref, w_out_ref)
    y_ref[...] = h + _moe_grouped(h, vec_ref, wr_ref, wg_ref, wu_ref, wd_ref, lstrict_ref,
                                  xs_ref, es_ref, ys_ref, tile)


def _sample_kernel(x_ref, ck_ref, cv_ref, vec_ref, w_in_ref, segq_ref, segk_ref, wsp_ref, bsp_ref,
                   bias_c_ref, bias_n_ref, sinks_ref, w_out_ref, wr_ref, wg_ref, wu_ref, wd_ref,
                   y_ref, wk_ref, wv_ref, va_ref, hcat_ref, *, tile, nb, dec):
    x = x_ref[...]
    u, va, q, k, v = _project(x, vec_ref, w_in_ref, segq_ref, segk_ref)
    for g in range(A_GROUPS):
        va_ref[pl.ds(g, tile, stride=A_GROUPS), :] = va[:, g * A_CH:(g + 1) * A_CH]
    a_out = u * _chunk_mlp(va.astype(BF16), wsp_ref, bsp_ref, tile)
    b_out = _swa_sample(q * Q_SCALE, k, v, ck_ref, cv_ref, wk_ref, wv_ref,
                        bias_c_ref, bias_n_ref, sinks_ref, nb, dec)
    h = _merge(x, a_out, b_out, vec_ref, w_out_ref)
    y_ref[...] = h + _moe_dense(h, vec_ref, wr_ref, wg_ref, wu_ref, wd_ref, hcat_ref)


def _const_spec(shape):
    nd = len(shape)
    return pl.BlockSpec(shape, lambda i: (0,) * nd, pipeline_mode=pl.Buffered(1))


def _row_spec(tile, width):
    return pl.BlockSpec((tile, width), lambda i: (i, 0))


def _smem_spec():
    return pl.BlockSpec(memory_space=pltpu.SMEM)


def _layer_weights(l, g_attn_norm, w_in, g_v_a, g_q, g_k, g_out_a, g_out_b, w_out, g_ffn_norm,
                   w_coarse, b_coarse, w_fine, b_fine):
    pad = ROUTER_LANES - N_EXPERT_GROUPS - N_EXPERTS
    wr = jnp.concatenate([w_coarse[l], w_fine[l], jnp.zeros((D_MODEL, pad), F32)], axis=1)
    vec = jnp.concatenate([
        g_attn_norm[l], g_v_a[l].reshape(A_WIDTH), jnp.tile(g_q[l], N_HEADS), jnp.tile(g_k[l], N_KV),
        g_out_a[l], g_out_b[l], g_ffn_norm[l], b_coarse[l], b_fine[l], jnp.zeros((pad,), F32)])
    return dict(
        vec=vec.reshape(1, VEC_LANES),
        w_in=w_in[l].astype(BF16),
        segq=jnp.asarray(_seg_ones(B_WIDTH, HEAD_DIM), BF16),
        segk=jnp.asarray(_seg_ones(KV_WIDTH, HEAD_DIM), BF16),
        w_out=w_out[l].astype(BF16),
        wr=wr.astype(BF16),
    )


def _spatial_tables(ws, bs, period, tile):
    pair = min(PAIR, tile)
    reps = pair // period
    wbd = (jnp.tile(ws[:, :period, :period], (1, reps, reps)) * _causal_block_mask(period, pair)).astype(BF16)
    bsp = jnp.repeat(bs[:, :period].T, A_CH, axis=1)
    return wbd, bsp


_HEAD_NAMES = ("vec", "w_in", "segq", "segk")
_TAIL_NAMES = ("w_out", "wr")


def _lower_triangle(n):
    return jnp.asarray(np.tril(np.ones((n, n), np.float32), -1), BF16)


def _run_prompt(x2d, lw, wbd, bsp, sinks, seq_len, w_gate, w_up, w_down):
    n_tok = x2d.shape[0]
    tile = PROMPT_TILE
    tiles_per_seq = seq_len // tile
    n_seq = n_tok // seq_len
    bias = jnp.asarray(_prompt_bias())
    head = [lw[n] for n in _HEAD_NAMES]
    tail = [lw[n] for n in _TAIL_NAMES]
    tail.append(_lower_triangle(tile))
    experts = [w_gate, w_up, w_down]
    consts = head + [wbd, bsp, bias]
    any_spec = pl.BlockSpec(memory_space=pl.ANY)
    in_specs = ([_row_spec(tile, D_MODEL)] + [_const_spec(a.shape) for a in consts] + [_smem_spec()]
                + [_const_spec(a.shape) for a in tail] + [any_spec] * len(experts))
    win_spec = pl.BlockSpec((KV_WIDTH, WINDOW), lambda i: (i // tiles_per_seq, 0))
    out_shape = (jax.ShapeDtypeStruct((n_tok, D_MODEL), F32),
                 jax.ShapeDtypeStruct((n_seq * KV_WIDTH, WINDOW), F32),
                 jax.ShapeDtypeStruct((n_seq * KV_WIDTH, WINDOW), F32),
                 *[jax.ShapeDtypeStruct(w.shape, BF16) for w in experts])
    out_specs = (_row_spec(tile, D_MODEL), win_spec, win_spec, any_spec, any_spec, any_spec)
    kern = functools.partial(_prompt_kernel, tile=tile, tiles_per_seq=tiles_per_seq)
    return pl.pallas_call(
        kern,
        out_shape=out_shape,
        grid=(n_tok // tile,),
        in_specs=in_specs,
        out_specs=out_specs,
        scratch_shapes=[pltpu.VMEM((N_KV, WINDOW, KV_WIDTH), BF16), pltpu.VMEM((N_KV, WINDOW, KV_WIDTH), BF16),
                        pltpu.VMEM((tile, D_MODEL), BF16), pltpu.VMEM((tile, LANES), F32),
                        pltpu.VMEM((tile, D_MODEL), F32),
                        *[pltpu.VMEM(w.shape, BF16) for w in experts],
                        pltpu.VMEM((2,) + w_gate.shape[1:], F32), pltpu.VMEM((2,) + w_down.shape[1:], F32),
                        pltpu.SemaphoreType.DMA((7,))],
        compiler_params=pltpu.CompilerParams(dimension_semantics=("arbitrary",),
                                             vmem_limit_bytes=VMEM_LIMIT_BYTES),
        name="layer_prompt",
    )(x2d, *consts, sinks, *tail, *experts)


def _run_sample(x2d, ck_t, cv_t, lw, wbd, bsp, sinks, dec, wg_bf, wu_bf, wd_bf):
    n_tok = x2d.shape[0]
    n_seq = ck_t.shape[0]
    tile = SAMPLE_TILE
    nb = tile // dec
    bc, bn = _sample_bias(nb, dec)
    head = [lw[n] for n in _HEAD_NAMES]
    tail = [lw[n] for n in _TAIL_NAMES] + [wg_bf, wu_bf, wd_bf.reshape(N_EXPERTS * D_EXPERT, D_MODEL)]
    consts = head + [wbd, bsp, jnp.asarray(bc), jnp.asarray(bn)]
    cache_spec = pl.BlockSpec((nb, KV_WIDTH, WINDOW), lambda i: (i, 0, 0))
    in_specs = ([_row_spec(tile, D_MODEL), cache_spec, cache_spec] + [_const_spec(a.shape) for a in consts]
                + [_smem_spec()] + [_const_spec(a.shape) for a in tail])
    out_shape = (jax.ShapeDtypeStruct((n_tok, D_MODEL), F32),
                 jax.ShapeDtypeStruct((n_seq, KV_WIDTH, WINDOW), F32),
                 jax.ShapeDtypeStruct((n_seq, KV_WIDTH, WINDOW), F32),
                 jax.ShapeDtypeStruct((n_tok * A_GROUPS, A_CH), F32))
    out_specs = (_row_spec(tile, D_MODEL), cache_spec, cache_spec, _row_spec(tile * A_GROUPS, A_CH))
    kern = functools.partial(_sample_kernel, tile=tile, nb=nb, dec=dec)
    return pl.pallas_call(
        kern,
        out_shape=out_shape,
        grid=(n_tok // tile,),
        in_specs=in_specs,
        out_specs=out_specs,
        scratch_shapes=[pltpu.VMEM((tile, N_EXPERTS * D_EXPERT), BF16)],
        compiler_params=pltpu.CompilerParams(dimension_semantics=("arbitrary",),
                                             vmem_limit_bytes=VMEM_LIMIT_BYTES),
        name="layer_sample",
    )(x2d, ck_t, cv_t, *consts, sinks, *tail)


def _positions_last(c):
    b, w = c.shape[:2]
    return jnp.transpose(c, (0, 2, 3, 1)).reshape(b, KV_WIDTH, w)


def _positions_first(c_t):
    b, _, w = c_t.shape
    return jnp.transpose(c_t.reshape(b, N_KV, HEAD_DIM, w), (0, 3, 1, 2))


def kernel(x_prompt, x_sample, cache_k, cache_v, g_attn_norm, w_in, g_v_a, w_spatial, b_spatial, g_q, g_k, attn_sinks, g_out_a, g_out_b, w_out, g_ffn_norm, w_coarse, b_coarse, w_fine, b_fine, w_gate, w_up, w_down):
    depth = w_in.shape[0]
    batch, seq, _ = x_prompt.shape
    dbatch, dec, _ = x_sample.shape
    win = cache_k.shape[2]
    assert win == WINDOW and seq % PROMPT_TILE == 0 and (dbatch * dec) % SAMPLE_TILE == 0
    assert PAIR % dec == 0 and SAMPLE_TILE % dec == 0 and LANES % dec == 0

    hp = x_prompt.reshape(batch * seq, D_MODEL)
    hs = x_sample.reshape(dbatch * dec, D_MODEL)
    kp_l, vp_l, ks_l, vs_l, cv_l = [], [], [], [], []
    for l in range(depth):
        lw = _layer_weights(l, g_attn_norm, w_in, g_v_a, g_q, g_k, g_out_a, g_out_b, w_out, g_ffn_norm,
                            w_coarse, b_coarse, w_fine, b_fine)
        sinks = attn_sinks[l].astype(F32)
        wbd_p, bsp_p = _spatial_tables(w_spatial[l], b_spatial[l], CHUNK, PROMPT_TILE)
        wbd_s, bsp_s = _spatial_tables(w_spatial[l], b_spatial[l], dec, SAMPLE_TILE)

        hp, kp, vp, wg_bf, wu_bf, wd_bf = _run_prompt(hp, lw, wbd_p, bsp_p, sinks, seq,
                                                      w_gate[l], w_up[l], w_down[l])
        kp_l.append(_positions_first(kp.reshape(batch, KV_WIDTH, WINDOW)))
        vp_l.append(_positions_first(vp.reshape(batch, KV_WIDTH, WINDOW)))

        hs, wk, wv, va = _run_sample(hs, _positions_last(cache_k[l]), _positions_last(cache_v[l]),
                                     lw, wbd_s, bsp_s, sinks, dec, wg_bf, wu_bf, wd_bf)
        ks_l.append(_positions_first(wk))
        vs_l.append(_positions_first(wv))
        cv_l.append(va.reshape(dbatch, dec, A_GROUPS, A_CH))

    return (hp.reshape(batch, seq, D_MODEL), hs.reshape(dbatch, dec, D_MODEL),
            jnp.stack(kp_l, axis=0), jnp.stack(vp_l, axis=0),
            jnp.stack(ks_l, axis=0), jnp.stack(vs_l, axis=0), jnp.stack(cv_l, axis=0))
```

```python
import functools

import numpy as np
import jax
import jax.numpy as jnp
from jax import lax
from jax.experimental import pallas as pl
from jax.experimental.pallas import tpu as pltpu

D_MODEL = 1024
CHUNK = 128
A_GROUPS = 4
A_WIDTH = 512
A_CH = 128
N_HEADS = 8
N_KV = 2
Q_PER_KV = 4
HEAD_DIM = 64
B_WIDTH = 512
KV_WIDTH = 128
WINDOW = 128
IN_COLS = 2 * A_WIDTH + B_WIDTH + 2 * KV_WIDTH
N_EXPERT_GROUPS = 4
EXPERTS_PER_GROUP = 4
N_EXPERTS = 16
D_EXPERT = 256
EPS = 1e-6

LANES = 128
PAIR = CHUNK
ROUTER_LANES = 128
PROMPT_TILE = 512
SAMPLE_TILE = 128
VMEM_LIMIT_BYTES = 60 * 1024 * 1024

F32 = jnp.float32
BF16 = jnp.bfloat16
NEG_INF = float("-inf")
LOG2E = 1.4426950408889634
Q_SCALE = (HEAD_DIM ** -0.5) * LOG2E


def _slopes():
    return np.array([2.0 ** (-8.0 * (h + 1) / N_HEADS) for h in range(N_HEADS)], np.float64)


def _prompt_bias():
    t = np.arange(WINDOW)[:, None]
    s = np.arange(2 * WINDOW)[None, :]
    dist = t + WINDOW - s
    valid = (dist >= 0) & (dist < WINDOW)
    sl = _slopes()
    out = np.full((N_KV, Q_PER_KV * WINDOW, 2 * WINDOW), -np.inf, np.float32)
    for kh in range(N_KV):
        for g in range(Q_PER_KV):
            b = np.where(valid, -sl[kh * Q_PER_KV + g] * LOG2E * dist, -np.inf)
            out[kh, g * WINDOW:(g + 1) * WINDOW] = b
    return out


def _sample_bias(nb, dec):
    sl = _slopes()
    t = np.arange(dec)[:, None]
    j = np.arange(WINDOW)[None, :]
    dist_c = t + WINDOW - j
    valid_c = (dist_c >= 0) & (dist_c < WINDOW)
    bc = np.full((N_HEADS * dec, WINDOW), -np.inf, np.float32)
    tp = np.arange(dec)[None, :]
    dist_n = t - tp
    valid_n = dist_n >= 0
    bn = np.full((nb, N_HEADS * dec, nb * dec), -np.inf, np.float32)
    for h in range(N_HEADS):
        bc[h * dec:(h + 1) * dec] = np.where(valid_c, -sl[h] * LOG2E * dist_c, -np.inf)
        blk = np.where(valid_n, -sl[h] * LOG2E * dist_n, -np.inf)
        for b in range(nb):
            bn[b, h * dec:(h + 1) * dec, b * dec:(b + 1) * dec] = blk
    return bc, bn


def _seg_ones(width, seg):
    i = np.arange(width)
    return (i[:, None] // seg == i[None, :] // seg).astype(np.float32)


def _causal_block_mask(period, size):
    i = np.arange(size)
    same = i[:, None] // period == i[None, :] // period
    return (same & (i[None, :] % period <= i[:, None] % period)).astype(np.float32)


def _dot(a, b):
    return jnp.dot(a, b, preferred_element_type=F32)


def _dot_nt(a, b):
    return lax.dot_general(a, b, (((1,), (1,)), ((), ())), preferred_element_type=F32)


def _rms(x, g):
    ms = jnp.mean(x * x, axis=-1, keepdims=True)
    return (x * lax.rsqrt(ms + EPS)) * g


def _seg_rms(x, ones_bf, g, seg):
    ss = _dot((x * x).astype(BF16), ones_bf)
    return (x * lax.rsqrt(ss * (1.0 / seg) + EPS)) * g


GELU_K0 = -2.0 * 0.7978845608028654 * LOG2E
GELU_K1 = GELU_K0 * 0.044715


def _gelu_tanh(x):
    t = (x * x) * GELU_K1 + GELU_K0
    return x * (1.0 / (1.0 + jnp.exp2(x * t)))


VEC_ATTN = 0
VEC_VA = VEC_ATTN + D_MODEL
VEC_Q = VEC_VA + A_WIDTH
VEC_K = VEC_Q + B_WIDTH
VEC_OUT = VEC_K + KV_WIDTH
VEC_FFN = VEC_OUT + D_MODEL
VEC_ROUTER_BIAS = VEC_FFN + D_MODEL
VEC_LANES = VEC_ROUTER_BIAS + ROUTER_LANES


def _vec(vec_ref, start, width):
    return vec_ref[:, start:start + width]


def _project(x, vec_ref, w_in_ref, segq_ref, segk_ref):
    xn = _rms(x, _vec(vec_ref, VEC_ATTN, D_MODEL)).astype(BF16)
    z = _dot(xn, w_in_ref[...])
    i1, i2, i3, i4 = A_WIDTH, 2 * A_WIDTH, 2 * A_WIDTH + B_WIDTH, 2 * A_WIDTH + B_WIDTH + KV_WIDTH
    u = _gelu_tanh(z[:, :i1])
    va_pre = _gelu_tanh(z[:, i1:i2])
    g_va = _vec(vec_ref, VEC_VA, A_WIDTH)
    va = jnp.concatenate(
        [_rms(va_pre[:, g * A_CH:(g + 1) * A_CH], g_va[:, g * A_CH:(g + 1) * A_CH]) for g in range(A_GROUPS)],
        axis=1)
    q = _seg_rms(z[:, i2:i3], segq_ref[...], _vec(vec_ref, VEC_Q, B_WIDTH), HEAD_DIM)
    k = _seg_rms(z[:, i3:i4], segk_ref[...], _vec(vec_ref, VEC_K, KV_WIDTH), HEAD_DIM)
    v = z[:, i4:]
    return u, va, q, k, v


def _chunk_mlp(va_bf, wsp_ref, bsp_ref, tile):
    pair = wsp_ref.shape[1]
    n_pairs = tile // pair
    outs = []
    for g in range(A_GROUPS):
        cols = [va_bf[p * pair:(p + 1) * pair, g * A_CH:(g + 1) * A_CH] for p in range(n_pairs)]
        rhs = cols[0] if n_pairs == 1 else jnp.concatenate(cols, axis=1)
        o = _dot(wsp_ref[g], rhs)
        rows = [o[:, p * A_CH:(p + 1) * A_CH] for p in range(n_pairs)]
        outs.append(rows[0] if n_pairs == 1 else jnp.concatenate(rows, axis=0))
    mixed = jnp.concatenate(outs, axis=1)
    bias = bsp_ref[...]
    reps = tile // bias.shape[0]
    return mixed + (bias if reps == 1 else jnp.concatenate([bias] * reps, axis=0))


def _lane_lo():
    return lax.broadcasted_iota(jnp.int32, (1, LANES), 1) < HEAD_DIM


def _swap_halves(x):
    return pltpu.roll(x, HEAD_DIM, axis=1)


def _head_lhs(q_groups, q_groups_swapped, head, lo):
    j, half = divmod(head, 2)
    kv = head // Q_PER_KV
    src = q_groups[j] if half == kv else q_groups_swapped[j]
    return jnp.where(lo if kv == 0 else jnp.logical_not(lo), src, 0.0)


def _merge_heads(o_heads, lo):
    groups = []
    for j in range(N_HEADS // 2):
        kv = (2 * j) // Q_PER_KV
        if kv == 0:
            groups.append(jnp.where(lo, o_heads[2 * j], _swap_halves(o_heads[2 * j + 1])))
        else:
            groups.append(jnp.where(lo, _swap_halves(o_heads[2 * j]), o_heads[2 * j + 1]))
    return jnp.concatenate(groups, axis=1)


def _dup_halves(x, lo):
    xs = _swap_halves(x)
    return jnp.where(lo, x, xs).astype(BF16), jnp.where(lo, xs, x).astype(BF16)


def _swa_prompt(q, k, v, kprev_ref, vprev_ref, bias_ref, sinks_ref, is_first, tile):
    lo = _lane_lo()
    hi = jnp.logical_not(lo)
    col = lax.broadcasted_iota(jnp.int32, (WINDOW, 2 * WINDOW), 1)
    first_mask = jnp.where(col < WINDOW, jnp.where(is_first, NEG_INF, 0.0).astype(F32), 0.0)
    kd = _dup_halves(k, lo)
    vd = _dup_halves(v, lo)
    nblk = tile // WINDOW
    rows_out = []
    for i in range(nblk):
        r0, r1 = i * WINDOW, (i + 1) * WINDOW
        qg = [q[r0:r1, j * LANES:(j + 1) * LANES] for j in range(N_HEADS // 2)]
        o_heads = []
        for kh in range(N_KV):
            if i == 0:
                kp, vp = kprev_ref[kh], vprev_ref[kh]
            else:
                kp, vp = kd[kh][r0 - WINDOW:r0], vd[kh][r0 - WINDOW:r0]
            kb = jnp.concatenate([kp, kd[kh][r0:r1]], axis=0)
            vb = jnp.concatenate([vp, vd[kh][r0:r1]], axis=0)
            heads = [kh * Q_PER_KV + g for g in range(Q_PER_KV)]
            lhs = jnp.concatenate(
                [jnp.where(lo if h % 2 == 0 else hi, qg[h // 2], 0.0) for h in heads], axis=0).astype(BF16)
            s = _dot_nt(lhs, kb) + bias_ref[kh]
            ps, linvs = [], []
            for g in range(Q_PER_KV):
                sg = s[g * WINDOW:(g + 1) * WINDOW]
                if i == 0:
                    sg = sg + first_mask
                sink = sinks_ref[heads[g]] * LOG2E
                mg = jnp.maximum(jnp.max(sg, axis=-1, keepdims=True), sink)
                pg = jnp.exp2(sg - mg)
                lg = jnp.sum(pg, axis=-1, keepdims=True) + jnp.exp2(sink - mg)
                ps.append(pg.astype(BF16))
                linvs.append(1.0 / lg)
            o = _dot(jnp.concatenate(ps, axis=0), vb)
            o_heads += [o[g * WINDOW:(g + 1) * WINDOW] * linvs[g] for g in range(Q_PER_KV)]
        rows_out.append(jnp.concatenate(
            [jnp.where(lo, o_heads[2 * j], o_heads[2 * j + 1]) for j in range(N_HEADS // 2)], axis=1))
    for kh in range(N_KV):
        kprev_ref[kh] = kd[kh][tile - WINDOW:tile]
        vprev_ref[kh] = vd[kh][tile - WINDOW:tile]
    return jnp.concatenate(rows_out, axis=0)


def _shift_window(cache_ref, new_t, out_ref, nb, dec):
    lane = lax.broadcasted_iota(jnp.int32, (1, WINDOW), 1)
    keep = lane < WINDOW - dec
    per_tile = LANES // dec
    for b in range(nb):
        src = new_t[:, (b // per_tile) * LANES:(b // per_tile + 1) * LANES]
        new_cols = pltpu.roll(src, (WINDOW - dec - (b % per_tile) * dec) % LANES, axis=1)
        old = pltpu.roll(cache_ref[b], WINDOW - dec, axis=1)
        out_ref[b] = jnp.where(keep, old, new_cols)


def _swa_sample(q, k, v, ck_ref, cv_ref, wk_ref, wv_ref, bias_c_ref, bias_n_ref, sinks_ref, nb, dec):
    lo = _lane_lo()
    qg = [q[:, j * LANES:(j + 1) * LANES] for j in range(N_HEADS // 2)]
    qgs = [_swap_halves(x) for x in qg]
    qb = jnp.concatenate(
        [_head_lhs(qg, qgs, h, lo).reshape(nb, dec, LANES) for h in range(N_HEADS)], axis=1)
    rows = N_HEADS * dec
    qb_bf = qb.astype(BF16)
    k_bf = k.astype(BF16)
    v_bf = v.astype(BF16)
    ck = ck_ref[...].astype(BF16)
    cv = cv_ref[...].astype(BF16)
    sc = jnp.einsum('bqc,bcp->bqp', qb_bf, ck, preferred_element_type=F32) + bias_c_ref[...][None]
    sn = _dot_nt(qb_bf.reshape(nb * rows, LANES), k_bf).reshape(nb, rows, nb * dec) + bias_n_ref[...]
    sink = jnp.concatenate(
        [jnp.full((1, dec, 1), sinks_ref[h] * LOG2E, F32) for h in range(N_HEADS)], axis=1)
    m = jnp.maximum(jnp.max(sc, axis=-1, keepdims=True), jnp.max(sn, axis=-1, keepdims=True))
    m = jnp.maximum(m, sink)
    pc = jnp.exp2(sc - m)
    pn = jnp.exp2(sn - m)
    l = jnp.sum(pc, axis=-1, keepdims=True) + jnp.sum(pn, axis=-1, keepdims=True) + jnp.exp2(sink - m)
    oc = jnp.einsum('bqp,bcp->bqc', pc.astype(BF16), cv, preferred_element_type=F32)
    on = _dot(pn.reshape(nb * rows, nb * dec).astype(BF16), v_bf).reshape(nb, rows, LANES)
    o = (oc + on) * (1.0 / l)
    o_heads = [o[:, h * dec:(h + 1) * dec, :].reshape(nb * dec, LANES) for h in range(N_HEADS)]
    _shift_window(ck_ref, k.T, wk_ref, nb, dec)
    _shift_window(cv_ref, v.T, wv_ref, nb, dec)
    return _merge_heads(o_heads, lo)


def _route(hn_bf, wr_ref, vec_ref):
    logits = _dot(hn_bf, wr_ref[...]) + _vec(vec_ref, VEC_ROUTER_BIAS, ROUTER_LANES)
    lane = lax.broadcasted_iota(jnp.int32, logits.shape, 1)
    big = jnp.int32(ROUTER_LANES)
    lc = jnp.where(lane < N_EXPERT_GROUPS, logits, NEG_INF)
    mx = jnp.max(lc, axis=-1, keepdims=True)
    g_idx = jnp.min(jnp.where(lc == mx, lane, big), axis=-1, keepdims=True)
    p_g = 1.0 / jnp.sum(jnp.exp(lc - mx), axis=-1, keepdims=True)
    e_lane = lane - N_EXPERT_GROUPS
    in_group = (e_lane >= 0) & (e_lane < N_EXPERTS) & ((e_lane >> 2) == g_idx)
    lf = jnp.where(in_group, logits, NEG_INF)
    v1 = jnp.max(lf, axis=-1, keepdims=True)
    i1 = jnp.min(jnp.where(lf == v1, lane, big), axis=-1, keepdims=True)
    lf2 = jnp.where(lane == i1, NEG_INF, lf)
    v2 = jnp.max(lf2, axis=-1, keepdims=True)
    i2 = jnp.min(jnp.where(lf2 == v2, lane, big), axis=-1, keepdims=True)
    e = jnp.exp(v2 - v1)
    w1 = p_g / (1.0 + e)
    w2 = w1 * e
    return lane, g_idx, i1, i2, w1, w2


def _silu(x):
    return x * (1.0 / (1.0 + jnp.exp2(x * (-LOG2E))))


def _moe_dense(h, vec_ref, wr_ref, wg_ref, wu_ref, wd_ref, hcat_ref):
    hn_bf = _rms(h, _vec(vec_ref, VEC_FFN, D_MODEL)).astype(BF16)
    lane, _, i1, i2, w1, w2 = _route(hn_bf, wr_ref, vec_ref)
    combine = jnp.where(lane == i1, w1, jnp.where(lane == i2, w2, 0.0))
    for e in range(N_EXPERTS):
        gate = _dot(hn_bf, wg_ref[e])
        up = _dot(hn_bf, wu_ref[e])
        c = combine[:, N_EXPERT_GROUPS + e:N_EXPERT_GROUPS + e + 1]
        hcat_ref[:, e * D_EXPERT:(e + 1) * D_EXPERT] = (_silu(gate) * up * c).astype(BF16)
    return _dot(hcat_ref[...], wd_ref[...])


GROUP_LANE = EXPERTS_PER_GROUP
LOW_SHIFT = 8
SORT_BLOCK = 128


def _moe_grouped(h, vec_ref, wr_ref, wg_ref, wu_ref, wd_ref, lstrict_ref, xs_ref, es_ref, ys_ref, tile):
    hn_bf = _rms(h, _vec(vec_ref, VEC_FFN, D_MODEL)).astype(BF16)
    lane, g_idx, i1, i2, w1, w2 = _route(hn_bf, wr_ref, vec_ref)

    onehot = lane == g_idx
    gmat = jnp.where(onehot, 1.0, 0.0)
    before = _dot(lstrict_ref[...], gmat.astype(BF16))
    rank = jnp.sum(jnp.where(onehot, before, 0.0), axis=-1, keepdims=True)
    counts = jnp.sum(gmat, axis=0, keepdims=True)
    lane1 = lax.broadcasted_iota(jnp.int32, counts.shape, 1)
    offs = [jnp.float32(0.0)]
    for g in range(N_EXPERT_GROUPS - 1):
        offs.append(offs[-1] + jnp.sum(jnp.where(lane1 == g, counts, 0.0)))
    offv = jnp.zeros_like(rank)
    for g in range(1, N_EXPERT_GROUPS):
        offv = offv + jnp.where(g_idx == g, offs[g], 0.0)
    pos = (rank + offv).astype(jnp.int32)
    dest = lax.broadcasted_iota(jnp.int32, (tile, tile), 1)
    pt = jnp.where(dest == pos, 1.0, 0.0).astype(BF16)

    j1 = (i1 - N_EXPERT_GROUPS) & (EXPERTS_PER_GROUP - 1)
    j2 = (i2 - N_EXPERT_GROUPS) & (EXPERTS_PER_GROUP - 1)
    gates = jnp.where(lane == j1, w1, jnp.where(lane == j2, w2, 0.0))
    gates_hi = gates.astype(BF16).astype(F32)
    gates_lo = (gates - gates_hi).astype(BF16).astype(F32)
    side = gates_hi + pltpu.roll(gates_lo, LOW_SHIFT, axis=1) + jnp.where(lane == GROUP_LANE, g_idx.astype(F32), 0.0)
    x_ext = jnp.concatenate([hn_bf, side.astype(BF16)], axis=1)
    srt = lax.dot_general(pt, x_ext, (((0,), (0,)), ((), ())), preferred_element_type=F32)
    xs_ref[...] = srt[:, :D_MODEL].astype(BF16)
    es = srt[:, D_MODEL:]
    es_ref[...] = es + pltpu.roll(es, LANES - LOW_SHIFT, axis=1)

    def group_of(row):
        g = jnp.int32(0)
        for k in range(1, N_EXPERT_GROUPS):
            g = g + (jnp.float32(row) >= offs[k]).astype(jnp.int32)
        return g

    def group_pass(rows, g, keep, accumulate):
        xb = xs_ref[rows, :]
        eb = es_ref[rows, :]
        mine = jnp.where(eb[:, GROUP_LANE:GROUP_LANE + 1] == jnp.asarray(g).astype(F32), keep, 0.0)
        parts = []
        for j in range(EXPERTS_PER_GROUP):
            e = g * EXPERTS_PER_GROUP + j
            gate = _dot(xb, wg_ref[e])
            up = _dot(xb, wu_ref[e])
            parts.append((_silu(gate) * up * (eb[:, j:j + 1] * mine)).astype(BF16))
        wd = wd_ref[pl.ds(g * EXPERTS_PER_GROUP, EXPERTS_PER_GROUP)]
        out = _dot(jnp.concatenate(parts, axis=1), wd.reshape(EXPERTS_PER_GROUP * D_EXPERT, D_MODEL))
        if accumulate:
            ys_ref[rows, :] += out
        else:
            ys_ref[rows, :] = out

    nblk = tile // SORT_BLOCK
    for b in range(nblk):
        group_pass(pl.ds(b * SORT_BLOCK, SORT_BLOCK), group_of(b * SORT_BLOCK), 1.0, False)
    for g in range(1, N_EXPERT_GROUPS):
        start = offs[g].astype(jnp.int32)
        blk = jnp.minimum(lax.div(start, jnp.int32(SORT_BLOCK)), nblk - 1)
        inside = jnp.where(lax.rem(start, jnp.int32(SORT_BLOCK)) != 0, 1.0, 0.0)
        group_pass(pl.ds(pl.multiple_of(blk * SORT_BLOCK, SORT_BLOCK), SORT_BLOCK), g, inside, True)
    return _dot(pt, ys_ref[...].astype(BF16))


def _merge(x, a_out, b_out, vec_ref, w_out_ref):
    g_out = _vec(vec_ref, VEC_OUT, D_MODEL)
    mix = jnp.concatenate(
        [_rms(a_out, g_out[:, :A_WIDTH]), _rms(b_out, g_out[:, A_WIDTH:])], axis=1).astype(BF16)
    return x + _dot(mix, w_out_ref[...])


def _stage_expert_weights(wg_hbm, wu_hbm, wd_hbm, wg_s, wu_s, wd_s, st_a, st_d, sems):
    def gate_cp(e):
        return pltpu.make_async_copy(wg_hbm.at[e], st_a.at[0], sems.at[0])

    def up_cp(e):
        return pltpu.make_async_copy(wu_hbm.at[e], st_a.at[1], sems.at[1])

    def down_cp(e, slot):
        return pltpu.make_async_copy(wd_hbm.at[e], st_d.at[slot], sems.at[2 + slot])

    gate_cp(0).start()
    up_cp(0).start()
    down_cp(0, 0).start()

    def pair(p, carry):
        for slot in (0, 1):
            e = 2 * p + slot
            nxt = e + 1
            more = nxt < N_EXPERTS

            @pl.when(more)
            def _():
                down_cp(nxt, 1 - slot).start()

            gate_cp(e).wait()
            wg_s[e] = st_a[0].astype(BF16)

            @pl.when(more)
            def _():
                gate_cp(nxt).start()

            up_cp(e).wait()
            wu_s[e] = st_a[1].astype(BF16)

            @pl.when(more)
            def _():
                up_cp(nxt).start()

            down_cp(e, slot).wait()
            wd_s[e] = st_d[slot].astype(BF16)
        return carry

    lax.fori_loop(0, N_EXPERTS // 2, pair, 0)


def _bf16_weight_copies(wg_s, wu_s, wd_s, wg_o, wu_o, wd_o, sems):
    return (pltpu.make_async_copy(wg_s, wg_o, sems.at[4]),
            pltpu.make_async_copy(wu_s, wu_o, sems.at[5]),
            pltpu.make_async_copy(wd_s, wd_o, sems.at[6]))


def _prompt_kernel(x_ref, vec_ref, w_in_ref, segq_ref, segk_ref, wsp_ref, bsp_ref, bias_ref, sinks_ref,
                   w_out_ref, wr_ref, lstrict_ref, wg_hbm, wu_hbm, wd_hbm,
                   y_ref, kwin_ref, vwin_ref, wg_o, wu_o, wd_o,
                   kprev_ref, vprev_ref, xs_ref, es_ref, ys_ref, wg_ref, wu_ref, wd_ref, st_a, st_d, sems,
                   *, tile, tiles_per_seq):
    step = pl.program_id(0)
    is_first = (step % tiles_per_seq) == 0

    @pl.when(step == 0)
    def _():
        _stage_expert_weights(wg_hbm, wu_hbm, wd_hbm, wg_ref, wu_ref, wd_ref, st_a, st_d, sems)
        for cp in _bf16_weight_copies(wg_ref, wu_ref, wd_ref, wg_o, wu_o, wd_o, sems):
            cp.start()

    @pl.when(step == pl.num_programs(0) - 1)
    def _():
        for cp in _bf16_weight_copies(wg_ref, wu_ref, wd_ref, wg_o, wu_o, wd_o, sems):
            cp.wait()

    @pl.when(is_first)
    def _():
        kprev_ref[...] = jnp.zeros_like(kprev_ref)
        vprev_ref[...] = jnp.zeros_like(vprev_ref)

    x = x_ref[...]
    u, va, q, k, v = _project(x, vec_ref, w_in_ref, segq_ref, segk_ref)
    kwin_ref[...] = k[tile - WINDOW:tile]
    vwin_ref[...] = v[tile - WINDOW:tile]
    a_out = u * _chunk_mlp(va.astype(BF16), wsp_ref, bsp_ref, tile)
    b_out = _swa_prompt(q * Q_SCALE, k, v, kprev_ref, vprev_ref, bias_ref, sinks_ref, is_first, tile)
    h = _merge(x, a_out, b_out, vec_ref, w_out_ref)
    y_ref[...] = h + _moe_grouped(h, vec_ref, wr_ref, wg_ref, wu_ref, wd_ref, lstrict_ref,
                                  xs_ref, es_ref, ys_ref, tile)


def _sample_kernel(x_ref, ck_ref, cv_ref, vec_ref, w_in_ref, segq_ref, segk_ref, wsp_ref, bsp_ref,
                   bias_c_ref, bias_n_ref, sinks_ref, w_out_ref, wr_ref, wg_ref, wu_ref, wd_ref,
                   y_ref, wk_ref, wv_ref, va_ref, hcat_ref, *, tile, nb, dec):
    x = x_ref[...]
    u, va, q, k, v = _project(x, vec_ref, w_in_ref, segq_ref, segk_ref)
    for g in range(A_GROUPS):
        va_ref[pl.ds(g, tile, stride=A_GROUPS), :] = va[:, g * A_CH:(g + 1) * A_CH]
    a_out = u * _chunk_mlp(va.astype(BF16), wsp_ref, bsp_ref, tile)
    b_out = _swa_sample(q * Q_SCALE, k, v, ck_ref, cv_ref, wk_ref, wv_ref,
                        bias_c_ref, bias_n_ref, sinks_ref, nb, dec)
    h = _merge(x, a_out, b_out, vec_ref, w_out_ref)
    y_ref[...] = h + _moe_dense(h, vec_ref, wr_ref, wg_ref, wu_ref, wd_ref, hcat_ref)


def _const_spec(shape):
    nd = len(shape)
    return pl.BlockSpec(shape, lambda i: (0,) * nd, pipeline_mode=pl.Buffered(1))


def _row_spec(tile, width):
    return pl.BlockSpec((tile, width), lambda i: (i, 0))


def _smem_spec():
    return pl.BlockSpec(memory_space=pltpu.SMEM)


def _layer_weights(l, g_attn_norm, w_in, g_v_a, g_q, g_k, g_out_a, g_out_b, w_out, g_ffn_norm,
                   w_coarse, b_coarse, w_fine, b_fine):
    pad = ROUTER_LANES - N_EXPERT_GROUPS - N_EXPERTS
    wr = jnp.concatenate([w_coarse[l], w_fine[l], jnp.zeros((D_MODEL, pad), F32)], axis=1)
    vec = jnp.concatenate([
        g_attn_norm[l], g_v_a[l].reshape(A_WIDTH), jnp.tile(g_q[l], N_HEADS), jnp.tile(g_k[l], N_KV),
        g_out_a[l], g_out_b[l], g_ffn_norm[l], b_coarse[l], b_fine[l], jnp.zeros((pad,), F32)])
    return dict(
        vec=vec.reshape(1, VEC_LANES),
        w_in=w_in[l].astype(BF16),
        segq=jnp.asarray(_seg_ones(B_WIDTH, HEAD_DIM), BF16),
        segk=jnp.asarray(_seg_ones(KV_WIDTH, HEAD_DIM), BF16),
        w_out=w_out[l].astype(BF16),
        wr=wr.astype(BF16),
    )


def _spatial_tables(ws, bs, period, tile):
    pair = min(PAIR, tile)
    reps = pair // period
    wbd = (jnp.tile(ws[:, :period, :period], (1, reps, reps)) * _causal_block_mask(period, pair)).astype(BF16)
    bsp = jnp.repeat(bs[:, :period].T, A_CH, axis=1)
    return wbd, bsp


_HEAD_NAMES = ("vec", "w_in", "segq", "segk")
_TAIL_NAMES = ("w_out", "wr")


def _lower_triangle(n):
    return jnp.asarray(np.tril(np.ones((n, n), np.float32), -1), BF16)


def _run_prompt(x2d, lw, wbd, bsp, sinks, seq_len, w_gate, w_up, w_down):
    n_tok = x2d.shape[0]
    tile = PROMPT_TILE
    tiles_per_seq = seq_len // tile
    n_seq = n_tok // seq_len
    bias = jnp.asarray(_prompt_bias())
    head = [lw[n] for n in _HEAD_NAMES]
    tail = [lw[n] for n in _TAIL_NAMES]
    tail.append(_lower_triangle(tile))
    experts = [w_gate, w_up, w_down]
    consts = head + [wbd, bsp, bias]
    any_spec = pl.BlockSpec(memory_space=pl.ANY)
    in_specs = ([_row_spec(tile, D_MODEL)] + [_const_spec(a.shape) for a in consts] + [_smem_spec()]
                + [_const_spec(a.shape) for a in tail] + [any_spec] * len(experts))
    win_spec = pl.BlockSpec((WINDOW, KV_WIDTH), lambda i: (i // tiles_per_seq, 0))
    out_shape = (jax.ShapeDtypeStruct((n_tok, D_MODEL), F32),
                 jax.ShapeDtypeStruct((n_seq * WINDOW, KV_WIDTH), F32),
                 jax.ShapeDtypeStruct((n_seq * WINDOW, KV_WIDTH), F32),
                 *[jax.ShapeDtypeStruct(w.shape, BF16) for w in experts])
    out_specs = (_row_spec(tile, D_MODEL), win_spec, win_spec, any_spec, any_spec, any_spec)
    kern = functools.partial(_prompt_kernel, tile=tile, tiles_per_seq=tiles_per_seq)
    return pl.pallas_call(
        kern,
        out_shape=out_shape,
        grid=(n_tok // tile,),
        in_specs=in_specs,
        out_specs=out_specs,
        scratch_shapes=[pltpu.VMEM((N_KV, WINDOW, KV_WIDTH), BF16), pltpu.VMEM((N_KV, WINDOW, KV_WIDTH), BF16),
                        pltpu.VMEM((tile, D_MODEL), BF16), pltpu.VMEM((tile, LANES), F32),
                        pltpu.VMEM((tile, D_MODEL), F32),
                        *[pltpu.VMEM(w.shape, BF16) for w in experts],
                        pltpu.VMEM((2,) + w_gate.shape[1:], F32), pltpu.VMEM((2,) + w_down.shape[1:], F32),
                        pltpu.SemaphoreType.DMA((7,))],
        compiler_params=pltpu.CompilerParams(dimension_semantics=("arbitrary",),
                                             vmem_limit_bytes=VMEM_LIMIT_BYTES),
        name="layer_prompt",
    )(x2d, *consts, sinks, *tail, *experts)


def _run_sample(x2d, ck_t, cv_t, lw, wbd, bsp, sinks, dec, wg_bf, wu_bf, wd_bf):
    n_tok = x2d.shape[0]
    n_seq = ck_t.shape[0]
    tile = SAMPLE_TILE
    nb = tile // dec
    bc, bn = _sample_bias(nb, dec)
    head = [lw[n] for n in _HEAD_NAMES]
    tail = [lw[n] for n in _TAIL_NAMES] + [wg_bf, wu_bf, wd_bf.reshape(N_EXPERTS * D_EXPERT, D_MODEL)]
    consts = head + [wbd, bsp, jnp.asarray(bc), jnp.asarray(bn)]
    cache_spec = pl.BlockSpec((nb, KV_WIDTH, WINDOW), lambda i: (i, 0, 0))
    in_specs = ([_row_spec(tile, D_MODEL), cache_spec, cache_spec] + [_const_spec(a.shape) for a in consts]
                + [_smem_spec()] + [_const_spec(a.shape) for a in tail])
    out_shape = (jax.ShapeDtypeStruct((n_tok, D_MODEL), F32),
                 jax.ShapeDtypeStruct((n_seq, KV_WIDTH, WINDOW), F32),
                 jax.ShapeDtypeStruct((n_seq, KV_WIDTH, WINDOW), F32),
                 jax.ShapeDtypeStruct((n_tok * A_GROUPS, A_CH), F32))
    out_specs = (_row_spec(tile, D_MODEL), cache_spec, cache_spec, _row_spec(tile * A_GROUPS, A_CH))
    kern = functools.partial(_sample_kernel, tile=tile, nb=nb, dec=dec)
    return pl.pallas_call(
        kern,
        out_shape=out_shape,
        grid=(n_tok // tile,),
        in_specs=in_specs,
        out_specs=out_specs,
        scratch_shapes=[pltpu.VMEM((tile, N_EXPERTS * D_EXPERT), BF16)],
        compiler_params=pltpu.CompilerParams(dimension_semantics=("arbitrary",),
                                             vmem_limit_bytes=VMEM_LIMIT_BYTES),
        name="layer_sample",
    )(x2d, ck_t, cv_t, *consts, sinks, *tail)


def _positions_last(c):
    b, w = c.shape[:2]
    return jnp.transpose(c, (0, 2, 3, 1)).reshape(b, KV_WIDTH, w)


def _positions_first(c_t):
    b, _, w = c_t.shape
    return jnp.transpose(c_t.reshape(b, N_KV, HEAD_DIM, w), (0, 3, 1, 2))


def kernel(x_prompt, x_sample, cache_k, cache_v, g_attn_norm, w_in, g_v_a, w_spatial, b_spatial, g_q, g_k, attn_sinks, g_out_a, g_out_b, w_out, g_ffn_norm, w_coarse, b_coarse, w_fine, b_fine, w_gate, w_up, w_down):
    depth = w_in.shape[0]
    batch, seq, _ = x_prompt.shape
    dbatch, dec, _ = x_sample.shape
    win = cache_k.shape[2]
    assert win == WINDOW and seq % PROMPT_TILE == 0 and (dbatch * dec) % SAMPLE_TILE == 0
    assert PAIR % dec == 0 and SAMPLE_TILE % dec == 0 and LANES % dec == 0

    hp = x_prompt.reshape(batch * seq, D_MODEL)
    hs = x_sample.reshape(dbatch * dec, D_MODEL)
    kp_l, vp_l, ks_l, vs_l, cv_l = [], [], [], [], []
    for l in range(depth):
        lw = _layer_weights(l, g_attn_norm, w_in, g_v_a, g_q, g_k, g_out_a, g_out_b, w_out, g_ffn_norm,
                            w_coarse, b_coarse, w_fine, b_fine)
        sinks = attn_sinks[l].astype(F32)
        wbd_p, bsp_p = _spatial_tables(w_spatial[l], b_spatial[l], CHUNK, PROMPT_TILE)
        wbd_s, bsp_s = _spatial_tables(w_spatial[l], b_spatial[l], dec, SAMPLE_TILE)

        hp, kp, vp, wg_bf, wu_bf, wd_bf = _run_prompt(hp, lw, wbd_p, bsp_p, sinks, seq,
                                                      w_gate[l], w_up[l], w_down[l])
        kp_l.append(kp.reshape(batch, WINDOW, N_KV, HEAD_DIM))
        vp_l.append(vp.reshape(batch, WINDOW, N_KV, HEAD_DIM))

        hs, wk, wv, va = _run_sample(hs, _positions_last(cache_k[l]), _positions_last(cache_v[l]),
                                     lw, wbd_s, bsp_s, sinks, dec, wg_bf, wu_bf, wd_bf)
        ks_l.append(_positions_first(wk))
        vs_l.append(_positions_first(wv))
        cv_l.append(va.reshape(dbatch, dec, A_GROUPS, A_CH))

    return (hp.reshape(batch, seq, D_MODEL), hs.reshape(dbatch, dec, D_MODEL),
            jnp.stack(kp_l, axis=0), jnp.stack(vp_l, axis=0),
            jnp.stack(ks_l, axis=0), jnp.stack(vs_l, axis=0), jnp.stack(cv_l, axis=0))
```

```python
import functools

import numpy as np
import jax
import jax.numpy as jnp
from jax import lax
from jax.experimental import pallas as pl
from jax.experimental.pallas import tpu as pltpu

D_MODEL = 1024
CHUNK = 128
A_GROUPS = 4
A_WIDTH = 512
A_CH = 128
N_HEADS = 8
N_KV = 2
Q_PER_KV = 4
HEAD_DIM = 64
B_WIDTH = 512
KV_WIDTH = 128
WINDOW = 128
IN_COLS = 2 * A_WIDTH + B_WIDTH + 2 * KV_WIDTH
N_EXPERT_GROUPS = 4
EXPERTS_PER_GROUP = 4
N_EXPERTS = 16
D_EXPERT = 256
EPS = 1e-6

LANES = 128
PAIR = 2 * CHUNK
ROUTER_LANES = 128
PROMPT_TILE = 512
SAMPLE_TILE = 128
VMEM_LIMIT_BYTES = 60 * 1024 * 1024

F32 = jnp.float32
BF16 = jnp.bfloat16
NEG_INF = float("-inf")
LOG2E = 1.4426950408889634
Q_SCALE = (HEAD_DIM ** -0.5) * LOG2E


def _slopes():
    return np.array([2.0 ** (-8.0 * (h + 1) / N_HEADS) for h in range(N_HEADS)], np.float64)


def _prompt_bias():
    t = np.arange(WINDOW)[:, None]
    s = np.arange(2 * WINDOW)[None, :]
    dist = t + WINDOW - s
    valid = (dist >= 0) & (dist < WINDOW)
    sl = _slopes()
    out = np.full((N_KV, Q_PER_KV * WINDOW, 2 * WINDOW), -np.inf, np.float32)
    for kh in range(N_KV):
        for g in range(Q_PER_KV):
            b = np.where(valid, -sl[kh * Q_PER_KV + g] * LOG2E * dist, -np.inf)
            out[kh, g * WINDOW:(g + 1) * WINDOW] = b
    return out


def _sample_bias(nb, dec):
    sl = _slopes()
    t = np.arange(dec)[:, None]
    j = np.arange(WINDOW)[None, :]
    dist_c = t + WINDOW - j
    valid_c = (dist_c >= 0) & (dist_c < WINDOW)
    bc = np.full((N_HEADS * dec, WINDOW), -np.inf, np.float32)
    tp = np.arange(dec)[None, :]
    dist_n = t - tp
    valid_n = dist_n >= 0
    bn = np.full((nb, N_HEADS * dec, nb * dec), -np.inf, np.float32)
    for h in range(N_HEADS):
        bc[h * dec:(h + 1) * dec] = np.where(valid_c, -sl[h] * LOG2E * dist_c, -np.inf)
        blk = np.where(valid_n, -sl[h] * LOG2E * dist_n, -np.inf)
        for b in range(nb):
            bn[b, h * dec:(h + 1) * dec, b * dec:(b + 1) * dec] = blk
    return bc, bn


def _seg_ones(width, seg):
    i = np.arange(width)
    return (i[:, None] // seg == i[None, :] // seg).astype(np.float32)


def _causal_block_mask(period, size):
    i = np.arange(size)
    same = i[:, None] // period == i[None, :] // period
    return (same & (i[None, :] % period <= i[:, None] % period)).astype(np.float32)


def _dot(a, b):
    return jnp.dot(a, b, preferred_element_type=F32)


def _dot_nt(a, b):
    return lax.dot_general(a, b, (((1,), (1,)), ((), ())), preferred_element_type=F32)


def _rms(x, g):
    ms = jnp.mean(x * x, axis=-1, keepdims=True)
    return (x * lax.rsqrt(ms + EPS)) * g


def _seg_rms(x, ones_bf, g, seg):
    ss = _dot((x * x).astype(BF16), ones_bf)
    return (x * lax.rsqrt(ss * (1.0 / seg) + EPS)) * g


GELU_K0 = -2.0 * 0.7978845608028654 * LOG2E
GELU_K1 = GELU_K0 * 0.044715


def _gelu_tanh(x):
    t = (x * x) * GELU_K1 + GELU_K0
    return x * (1.0 / (1.0 + jnp.exp2(x * t)))


VEC_ATTN = 0
VEC_VA = VEC_ATTN + D_MODEL
VEC_Q = VEC_VA + A_WIDTH
VEC_K = VEC_Q + B_WIDTH
VEC_OUT = VEC_K + KV_WIDTH
VEC_FFN = VEC_OUT + D_MODEL
VEC_ROUTER_BIAS = VEC_FFN + D_MODEL
VEC_LANES = VEC_ROUTER_BIAS + ROUTER_LANES


def _vec(vec_ref, start, width):
    return vec_ref[:, start:start + width]


def _project(x, vec_ref, w_in_ref, segq_ref, segk_ref):
    xn = _rms(x, _vec(vec_ref, VEC_ATTN, D_MODEL)).astype(BF16)
    z = _dot(xn, w_in_ref[...])
    i1, i2, i3, i4 = A_WIDTH, 2 * A_WIDTH, 2 * A_WIDTH + B_WIDTH, 2 * A_WIDTH + B_WIDTH + KV_WIDTH
    u = _gelu_tanh(z[:, :i1])
    va_pre = _gelu_tanh(z[:, i1:i2])
    g_va = _vec(vec_ref, VEC_VA, A_WIDTH)
    va = jnp.concatenate(
        [_rms(va_pre[:, g * A_CH:(g + 1) * A_CH], g_va[:, g * A_CH:(g + 1) * A_CH]) for g in range(A_GROUPS)],
        axis=1)
    q = _seg_rms(z[:, i2:i3], segq_ref[...], _vec(vec_ref, VEC_Q, B_WIDTH), HEAD_DIM)
    k = _seg_rms(z[:, i3:i4], segk_ref[...], _vec(vec_ref, VEC_K, KV_WIDTH), HEAD_DIM)
    v = z[:, i4:]
    return u, va, q, k, v


def _chunk_mlp(va_bf, wsp_ref, bsp_ref, tile):
    pair = wsp_ref.shape[1]
    n_pairs = tile // pair
    outs = []
    for g in range(A_GROUPS):
        cols = [va_bf[p * pair:(p + 1) * pair, g * A_CH:(g + 1) * A_CH] for p in range(n_pairs)]
        rhs = cols[0] if n_pairs == 1 else jnp.concatenate(cols, axis=1)
        o = _dot(wsp_ref[g], rhs)
        rows = [o[:, p * A_CH:(p + 1) * A_CH] for p in range(n_pairs)]
        outs.append(rows[0] if n_pairs == 1 else jnp.concatenate(rows, axis=0))
    mixed = jnp.concatenate(outs, axis=1)
    bias = bsp_ref[...]
    reps = tile // bias.shape[0]
    return mixed + (bias if reps == 1 else jnp.concatenate([bias] * reps, axis=0))


def _lane_lo():
    return lax.broadcasted_iota(jnp.int32, (1, LANES), 1) < HEAD_DIM


def _swap_halves(x):
    return pltpu.roll(x, HEAD_DIM, axis=1)


def _head_lhs(q_groups, q_groups_swapped, head, lo):
    j, half = divmod(head, 2)
    kv = head // Q_PER_KV
    src = q_groups[j] if half == kv else q_groups_swapped[j]
    return jnp.where(lo if kv == 0 else jnp.logical_not(lo), src, 0.0)


def _merge_heads(o_heads, lo):
    groups = []
    for j in range(N_HEADS // 2):
        kv = (2 * j) // Q_PER_KV
        if kv == 0:
            groups.append(jnp.where(lo, o_heads[2 * j], _swap_halves(o_heads[2 * j + 1])))
        else:
            groups.append(jnp.where(lo, _swap_halves(o_heads[2 * j]), o_heads[2 * j + 1]))
    return jnp.concatenate(groups, axis=1)


def _dup_halves(x, lo):
    xs = _swap_halves(x)
    return jnp.where(lo, x, xs).astype(BF16), jnp.where(lo, xs, x).astype(BF16)


def _swa_prompt(q, k, v, kprev_ref, vprev_ref, bias_ref, sinks_ref, is_first, tile):
    lo = _lane_lo()
    hi = jnp.logical_not(lo)
    col = lax.broadcasted_iota(jnp.int32, (WINDOW, 2 * WINDOW), 1)
    first_mask = jnp.where(col < WINDOW, jnp.where(is_first, NEG_INF, 0.0).astype(F32), 0.0)
    kd = _dup_halves(k, lo)
    vd = _dup_halves(v, lo)
    nblk = tile // WINDOW
    rows_out = []
    for i in range(nblk):
        r0, r1 = i * WINDOW, (i + 1) * WINDOW
        qg = [q[r0:r1, j * LANES:(j + 1) * LANES] for j in range(N_HEADS // 2)]
        o_heads = []
        for kh in range(N_KV):
            if i == 0:
                kp, vp = kprev_ref[kh], vprev_ref[kh]
            else:
                kp, vp = kd[kh][r0 - WINDOW:r0], vd[kh][r0 - WINDOW:r0]
            kb = jnp.concatenate([kp, kd[kh][r0:r1]], axis=0)
            vb = jnp.concatenate([vp, vd[kh][r0:r1]], axis=0)
            heads = [kh * Q_PER_KV + g for g in range(Q_PER_KV)]
            lhs = jnp.concatenate(
                [jnp.where(lo if h % 2 == 0 else hi, qg[h // 2], 0.0) for h in heads], axis=0).astype(BF16)
            s = _dot_nt(lhs, kb) + bias_ref[kh]
            ps, linvs = [], []
            for g in range(Q_PER_KV):
                sg = s[g * WINDOW:(g + 1) * WINDOW]
                if i == 0:
                    sg = sg + first_mask
                sink = sinks_ref[heads[g]] * LOG2E
                mg = jnp.maximum(jnp.max(sg, axis=-1, keepdims=True), sink)
                pg = jnp.exp2(sg - mg)
                lg = jnp.sum(pg, axis=-1, keepdims=True) + jnp.exp2(sink - mg)
                ps.append(pg.astype(BF16))
                linvs.append(1.0 / lg)
            o = _dot(jnp.concatenate(ps, axis=0), vb)
            o_heads += [o[g * WINDOW:(g + 1) * WINDOW] * linvs[g] for g in range(Q_PER_KV)]
        rows_out.append(jnp.concatenate(
            [jnp.where(lo, o_heads[2 * j], o_heads[2 * j + 1]) for j in range(N_HEADS // 2)], axis=1))
    for kh in range(N_KV):
        kprev_ref[kh] = kd[kh][tile - WINDOW:tile]
        vprev_ref[kh] = vd[kh][tile - WINDOW:tile]
    return jnp.concatenate(rows_out, axis=0)


def _shift_window(cache_ref, new_t, out_ref, nb, dec):
    lane = lax.broadcasted_iota(jnp.int32, (1, WINDOW), 1)
    keep = lane < WINDOW - dec
    per_tile = LANES // dec
    for b in range(nb):
        src = new_t[:, (b // per_tile) * LANES:(b // per_tile + 1) * LANES]
        new_cols = pltpu.roll(src, (WINDOW - dec - (b % per_tile) * dec) % LANES, axis=1)
        old = pltpu.roll(cache_ref[b], WINDOW - dec, axis=1)
        out_ref[b] = jnp.where(keep, old, new_cols)


def _swa_sample(q, k, v, ck_ref, cv_ref, wk_ref, wv_ref, bias_c_ref, bias_n_ref, sinks_ref, nb, dec):
    lo = _lane_lo()
    qg = [q[:, j * LANES:(j + 1) * LANES] for j in range(N_HEADS // 2)]
    qgs = [_swap_halves(x) for x in qg]
    qb = jnp.concatenate(
        [_head_lhs(qg, qgs, h, lo).reshape(nb, dec, LANES) for h in range(N_HEADS)], axis=1)
    rows = N_HEADS * dec
    qb_bf = qb.astype(BF16)
    k_bf = k.astype(BF16)
    v_bf = v.astype(BF16)
    ck = ck_ref[...].astype(BF16)
    cv = cv_ref[...].astype(BF16)
    sc = jnp.einsum('bqc,bcp->bqp', qb_bf, ck, preferred_element_type=F32) + bias_c_ref[...][None]
    sn = _dot_nt(qb_bf.reshape(nb * rows, LANES), k_bf).reshape(nb, rows, nb * dec) + bias_n_ref[...]
    sink = jnp.concatenate(
        [jnp.full((1, dec, 1), sinks_ref[h] * LOG2E, F32) for h in range(N_HEADS)], axis=1)
    m = jnp.maximum(jnp.max(sc, axis=-1, keepdims=True), jnp.max(sn, axis=-1, keepdims=True))
    m = jnp.maximum(m, sink)
    pc = jnp.exp2(sc - m)
    pn = jnp.exp2(sn - m)
    l = jnp.sum(pc, axis=-1, keepdims=True) + jnp.sum(pn, axis=-1, keepdims=True) + jnp.exp2(sink - m)
    oc = jnp.einsum('bqp,bcp->bqc', pc.astype(BF16), cv, preferred_element_type=F32)
    on = _dot(pn.reshape(nb * rows, nb * dec).astype(BF16), v_bf).reshape(nb, rows, LANES)
    o = (oc + on) * (1.0 / l)
    o_heads = [o[:, h * dec:(h + 1) * dec, :].reshape(nb * dec, LANES) for h in range(N_HEADS)]
    _shift_window(ck_ref, k.T, wk_ref, nb, dec)
    _shift_window(cv_ref, v.T, wv_ref, nb, dec)
    return _merge_heads(o_heads, lo)


def _route(hn_bf, wr_ref, vec_ref):
    logits = _dot(hn_bf, wr_ref[...]) + _vec(vec_ref, VEC_ROUTER_BIAS, ROUTER_LANES)
    lane = lax.broadcasted_iota(jnp.int32, logits.shape, 1)
    big = jnp.int32(ROUTER_LANES)
    lc = jnp.where(lane < N_EXPERT_GROUPS, logits, NEG_INF)
    mx = jnp.max(lc, axis=-1, keepdims=True)
    g_idx = jnp.min(jnp.where(lc == mx, lane, big), axis=-1, keepdims=True)
    p_g = 1.0 / jnp.sum(jnp.exp(lc - mx), axis=-1, keepdims=True)
    e_lane = lane - N_EXPERT_GROUPS
    in_group = (e_lane >= 0) & (e_lane < N_EXPERTS) & ((e_lane >> 2) == g_idx)
    lf = jnp.where(in_group, logits, NEG_INF)
    v1 = jnp.max(lf, axis=-1, keepdims=True)
    i1 = jnp.min(jnp.where(lf == v1, lane, big), axis=-1, keepdims=True)
    lf2 = jnp.where(lane == i1, NEG_INF, lf)
    v2 = jnp.max(lf2, axis=-1, keepdims=True)
    i2 = jnp.min(jnp.where(lf2 == v2, lane, big), axis=-1, keepdims=True)
    e = jnp.exp(v2 - v1)
    w1 = p_g / (1.0 + e)
    w2 = w1 * e
    return lane, g_idx, i1, i2, w1, w2


def _silu(x):
    return x * (1.0 / (1.0 + jnp.exp2(x * (-LOG2E))))


def _moe_dense(h, vec_ref, wr_ref, wg_ref, wu_ref, wd_ref, hcat_ref):
    hn_bf = _rms(h, _vec(vec_ref, VEC_FFN, D_MODEL)).astype(BF16)
    lane, _, i1, i2, w1, w2 = _route(hn_bf, wr_ref, vec_ref)
    combine = jnp.where(lane == i1, w1, jnp.where(lane == i2, w2, 0.0))
    for e in range(N_EXPERTS):
        gate = _dot(hn_bf, wg_ref[e])
        up = _dot(hn_bf, wu_ref[e])
        c = combine[:, N_EXPERT_GROUPS + e:N_EXPERT_GROUPS + e + 1]
        hcat_ref[:, e * D_EXPERT:(e + 1) * D_EXPERT] = (_silu(gate) * up * c).astype(BF16)
    return _dot(hcat_ref[...], wd_ref[...])


GROUP_LANE = EXPERTS_PER_GROUP
LOW_SHIFT = 8
SORT_BLOCK = 128


def _moe_grouped(h, vec_ref, wr_ref, wg_ref, wu_ref, wd_ref, lstrict_ref, xs_ref, es_ref, ys_ref, tile):
    hn_bf = _rms(h, _vec(vec_ref, VEC_FFN, D_MODEL)).astype(BF16)
    lane, g_idx, i1, i2, w1, w2 = _route(hn_bf, wr_ref, vec_ref)

    onehot = lane == g_idx
    gmat = jnp.where(onehot, 1.0, 0.0)
    before = _dot(lstrict_ref[...], gmat.astype(BF16))
    rank = jnp.sum(jnp.where(onehot, before, 0.0), axis=-1, keepdims=True)
    counts = jnp.sum(gmat, axis=0, keepdims=True)
    lane1 = lax.broadcasted_iota(jnp.int32, counts.shape, 1)
    offs = [jnp.float32(0.0)]
    for g in range(N_EXPERT_GROUPS - 1):
        offs.append(offs[-1] + jnp.sum(jnp.where(lane1 == g, counts, 0.0)))
    offv = jnp.zeros_like(rank)
    for g in range(1, N_EXPERT_GROUPS):
        offv = offv + jnp.where(g_idx == g, offs[g], 0.0)
    pos = (rank + offv).astype(jnp.int32)
    dest = lax.broadcasted_iota(jnp.int32, (tile, tile), 1)
    pt = jnp.where(dest == pos, 1.0, 0.0).astype(BF16)

    j1 = (i1 - N_EXPERT_GROUPS) & (EXPERTS_PER_GROUP - 1)
    j2 = (i2 - N_EXPERT_GROUPS) & (EXPERTS_PER_GROUP - 1)
    gates = jnp.where(lane == j1, w1, jnp.where(lane == j2, w2, 0.0))
    gates_hi = gates.astype(BF16).astype(F32)
    gates_lo = (gates - gates_hi).astype(BF16).astype(F32)
    side = gates_hi + pltpu.roll(gates_lo, LOW_SHIFT, axis=1) + jnp.where(lane == GROUP_LANE, g_idx.astype(F32), 0.0)
    x_ext = jnp.concatenate([hn_bf, side.astype(BF16)], axis=1)
    srt = lax.dot_general(pt, x_ext, (((0,), (0,)), ((), ())), preferred_element_type=F32)
    xs_ref[...] = srt[:, :D_MODEL].astype(BF16)
    es = srt[:, D_MODEL:]
    es_ref[...] = es + pltpu.roll(es, LANES - LOW_SHIFT, axis=1)

    def group_of(row):
        g = jnp.int32(0)
        for k in range(1, N_EXPERT_GROUPS):
            g = g + (jnp.float32(row) >= offs[k]).astype(jnp.int32)
        return g

    def group_pass(rows, g, keep, accumulate):
        xb = xs_ref[rows, :]
        eb = es_ref[rows, :]
        mine = jnp.where(eb[:, GROUP_LANE:GROUP_LANE + 1] == jnp.asarray(g).astype(F32), keep, 0.0)
        parts = []
        for j in range(EXPERTS_PER_GROUP):
            e = g * EXPERTS_PER_GROUP + j
            gate = _dot(xb, wg_ref[e])
            up = _dot(xb, wu_ref[e])
            parts.append((_silu(gate) * up * (eb[:, j:j + 1] * mine)).astype(BF16))
        wd = wd_ref[pl.ds(g * EXPERTS_PER_GROUP, EXPERTS_PER_GROUP)]
        out = _dot(jnp.concatenate(parts, axis=1), wd.reshape(EXPERTS_PER_GROUP * D_EXPERT, D_MODEL))
        if accumulate:
            ys_ref[rows, :] += out
        else:
            ys_ref[rows, :] = out

    nblk = tile // SORT_BLOCK
    for b in range(nblk):
        group_pass(pl.ds(b * SORT_BLOCK, SORT_BLOCK), group_of(b * SORT_BLOCK), 1.0, False)
    for g in range(1, N_EXPERT_GROUPS):
        start = offs[g].astype(jnp.int32)
        blk = jnp.minimum(lax.div(start, jnp.int32(SORT_BLOCK)), nblk - 1)
        inside = jnp.where(lax.rem(start, jnp.int32(SORT_BLOCK)) != 0, 1.0, 0.0)
        group_pass(pl.ds(pl.multiple_of(blk * SORT_BLOCK, SORT_BLOCK), SORT_BLOCK), g, inside, True)
    return _dot(pt, ys_ref[...].astype(BF16))


def _merge(x, a_out, b_out, vec_ref, w_out_ref):
    g_out = _vec(vec_ref, VEC_OUT, D_MODEL)
    mix = jnp.concatenate(
        [_rms(a_out, g_out[:, :A_WIDTH]), _rms(b_out, g_out[:, A_WIDTH:])], axis=1).astype(BF16)
    return x + _dot(mix, w_out_ref[...])


def _stage_expert_weights(wg_hbm, wu_hbm, wd_hbm, wg_s, wu_s, wd_s, st_a, st_d, sems):
    def gate_cp(e):
        return pltpu.make_async_copy(wg_hbm.at[e], st_a.at[0], sems.at[0])

    def up_cp(e):
        return pltpu.make_async_copy(wu_hbm.at[e], st_a.at[1], sems.at[1])

    def down_cp(e, slot):
        return pltpu.make_async_copy(wd_hbm.at[e], st_d.at[slot], sems.at[2 + slot])

    gate_cp(0).start()
    up_cp(0).start()
    down_cp(0, 0).start()

    def pair(p, carry):
        for slot in (0, 1):
            e = 2 * p + slot
            nxt = e + 1
            more = nxt < N_EXPERTS

            @pl.when(more)
            def _():
                down_cp(nxt, 1 - slot).start()

            gate_cp(e).wait()
            wg_s[e] = st_a[0].astype(BF16)

            @pl.when(more)
            def _():
                gate_cp(nxt).start()

            up_cp(e).wait()
            wu_s[e] = st_a[1].astype(BF16)

            @pl.when(more)
            def _():
                up_cp(nxt).start()

            down_cp(e, slot).wait()
            wd_s[e] = st_d[slot].astype(BF16)
        return carry

    lax.fori_loop(0, N_EXPERTS // 2, pair, 0)


def _bf16_weight_copies(wg_s, wu_s, wd_s, wg_o, wu_o, wd_o, sems):
    return (pltpu.make_async_copy(wg_s, wg_o, sems.at[4]),
            pltpu.make_async_copy(wu_s, wu_o, sems.at[5]),
            pltpu.make_async_copy(wd_s, wd_o, sems.at[6]))


def _prompt_kernel(x_ref, vec_ref, w_in_ref, segq_ref, segk_ref, wsp_ref, bsp_ref, bias_ref, sinks_ref,
                   w_out_ref, wr_ref, lstrict_ref, wg_hbm, wu_hbm, wd_hbm,
                   y_ref, kwin_ref, vwin_ref, wg_o, wu_o, wd_o,
                   kprev_ref, vprev_ref, xs_ref, es_ref, ys_ref, wg_ref, wu_ref, wd_ref, st_a, st_d, sems,
                   *, tile, tiles_per_seq):
    step = pl.program_id(0)
    is_first = (step % tiles_per_seq) == 0

    @pl.when(step == 0)
    def _():
        _stage_expert_weights(wg_hbm, wu_hbm, wd_hbm, wg_ref, wu_ref, wd_ref, st_a, st_d, sems)
        for cp in _bf16_weight_copies(wg_ref, wu_ref, wd_ref, wg_o, wu_o, wd_o, sems):
            cp.start()

    @pl.when(step == pl.num_programs(0) - 1)
    def _():
        for cp in _bf16_weight_copies(wg_ref, wu_ref, wd_ref, wg_o, wu_o, wd_o, sems):
            cp.wait()

    @pl.when(is_first)
    def _():
        kprev_ref[...] = jnp.zeros_like(kprev_ref)
        vprev_ref[...] = jnp.zeros_like(vprev_ref)

    x = x_ref[...]
    u, va, q, k, v = _project(x, vec_ref, w_in_ref, segq_ref, segk_ref)
    kwin_ref[...] = k[tile - WINDOW:tile].T
    vwin_ref[...] = v[tile - WINDOW:tile].T
    a_out = u * _chunk_mlp(va.astype(BF16), wsp_ref, bsp_ref, tile)
    b_out = _swa_prompt(q * Q_SCALE, k, v, kprev_ref, vprev_ref, bias_ref, sinks_ref, is_first, tile)
    h = _merge(x, a_out, b_out, vec_ref, w_out_ref)
    y_ref[...] = h + _moe_grouped(h, vec_ref, wr_ref, wg_ref, wu_ref, wd_ref, lstrict_ref,
                                  xs_ref, es_ref, ys_ref, tile)


def _sample_kernel(x_ref, ck_ref, cv_ref, vec_ref, w_in_ref, segq_ref, segk_ref, wsp_ref, bsp_ref,
                   bias_c_ref, bias_n_ref, sinks_ref, w_out_ref, wr_ref, wg_ref, wu_ref, wd_ref,
                   y_ref, wk_ref, wv_ref, va_ref, hcat_ref, *, tile, nb, dec):
    x = x_ref[...]
    u, va, q, k, v = _project(x, vec_ref, w_in_ref, segq_ref, segk_ref)
    for g in range(A_GROUPS):
        va_ref[pl.ds(g, tile, stride=A_GROUPS), :] = va[:, g * A_CH:(g + 1) * A_CH]
    a_out = u * _chunk_mlp(va.astype(BF16), wsp_ref, bsp_ref, tile)
    b_out = _swa_sample(q * Q_SCALE, k, v, ck_ref, cv_ref, wk_ref, wv_ref,
                        bias_c_ref, bias_n_ref, sinks_ref, nb, dec)
    h = _merge(x, a_out, b_out, vec_ref, w_out_ref)
    y_ref[...] = h + _moe_dense(h, vec_ref, wr_ref, wg_ref, wu_ref, wd_ref, hcat_ref)


def _const_spec(shape):
    nd = len(shape)
    return pl.BlockSpec(shape, lambda i: (0,) * nd, pipeline_mode=pl.Buffered(1))


def _row_spec(tile, width):
    return pl.BlockSpec((tile, width), lambda i: (i, 0))


def _smem_spec():
    return pl.BlockSpec(memory_space=pltpu.SMEM)


def _layer_weights(l, g_attn_norm, w_in, g_v_a, g_q, g_k, g_out_a, g_out_b, w_out, g_ffn_norm,
                   w_coarse, b_coarse, w_fine, b_fine):
    pad = ROUTER_LANES - N_EXPERT_GROUPS - N_EXPERTS
    wr = jnp.concatenate([w_coarse[l], w_fine[l], jnp.zeros((D_MODEL, pad), F32)], axis=1)
    vec = jnp.concatenate([
        g_attn_norm[l], g_v_a[l].reshape(A_WIDTH), jnp.tile(g_q[l], N_HEADS), jnp.tile(g_k[l], N_KV),
        g_out_a[l], g_out_b[l], g_ffn_norm[l], b_coarse[l], b_fine[l], jnp.zeros((pad,), F32)])
    return dict(
        vec=vec.reshape(1, VEC_LANES),
        w_in=w_in[l].astype(BF16),
        segq=jnp.asarray(_seg_ones(B_WIDTH, HEAD_DIM), BF16),
        segk=jnp.asarray(_seg_ones(KV_WIDTH, HEAD_DIM), BF16),
        w_out=w_out[l].astype(BF16),
        wr=wr.astype(BF16),
    )


def _spatial_tables(ws, bs, period, tile):
    pair = min(PAIR, tile)
    reps = pair // period
    wbd = (jnp.tile(ws[:, :period, :period], (1, reps, reps)) * _causal_block_mask(period, pair)).astype(BF16)
    bsp = jnp.repeat(bs[:, :period].T, A_CH, axis=1)
    return wbd, bsp


_HEAD_NAMES = ("vec", "w_in", "segq", "segk")
_TAIL_NAMES = ("w_out", "wr")


def _lower_triangle(n):
    return jnp.asarray(np.tril(np.ones((n, n), np.float32), -1), BF16)


def _run_prompt(x2d, lw, wbd, bsp, sinks, seq_len, w_gate, w_up, w_down):
    n_tok = x2d.shape[0]
    tile = PROMPT_TILE
    tiles_per_seq = seq_len // tile
    n_seq = n_tok // seq_len
    bias = jnp.asarray(_prompt_bias())
    head = [lw[n] for n in _HEAD_NAMES]
    tail = [lw[n] for n in _TAIL_NAMES]
    tail.append(_lower_triangle(tile))
    experts = [w_gate, w_up, w_down]
    consts = head + [wbd, bsp, bias]
    any_spec = pl.BlockSpec(memory_space=pl.ANY)
    in_specs = ([_row_spec(tile, D_MODEL)] + [_const_spec(a.shape) for a in consts] + [_smem_spec()]
                + [_const_spec(a.shape) for a in tail] + [any_spec] * len(experts))
    win_spec = pl.BlockSpec((KV_WIDTH, WINDOW), lambda i: (i // tiles_per_seq, 0))
    out_shape = (jax.ShapeDtypeStruct((n_tok, D_MODEL), F32),
                 jax.ShapeDtypeStruct((n_seq * KV_WIDTH, WINDOW), F32),
                 jax.ShapeDtypeStruct((n_seq * KV_WIDTH, WINDOW), F32),
                 *[jax.ShapeDtypeStruct(w.shape, BF16) for w in experts])
    out_specs = (_row_spec(tile, D_MODEL), win_spec, win_spec, any_spec, any_spec, any_spec)
    kern = functools.partial(_prompt_kernel, tile=tile, tiles_per_seq=tiles_per_seq)
    return pl.pallas_call(
        kern,
        out_shape=out_shape,
        grid=(n_tok // tile,),
        in_specs=in_specs,
        out_specs=out_specs,
        scratch_shapes=[pltpu.VMEM((N_KV, WINDOW, KV_WIDTH), BF16), pltpu.VMEM((N_KV, WINDOW, KV_WIDTH), BF16),
                        pltpu.VMEM((tile, D_MODEL), BF16), pltpu.VMEM((tile, LANES), F32),
                        pltpu.VMEM((tile, D_MODEL), F32),
                        *[pltpu.VMEM(w.shape, BF16) for w in experts],
                        pltpu.VMEM((2,) + w_gate.shape[1:], F32), pltpu.VMEM((2,) + w_down.shape[1:], F32),
                        pltpu.SemaphoreType.DMA((7,))],
        compiler_params=pltpu.CompilerParams(dimension_semantics=("arbitrary",),
                                             vmem_limit_bytes=VMEM_LIMIT_BYTES),
        name="layer_prompt",
    )(x2d, *consts, sinks, *tail, *experts)


def _run_sample(x2d, ck_t, cv_t, lw, wbd, bsp, sinks, dec, wg_bf, wu_bf, wd_bf):
    n_tok = x2d.shape[0]
    n_seq = ck_t.shape[0]
    tile = SAMPLE_TILE
    nb = tile // dec
    bc, bn = _sample_bias(nb, dec)
    head = [lw[n] for n in _HEAD_NAMES]
    tail = [lw[n] for n in _TAIL_NAMES] + [wg_bf, wu_bf, wd_bf.reshape(N_EXPERTS * D_EXPERT, D_MODEL)]
    consts = head + [wbd, bsp, jnp.asarray(bc), jnp.asarray(bn)]
    cache_spec = pl.BlockSpec((nb, KV_WIDTH, WINDOW), lambda i: (i, 0, 0))
    in_specs = ([_row_spec(tile, D_MODEL), cache_spec, cache_spec] + [_const_spec(a.shape) for a in consts]
                + [_smem_spec()] + [_const_spec(a.shape) for a in tail])
    out_shape = (jax.ShapeDtypeStruct((n_tok, D_MODEL), F32),
                 jax.ShapeDtypeStruct((n_seq, KV_WIDTH, WINDOW), F32),
                 jax.ShapeDtypeStruct((n_seq, KV_WIDTH, WINDOW), F32),
                 jax.ShapeDtypeStruct((n_tok * A_GROUPS, A_CH), F32))
    out_specs = (_row_spec(tile, D_MODEL), cache_spec, cache_spec, _row_spec(tile * A_GROUPS, A_CH))
    kern = functools.partial(_sample_kernel, tile=tile, nb=nb, dec=dec)
    return pl.pallas_call(
        kern,
        out_shape=out_shape,
        grid=(n_tok // tile,),
        in_specs=in_specs,
        out_specs=out_specs,
        scratch_shapes=[pltpu.VMEM((tile, N_EXPERTS * D_EXPERT), BF16)],
        compiler_params=pltpu.CompilerParams(dimension_semantics=("arbitrary",),
                                             vmem_limit_bytes=VMEM_LIMIT_BYTES),
        name="layer_sample",
    )(x2d, ck_t, cv_t, *consts, sinks, *tail)


def _positions_last(c):
    b, w = c.shape[:2]
    return jnp.transpose(c, (0, 2, 3, 1)).reshape(b, KV_WIDTH, w)


def _positions_first(c_t):
    b, _, w = c_t.shape
    return jnp.transpose(c_t.reshape(b, N_KV, HEAD_DIM, w), (0, 3, 1, 2))


def kernel(x_prompt, x_sample, cache_k, cache_v, g_attn_norm, w_in, g_v_a, w_spatial, b_spatial, g_q, g_k, attn_sinks, g_out_a, g_out_b, w_out, g_ffn_norm, w_coarse, b_coarse, w_fine, b_fine, w_gate, w_up, w_down):
    depth = w_in.shape[0]
    batch, seq, _ = x_prompt.shape
    dbatch, dec, _ = x_sample.shape
    win = cache_k.shape[2]
    assert win == WINDOW and seq % PROMPT_TILE == 0 and (dbatch * dec) % SAMPLE_TILE == 0
    assert PAIR % dec == 0 and SAMPLE_TILE % dec == 0 and LANES % dec == 0

    hp = x_prompt.reshape(batch * seq, D_MODEL)
    hs = x_sample.reshape(dbatch * dec, D_MODEL)
    kp_l, vp_l, ks_l, vs_l, cv_l = [], [], [], [], []
    for l in range(depth):
        lw = _layer_weights(l, g_attn_norm, w_in, g_v_a, g_q, g_k, g_out_a, g_out_b, w_out, g_ffn_norm,
                            w_coarse, b_coarse, w_fine, b_fine)
        sinks = attn_sinks[l].astype(F32)
        wbd_p, bsp_p = _spatial_tables(w_spatial[l], b_spatial[l], CHUNK, PROMPT_TILE)
        wbd_s, bsp_s = _spatial_tables(w_spatial[l], b_spatial[l], dec, SAMPLE_TILE)

        hp, kp, vp, wg_bf, wu_bf, wd_bf = _run_prompt(hp, lw, wbd_p, bsp_p, sinks, seq,
                                                      w_gate[l], w_up[l], w_down[l])
        kp_l.append(_positions_first(kp.reshape(batch, KV_WIDTH, WINDOW)))
        vp_l.append(_positions_first(vp.reshape(batch, KV_WIDTH, WINDOW)))

        hs, wk, wv, va = _run_sample(hs, _positions_last(cache_k[l]), _positions_last(cache_v[l]),
                                     lw, wbd_s, bsp_s, sinks, dec, wg_bf, wu_bf, wd_bf)
        ks_l.append(_positions_first(wk))
        vs_l.append(_positions_first(wv))
        cv_l.append(va.reshape(dbatch, dec, A_GROUPS, A_CH))

    return (hp.reshape(batch, seq, D_MODEL), hs.reshape(dbatch, dec, D_MODEL),
            jnp.stack(kp_l, axis=0), jnp.stack(vp_l, axis=0),
            jnp.stack(ks_l, axis=0), jnp.stack(vs_l, axis=0), jnp.stack(cv_l, axis=0))
```

```python
import functools

import numpy as np
import jax
import jax.numpy as jnp
from jax import lax
from jax.experimental import pallas as pl
from jax.experimental.pallas import tpu as pltpu

D_MODEL = 1024
CHUNK = 128
A_GROUPS = 4
A_WIDTH = 512
A_CH = 128
N_HEADS = 8
N_KV = 2
Q_PER_KV = 4
HEAD_DIM = 64
B_WIDTH = 512
KV_WIDTH = 128
WINDOW = 128
IN_COLS = 2 * A_WIDTH + B_WIDTH + 2 * KV_WIDTH
N_EXPERT_GROUPS = 4
EXPERTS_PER_GROUP = 4
N_EXPERTS = 16
D_EXPERT = 256
EPS = 1e-6

LANES = 128
PAIR = 2 * CHUNK
ROUTER_LANES = 128
PROMPT_TILE = 512
SAMPLE_TILE = 128
VMEM_LIMIT_BYTES = 60 * 1024 * 1024

F32 = jnp.float32
BF16 = jnp.bfloat16
NEG_INF = float("-inf")
LOG2E = 1.4426950408889634
Q_SCALE = (HEAD_DIM ** -0.5) * LOG2E


def _slopes():
    return np.array([2.0 ** (-8.0 * (h + 1) / N_HEADS) for h in range(N_HEADS)], np.float64)


def _prompt_bias():
    t = np.arange(WINDOW)[:, None]
    s = np.arange(2 * WINDOW)[None, :]
    dist = t + WINDOW - s
    valid = (dist >= 0) & (dist < WINDOW)
    sl = _slopes()
    out = np.full((N_KV, Q_PER_KV * WINDOW, 2 * WINDOW), -np.inf, np.float32)
    for kh in range(N_KV):
        for g in range(Q_PER_KV):
            b = np.where(valid, -sl[kh * Q_PER_KV + g] * LOG2E * dist, -np.inf)
            out[kh, g * WINDOW:(g + 1) * WINDOW] = b
    return out


def _sample_bias(nb, dec):
    sl = _slopes()
    t = np.arange(dec)[:, None]
    j = np.arange(WINDOW)[None, :]
    dist_c = t + WINDOW - j
    valid_c = (dist_c >= 0) & (dist_c < WINDOW)
    bc = np.full((N_HEADS * dec, WINDOW), -np.inf, np.float32)
    tp = np.arange(dec)[None, :]
    dist_n = t - tp
    valid_n = dist_n >= 0
    bn = np.full((nb, N_HEADS * dec, nb * dec), -np.inf, np.float32)
    for h in range(N_HEADS):
        bc[h * dec:(h + 1) * dec] = np.where(valid_c, -sl[h] * LOG2E * dist_c, -np.inf)
        blk = np.where(valid_n, -sl[h] * LOG2E * dist_n, -np.inf)
        for b in range(nb):
            bn[b, h * dec:(h + 1) * dec, b * dec:(b + 1) * dec] = blk
    return bc, bn


def _seg_ones(width, seg):
    i = np.arange(width)
    return (i[:, None] // seg == i[None, :] // seg).astype(np.float32)


def _causal_block_mask(period, size):
    i = np.arange(size)
    same = i[:, None] // period == i[None, :] // period
    return (same & (i[None, :] % period <= i[:, None] % period)).astype(np.float32)


def _dot(a, b):
    return jnp.dot(a, b, preferred_element_type=F32)


def _dot_nt(a, b):
    return lax.dot_general(a, b, (((1,), (1,)), ((), ())), preferred_element_type=F32)


def _rms(x, g):
    ms = jnp.mean(x * x, axis=-1, keepdims=True)
    return (x * lax.rsqrt(ms + EPS)) * g


def _seg_rms(x, ones_bf, g, seg):
    ss = _dot((x * x).astype(BF16), ones_bf)
    return (x * lax.rsqrt(ss * (1.0 / seg) + EPS)) * g


GELU_K0 = -2.0 * 0.7978845608028654 * LOG2E
GELU_K1 = GELU_K0 * 0.044715


def _gelu_tanh(x):
    t = (x * x) * GELU_K1 + GELU_K0
    return x * (1.0 / (1.0 + jnp.exp2(x * t)))


VEC_ATTN = 0
VEC_VA = VEC_ATTN + D_MODEL
VEC_Q = VEC_VA + A_WIDTH
VEC_K = VEC_Q + B_WIDTH
VEC_OUT = VEC_K + KV_WIDTH
VEC_FFN = VEC_OUT + D_MODEL
VEC_ROUTER_BIAS = VEC_FFN + D_MODEL
VEC_LANES = VEC_ROUTER_BIAS + ROUTER_LANES


def _vec(vec_ref, start, width):
    return vec_ref[:, start:start + width]


def _project(x, vec_ref, w_in_ref, segq_ref, segk_ref):
    xn = _rms(x, _vec(vec_ref, VEC_ATTN, D_MODEL)).astype(BF16)
    z = _dot(xn, w_in_ref[...])
    i1, i2, i3, i4 = A_WIDTH, 2 * A_WIDTH, 2 * A_WIDTH + B_WIDTH, 2 * A_WIDTH + B_WIDTH + KV_WIDTH
    u = _gelu_tanh(z[:, :i1])
    va_pre = _gelu_tanh(z[:, i1:i2])
    g_va = _vec(vec_ref, VEC_VA, A_WIDTH)
    va = jnp.concatenate(
        [_rms(va_pre[:, g * A_CH:(g + 1) * A_CH], g_va[:, g * A_CH:(g + 1) * A_CH]) for g in range(A_GROUPS)],
        axis=1)
    q = _seg_rms(z[:, i2:i3], segq_ref[...], _vec(vec_ref, VEC_Q, B_WIDTH), HEAD_DIM)
    k = _seg_rms(z[:, i3:i4], segk_ref[...], _vec(vec_ref, VEC_K, KV_WIDTH), HEAD_DIM)
    v = z[:, i4:]
    return u, va, q, k, v


def _chunk_mlp(va_bf, wsp_ref, bsp_ref, tile):
    pair = wsp_ref.shape[1]
    n_pairs = tile // pair
    outs = []
    for g in range(A_GROUPS):
        cols = [va_bf[p * pair:(p + 1) * pair, g * A_CH:(g + 1) * A_CH] for p in range(n_pairs)]
        rhs = cols[0] if n_pairs == 1 else jnp.concatenate(cols, axis=1)
        o = _dot(wsp_ref[g], rhs)
        rows = [o[:, p * A_CH:(p + 1) * A_CH] for p in range(n_pairs)]
        outs.append(rows[0] if n_pairs == 1 else jnp.concatenate(rows, axis=0))
    mixed = jnp.concatenate(outs, axis=1)
    bias = bsp_ref[...]
    reps = tile // bias.shape[0]
    return mixed + (bias if reps == 1 else jnp.concatenate([bias] * reps, axis=0))


def _lane_lo():
    return lax.broadcasted_iota(jnp.int32, (1, LANES), 1) < HEAD_DIM


def _swap_halves(x):
    return pltpu.roll(x, HEAD_DIM, axis=1)


def _head_lhs(q_groups, q_groups_swapped, head, lo):
    j, half = divmod(head, 2)
    kv = head // Q_PER_KV
    src = q_groups[j] if half == kv else q_groups_swapped[j]
    return jnp.where(lo if kv == 0 else jnp.logical_not(lo), src, 0.0)


def _merge_heads(o_heads, lo):
    groups = []
    for j in range(N_HEADS // 2):
        kv = (2 * j) // Q_PER_KV
        if kv == 0:
            groups.append(jnp.where(lo, o_heads[2 * j], _swap_halves(o_heads[2 * j + 1])))
        else:
            groups.append(jnp.where(lo, _swap_halves(o_heads[2 * j]), o_heads[2 * j + 1]))
    return jnp.concatenate(groups, axis=1)


def _dup_halves(x, lo):
    xs = _swap_halves(x)
    return jnp.where(lo, x, xs).astype(BF16), jnp.where(lo, xs, x).astype(BF16)


def _swa_prompt(q, k, v, kprev_ref, vprev_ref, bias_ref, sinks_ref, is_first, tile):
    lo = _lane_lo()
    hi = jnp.logical_not(lo)
    col = lax.broadcasted_iota(jnp.int32, (WINDOW, 2 * WINDOW), 1)
    first_mask = jnp.where(col < WINDOW, jnp.where(is_first, NEG_INF, 0.0).astype(F32), 0.0)
    kd = _dup_halves(k, lo)
    vd = _dup_halves(v, lo)
    nblk = tile // WINDOW
    rows_out = []
    for i in range(nblk):
        r0, r1 = i * WINDOW, (i + 1) * WINDOW
        qg = [q[r0:r1, j * LANES:(j + 1) * LANES] for j in range(N_HEADS // 2)]
        o_heads = []
        for kh in range(N_KV):
            if i == 0:
                kp, vp = kprev_ref[kh], vprev_ref[kh]
            else:
                kp, vp = kd[kh][r0 - WINDOW:r0], vd[kh][r0 - WINDOW:r0]
            kb = jnp.concatenate([kp, kd[kh][r0:r1]], axis=0)
            vb = jnp.concatenate([vp, vd[kh][r0:r1]], axis=0)
            heads = [kh * Q_PER_KV + g for g in range(Q_PER_KV)]
            lhs = jnp.concatenate(
                [jnp.where(lo if h % 2 == 0 else hi, qg[h // 2], 0.0) for h in heads], axis=0).astype(BF16)
            s = _dot_nt(lhs, kb) + bias_ref[kh]
            ps, linvs = [], []
            for g in range(Q_PER_KV):
                sg = s[g * WINDOW:(g + 1) * WINDOW]
                if i == 0:
                    sg = sg + first_mask
                sink = sinks_ref[heads[g]] * LOG2E
                mg = jnp.maximum(jnp.max(sg, axis=-1, keepdims=True), sink)
                pg = jnp.exp2(sg - mg)
                lg = jnp.sum(pg, axis=-1, keepdims=True) + jnp.exp2(sink - mg)
                ps.append(pg.astype(BF16))
                linvs.append(1.0 / lg)
            o = _dot(jnp.concatenate(ps, axis=0), vb)
            o_heads += [o[g * WINDOW:(g + 1) * WINDOW] * linvs[g] for g in range(Q_PER_KV)]
        rows_out.append(jnp.concatenate(
            [jnp.where(lo, o_heads[2 * j], o_heads[2 * j + 1]) for j in range(N_HEADS // 2)], axis=1))
    for kh in range(N_KV):
        kprev_ref[kh] = kd[kh][tile - WINDOW:tile]
        vprev_ref[kh] = vd[kh][tile - WINDOW:tile]
    return jnp.concatenate(rows_out, axis=0)


def _shift_window(cache_ref, new_t, out_ref, nb, dec):
    lane = lax.broadcasted_iota(jnp.int32, (1, WINDOW), 1)
    keep = lane < WINDOW - dec
    per_tile = LANES // dec
    for b in range(nb):
        src = new_t[:, (b // per_tile) * LANES:(b // per_tile + 1) * LANES]
        new_cols = pltpu.roll(src, (WINDOW - dec - (b % per_tile) * dec) % LANES, axis=1)
        old = pltpu.roll(cache_ref[b], WINDOW - dec, axis=1)
        out_ref[b] = jnp.where(keep, old, new_cols)


def _swa_sample(q, k, v, ck_ref, cv_ref, wk_ref, wv_ref, bias_c_ref, bias_n_ref, sinks_ref, nb, dec):
    lo = _lane_lo()
    qg = [q[:, j * LANES:(j + 1) * LANES] for j in range(N_HEADS // 2)]
    qgs = [_swap_halves(x) for x in qg]
    qb = jnp.concatenate(
        [_head_lhs(qg, qgs, h, lo).reshape(nb, dec, LANES) for h in range(N_HEADS)], axis=1)
    rows = N_HEADS * dec
    qb_bf = qb.astype(BF16)
    k_bf = k.astype(BF16)
    v_bf = v.astype(BF16)
    ck = ck_ref[...].astype(BF16)
    cv = cv_ref[...].astype(BF16)
    sc = jnp.einsum('bqc,bcp->bqp', qb_bf, ck, preferred_element_type=F32) + bias_c_ref[...][None]
    sn = _dot_nt(qb_bf.reshape(nb * rows, LANES), k_bf).reshape(nb, rows, nb * dec) + bias_n_ref[...]
    sink = jnp.concatenate(
        [jnp.full((1, dec, 1), sinks_ref[h] * LOG2E, F32) for h in range(N_HEADS)], axis=1)
    m = jnp.maximum(jnp.max(sc, axis=-1, keepdims=True), jnp.max(sn, axis=-1, keepdims=True))
    m = jnp.maximum(m, sink)
    pc = jnp.exp2(sc - m)
    pn = jnp.exp2(sn - m)
    l = jnp.sum(pc, axis=-1, keepdims=True) + jnp.sum(pn, axis=-1, keepdims=True) + jnp.exp2(sink - m)
    oc = jnp.einsum('bqp,bcp->bqc', pc.astype(BF16), cv, preferred_element_type=F32)
    on = _dot(pn.reshape(nb * rows, nb * dec).astype(BF16), v_bf).reshape(nb, rows, LANES)
    o = (oc + on) * (1.0 / l)
    o_heads = [o[:, h * dec:(h + 1) * dec, :].reshape(nb * dec, LANES) for h in range(N_HEADS)]
    _shift_window(ck_ref, k.T, wk_ref, nb, dec)
    _shift_window(cv_ref, v.T, wv_ref, nb, dec)
    return _merge_heads(o_heads, lo)


def _route(hn_bf, wr_ref, vec_ref):
    logits = _dot(hn_bf, wr_ref[...]) + _vec(vec_ref, VEC_ROUTER_BIAS, ROUTER_LANES)
    lane = lax.broadcasted_iota(jnp.int32, logits.shape, 1)
    big = jnp.int32(ROUTER_LANES)
    lc = jnp.where(lane < N_EXPERT_GROUPS, logits, NEG_INF)
    mx = jnp.max(lc, axis=-1, keepdims=True)
    g_idx = jnp.min(jnp.where(lc == mx, lane, big), axis=-1, keepdims=True)
    p_g = 1.0 / jnp.sum(jnp.exp(lc - mx), axis=-1, keepdims=True)
    e_lane = lane - N_EXPERT_GROUPS
    in_group = (e_lane >= 0) & (e_lane < N_EXPERTS) & ((e_lane >> 2) == g_idx)
    lf = jnp.where(in_group, logits, NEG_INF)
    v1 = jnp.max(lf, axis=-1, keepdims=True)
    i1 = jnp.min(jnp.where(lf == v1, lane, big), axis=-1, keepdims=True)
    lf2 = jnp.where(lane == i1, NEG_INF, lf)
    v2 = jnp.max(lf2, axis=-1, keepdims=True)
    i2 = jnp.min(jnp.where(lf2 == v2, lane, big), axis=-1, keepdims=True)
    e = jnp.exp(v2 - v1)
    w1 = p_g / (1.0 + e)
    w2 = w1 * e
    return lane, g_idx, i1, i2, w1, w2


ROUTER_ROWS = 32


def _route_rows(hn_bf, wrt_ref, brt_ref, tile):
    logits = _dot_nt(wrt_ref[...], hn_bf) + jnp.tile(brt_ref[...], (1, tile // LANES))
    row = lax.broadcasted_iota(jnp.int32, logits.shape, 0)
    big = jnp.int32(ROUTER_ROWS)
    lc = jnp.where(row < N_EXPERT_GROUPS, logits, NEG_INF)
    mx = jnp.max(lc, axis=0, keepdims=True)
    g_idx = jnp.min(jnp.where(lc == mx, row, big), axis=0, keepdims=True)
    p_g = 1.0 / jnp.sum(jnp.exp(lc - mx), axis=0, keepdims=True)
    e_row = row - N_EXPERT_GROUPS
    in_group = (e_row >= 0) & (e_row < N_EXPERTS) & ((e_row >> 2) == g_idx)
    lf = jnp.where(in_group, logits, NEG_INF)
    v1 = jnp.max(lf, axis=0, keepdims=True)
    i1 = jnp.min(jnp.where(lf == v1, row, big), axis=0, keepdims=True)
    lf2 = jnp.where(row == i1, NEG_INF, lf)
    v2 = jnp.max(lf2, axis=0, keepdims=True)
    i2 = jnp.min(jnp.where(lf2 == v2, row, big), axis=0, keepdims=True)
    e = jnp.exp(v2 - v1)
    w1 = p_g / (1.0 + e)
    w2 = w1 * e
    return g_idx, i1, i2, w1, w2


def _silu(x):
    return x * (1.0 / (1.0 + jnp.exp2(x * (-LOG2E))))


def _moe_dense(h, vec_ref, wr_ref, wg_ref, wu_ref, wd_ref, hcat_ref):
    hn_bf = _rms(h, _vec(vec_ref, VEC_FFN, D_MODEL)).astype(BF16)
    lane, _, i1, i2, w1, w2 = _route(hn_bf, wr_ref, vec_ref)
    combine = jnp.where(lane == i1, w1, jnp.where(lane == i2, w2, 0.0))
    for e in range(N_EXPERTS):
        gate = _dot(hn_bf, wg_ref[e])
        up = _dot(hn_bf, wu_ref[e])
        c = combine[:, N_EXPERT_GROUPS + e:N_EXPERT_GROUPS + e + 1]
        hcat_ref[:, e * D_EXPERT:(e + 1) * D_EXPERT] = (_silu(gate) * up * c).astype(BF16)
    return _dot(hcat_ref[...], wd_ref[...])


GROUP_LANE = EXPERTS_PER_GROUP
LOW_SHIFT = 8
SORT_BLOCK = 128


def _moe_grouped(h, vec_ref, wrt_ref, brt_ref, wg_ref, wu_ref, wd_ref, ustrict_ref, xs_ref, es_ref, ys_ref, tile):
    hn_bf = _rms(h, _vec(vec_ref, VEC_FFN, D_MODEL)).astype(BF16)
    g_idx, i1, i2, w1, w2 = _route_rows(hn_bf, wrt_ref, brt_ref, tile)

    row16 = lax.broadcasted_iota(jnp.int32, (16, tile), 0)
    onehot = row16 == g_idx
    gmat = jnp.where(onehot, 1.0, 0.0)
    before = _dot(gmat.astype(BF16), ustrict_ref[...])
    rank = jnp.sum(jnp.where(onehot, before, 0.0), axis=0, keepdims=True)
    offs = [jnp.float32(0.0)]
    for g in range(N_EXPERT_GROUPS - 1):
        offs.append(offs[-1] + jnp.sum(jnp.where(g_idx == g, 1.0, 0.0)))
    offv = jnp.zeros_like(rank)
    for g in range(1, N_EXPERT_GROUPS):
        offv = offv + jnp.where(g_idx == g, offs[g], 0.0)
    pos = (rank + offv).astype(jnp.int32)
    dest = lax.broadcasted_iota(jnp.int32, (tile, tile), 0)
    perm = jnp.where(dest == pos, 1.0, 0.0).astype(BF16)

    row8 = lax.broadcasted_iota(jnp.int32, (8, tile), 0)
    j1 = (i1 - N_EXPERT_GROUPS) & (EXPERTS_PER_GROUP - 1)
    j2 = (i2 - N_EXPERT_GROUPS) & (EXPERTS_PER_GROUP - 1)
    gates = jnp.where(row8 == j1, w1, jnp.where(row8 == j2, w2, 0.0))
    gates_hi = gates.astype(BF16).astype(F32)
    gates_lo = (gates - gates_hi).astype(BF16).astype(F32)
    top = gates_hi + jnp.where(row8 == GROUP_LANE, g_idx.astype(F32), 0.0)
    side = jnp.concatenate([top, gates_lo, jnp.zeros((LANES - 2 * LOW_SHIFT, tile), F32)], axis=0).T
    x_ext = jnp.concatenate([hn_bf, side.astype(BF16)], axis=1)
    srt = _dot(perm, x_ext)
    xs_ref[...] = srt[:, :D_MODEL].astype(BF16)
    es = srt[:, D_MODEL:]
    es_ref[...] = es + pltpu.roll(es, LANES - LOW_SHIFT, axis=1)

    def group_of(row):
        g = jnp.int32(0)
        for k in range(1, N_EXPERT_GROUPS):
            g = g + (jnp.float32(row) >= offs[k]).astype(jnp.int32)
        return g

    def group_pass(rows, g, keep, accumulate):
        xb = xs_ref[rows, :]
        eb = es_ref[rows, :]
        mine = jnp.where(eb[:, GROUP_LANE:GROUP_LANE + 1] == jnp.asarray(g).astype(F32), keep, 0.0)
        parts = []
        for j in range(EXPERTS_PER_GROUP):
            e = g * EXPERTS_PER_GROUP + j
            gate = _dot(xb, wg_ref[e])
            up = _dot(xb, wu_ref[e])
            parts.append((_silu(gate) * up * (eb[:, j:j + 1] * mine)).astype(BF16))
        wd = wd_ref[pl.ds(g * EXPERTS_PER_GROUP, EXPERTS_PER_GROUP)]
        out = _dot(jnp.concatenate(parts, axis=1), wd.reshape(EXPERTS_PER_GROUP * D_EXPERT, D_MODEL))
        if accumulate:
            ys_ref[rows, :] += out
        else:
            ys_ref[rows, :] = out

    nblk = tile // SORT_BLOCK
    for b in range(nblk):
        group_pass(pl.ds(b * SORT_BLOCK, SORT_BLOCK), group_of(b * SORT_BLOCK), 1.0, False)
    for g in range(1, N_EXPERT_GROUPS):
        start = offs[g].astype(jnp.int32)
        blk = jnp.minimum(lax.div(start, jnp.int32(SORT_BLOCK)), nblk - 1)
        inside = jnp.where(lax.rem(start, jnp.int32(SORT_BLOCK)) != 0, 1.0, 0.0)
        group_pass(pl.ds(pl.multiple_of(blk * SORT_BLOCK, SORT_BLOCK), SORT_BLOCK), g, inside, True)
    return lax.dot_general(perm, ys_ref[...].astype(BF16), (((0,), (0,)), ((), ())),
                           preferred_element_type=F32)


def _merge(x, a_out, b_out, vec_ref, w_out_ref):
    g_out = _vec(vec_ref, VEC_OUT, D_MODEL)
    mix = jnp.concatenate(
        [_rms(a_out, g_out[:, :A_WIDTH]), _rms(b_out, g_out[:, A_WIDTH:])], axis=1).astype(BF16)
    return x + _dot(mix, w_out_ref[...])


def _stage_expert_weights(wg_hbm, wu_hbm, wd_hbm, wg_s, wu_s, wd_s, st_a, st_d, sems):
    def gate_cp(e):
        return pltpu.make_async_copy(wg_hbm.at[e], st_a.at[0], sems.at[0])

    def up_cp(e):
        return pltpu.make_async_copy(wu_hbm.at[e], st_a.at[1], sems.at[1])

    def down_cp(e, slot):
        return pltpu.make_async_copy(wd_hbm.at[e], st_d.at[slot], sems.at[2 + slot])

    gate_cp(0).start()
    up_cp(0).start()
    down_cp(0, 0).start()

    def pair(p, carry):
        for slot in (0, 1):
            e = 2 * p + slot
            nxt = e + 1
            more = nxt < N_EXPERTS

            @pl.when(more)
            def _():
                down_cp(nxt, 1 - slot).start()

            gate_cp(e).wait()
            wg_s[e] = st_a[0].astype(BF16)

            @pl.when(more)
            def _():
                gate_cp(nxt).start()

            up_cp(e).wait()
            wu_s[e] = st_a[1].astype(BF16)

            @pl.when(more)
            def _():
                up_cp(nxt).start()

            down_cp(e, slot).wait()
            wd_s[e] = st_d[slot].astype(BF16)
        return carry

    lax.fori_loop(0, N_EXPERTS // 2, pair, 0)


def _bf16_weight_copies(wg_s, wu_s, wd_s, wg_o, wu_o, wd_o, sems):
    return (pltpu.make_async_copy(wg_s, wg_o, sems.at[4]),
            pltpu.make_async_copy(wu_s, wu_o, sems.at[5]),
            pltpu.make_async_copy(wd_s, wd_o, sems.at[6]))


def _prompt_kernel(x_ref, vec_ref, w_in_ref, segq_ref, segk_ref, wsp_ref, bsp_ref, bias_ref, sinks_ref,
                   w_out_ref, wrt_ref, brt_ref, ustrict_ref, wg_hbm, wu_hbm, wd_hbm,
                   y_ref, kwin_ref, vwin_ref, wg_o, wu_o, wd_o,
                   kprev_ref, vprev_ref, xs_ref, es_ref, ys_ref, wg_ref, wu_ref, wd_ref, st_a, st_d, sems,
                   *, tile, tiles_per_seq):
    step = pl.program_id(0)
    is_first = (step % tiles_per_seq) == 0

    @pl.when(step == 0)
    def _():
        _stage_expert_weights(wg_hbm, wu_hbm, wd_hbm, wg_ref, wu_ref, wd_ref, st_a, st_d, sems)
        for cp in _bf16_weight_copies(wg_ref, wu_ref, wd_ref, wg_o, wu_o, wd_o, sems):
            cp.start()

    @pl.when(step == pl.num_programs(0) - 1)
    def _():
        for cp in _bf16_weight_copies(wg_ref, wu_ref, wd_ref, wg_o, wu_o, wd_o, sems):
            cp.wait()

    @pl.when(is_first)
    def _():
        kprev_ref[...] = jnp.zeros_like(kprev_ref)
        vprev_ref[...] = jnp.zeros_like(vprev_ref)

    x = x_ref[...]
    u, va, q, k, v = _project(x, vec_ref, w_in_ref, segq_ref, segk_ref)
    kwin_ref[...] = k[tile - WINDOW:tile].T
    vwin_ref[...] = v[tile - WINDOW:tile].T
    a_out = u * _chunk_mlp(va.astype(BF16), wsp_ref, bsp_ref, tile)
    b_out = _swa_prompt(q * Q_SCALE, k, v, kprev_ref, vprev_ref, bias_ref, sinks_ref, is_first, tile)
    h = _merge(x, a_out, b_out, vec_ref, w_out_ref)
    y_ref[...] = h + _moe_grouped(h, vec_ref, wrt_ref, brt_ref, wg_ref, wu_ref, wd_ref, ustrict_ref,
                                  xs_ref, es_ref, ys_ref, tile)


def _sample_kernel(x_ref, ck_ref, cv_ref, vec_ref, w_in_ref, segq_ref, segk_ref, wsp_ref, bsp_ref,
                   bias_c_ref, bias_n_ref, sinks_ref, w_out_ref, wr_ref, wg_ref, wu_ref, wd_ref,
                   y_ref, wk_ref, wv_ref, va_ref, hcat_ref, *, tile, nb, dec):
    x = x_ref[...]
    u, va, q, k, v = _project(x, vec_ref, w_in_ref, segq_ref, segk_ref)
    for g in range(A_GROUPS):
        va_ref[pl.ds(g, tile, stride=A_GROUPS), :] = va[:, g * A_CH:(g + 1) * A_CH]
    a_out = u * _chunk_mlp(va.astype(BF16), wsp_ref, bsp_ref, tile)
    b_out = _swa_sample(q * Q_SCALE, k, v, ck_ref, cv_ref, wk_ref, wv_ref,
                        bias_c_ref, bias_n_ref, sinks_ref, nb, dec)
    h = _merge(x, a_out, b_out, vec_ref, w_out_ref)
    y_ref[...] = h + _moe_dense(h, vec_ref, wr_ref, wg_ref, wu_ref, wd_ref, hcat_ref)


def _const_spec(shape):
    nd = len(shape)
    return pl.BlockSpec(shape, lambda i: (0,) * nd, pipeline_mode=pl.Buffered(1))


def _row_spec(tile, width):
    return pl.BlockSpec((tile, width), lambda i: (i, 0))


def _smem_spec():
    return pl.BlockSpec(memory_space=pltpu.SMEM)


def _layer_weights(l, g_attn_norm, w_in, g_v_a, g_q, g_k, g_out_a, g_out_b, w_out, g_ffn_norm,
                   w_coarse, b_coarse, w_fine, b_fine):
    pad = ROUTER_LANES - N_EXPERT_GROUPS - N_EXPERTS
    wr = jnp.concatenate([w_coarse[l], w_fine[l], jnp.zeros((D_MODEL, pad), F32)], axis=1)
    row_pad = ROUTER_ROWS - N_EXPERT_GROUPS - N_EXPERTS
    wrt = jnp.concatenate([w_coarse[l].T, w_fine[l].T, jnp.zeros((row_pad, D_MODEL), F32)], axis=0)
    brt = jnp.concatenate([b_coarse[l], b_fine[l], jnp.zeros((row_pad,), F32)])
    vec = jnp.concatenate([
        g_attn_norm[l], g_v_a[l].reshape(A_WIDTH), jnp.tile(g_q[l], N_HEADS), jnp.tile(g_k[l], N_KV),
        g_out_a[l], g_out_b[l], g_ffn_norm[l], b_coarse[l], b_fine[l], jnp.zeros((pad,), F32)])
    return dict(
        vec=vec.reshape(1, VEC_LANES),
        w_in=w_in[l].astype(BF16),
        segq=jnp.asarray(_seg_ones(B_WIDTH, HEAD_DIM), BF16),
        segk=jnp.asarray(_seg_ones(KV_WIDTH, HEAD_DIM), BF16),
        w_out=w_out[l].astype(BF16),
        wr=wr.astype(BF16),
        wrt=wrt.astype(BF16),
        brt=jnp.broadcast_to(brt[:, None], (ROUTER_ROWS, LANES)),
    )


def _spatial_tables(ws, bs, period, tile):
    pair = min(PAIR, tile)
    reps = pair // period
    wbd = (jnp.tile(ws[:, :period, :period], (1, reps, reps)) * _causal_block_mask(period, pair)).astype(BF16)
    bsp = jnp.repeat(bs[:, :period].T, A_CH, axis=1)
    return wbd, bsp


_HEAD_NAMES = ("vec", "w_in", "segq", "segk")
_TAIL_NAMES = ("w_out", "wr")


def _upper_triangle(n):
    return jnp.asarray(np.triu(np.ones((n, n), np.float32), 1), BF16)


def _run_prompt(x2d, lw, wbd, bsp, sinks, seq_len, w_gate, w_up, w_down):
    n_tok = x2d.shape[0]
    tile = PROMPT_TILE
    tiles_per_seq = seq_len // tile
    n_seq = n_tok // seq_len
    bias = jnp.asarray(_prompt_bias())
    head = [lw[n] for n in _HEAD_NAMES]
    tail = [lw["w_out"], lw["wrt"], lw["brt"], _upper_triangle(tile)]
    experts = [w_gate, w_up, w_down]
    consts = head + [wbd, bsp, bias]
    any_spec = pl.BlockSpec(memory_space=pl.ANY)
    in_specs = ([_row_spec(tile, D_MODEL)] + [_const_spec(a.shape) for a in consts] + [_smem_spec()]
                + [_const_spec(a.shape) for a in tail] + [any_spec] * len(experts))
    win_spec = pl.BlockSpec((KV_WIDTH, WINDOW), lambda i: (i // tiles_per_seq, 0))
    out_shape = (jax.ShapeDtypeStruct((n_tok, D_MODEL), F32),
                 jax.ShapeDtypeStruct((n_seq * KV_WIDTH, WINDOW), F32),
                 jax.ShapeDtypeStruct((n_seq * KV_WIDTH, WINDOW), F32),
                 *[jax.ShapeDtypeStruct(w.shape, BF16) for w in experts])
    out_specs = (_row_spec(tile, D_MODEL), win_spec, win_spec, any_spec, any_spec, any_spec)
    kern = functools.partial(_prompt_kernel, tile=tile, tiles_per_seq=tiles_per_seq)
    return pl.pallas_call(
        kern,
        out_shape=out_shape,
        grid=(n_tok // tile,),
        in_specs=in_specs,
        out_specs=out_specs,
        scratch_shapes=[pltpu.VMEM((N_KV, WINDOW, KV_WIDTH), BF16), pltpu.VMEM((N_KV, WINDOW, KV_WIDTH), BF16),
                        pltpu.VMEM((tile, D_MODEL), BF16), pltpu.VMEM((tile, LANES), F32),
                        pltpu.VMEM((tile, D_MODEL), F32),
                        *[pltpu.VMEM(w.shape, BF16) for w in experts],
                        pltpu.VMEM((2,) + w_gate.shape[1:], F32), pltpu.VMEM((2,) + w_down.shape[1:], F32),
                        pltpu.SemaphoreType.DMA((7,))],
        compiler_params=pltpu.CompilerParams(dimension_semantics=("arbitrary",),
                                             vmem_limit_bytes=VMEM_LIMIT_BYTES),
        name="layer_prompt",
    )(x2d, *consts, sinks, *tail, *experts)


def _run_sample(x2d, ck_t, cv_t, lw, wbd, bsp, sinks, dec, wg_bf, wu_bf, wd_bf):
    n_tok = x2d.shape[0]
    n_seq = ck_t.shape[0]
    tile = SAMPLE_TILE
    nb = tile // dec
    bc, bn = _sample_bias(nb, dec)
    head = [lw[n] for n in _HEAD_NAMES]
    tail = [lw[n] for n in _TAIL_NAMES] + [wg_bf, wu_bf, wd_bf.reshape(N_EXPERTS * D_EXPERT, D_MODEL)]
    consts = head + [wbd, bsp, jnp.asarray(bc), jnp.asarray(bn)]
    cache_spec = pl.BlockSpec((nb, KV_WIDTH, WINDOW), lambda i: (i, 0, 0))
    in_specs = ([_row_spec(tile, D_MODEL), cache_spec, cache_spec] + [_const_spec(a.shape) for a in consts]
                + [_smem_spec()] + [_const_spec(a.shape) for a in tail])
    out_shape = (jax.ShapeDtypeStruct((n_tok, D_MODEL), F32),
                 jax.ShapeDtypeStruct((n_seq, KV_WIDTH, WINDOW), F32),
                 jax.ShapeDtypeStruct((n_seq, KV_WIDTH, WINDOW), F32),
                 jax.ShapeDtypeStruct((n_tok * A_GROUPS, A_CH), F32))
    out_specs = (_row_spec(tile, D_MODEL), cache_spec, cache_spec, _row_spec(tile * A_GROUPS, A_CH))
    kern = functools.partial(_sample_kernel, tile=tile, nb=nb, dec=dec)
    return pl.pallas_call(
        kern,
        out_shape=out_shape,
        grid=(n_tok // tile,),
        in_specs=in_specs,
        out_specs=out_specs,
        scratch_shapes=[pltpu.VMEM((tile, N_EXPERTS * D_EXPERT), BF16)],
        compiler_params=pltpu.CompilerParams(dimension_semantics=("arbitrary",),
                                             vmem_limit_bytes=VMEM_LIMIT_BYTES),
        name="layer_sample",
    )(x2d, ck_t, cv_t, *consts, sinks, *tail)


def _positions_last(c):
    b, w = c.shape[:2]
    return jnp.transpose(c, (0, 2, 3, 1)).reshape(b, KV_WIDTH, w)


def _positions_first(c_t):
    b, _, w = c_t.shape
    return jnp.transpose(c_t.reshape(b, N_KV, HEAD_DIM, w), (0, 3, 1, 2))


def kernel(x_prompt, x_sample, cache_k, cache_v, g_attn_norm, w_in, g_v_a, w_spatial, b_spatial, g_q, g_k, attn_sinks, g_out_a, g_out_b, w_out, g_ffn_norm, w_coarse, b_coarse, w_fine, b_fine, w_gate, w_up, w_down):
    depth = w_in.shape[0]
    batch, seq, _ = x_prompt.shape
    dbatch, dec, _ = x_sample.shape
    win = cache_k.shape[2]
    assert win == WINDOW and seq % PROMPT_TILE == 0 and (dbatch * dec) % SAMPLE_TILE == 0
    assert PAIR % dec == 0 and SAMPLE_TILE % dec == 0 and LANES % dec == 0

    hp = x_prompt.reshape(batch * seq, D_MODEL)
    hs = x_sample.reshape(dbatch * dec, D_MODEL)
    kp_l, vp_l, ks_l, vs_l, cv_l = [], [], [], [], []
    for l in range(depth):
        lw = _layer_weights(l, g_attn_norm, w_in, g_v_a, g_q, g_k, g_out_a, g_out_b, w_out, g_ffn_norm,
                            w_coarse, b_coarse, w_fine, b_fine)
        sinks = attn_sinks[l].astype(F32)
        wbd_p, bsp_p = _spatial_tables(w_spatial[l], b_spatial[l], CHUNK, PROMPT_TILE)
        wbd_s, bsp_s = _spatial_tables(w_spatial[l], b_spatial[l], dec, SAMPLE_TILE)

        hp, kp, vp, wg_bf, wu_bf, wd_bf = _run_prompt(hp, lw, wbd_p, bsp_p, sinks, seq,
                                                      w_gate[l], w_up[l], w_down[l])
        kp_l.append(_positions_first(kp.reshape(batch, KV_WIDTH, WINDOW)))
        vp_l.append(_positions_first(vp.reshape(batch, KV_WIDTH, WINDOW)))

        hs, wk, wv, va = _run_sample(hs, _positions_last(cache_k[l]), _positions_last(cache_v[l]),
                                     lw, wbd_s, bsp_s, sinks, dec, wg_bf, wu_bf, wd_bf)
        ks_l.append(_positions_first(wk))
        vs_l.append(_positions_first(wv))
        cv_l.append(va.reshape(dbatch, dec, A_GROUPS, A_CH))

    return (hp.reshape(batch, seq, D_MODEL), hs.reshape(dbatch, dec, D_MODEL),
            jnp.stack(kp_l, axis=0), jnp.stack(vp_l, axis=0),
            jnp.stack(ks_l, axis=0), jnp.stack(vs_l, axis=0), jnp.stack(cv_l, axis=0))
```

```python
import functools

import numpy as np
import jax
import jax.numpy as jnp
from jax import lax
from jax.experimental import pallas as pl
from jax.experimental.pallas import tpu as pltpu

D_MODEL = 1024
CHUNK = 128
A_GROUPS = 4
A_WIDTH = 512
A_CH = 128
N_HEADS = 8
N_KV = 2
Q_PER_KV = 4
HEAD_DIM = 64
B_WIDTH = 512
KV_WIDTH = 128
WINDOW = 128
IN_COLS = 2 * A_WIDTH + B_WIDTH + 2 * KV_WIDTH
N_EXPERT_GROUPS = 4
EXPERTS_PER_GROUP = 4
N_EXPERTS = 16
D_EXPERT = 256
EPS = 1e-6

LANES = 128
PAIR = 2 * CHUNK
ROUTER_LANES = 128
PROMPT_TILE = 512
SAMPLE_TILE = 128
VMEM_LIMIT_BYTES = 60 * 1024 * 1024

F32 = jnp.float32
BF16 = jnp.bfloat16
NEG_INF = float("-inf")
LOG2E = 1.4426950408889634
Q_SCALE = (HEAD_DIM ** -0.5) * LOG2E


def _slopes():
    return np.array([2.0 ** (-8.0 * (h + 1) / N_HEADS) for h in range(N_HEADS)], np.float64)


def _prompt_bias():
    t = np.arange(WINDOW)[:, None]
    s = np.arange(2 * WINDOW)[None, :]
    dist = t + WINDOW - s
    valid = (dist >= 0) & (dist < WINDOW)
    sl = _slopes()
    out = np.full((N_KV, Q_PER_KV * WINDOW, 2 * WINDOW), -np.inf, np.float32)
    for kh in range(N_KV):
        for g in range(Q_PER_KV):
            b = np.where(valid, -sl[kh * Q_PER_KV + g] * LOG2E * dist, -np.inf)
            out[kh, g * WINDOW:(g + 1) * WINDOW] = b
    return out


def _sample_bias(nb, dec):
    sl = _slopes()
    t = np.arange(dec)[:, None]
    j = np.arange(WINDOW)[None, :]
    dist_c = t + WINDOW - j
    valid_c = (dist_c >= 0) & (dist_c < WINDOW)
    bc = np.full((N_HEADS * dec, WINDOW), -np.inf, np.float32)
    tp = np.arange(dec)[None, :]
    dist_n = t - tp
    valid_n = dist_n >= 0
    bn = np.full((nb, N_HEADS * dec, nb * dec), -np.inf, np.float32)
    for h in range(N_HEADS):
        bc[h * dec:(h + 1) * dec] = np.where(valid_c, -sl[h] * LOG2E * dist_c, -np.inf)
        blk = np.where(valid_n, -sl[h] * LOG2E * dist_n, -np.inf)
        for b in range(nb):
            bn[b, h * dec:(h + 1) * dec, b * dec:(b + 1) * dec] = blk
    return bc, bn


def _seg_ones(width, seg):
    i = np.arange(width)
    return (i[:, None] // seg == i[None, :] // seg).astype(np.float32)


def _causal_block_mask(period, size):
    i = np.arange(size)
    same = i[:, None] // period == i[None, :] // period
    return (same & (i[None, :] % period <= i[:, None] % period)).astype(np.float32)


def _dot(a, b):
    return jnp.dot(a, b, preferred_element_type=F32)


def _dot_nt(a, b):
    return lax.dot_general(a, b, (((1,), (1,)), ((), ())), preferred_element_type=F32)


def _rms(x, g):
    ms = jnp.mean(x * x, axis=-1, keepdims=True)
    return (x * lax.rsqrt(ms + EPS)) * g


def _seg_rms(x, ones_bf, g, seg):
    ss = _dot((x * x).astype(BF16), ones_bf)
    return (x * lax.rsqrt(ss * (1.0 / seg) + EPS)) * g


GELU_K0 = -2.0 * 0.7978845608028654 * LOG2E
GELU_K1 = GELU_K0 * 0.044715


def _gelu_tanh(x):
    t = (x * x) * GELU_K1 + GELU_K0
    return x * (1.0 / (1.0 + jnp.exp2(x * t)))


VEC_ATTN = 0
VEC_VA = VEC_ATTN + D_MODEL
VEC_Q = VEC_VA + A_WIDTH
VEC_K = VEC_Q + B_WIDTH
VEC_OUT = VEC_K + KV_WIDTH
VEC_FFN = VEC_OUT + D_MODEL
VEC_ROUTER_BIAS = VEC_FFN + D_MODEL
VEC_LANES = VEC_ROUTER_BIAS + ROUTER_LANES


def _vec(vec_ref, start, width):
    return vec_ref[:, start:start + width]


def _project(x, vec_ref, w_in_ref, segq_ref, segk_ref):
    xn = _rms(x, _vec(vec_ref, VEC_ATTN, D_MODEL)).astype(BF16)
    z = _dot(xn, w_in_ref[...])
    i1, i2, i3, i4 = A_WIDTH, 2 * A_WIDTH, 2 * A_WIDTH + B_WIDTH, 2 * A_WIDTH + B_WIDTH + KV_WIDTH
    u = _gelu_tanh(z[:, :i1])
    va_pre = _gelu_tanh(z[:, i1:i2])
    g_va = _vec(vec_ref, VEC_VA, A_WIDTH)
    va = jnp.concatenate(
        [_rms(va_pre[:, g * A_CH:(g + 1) * A_CH], g_va[:, g * A_CH:(g + 1) * A_CH]) for g in range(A_GROUPS)],
        axis=1)
    q = _seg_rms(z[:, i2:i3], segq_ref[...], _vec(vec_ref, VEC_Q, B_WIDTH), HEAD_DIM)
    k = _seg_rms(z[:, i3:i4], segk_ref[...], _vec(vec_ref, VEC_K, KV_WIDTH), HEAD_DIM)
    v = z[:, i4:]
    return u, va, q, k, v


def _chunk_mlp(va_bf, wsp_ref, bsp_ref, tile):
    pair = wsp_ref.shape[1]
    n_pairs = tile // pair
    outs = []
    for g in range(A_GROUPS):
        cols = [va_bf[p * pair:(p + 1) * pair, g * A_CH:(g + 1) * A_CH] for p in range(n_pairs)]
        rhs = cols[0] if n_pairs == 1 else jnp.concatenate(cols, axis=1)
        o = _dot(wsp_ref[g], rhs)
        rows = [o[:, p * A_CH:(p + 1) * A_CH] for p in range(n_pairs)]
        outs.append(rows[0] if n_pairs == 1 else jnp.concatenate(rows, axis=0))
    mixed = jnp.concatenate(outs, axis=1)
    bias = bsp_ref[...]
    reps = tile // bias.shape[0]
    return mixed + (bias if reps == 1 else jnp.concatenate([bias] * reps, axis=0))


def _lane_lo():
    return lax.broadcasted_iota(jnp.int32, (1, LANES), 1) < HEAD_DIM


def _swap_halves(x):
    return pltpu.roll(x, HEAD_DIM, axis=1)


def _head_lhs(q_groups, q_groups_swapped, head, lo):
    j, half = divmod(head, 2)
    kv = head // Q_PER_KV
    src = q_groups[j] if half == kv else q_groups_swapped[j]
    return jnp.where(lo if kv == 0 else jnp.logical_not(lo), src, 0.0)


def _merge_heads(o_heads, lo):
    groups = []
    for j in range(N_HEADS // 2):
        kv = (2 * j) // Q_PER_KV
        if kv == 0:
            groups.append(jnp.where(lo, o_heads[2 * j], _swap_halves(o_heads[2 * j + 1])))
        else:
            groups.append(jnp.where(lo, _swap_halves(o_heads[2 * j]), o_heads[2 * j + 1]))
    return jnp.concatenate(groups, axis=1)


def _dup_halves(x, lo):
    xs = _swap_halves(x)
    return jnp.where(lo, x, xs).astype(BF16), jnp.where(lo, xs, x).astype(BF16)


def _swa_prompt(q, k, v, kprev_ref, vprev_ref, bias_ref, sinks_ref, is_first, tile):
    lo = _lane_lo()
    hi = jnp.logical_not(lo)
    col = lax.broadcasted_iota(jnp.int32, (WINDOW, 2 * WINDOW), 1)
    first_mask = jnp.where(col < WINDOW, jnp.where(is_first, NEG_INF, 0.0).astype(F32), 0.0)
    kd = _dup_halves(k, lo)
    vd = _dup_halves(v, lo)
    nblk = tile // WINDOW
    rows_out = []
    for i in range(nblk):
        r0, r1 = i * WINDOW, (i + 1) * WINDOW
        qg = [q[r0:r1, j * LANES:(j + 1) * LANES] for j in range(N_HEADS // 2)]
        o_heads = []
        for kh in range(N_KV):
            if i == 0:
                kp, vp = kprev_ref[kh], vprev_ref[kh]
            else:
                kp, vp = kd[kh][r0 - WINDOW:r0], vd[kh][r0 - WINDOW:r0]
            kb = jnp.concatenate([kp, kd[kh][r0:r1]], axis=0)
            vb = jnp.concatenate([vp, vd[kh][r0:r1]], axis=0)
            heads = [kh * Q_PER_KV + g for g in range(Q_PER_KV)]
            lhs = jnp.concatenate(
                [jnp.where(lo if h % 2 == 0 else hi, qg[h // 2], 0.0) for h in heads], axis=0).astype(BF16)
            s = _dot_nt(lhs, kb) + bias_ref[kh]
            ps, linvs = [], []
            for g in range(Q_PER_KV):
                sg = s[g * WINDOW:(g + 1) * WINDOW]
                if i == 0:
                    sg = sg + first_mask
                sink = sinks_ref[heads[g]] * LOG2E
                mg = jnp.maximum(jnp.max(sg, axis=-1, keepdims=True), sink)
                pg = jnp.exp2(sg - mg)
                lg = jnp.sum(pg, axis=-1, keepdims=True) + jnp.exp2(sink - mg)
                ps.append(pg.astype(BF16))
                linvs.append(1.0 / lg)
            o = _dot(jnp.concatenate(ps, axis=0), vb)
            o_heads += [o[g * WINDOW:(g + 1) * WINDOW] * linvs[g] for g in range(Q_PER_KV)]
        rows_out.append(jnp.concatenate(
            [jnp.where(lo, o_heads[2 * j], o_heads[2 * j + 1]) for j in range(N_HEADS // 2)], axis=1))
    for kh in range(N_KV):
        kprev_ref[kh] = kd[kh][tile - WINDOW:tile]
        vprev_ref[kh] = vd[kh][tile - WINDOW:tile]
    return jnp.concatenate(rows_out, axis=0)


def _shift_window(cache_ref, new_t, out_ref, nb, dec):
    lane = lax.broadcasted_iota(jnp.int32, (1, WINDOW), 1)
    keep = lane < WINDOW - dec
    per_tile = LANES // dec
    for b in range(nb):
        src = new_t[:, (b // per_tile) * LANES:(b // per_tile + 1) * LANES]
        new_cols = pltpu.roll(src, (WINDOW - dec - (b % per_tile) * dec) % LANES, axis=1)
        old = pltpu.roll(cache_ref[b], WINDOW - dec, axis=1)
        out_ref[b] = jnp.where(keep, old, new_cols)


def _swa_sample(q, k, v, ck_ref, cv_ref, wk_ref, wv_ref, bias_c_ref, bias_n_ref, sinks_ref, nb, dec):
    lo = _lane_lo()
    qg = [q[:, j * LANES:(j + 1) * LANES] for j in range(N_HEADS // 2)]
    qgs = [_swap_halves(x) for x in qg]
    qb = jnp.concatenate(
        [_head_lhs(qg, qgs, h, lo).reshape(nb, dec, LANES) for h in range(N_HEADS)], axis=1)
    rows = N_HEADS * dec
    qb_bf = qb.astype(BF16)
    k_bf = k.astype(BF16)
    v_bf = v.astype(BF16)
    ck = ck_ref[...].astype(BF16)
    cv = cv_ref[...].astype(BF16)
    sc = jnp.einsum('bqc,bcp->bqp', qb_bf, ck, preferred_element_type=F32) + bias_c_ref[...][None]
    sn = _dot_nt(qb_bf.reshape(nb * rows, LANES), k_bf).reshape(nb, rows, nb * dec) + bias_n_ref[...]
    sink = jnp.concatenate(
        [jnp.full((1, dec, 1), sinks_ref[h] * LOG2E, F32) for h in range(N_HEADS)], axis=1)
    m = jnp.maximum(jnp.max(sc, axis=-1, keepdims=True), jnp.max(sn, axis=-1, keepdims=True))
    m = jnp.maximum(m, sink)
    pc = jnp.exp2(sc - m)
    pn = jnp.exp2(sn - m)
    l = jnp.sum(pc, axis=-1, keepdims=True) + jnp.sum(pn, axis=-1, keepdims=True) + jnp.exp2(sink - m)
    oc = jnp.einsum('bqp,bcp->bqc', pc.astype(BF16), cv, preferred_element_type=F32)
    on = _dot(pn.reshape(nb * rows, nb * dec).astype(BF16), v_bf).reshape(nb, rows, LANES)
    o = (oc + on) * (1.0 / l)
    o_heads = [o[:, h * dec:(h + 1) * dec, :].reshape(nb * dec, LANES) for h in range(N_HEADS)]
    _shift_window(ck_ref, k.T, wk_ref, nb, dec)
    _shift_window(cv_ref, v.T, wv_ref, nb, dec)
    return _merge_heads(o_heads, lo)


def _route(hn_bf, wr_ref, vec_ref):
    logits = _dot(hn_bf, wr_ref[...]) + _vec(vec_ref, VEC_ROUTER_BIAS, ROUTER_LANES)
    lane = lax.broadcasted_iota(jnp.int32, logits.shape, 1)
    big = jnp.int32(ROUTER_LANES)
    lc = jnp.where(lane < N_EXPERT_GROUPS, logits, NEG_INF)
    mx = jnp.max(lc, axis=-1, keepdims=True)
    g_idx = jnp.min(jnp.where(lc == mx, lane, big), axis=-1, keepdims=True)
    p_g = 1.0 / jnp.sum(jnp.exp(lc - mx), axis=-1, keepdims=True)
    e_lane = lane - N_EXPERT_GROUPS
    in_group = (e_lane >= 0) & (e_lane < N_EXPERTS) & ((e_lane >> 2) == g_idx)
    lf = jnp.where(in_group, logits, NEG_INF)
    v1 = jnp.max(lf, axis=-1, keepdims=True)
    i1 = jnp.min(jnp.where(lf == v1, lane, big), axis=-1, keepdims=True)
    lf2 = jnp.where(lane == i1, NEG_INF, lf)
    v2 = jnp.max(lf2, axis=-1, keepdims=True)
    i2 = jnp.min(jnp.where(lf2 == v2, lane, big), axis=-1, keepdims=True)
    e = jnp.exp(v2 - v1)
    w1 = p_g / (1.0 + e)
    w2 = w1 * e
    return lane, g_idx, i1, i2, w1, w2


ROUTER_ROWS = 32


def _route_rows(hn_bf, wrt_ref, brt_ref, tile):
    logits = _dot_nt(wrt_ref[...], hn_bf) + jnp.tile(brt_ref[...], (1, tile // LANES))
    row = lax.broadcasted_iota(jnp.int32, logits.shape, 0)
    big = jnp.int32(ROUTER_ROWS)
    lc = jnp.where(row < N_EXPERT_GROUPS, logits, NEG_INF)
    mx = jnp.max(lc, axis=0, keepdims=True)
    g_idx = jnp.min(jnp.where(lc == mx, row, big), axis=0, keepdims=True)
    p_g = 1.0 / jnp.sum(jnp.exp(lc - mx), axis=0, keepdims=True)
    e_row = row - N_EXPERT_GROUPS
    in_group = (e_row >= 0) & (e_row < N_EXPERTS) & ((e_row >> 2) == g_idx)
    lf = jnp.where(in_group, logits, NEG_INF)
    v1 = jnp.max(lf, axis=0, keepdims=True)
    i1 = jnp.min(jnp.where(lf == v1, row, big), axis=0, keepdims=True)
    lf2 = jnp.where(row == i1, NEG_INF, lf)
    v2 = jnp.max(lf2, axis=0, keepdims=True)
    i2 = jnp.min(jnp.where(lf2 == v2, row, big), axis=0, keepdims=True)
    e = jnp.exp(v2 - v1)
    w1 = p_g / (1.0 + e)
    w2 = w1 * e
    return g_idx, i1, i2, w1, w2


def _silu(x):
    return x * (1.0 / (1.0 + jnp.exp2(x * (-LOG2E))))


def _moe_dense(h, vec_ref, wr_ref, wg_ref, wu_ref, wd_ref, hcat_ref):
    hn_bf = _rms(h, _vec(vec_ref, VEC_FFN, D_MODEL)).astype(BF16)
    lane, _, i1, i2, w1, w2 = _route(hn_bf, wr_ref, vec_ref)
    combine = jnp.where(lane == i1, w1, jnp.where(lane == i2, w2, 0.0))
    for e in range(N_EXPERTS):
        gate = _dot(hn_bf, wg_ref[e])
        up = _dot(hn_bf, wu_ref[e])
        c = combine[:, N_EXPERT_GROUPS + e:N_EXPERT_GROUPS + e + 1]
        hcat_ref[:, e * D_EXPERT:(e + 1) * D_EXPERT] = (_silu(gate) * up * c).astype(BF16)
    return _dot(hcat_ref[...], wd_ref[...])


GROUP_LANE = EXPERTS_PER_GROUP
LOW_SHIFT = 8
SORT_BLOCK = 128


def _moe_grouped(h, vec_ref, wrt_ref, brt_ref, wg_ref, wu_ref, wd_ref, ustrict_ref, xs_ref, es_ref, ys_ref, tile):
    hn_bf = _rms(h, _vec(vec_ref, VEC_FFN, D_MODEL)).astype(BF16)
    g_idx, i1, i2, w1, w2 = _route_rows(hn_bf, wrt_ref, brt_ref, tile)

    row16 = lax.broadcasted_iota(jnp.int32, (16, tile), 0)
    onehot = row16 == g_idx
    gmat = jnp.where(onehot, 1.0, 0.0)
    before = _dot(gmat.astype(BF16), ustrict_ref[...])
    rank = jnp.sum(jnp.where(onehot, before, 0.0), axis=0, keepdims=True)
    offs = [jnp.float32(0.0)]
    for g in range(N_EXPERT_GROUPS - 1):
        offs.append(offs[-1] + jnp.sum(jnp.where(g_idx == g, 1.0, 0.0)))
    offv = jnp.zeros_like(rank)
    for g in range(1, N_EXPERT_GROUPS):
        offv = offv + jnp.where(g_idx == g, offs[g], 0.0)
    pos = (rank + offv).astype(jnp.int32)
    dest = lax.broadcasted_iota(jnp.int32, (tile, tile), 0)
    perm = jnp.where(dest == pos, 1.0, 0.0).astype(BF16)

    row8 = lax.broadcasted_iota(jnp.int32, (8, tile), 0)
    j1 = (i1 - N_EXPERT_GROUPS) & (EXPERTS_PER_GROUP - 1)
    j2 = (i2 - N_EXPERT_GROUPS) & (EXPERTS_PER_GROUP - 1)
    gates = jnp.where(row8 == j1, w1, jnp.where(row8 == j2, w2, 0.0))
    gates_hi = gates.astype(BF16).astype(F32)
    gates_lo = (gates - gates_hi).astype(BF16).astype(F32)
    top = gates_hi + jnp.where(row8 == GROUP_LANE, g_idx.astype(F32), 0.0)
    side = jnp.concatenate([top, gates_lo, jnp.zeros((LANES - 2 * LOW_SHIFT, tile), F32)], axis=0).T
    x_ext = jnp.concatenate([hn_bf, side.astype(BF16)], axis=1)
    srt = _dot(perm, x_ext)
    xs_ref[...] = srt[:, :D_MODEL].astype(BF16)
    es = srt[:, D_MODEL:]
    es_ref[...] = es + pltpu.roll(es, LANES - LOW_SHIFT, axis=1)

    def group_of(row):
        g = jnp.int32(0)
        for k in range(1, N_EXPERT_GROUPS):
            g = g + (jnp.float32(row) >= offs[k]).astype(jnp.int32)
        return g

    def group_pass(rows, g, keep, accumulate):
        xb = xs_ref[rows, :]
        eb = es_ref[rows, :]
        mine = jnp.where(eb[:, GROUP_LANE:GROUP_LANE + 1] == jnp.asarray(g).astype(F32), keep, 0.0)
        parts = []
        for j in range(EXPERTS_PER_GROUP):
            e = g * EXPERTS_PER_GROUP + j
            gate = _dot(xb, wg_ref[e])
            up = _dot(xb, wu_ref[e])
            parts.append((_silu(gate) * up * (eb[:, j:j + 1] * mine)).astype(BF16))
        wd = wd_ref[pl.ds(g * EXPERTS_PER_GROUP, EXPERTS_PER_GROUP)]
        out = _dot(jnp.concatenate(parts, axis=1), wd.reshape(EXPERTS_PER_GROUP * D_EXPERT, D_MODEL))
        if accumulate:
            ys_ref[rows, :] += out
        else:
            ys_ref[rows, :] = out

    nblk = tile // SORT_BLOCK
    for b in range(nblk):
        group_pass(pl.ds(b * SORT_BLOCK, SORT_BLOCK), group_of(b * SORT_BLOCK), 1.0, False)
    for g in range(1, N_EXPERT_GROUPS):
        start = offs[g].astype(jnp.int32)
        blk = jnp.minimum(lax.div(start, jnp.int32(SORT_BLOCK)), nblk - 1)
        inside = jnp.where(lax.rem(start, jnp.int32(SORT_BLOCK)) != 0, 1.0, 0.0)
        group_pass(pl.ds(pl.multiple_of(blk * SORT_BLOCK, SORT_BLOCK), SORT_BLOCK), g, inside, True)
    return lax.dot_general(perm, ys_ref[...].astype(BF16), (((0,), (0,)), ((), ())),
                           preferred_element_type=F32)


def _merge(x, a_out, b_out, vec_ref, w_out_ref):
    g_out = _vec(vec_ref, VEC_OUT, D_MODEL)
    mix = jnp.concatenate(
        [_rms(a_out, g_out[:, :A_WIDTH]), _rms(b_out, g_out[:, A_WIDTH:])], axis=1).astype(BF16)
    return x + _dot(mix, w_out_ref[...])


def _stage_expert_weights(wg_hbm, wu_hbm, wd_hbm, wg_s, wu_s, wd_s, st_a, st_d, sems):
    def gate_cp(e):
        return pltpu.make_async_copy(wg_hbm.at[e], st_a.at[0], sems.at[0])

    def up_cp(e):
        return pltpu.make_async_copy(wu_hbm.at[e], st_a.at[1], sems.at[1])

    def down_cp(e, slot):
        return pltpu.make_async_copy(wd_hbm.at[e], st_d.at[slot], sems.at[2 + slot])

    gate_cp(0).start()
    up_cp(0).start()
    down_cp(0, 0).start()

    def pair(p, carry):
        for slot in (0, 1):
            e = 2 * p + slot
            nxt = e + 1
            more = nxt < N_EXPERTS

            @pl.when(more)
            def _():
                down_cp(nxt, 1 - slot).start()

            gate_cp(e).wait()
            wg_s[e] = st_a[0].astype(BF16)

            @pl.when(more)
            def _():
                gate_cp(nxt).start()

            up_cp(e).wait()
            wu_s[e] = st_a[1].astype(BF16)

            @pl.when(more)
            def _():
                up_cp(nxt).start()

            down_cp(e, slot).wait()
            wd_s[e] = st_d[slot].astype(BF16)
        return carry

    lax.fori_loop(0, N_EXPERTS // 2, pair, 0)


def _bf16_weight_copies(wg_s, wu_s, wd_s, wg_o, wu_o, wd_o, sems):
    return (pltpu.make_async_copy(wg_s, wg_o, sems.at[4]),
            pltpu.make_async_copy(wu_s, wu_o, sems.at[5]),
            pltpu.make_async_copy(wd_s, wd_o, sems.at[6]))


def _prompt_kernel(x_ref, vec_ref, w_in_ref, segq_ref, segk_ref, wsp_ref, bsp_ref, bias_ref, sinks_ref,
                   w_out_ref, wrt_ref, brt_ref, ustrict_ref, wg_hbm, wu_hbm, wd_hbm,
                   y_ref, kwin_ref, vwin_ref, wg_o, wu_o, wd_o,
                   kprev_ref, vprev_ref, xs_ref, es_ref, ys_ref, wg_ref, wu_ref, wd_ref, st_a, st_d, sems,
                   *, tile, tiles_per_seq):
    step = pl.program_id(0)
    is_first = (step % tiles_per_seq) == 0

    @pl.when(step == 0)
    def _():
        _stage_expert_weights(wg_hbm, wu_hbm, wd_hbm, wg_ref, wu_ref, wd_ref, st_a, st_d, sems)
        for cp in _bf16_weight_copies(wg_ref, wu_ref, wd_ref, wg_o, wu_o, wd_o, sems):
            cp.start()

    @pl.when(step == pl.num_programs(0) - 1)
    def _():
        for cp in _bf16_weight_copies(wg_ref, wu_ref, wd_ref, wg_o, wu_o, wd_o, sems):
            cp.wait()

    @pl.when(is_first)
    def _():
        kprev_ref[...] = jnp.zeros_like(kprev_ref)
        vprev_ref[...] = jnp.zeros_like(vprev_ref)

    x = x_ref[...]
    u, va, q, k, v = _project(x, vec_ref, w_in_ref, segq_ref, segk_ref)
    kwin_ref[...] = k[tile - WINDOW:tile].T
    vwin_ref[...] = v[tile - WINDOW:tile].T
    a_out = u * _chunk_mlp(va.astype(BF16), wsp_ref, bsp_ref, tile)
    b_out = _swa_prompt(q * Q_SCALE, k, v, kprev_ref, vprev_ref, bias_ref, sinks_ref, is_first, tile)
    h = _merge(x, a_out, b_out, vec_ref, w_out_ref)
    y_ref[...] = h + _moe_grouped(h, vec_ref, wrt_ref, brt_ref, wg_ref, wu_ref, wd_ref, ustrict_ref,
                                  xs_ref, es_ref, ys_ref, tile)


def _sample_kernel(x_ref, ck_ref, cv_ref, vec_ref, w_in_ref, segq_ref, segk_ref, wsp_ref, bsp_ref,
                   bias_c_ref, bias_n_ref, sinks_ref, w_out_ref, wr_ref, wg_ref, wu_ref, wd_ref,
                   y_ref, wk_ref, wv_ref, va_ref, hcat_ref, *, tile, nb, dec):
    x = x_ref[...]
    u, va, q, k, v = _project(x, vec_ref, w_in_ref, segq_ref, segk_ref)
    for g in range(A_GROUPS):
        va_ref[pl.ds(g, tile, stride=A_GROUPS), :] = va[:, g * A_CH:(g + 1) * A_CH]
    a_out = u * _chunk_mlp(va.astype(BF16), wsp_ref, bsp_ref, tile)
    b_out = _swa_sample(q * Q_SCALE, k, v, ck_ref, cv_ref, wk_ref, wv_ref,
                        bias_c_ref, bias_n_ref, sinks_ref, nb, dec)
    h = _merge(x, a_out, b_out, vec_ref, w_out_ref)
    y_ref[...] = h + _moe_dense(h, vec_ref, wr_ref, wg_ref, wu_ref, wd_ref, hcat_ref)


def _const_spec(shape):
    nd = len(shape)
    return pl.BlockSpec(shape, lambda i: (0,) * nd, pipeline_mode=pl.Buffered(1))


def _row_spec(tile, width):
    return pl.BlockSpec((tile, width), lambda i: (i, 0))


def _smem_spec():
    return pl.BlockSpec(memory_space=pltpu.SMEM)


def _layer_weights(l, g_attn_norm, w_in, g_v_a, g_q, g_k, g_out_a, g_out_b, w_out, g_ffn_norm,
                   w_coarse, b_coarse, w_fine, b_fine):
    pad = ROUTER_LANES - N_EXPERT_GROUPS - N_EXPERTS
    wr = jnp.concatenate([w_coarse[l], w_fine[l], jnp.zeros((D_MODEL, pad), F32)], axis=1)
    row_pad = ROUTER_ROWS - N_EXPERT_GROUPS - N_EXPERTS
    wrt = jnp.concatenate([w_coarse[l].T, w_fine[l].T, jnp.zeros((row_pad, D_MODEL), F32)], axis=0)
    brt = jnp.concatenate([b_coarse[l], b_fine[l], jnp.zeros((row_pad,), F32)])
    vec = jnp.concatenate([
        g_attn_norm[l], g_v_a[l].reshape(A_WIDTH), jnp.tile(g_q[l], N_HEADS), jnp.tile(g_k[l], N_KV),
        g_out_a[l], g_out_b[l], g_ffn_norm[l], b_coarse[l], b_fine[l], jnp.zeros((pad,), F32)])
    return dict(
        vec=vec.reshape(1, VEC_LANES),
        w_in=w_in[l].astype(BF16),
        segq=jnp.asarray(_seg_ones(B_WIDTH, HEAD_DIM), BF16),
        segk=jnp.asarray(_seg_ones(KV_WIDTH, HEAD_DIM), BF16),
        w_out=w_out[l].astype(BF16),
        wr=wr.astype(BF16),
        wrt=wrt.astype(BF16),
        brt=jnp.broadcast_to(brt[:, None], (ROUTER_ROWS, LANES)),
    )


def _spatial_tables(ws, bs, period, tile):
    pair = min(PAIR, tile)
    reps = pair // period
    if reps <= 2:
        tri = (ws[:, :period, :period] * _causal_block_mask(period, period)).astype(BF16)
        wbd = jnp.concatenate([jnp.pad(tri, ((0, 0), (0, 0), (r * period, (reps - 1 - r) * period)))
                               for r in range(reps)], axis=1)
    else:
        wbd = (jnp.tile(ws[:, :period, :period], (1, reps, reps)) * _causal_block_mask(period, pair)).astype(BF16)
    bsp = jnp.repeat(bs[:, :period].T, A_CH, axis=1)
    return wbd, bsp


_HEAD_NAMES = ("vec", "w_in", "segq", "segk")
_TAIL_NAMES = ("w_out", "wr")


def _upper_triangle(n):
    return jnp.asarray(np.triu(np.ones((n, n), np.float32), 1), BF16)


def _run_prompt(x2d, lw, wbd, bsp, sinks, seq_len, w_gate, w_up, w_down):
    n_tok = x2d.shape[0]
    tile = PROMPT_TILE
    tiles_per_seq = seq_len // tile
    n_seq = n_tok // seq_len
    bias = jnp.asarray(_prompt_bias())
    head = [lw[n] for n in _HEAD_NAMES]
    tail = [lw["w_out"], lw["wrt"], lw["brt"], _upper_triangle(tile)]
    experts = [w_gate, w_up, w_down]
    consts = head + [wbd, bsp, bias]
    any_spec = pl.BlockSpec(memory_space=pl.ANY)
    in_specs = ([_row_spec(tile, D_MODEL)] + [_const_spec(a.shape) for a in consts] + [_smem_spec()]
                + [_const_spec(a.shape) for a in tail] + [any_spec] * len(experts))
    win_spec = pl.BlockSpec((KV_WIDTH, WINDOW), lambda i: (i // tiles_per_seq, 0))
    out_shape = (jax.ShapeDtypeStruct((n_tok, D_MODEL), F32),
                 jax.ShapeDtypeStruct((n_seq * KV_WIDTH, WINDOW), F32),
                 jax.ShapeDtypeStruct((n_seq * KV_WIDTH, WINDOW), F32),
                 *[jax.ShapeDtypeStruct(w.shape, BF16) for w in experts])
    out_specs = (_row_spec(tile, D_MODEL), win_spec, win_spec, any_spec, any_spec, any_spec)
    kern = functools.partial(_prompt_kernel, tile=tile, tiles_per_seq=tiles_per_seq)
    return pl.pallas_call(
        kern,
        out_shape=out_shape,
        grid=(n_tok // tile,),
        in_specs=in_specs,
        out_specs=out_specs,
        scratch_shapes=[pltpu.VMEM((N_KV, WINDOW, KV_WIDTH), BF16), pltpu.VMEM((N_KV, WINDOW, KV_WIDTH), BF16),
                        pltpu.VMEM((tile, D_MODEL), BF16), pltpu.VMEM((tile, LANES), F32),
                        pltpu.VMEM((tile, D_MODEL), F32),
                        *[pltpu.VMEM(w.shape, BF16) for w in experts],
                        pltpu.VMEM((2,) + w_gate.shape[1:], F32), pltpu.VMEM((2,) + w_down.shape[1:], F32),
                        pltpu.SemaphoreType.DMA((7,))],
        compiler_params=pltpu.CompilerParams(dimension_semantics=("arbitrary",),
                                             vmem_limit_bytes=VMEM_LIMIT_BYTES),
        name="layer_prompt",
    )(x2d, *consts, sinks, *tail, *experts)


def _run_sample(x2d, ck_t, cv_t, lw, wbd, bsp, sinks, dec, wg_bf, wu_bf, wd_bf):
    n_tok = x2d.shape[0]
    n_seq = ck_t.shape[0]
    tile = SAMPLE_TILE
    nb = tile // dec
    bc, bn = _sample_bias(nb, dec)
    head = [lw[n] for n in _HEAD_NAMES]
    tail = [lw[n] for n in _TAIL_NAMES] + [wg_bf, wu_bf, wd_bf.reshape(N_EXPERTS * D_EXPERT, D_MODEL)]
    consts = head + [wbd, bsp, jnp.asarray(bc), jnp.asarray(bn)]
    cache_spec = pl.BlockSpec((nb, KV_WIDTH, WINDOW), lambda i: (i, 0, 0))
    in_specs = ([_row_spec(tile, D_MODEL), cache_spec, cache_spec] + [_const_spec(a.shape) for a in consts]
                + [_smem_spec()] + [_const_spec(a.shape) for a in tail])
    out_shape = (jax.ShapeDtypeStruct((n_tok, D_MODEL), F32),
                 jax.ShapeDtypeStruct((n_seq, KV_WIDTH, WINDOW), F32),
                 jax.ShapeDtypeStruct((n_seq, KV_WIDTH, WINDOW), F32),
                 jax.ShapeDtypeStruct((n_tok * A_GROUPS, A_CH), F32))
    out_specs = (_row_spec(tile, D_MODEL), cache_spec, cache_spec, _row_spec(tile * A_GROUPS, A_CH))
    kern = functools.partial(_sample_kernel, tile=tile, nb=nb, dec=dec)
    return pl.pallas_call(
        kern,
        out_shape=out_shape,
        grid=(n_tok // tile,),
        in_specs=in_specs,
        out_specs=out_specs,
        scratch_shapes=[pltpu.VMEM((tile, N_EXPERTS * D_EXPERT), BF16)],
        compiler_params=pltpu.CompilerParams(dimension_semantics=("arbitrary",),
                                             vmem_limit_bytes=VMEM_LIMIT_BYTES),
        name="layer_sample",
    )(x2d, ck_t, cv_t, *consts, sinks, *tail)


def _positions_last(c):
    b, w = c.shape[:2]
    return jnp.transpose(c, (0, 2, 3, 1)).reshape(b, KV_WIDTH, w)


def _positions_first(c_t):
    b, _, w = c_t.shape
    return jnp.transpose(c_t.reshape(b, N_KV, HEAD_DIM, w), (0, 3, 1, 2))


def kernel(x_prompt, x_sample, cache_k, cache_v, g_attn_norm, w_in, g_v_a, w_spatial, b_spatial, g_q, g_k, attn_sinks, g_out_a, g_out_b, w_out, g_ffn_norm, w_coarse, b_coarse, w_fine, b_fine, w_gate, w_up, w_down):
    depth = w_in.shape[0]
    batch, seq, _ = x_prompt.shape
    dbatch, dec, _ = x_sample.shape
    win = cache_k.shape[2]
    assert win == WINDOW and seq % PROMPT_TILE == 0 and (dbatch * dec) % SAMPLE_TILE == 0
    assert PAIR % dec == 0 and SAMPLE_TILE % dec == 0 and LANES % dec == 0

    hp = x_prompt.reshape(batch * seq, D_MODEL)
    hs = x_sample.reshape(dbatch * dec, D_MODEL)
    kp_l, vp_l, ks_l, vs_l, cv_l = [], [], [], [], []
    for l in range(depth):
        lw = _layer_weights(l, g_attn_norm, w_in, g_v_a, g_q, g_k, g_out_a, g_out_b, w_out, g_ffn_norm,
                            w_coarse, b_coarse, w_fine, b_fine)
        sinks = attn_sinks[l].astype(F32)
        wbd_p, bsp_p = _spatial_tables(w_spatial[l], b_spatial[l], CHUNK, PROMPT_TILE)
        wbd_s, bsp_s = _spatial_tables(w_spatial[l], b_spatial[l], dec, SAMPLE_TILE)

        hp, kp, vp, wg_bf, wu_bf, wd_bf = _run_prompt(hp, lw, wbd_p, bsp_p, sinks, seq,
                                                      w_gate[l], w_up[l], w_down[l])
        kp_l.append(_positions_first(kp.reshape(batch, KV_WIDTH, WINDOW)))
        vp_l.append(_positions_first(vp.reshape(batch, KV_WIDTH, WINDOW)))

        hs, wk, wv, va = _run_sample(hs, _positions_last(cache_k[l]), _positions_last(cache_v[l]),
                                     lw, wbd_s, bsp_s, sinks, dec, wg_bf, wu_bf, wd_bf)
        ks_l.append(_positions_first(wk))
        vs_l.append(_positions_first(wv))
        cv_l.append(va.reshape(dbatch, dec, A_GROUPS, A_CH))

    return (hp.reshape(batch, seq, D_MODEL), hs.reshape(dbatch, dec, D_MODEL),
            jnp.stack(kp_l, axis=0), jnp.stack(vp_l, axis=0),
            jnp.stack(ks_l, axis=0), jnp.stack(vs_l, axis=0), jnp.stack(cv_l, axis=0))
```

```python
import functools

import numpy as np
import jax
import jax.numpy as jnp
from jax import lax
from jax.experimental import pallas as pl
from jax.experimental.pallas import tpu as pltpu

D_MODEL = 1024
CHUNK = 128
A_GROUPS = 4
A_WIDTH = 512
A_CH = 128
N_HEADS = 8
N_KV = 2
Q_PER_KV = 4
HEAD_DIM = 64
B_WIDTH = 512
KV_WIDTH = 128
WINDOW = 128
IN_COLS = 2 * A_WIDTH + B_WIDTH + 2 * KV_WIDTH
N_EXPERT_GROUPS = 4
EXPERTS_PER_GROUP = 4
N_EXPERTS = 16
D_EXPERT = 256
EPS = 1e-6

LANES = 128
PAIR = 2 * CHUNK
ROUTER_LANES = 128
PROMPT_TILE = 512
SAMPLE_TILE = 128
VMEM_LIMIT_BYTES = 60 * 1024 * 1024

F32 = jnp.float32
BF16 = jnp.bfloat16
NEG_INF = float("-inf")
LOG2E = 1.4426950408889634
Q_SCALE = (HEAD_DIM ** -0.5) * LOG2E


def _slopes():
    return np.array([2.0 ** (-8.0 * (h + 1) / N_HEADS) for h in range(N_HEADS)], np.float64)


def _prompt_bias():
    t = np.arange(WINDOW)[:, None]
    s = np.arange(2 * WINDOW)[None, :]
    dist = t + WINDOW - s
    valid = (dist >= 0) & (dist < WINDOW)
    sl = _slopes()
    out = np.full((N_KV, Q_PER_KV * WINDOW, 2 * WINDOW), -np.inf, np.float32)
    for kh in range(N_KV):
        for g in range(Q_PER_KV):
            b = np.where(valid, -sl[kh * Q_PER_KV + g] * LOG2E * dist, -np.inf)
            out[kh, g * WINDOW:(g + 1) * WINDOW] = b
    return out


def _sample_bias(nb, dec):
    sl = _slopes()
    t = np.arange(dec)[:, None]
    j = np.arange(WINDOW)[None, :]
    dist_c = t + WINDOW - j
    valid_c = (dist_c >= 0) & (dist_c < WINDOW)
    bc = np.full((N_HEADS * dec, WINDOW), -np.inf, np.float32)
    tp = np.arange(dec)[None, :]
    dist_n = t - tp
    valid_n = dist_n >= 0
    bn = np.full((nb, N_HEADS * dec, nb * dec), -np.inf, np.float32)
    for h in range(N_HEADS):
        bc[h * dec:(h + 1) * dec] = np.where(valid_c, -sl[h] * LOG2E * dist_c, -np.inf)
        blk = np.where(valid_n, -sl[h] * LOG2E * dist_n, -np.inf)
        for b in range(nb):
            bn[b, h * dec:(h + 1) * dec, b * dec:(b + 1) * dec] = blk
    return bc, bn


def _seg_ones(width, seg):
    i = np.arange(width)
    return (i[:, None] // seg == i[None, :] // seg).astype(np.float32)


def _causal_block_mask(period, size):
    i = np.arange(size)
    same = i[:, None] // period == i[None, :] // period
    return (same & (i[None, :] % period <= i[:, None] % period)).astype(np.float32)


def _dot(a, b):
    return jnp.dot(a, b, preferred_element_type=F32)


def _dot_nt(a, b):
    return lax.dot_general(a, b, (((1,), (1,)), ((), ())), preferred_element_type=F32)


def _rms(x, g):
    ms = jnp.mean(x * x, axis=-1, keepdims=True)
    return (x * lax.rsqrt(ms + EPS)) * g


def _seg_rms(x, ones_bf, g, seg):
    ss = _dot((x * x).astype(BF16), ones_bf)
    return (x * lax.rsqrt(ss * (1.0 / seg) + EPS)) * g


GELU_K0 = -2.0 * 0.7978845608028654 * LOG2E
GELU_K1 = GELU_K0 * 0.044715


def _gelu_tanh(x):
    t = (x * x) * GELU_K1 + GELU_K0
    return x * (1.0 / (1.0 + jnp.exp2(x * t)))


VEC_ATTN = 0
VEC_VA = VEC_ATTN + D_MODEL
VEC_Q = VEC_VA + A_WIDTH
VEC_K = VEC_Q + B_WIDTH
VEC_OUT = VEC_K + KV_WIDTH
VEC_FFN = VEC_OUT + D_MODEL
VEC_ROUTER_BIAS = VEC_FFN + D_MODEL
VEC_LANES = VEC_ROUTER_BIAS + ROUTER_LANES


def _vec(vec_ref, start, width):
    return vec_ref[:, start:start + width]


def _project(x, vec_ref, w_in_ref, segq_ref, segk_ref):
    xn = _rms(x, _vec(vec_ref, VEC_ATTN, D_MODEL)).astype(BF16)
    z = _dot(xn, w_in_ref[...])
    i1, i2, i3, i4 = A_WIDTH, 2 * A_WIDTH, 2 * A_WIDTH + B_WIDTH, 2 * A_WIDTH + B_WIDTH + KV_WIDTH
    u = _gelu_tanh(z[:, :i1])
    va_pre = _gelu_tanh(z[:, i1:i2])
    g_va = _vec(vec_ref, VEC_VA, A_WIDTH)
    va = jnp.concatenate(
        [_rms(va_pre[:, g * A_CH:(g + 1) * A_CH], g_va[:, g * A_CH:(g + 1) * A_CH]) for g in range(A_GROUPS)],
        axis=1)
    q = _seg_rms(z[:, i2:i3], segq_ref[...], _vec(vec_ref, VEC_Q, B_WIDTH), HEAD_DIM)
    k = _seg_rms(z[:, i3:i4], segk_ref[...], _vec(vec_ref, VEC_K, KV_WIDTH), HEAD_DIM)
    v = z[:, i4:]
    return u, va, q, k, v


def _chunk_mlp(va_bf, wsp_ref, bsp_ref, tile):
    pair = wsp_ref.shape[1]
    n_pairs = tile // pair
    outs = []
    for g in range(A_GROUPS):
        cols = [va_bf[p * pair:(p + 1) * pair, g * A_CH:(g + 1) * A_CH] for p in range(n_pairs)]
        rhs = cols[0] if n_pairs == 1 else jnp.concatenate(cols, axis=1)
        o = _dot(wsp_ref[g], rhs)
        rows = [o[:, p * A_CH:(p + 1) * A_CH] for p in range(n_pairs)]
        outs.append(rows[0] if n_pairs == 1 else jnp.concatenate(rows, axis=0))
    mixed = jnp.concatenate(outs, axis=1)
    bias = bsp_ref[...]
    reps = tile // bias.shape[0]
    return mixed + (bias if reps == 1 else jnp.concatenate([bias] * reps, axis=0))


def _lane_lo():
    return lax.broadcasted_iota(jnp.int32, (1, LANES), 1) < HEAD_DIM


def _swap_halves(x):
    return pltpu.roll(x, HEAD_DIM, axis=1)


def _head_lhs(q_groups, q_groups_swapped, head, lo):
    j, half = divmod(head, 2)
    kv = head // Q_PER_KV
    src = q_groups[j] if half == kv else q_groups_swapped[j]
    return jnp.where(lo if kv == 0 else jnp.logical_not(lo), src, 0.0)


def _merge_heads(o_heads, lo):
    groups = []
    for j in range(N_HEADS // 2):
        kv = (2 * j) // Q_PER_KV
        if kv == 0:
            groups.append(jnp.where(lo, o_heads[2 * j], _swap_halves(o_heads[2 * j + 1])))
        else:
            groups.append(jnp.where(lo, _swap_halves(o_heads[2 * j]), o_heads[2 * j + 1]))
    return jnp.concatenate(groups, axis=1)


def _dup_halves(x, lo):
    xs = _swap_halves(x)
    return jnp.where(lo, x, xs).astype(BF16), jnp.where(lo, xs, x).astype(BF16)


def _swa_prompt(q, k, v, kprev_ref, vprev_ref, bias_ref, sinks_ref, is_first, tile):
    lo = _lane_lo()
    hi = jnp.logical_not(lo)
    col = lax.broadcasted_iota(jnp.int32, (WINDOW, 2 * WINDOW), 1)
    first_mask = jnp.where(col < WINDOW, jnp.where(is_first, NEG_INF, 0.0).astype(F32), 0.0)
    kd = _dup_halves(k, lo)
    vd = _dup_halves(v, lo)
    nblk = tile // WINDOW
    rows_out = []
    for i in range(nblk):
        r0, r1 = i * WINDOW, (i + 1) * WINDOW
        qg = [q[r0:r1, j * LANES:(j + 1) * LANES] for j in range(N_HEADS // 2)]
        o_heads = []
        for kh in range(N_KV):
            if i == 0:
                kp, vp = kprev_ref[kh], vprev_ref[kh]
            else:
                kp, vp = kd[kh][r0 - WINDOW:r0], vd[kh][r0 - WINDOW:r0]
            kb = jnp.concatenate([kp, kd[kh][r0:r1]], axis=0)
            vb = jnp.concatenate([vp, vd[kh][r0:r1]], axis=0)
            heads = [kh * Q_PER_KV + g for g in range(Q_PER_KV)]
            lhs = jnp.concatenate(
                [jnp.where(lo if h % 2 == 0 else hi, qg[h // 2], 0.0) for h in heads], axis=0).astype(BF16)
            s = _dot_nt(lhs, kb) + bias_ref[kh]
            ps, linvs = [], []
            for g in range(Q_PER_KV):
                sg = s[g * WINDOW:(g + 1) * WINDOW]
                if i == 0:
                    sg = sg + first_mask
                sink = sinks_ref[heads[g]] * LOG2E
                mg = jnp.maximum(jnp.max(sg, axis=-1, keepdims=True), sink)
                pg = jnp.exp2(sg - mg)
                lg = jnp.sum(pg, axis=-1, keepdims=True) + jnp.exp2(sink - mg)
                ps.append(pg.astype(BF16))
                linvs.append(1.0 / lg)
            o = _dot(jnp.concatenate(ps, axis=0), vb)
            o_heads += [o[g * WINDOW:(g + 1) * WINDOW] * linvs[g] for g in range(Q_PER_KV)]
        rows_out.append(jnp.concatenate(
            [jnp.where(lo, o_heads[2 * j], o_heads[2 * j + 1]) for j in range(N_HEADS // 2)], axis=1))
    for kh in range(N_KV):
        kprev_ref[kh] = kd[kh][tile - WINDOW:tile]
        vprev_ref[kh] = vd[kh][tile - WINDOW:tile]
    return jnp.concatenate(rows_out, axis=0)


def _shift_window(cache_ref, new_t, out_ref, nb, dec):
    lane = lax.broadcasted_iota(jnp.int32, (1, WINDOW), 1)
    keep = lane < WINDOW - dec
    per_tile = LANES // dec
    for b in range(nb):
        src = new_t[:, (b // per_tile) * LANES:(b // per_tile + 1) * LANES]
        new_cols = pltpu.roll(src, (WINDOW - dec - (b % per_tile) * dec) % LANES, axis=1)
        old = pltpu.roll(cache_ref[b], WINDOW - dec, axis=1)
        out_ref[b] = jnp.where(keep, old, new_cols)


def _swa_sample(q, k, v, ck_ref, cv_ref, wk_ref, wv_ref, bias_c_ref, bias_n_ref, sinks_ref, nb, dec):
    lo = _lane_lo()
    qg = [q[:, j * LANES:(j + 1) * LANES] for j in range(N_HEADS // 2)]
    qgs = [_swap_halves(x) for x in qg]
    qb = jnp.concatenate(
        [_head_lhs(qg, qgs, h, lo).reshape(nb, dec, LANES) for h in range(N_HEADS)], axis=1)
    rows = N_HEADS * dec
    qb_bf = qb.astype(BF16)
    k_bf = k.astype(BF16)
    v_bf = v.astype(BF16)
    ck = ck_ref[...].astype(BF16)
    cv = cv_ref[...].astype(BF16)
    sc = jnp.einsum('bqc,bcp->bqp', qb_bf, ck, preferred_element_type=F32) + bias_c_ref[...][None]
    sn = _dot_nt(qb_bf.reshape(nb * rows, LANES), k_bf).reshape(nb, rows, nb * dec) + bias_n_ref[...]
    sink = jnp.concatenate(
        [jnp.full((1, dec, 1), sinks_ref[h] * LOG2E, F32) for h in range(N_HEADS)], axis=1)
    m = jnp.maximum(jnp.max(sc, axis=-1, keepdims=True), jnp.max(sn, axis=-1, keepdims=True))
    m = jnp.maximum(m, sink)
    pc = jnp.exp2(sc - m)
    pn = jnp.exp2(sn - m)
    l = jnp.sum(pc, axis=-1, keepdims=True) + jnp.sum(pn, axis=-1, keepdims=True) + jnp.exp2(sink - m)
    oc = jnp.einsum('bqp,bcp->bqc', pc.astype(BF16), cv, preferred_element_type=F32)
    on = _dot(pn.reshape(nb * rows, nb * dec).astype(BF16), v_bf).reshape(nb, rows, LANES)
    o = (oc + on) * (1.0 / l)
    o_heads = [o[:, h * dec:(h + 1) * dec, :].reshape(nb * dec, LANES) for h in range(N_HEADS)]
    _shift_window(ck_ref, k.T, wk_ref, nb, dec)
    _shift_window(cv_ref, v.T, wv_ref, nb, dec)
    return _merge_heads(o_heads, lo)


def _route(hn_bf, wr_ref, vec_ref):
    logits = _dot(hn_bf, wr_ref[...]) + _vec(vec_ref, VEC_ROUTER_BIAS, ROUTER_LANES)
    lane = lax.broadcasted_iota(jnp.int32, logits.shape, 1)
    big = jnp.int32(ROUTER_LANES)
    lc = jnp.where(lane < N_EXPERT_GROUPS, logits, NEG_INF)
    mx = jnp.max(lc, axis=-1, keepdims=True)
    g_idx = jnp.min(jnp.where(lc == mx, lane, big), axis=-1, keepdims=True)
    p_g = 1.0 / jnp.sum(jnp.exp(lc - mx), axis=-1, keepdims=True)
    e_lane = lane - N_EXPERT_GROUPS
    in_group = (e_lane >= 0) & (e_lane < N_EXPERTS) & ((e_lane >> 2) == g_idx)
    lf = jnp.where(in_group, logits, NEG_INF)
    v1 = jnp.max(lf, axis=-1, keepdims=True)
    i1 = jnp.min(jnp.where(lf == v1, lane, big), axis=-1, keepdims=True)
    lf2 = jnp.where(lane == i1, NEG_INF, lf)
    v2 = jnp.max(lf2, axis=-1, keepdims=True)
    i2 = jnp.min(jnp.where(lf2 == v2, lane, big), axis=-1, keepdims=True)
    e = jnp.exp(v2 - v1)
    w1 = p_g / (1.0 + e)
    w2 = w1 * e
    return lane, g_idx, i1, i2, w1, w2


ROUTER_ROWS = 32


def _route_rows(hn_bf, wrt_ref, brt_ref, tile):
    logits = _dot_nt(wrt_ref[...], hn_bf) + jnp.tile(brt_ref[...], (1, tile // LANES))
    row = lax.broadcasted_iota(jnp.int32, logits.shape, 0)
    big = jnp.int32(ROUTER_ROWS)
    lc = jnp.where(row < N_EXPERT_GROUPS, logits, NEG_INF)
    mx = jnp.max(lc, axis=0, keepdims=True)
    g_idx = jnp.min(jnp.where(lc == mx, row, big), axis=0, keepdims=True)
    p_g = 1.0 / jnp.sum(jnp.exp(lc - mx), axis=0, keepdims=True)
    e_row = row - N_EXPERT_GROUPS
    in_group = (e_row >= 0) & (e_row < N_EXPERTS) & ((e_row >> 2) == g_idx)
    lf = jnp.where(in_group, logits, NEG_INF)
    v1 = jnp.max(lf, axis=0, keepdims=True)
    i1 = jnp.min(jnp.where(lf == v1, row, big), axis=0, keepdims=True)
    lf2 = jnp.where(row == i1, NEG_INF, lf)
    v2 = jnp.max(lf2, axis=0, keepdims=True)
    i2 = jnp.min(jnp.where(lf2 == v2, row, big), axis=0, keepdims=True)
    e = jnp.exp(v2 - v1)
    w1 = p_g / (1.0 + e)
    w2 = w1 * e
    return g_idx, i1, i2, w1, w2


def _silu(x):
    return x * (1.0 / (1.0 + jnp.exp2(x * (-LOG2E))))


def _moe_dense(h, vec_ref, wr_ref, wg_ref, wu_ref, wd_ref, hcat_ref):
    hn_bf = _rms(h, _vec(vec_ref, VEC_FFN, D_MODEL)).astype(BF16)
    lane, _, i1, i2, w1, w2 = _route(hn_bf, wr_ref, vec_ref)
    combine = jnp.where(lane == i1, w1, jnp.where(lane == i2, w2, 0.0))
    for e in range(N_EXPERTS):
        gate = _dot(hn_bf, wg_ref[e])
        up = _dot(hn_bf, wu_ref[e])
        c = combine[:, N_EXPERT_GROUPS + e:N_EXPERT_GROUPS + e + 1]
        hcat_ref[:, e * D_EXPERT:(e + 1) * D_EXPERT] = (_silu(gate) * up * c).astype(BF16)
    return _dot(hcat_ref[...], wd_ref[...])


GROUP_LANE = EXPERTS_PER_GROUP
LOW_SHIFT = 8
SORT_BLOCK = 128


def _moe_grouped(h, vec_ref, wrt_ref, brt_ref, wg_ref, wu_ref, wd_ref, ustrict_ref, xs_ref, es_ref, ys_ref, tile):
    hn_bf = _rms(h, _vec(vec_ref, VEC_FFN, D_MODEL)).astype(BF16)
    g_idx, i1, i2, w1, w2 = _route_rows(hn_bf, wrt_ref, brt_ref, tile)

    row16 = lax.broadcasted_iota(jnp.int32, (16, tile), 0)
    onehot = row16 == g_idx
    gmat = jnp.where(onehot, 1.0, 0.0)
    before = _dot(gmat.astype(BF16), ustrict_ref[...])
    rank = jnp.sum(jnp.where(onehot, before, 0.0), axis=0, keepdims=True)
    offs = [jnp.float32(0.0)]
    for g in range(N_EXPERT_GROUPS - 1):
        offs.append(offs[-1] + jnp.sum(jnp.where(g_idx == g, 1.0, 0.0)))
    offv = jnp.zeros_like(rank)
    for g in range(1, N_EXPERT_GROUPS):
        offv = offv + jnp.where(g_idx == g, offs[g], 0.0)
    pos = (rank + offv).astype(jnp.int32)
    dest = lax.broadcasted_iota(jnp.int32, (tile, tile), 0)
    perm = jnp.where(dest == pos, 1.0, 0.0).astype(BF16)

    row8 = lax.broadcasted_iota(jnp.int32, (8, tile), 0)
    j1 = (i1 - N_EXPERT_GROUPS) & (EXPERTS_PER_GROUP - 1)
    j2 = (i2 - N_EXPERT_GROUPS) & (EXPERTS_PER_GROUP - 1)
    gates = jnp.where(row8 == j1, w1, jnp.where(row8 == j2, w2, 0.0))
    gates_hi = gates.astype(BF16).astype(F32)
    gates_lo = (gates - gates_hi).astype(BF16).astype(F32)
    top = gates_hi + jnp.where(row8 == GROUP_LANE, g_idx.astype(F32), 0.0)
    side = jnp.concatenate([top, gates_lo, jnp.zeros((LANES - 2 * LOW_SHIFT, tile), F32)], axis=0).T
    x_ext = jnp.concatenate([hn_bf, side.astype(BF16)], axis=1)
    srt = _dot(perm, x_ext)
    xs_ref[...] = srt[:, :D_MODEL].astype(BF16)
    es = srt[:, D_MODEL:]
    es_ref[...] = es + pltpu.roll(es, LANES - LOW_SHIFT, axis=1)

    def group_of(row):
        g = jnp.int32(0)
        for k in range(1, N_EXPERT_GROUPS):
            g = g + (jnp.float32(row) >= offs[k]).astype(jnp.int32)
        return g

    def group_pass(rows, g, keep, accumulate):
        xb = xs_ref[rows, :]
        eb = es_ref[rows, :]
        mine = jnp.where(eb[:, GROUP_LANE:GROUP_LANE + 1] == jnp.asarray(g).astype(F32), keep, 0.0)
        parts = []
        for j in range(EXPERTS_PER_GROUP):
            e = g * EXPERTS_PER_GROUP + j
            gate = _dot(xb, wg_ref[e])
            up = _dot(xb, wu_ref[e])
            parts.append((_silu(gate) * up * (eb[:, j:j + 1] * mine)).astype(BF16))
        wd = wd_ref[pl.ds(g * EXPERTS_PER_GROUP, EXPERTS_PER_GROUP)]
        out = _dot(jnp.concatenate(parts, axis=1), wd.reshape(EXPERTS_PER_GROUP * D_EXPERT, D_MODEL))
        if accumulate:
            ys_ref[rows, :] += out
        else:
            ys_ref[rows, :] = out

    nblk = tile // SORT_BLOCK
    for b in range(nblk):
        group_pass(pl.ds(b * SORT_BLOCK, SORT_BLOCK), group_of(b * SORT_BLOCK), 1.0, False)
    for g in range(1, N_EXPERT_GROUPS):
        start = offs[g].astype(jnp.int32)
        blk = jnp.minimum(lax.div(start, jnp.int32(SORT_BLOCK)), nblk - 1)
        inside = jnp.where(lax.rem(start, jnp.int32(SORT_BLOCK)) != 0, 1.0, 0.0)
        group_pass(pl.ds(pl.multiple_of(blk * SORT_BLOCK, SORT_BLOCK), SORT_BLOCK), g, inside, True)
    return lax.dot_general(perm, ys_ref[...].astype(BF16), (((0,), (0,)), ((), ())),
                           preferred_element_type=F32)


def _merge(x, a_out, b_out, vec_ref, w_out_ref):
    g_out = _vec(vec_ref, VEC_OUT, D_MODEL)
    mix = jnp.concatenate(
        [_rms(a_out, g_out[:, :A_WIDTH]), _rms(b_out, g_out[:, A_WIDTH:])], axis=1).astype(BF16)
    return x + _dot(mix, w_out_ref[...])


def _stage_expert_weights(wg_hbm, wu_hbm, wd_hbm, wg_s, wu_s, wd_s, st_a, st_d, sems):
    def gate_cp(e):
        return pltpu.make_async_copy(wg_hbm.at[e], st_a.at[0], sems.at[0])

    def up_cp(e):
        return pltpu.make_async_copy(wu_hbm.at[e], st_a.at[1], sems.at[1])

    def down_cp(e, slot):
        return pltpu.make_async_copy(wd_hbm.at[e], st_d.at[slot], sems.at[2 + slot])

    gate_cp(0).start()
    up_cp(0).start()
    down_cp(0, 0).start()

    def pair(p, carry):
        for slot in (0, 1):
            e = 2 * p + slot
            nxt = e + 1
            more = nxt < N_EXPERTS

            @pl.when(more)
            def _():
                down_cp(nxt, 1 - slot).start()

            gate_cp(e).wait()
            wg_s[e] = st_a[0].astype(BF16)

            @pl.when(more)
            def _():
                gate_cp(nxt).start()

            up_cp(e).wait()
            wu_s[e] = st_a[1].astype(BF16)

            @pl.when(more)
            def _():
                up_cp(nxt).start()

            down_cp(e, slot).wait()
            wd_s[e] = st_d[slot].astype(BF16)
        return carry

    lax.fori_loop(0, N_EXPERTS // 2, pair, 0)


def _stage_dense_weights(w_in_hbm, w_out_hbm, w_in_s, w_out_s, st_a, st_d, sems):
    cols = st_a.shape[2]
    rows = st_d.shape[1]
    n_in = w_in_s.shape[1] // cols
    n_out = w_out_s.shape[0] // rows

    def in_cp(c):
        return pltpu.make_async_copy(w_in_hbm.at[:, pl.ds(c * cols, cols)], st_a.at[c % 2], sems.at[c % 2])

    def out_cp(c):
        return pltpu.make_async_copy(w_out_hbm.at[pl.ds(c * rows, rows), :], st_d.at[c % 2], sems.at[2 + c % 2])

    for c in range(min(2, n_in)):
        in_cp(c).start()
    for c in range(min(2, n_out)):
        out_cp(c).start()
    for c in range(n_in):
        in_cp(c).wait()
        w_in_s[:, c * cols:(c + 1) * cols] = st_a[c % 2].astype(BF16)
        if c + 2 < n_in:
            in_cp(c + 2).start()
    for c in range(n_out):
        out_cp(c).wait()
        w_out_s[c * rows:(c + 1) * rows, :] = st_d[c % 2].astype(BF16)
        if c + 2 < n_out:
            out_cp(c + 2).start()


N_STAGE_SEMS = 4


def _bf16_weight_copies(srcs, dsts, sems):
    return tuple(pltpu.make_async_copy(src, dst, sems.at[N_STAGE_SEMS + i])
                 for i, (src, dst) in enumerate(zip(srcs, dsts)))


def _prompt_kernel(x_ref, vec_ref, segq_ref, segk_ref, wsp_ref, bsp_ref, bias_ref, sinks_ref,
                   wrt_ref, brt_ref, ustrict_ref, w_in_hbm, w_out_hbm, wg_hbm, wu_hbm, wd_hbm,
                   y_ref, kwin_ref, vwin_ref, w_in_o, w_out_o, wg_o, wu_o, wd_o,
                   kprev_ref, vprev_ref, xs_ref, es_ref, ys_ref, w_in_ref, w_out_ref, wg_ref, wu_ref, wd_ref,
                   st_a, st_d, sems, *, tile, tiles_per_seq):
    step = pl.program_id(0)
    is_first = (step % tiles_per_seq) == 0

    kept = (w_in_ref, w_out_ref, wg_ref, wu_ref, wd_ref)
    kept_out = (w_in_o, w_out_o, wg_o, wu_o, wd_o)

    @pl.when(step == 0)
    def _():
        _stage_dense_weights(w_in_hbm, w_out_hbm, w_in_ref, w_out_ref, st_a, st_d, sems)
        _stage_expert_weights(wg_hbm, wu_hbm, wd_hbm, wg_ref, wu_ref, wd_ref, st_a, st_d, sems)
        for cp in _bf16_weight_copies(kept, kept_out, sems):
            cp.start()

    @pl.when(step == pl.num_programs(0) - 1)
    def _():
        for cp in _bf16_weight_copies(kept, kept_out, sems):
            cp.wait()

    @pl.when(is_first)
    def _():
        kprev_ref[...] = jnp.zeros_like(kprev_ref)
        vprev_ref[...] = jnp.zeros_like(vprev_ref)

    x = x_ref[...]
    u, va, q, k, v = _project(x, vec_ref, w_in_ref, segq_ref, segk_ref)
    kwin_ref[...] = k[tile - WINDOW:tile].T
    vwin_ref[...] = v[tile - WINDOW:tile].T
    a_out = u * _chunk_mlp(va.astype(BF16), wsp_ref, bsp_ref, tile)
    b_out = _swa_prompt(q * Q_SCALE, k, v, kprev_ref, vprev_ref, bias_ref, sinks_ref, is_first, tile)
    h = _merge(x, a_out, b_out, vec_ref, w_out_ref)
    y_ref[...] = h + _moe_grouped(h, vec_ref, wrt_ref, brt_ref, wg_ref, wu_ref, wd_ref, ustrict_ref,
                                  xs_ref, es_ref, ys_ref, tile)


def _sample_kernel(x_ref, ck_ref, cv_ref, vec_ref, w_in_ref, segq_ref, segk_ref, wsp_ref, bsp_ref,
                   bias_c_ref, bias_n_ref, sinks_ref, w_out_ref, wr_ref, wg_ref, wu_ref, wd_ref,
                   y_ref, wk_ref, wv_ref, va_ref, hcat_ref, *, tile, nb, dec):
    x = x_ref[...]
    u, va, q, k, v = _project(x, vec_ref, w_in_ref, segq_ref, segk_ref)
    for g in range(A_GROUPS):
        va_ref[pl.ds(g, tile, stride=A_GROUPS), :] = va[:, g * A_CH:(g + 1) * A_CH]
    a_out = u * _chunk_mlp(va.astype(BF16), wsp_ref, bsp_ref, tile)
    b_out = _swa_sample(q * Q_SCALE, k, v, ck_ref, cv_ref, wk_ref, wv_ref,
                        bias_c_ref, bias_n_ref, sinks_ref, nb, dec)
    h = _merge(x, a_out, b_out, vec_ref, w_out_ref)
    y_ref[...] = h + _moe_dense(h, vec_ref, wr_ref, wg_ref, wu_ref, wd_ref, hcat_ref)


def _const_spec(shape):
    nd = len(shape)
    return pl.BlockSpec(shape, lambda i: (0,) * nd, pipeline_mode=pl.Buffered(1))


def _row_spec(tile, width):
    return pl.BlockSpec((tile, width), lambda i: (i, 0))


def _smem_spec():
    return pl.BlockSpec(memory_space=pltpu.SMEM)


def _layer_weights(l, g_attn_norm, g_v_a, g_q, g_k, g_out_a, g_out_b, g_ffn_norm,
                   w_coarse, b_coarse, w_fine, b_fine):
    pad = ROUTER_LANES - N_EXPERT_GROUPS - N_EXPERTS
    wr = jnp.concatenate([w_coarse[l], w_fine[l], jnp.zeros((D_MODEL, pad), F32)], axis=1)
    row_pad = ROUTER_ROWS - N_EXPERT_GROUPS - N_EXPERTS
    wrt = jnp.concatenate([w_coarse[l].T, w_fine[l].T, jnp.zeros((row_pad, D_MODEL), F32)], axis=0)
    brt = jnp.concatenate([b_coarse[l], b_fine[l], jnp.zeros((row_pad,), F32)])
    vec = jnp.concatenate([
        g_attn_norm[l], g_v_a[l].reshape(A_WIDTH), jnp.tile(g_q[l], N_HEADS), jnp.tile(g_k[l], N_KV),
        g_out_a[l], g_out_b[l], g_ffn_norm[l], b_coarse[l], b_fine[l], jnp.zeros((pad,), F32)])
    return dict(
        vec=vec.reshape(1, VEC_LANES),
        segq=jnp.asarray(_seg_ones(B_WIDTH, HEAD_DIM), BF16),
        segk=jnp.asarray(_seg_ones(KV_WIDTH, HEAD_DIM), BF16),
        wr=wr.astype(BF16),
        wrt=wrt.astype(BF16),
        brt=jnp.broadcast_to(brt[:, None], (ROUTER_ROWS, LANES)),
    )


def _spatial_tables(ws, bs, period, tile):
    pair = min(PAIR, tile)
    reps = pair // period
    wbd = (jnp.tile(ws[:, :period, :period], (1, reps, reps)) * _causal_block_mask(period, pair)).astype(BF16)
    bsp = jnp.repeat(bs[:, :period].T, A_CH, axis=1)
    return wbd, bsp


def _upper_triangle(n):
    return jnp.asarray(np.triu(np.ones((n, n), np.float32), 1), BF16)


def _run_prompt(x2d, lw, wbd, bsp, sinks, seq_len, w_in, w_out, w_gate, w_up, w_down):
    assert w_in.shape[0] == w_gate.shape[1] and w_in.shape[1] % w_gate.shape[2] == 0
    assert w_out.shape[1] == w_down.shape[2] and w_out.shape[0] % w_down.shape[1] == 0
    n_tok = x2d.shape[0]
    tile = PROMPT_TILE
    tiles_per_seq = seq_len // tile
    n_seq = n_tok // seq_len
    bias = jnp.asarray(_prompt_bias())
    tail = [lw["wrt"], lw["brt"], _upper_triangle(tile)]
    mats = [w_in, w_out, w_gate, w_up, w_down]
    consts = [lw["vec"], lw["segq"], lw["segk"], wbd, bsp, bias]
    any_spec = pl.BlockSpec(memory_space=pl.ANY)
    in_specs = ([_row_spec(tile, D_MODEL)] + [_const_spec(a.shape) for a in consts] + [_smem_spec()]
                + [_const_spec(a.shape) for a in tail] + [any_spec] * len(mats))
    win_spec = pl.BlockSpec((KV_WIDTH, WINDOW), lambda i: (i // tiles_per_seq, 0))
    out_shape = (jax.ShapeDtypeStruct((n_tok, D_MODEL), F32),
                 jax.ShapeDtypeStruct((n_seq * KV_WIDTH, WINDOW), F32),
                 jax.ShapeDtypeStruct((n_seq * KV_WIDTH, WINDOW), F32),
                 *[jax.ShapeDtypeStruct(w.shape, BF16) for w in mats])
    out_specs = (_row_spec(tile, D_MODEL), win_spec, win_spec) + (any_spec,) * len(mats)
    kern = functools.partial(_prompt_kernel, tile=tile, tiles_per_seq=tiles_per_seq)
    return pl.pallas_call(
        kern,
        out_shape=out_shape,
        grid=(n_tok // tile,),
        in_specs=in_specs,
        out_specs=out_specs,
        scratch_shapes=[pltpu.VMEM((N_KV, WINDOW, KV_WIDTH), BF16), pltpu.VMEM((N_KV, WINDOW, KV_WIDTH), BF16),
                        pltpu.VMEM((tile, D_MODEL), BF16), pltpu.VMEM((tile, LANES), F32),
                        pltpu.VMEM((tile, D_MODEL), F32),
                        *[pltpu.VMEM(w.shape, BF16) for w in mats],
                        pltpu.VMEM((2,) + w_gate.shape[1:], F32), pltpu.VMEM((2,) + w_down.shape[1:], F32),
                        pltpu.SemaphoreType.DMA((N_STAGE_SEMS + len(mats),))],
        compiler_params=pltpu.CompilerParams(dimension_semantics=("arbitrary",),
                                             vmem_limit_bytes=VMEM_LIMIT_BYTES),
        name="layer_prompt",
    )(x2d, *consts, sinks, *tail, *mats)


def _run_sample(x2d, ck_t, cv_t, lw, wbd, bsp, sinks, dec, w_in_bf, w_out_bf, wg_bf, wu_bf, wd_bf):
    n_tok = x2d.shape[0]
    n_seq = ck_t.shape[0]
    tile = SAMPLE_TILE
    nb = tile // dec
    bc, bn = _sample_bias(nb, dec)
    head = [lw["vec"], w_in_bf, lw["segq"], lw["segk"]]
    tail = [w_out_bf, lw["wr"], wg_bf, wu_bf, wd_bf.reshape(N_EXPERTS * D_EXPERT, D_MODEL)]
    consts = head + [wbd, bsp, jnp.asarray(bc), jnp.asarray(bn)]
    cache_spec = pl.BlockSpec((nb, KV_WIDTH, WINDOW), lambda i: (i, 0, 0))
    in_specs = ([_row_spec(tile, D_MODEL), cache_spec, cache_spec] + [_const_spec(a.shape) for a in consts]
                + [_smem_spec()] + [_const_spec(a.shape) for a in tail])
    out_shape = (jax.ShapeDtypeStruct((n_tok, D_MODEL), F32),
                 jax.ShapeDtypeStruct((n_seq, KV_WIDTH, WINDOW), F32),
                 jax.ShapeDtypeStruct((n_seq, KV_WIDTH, WINDOW), F32),
                 jax.ShapeDtypeStruct((n_tok * A_GROUPS, A_CH), F32))
    out_specs = (_row_spec(tile, D_MODEL), cache_spec, cache_spec, _row_spec(tile * A_GROUPS, A_CH))
    kern = functools.partial(_sample_kernel, tile=tile, nb=nb, dec=dec)
    return pl.pallas_call(
        kern,
        out_shape=out_shape,
        grid=(n_tok // tile,),
        in_specs=in_specs,
        out_specs=out_specs,
        scratch_shapes=[pltpu.VMEM((tile, N_EXPERTS * D_EXPERT), BF16)],
        compiler_params=pltpu.CompilerParams(dimension_semantics=("arbitrary",),
                                             vmem_limit_bytes=VMEM_LIMIT_BYTES),
        name="layer_sample",
    )(x2d, ck_t, cv_t, *consts, sinks, *tail)


def _positions_last(c):
    b, w = c.shape[:2]
    return jnp.transpose(c, (0, 2, 3, 1)).reshape(b, KV_WIDTH, w)


def _positions_first(c_t):
    b, _, w = c_t.shape
    return jnp.transpose(c_t.reshape(b, N_KV, HEAD_DIM, w), (0, 3, 1, 2))


def kernel(x_prompt, x_sample, cache_k, cache_v, g_attn_norm, w_in, g_v_a, w_spatial, b_spatial, g_q, g_k, attn_sinks, g_out_a, g_out_b, w_out, g_ffn_norm, w_coarse, b_coarse, w_fine, b_fine, w_gate, w_up, w_down):
    depth = w_in.shape[0]
    batch, seq, _ = x_prompt.shape
    dbatch, dec, _ = x_sample.shape
    win = cache_k.shape[2]
    assert win == WINDOW and seq % PROMPT_TILE == 0 and (dbatch * dec) % SAMPLE_TILE == 0
    assert PAIR % dec == 0 and SAMPLE_TILE % dec == 0 and LANES % dec == 0

    hp = x_prompt.reshape(batch * seq, D_MODEL)
    hs = x_sample.reshape(dbatch * dec, D_MODEL)
    kp_l, vp_l, ks_l, vs_l, cv_l = [], [], [], [], []
    for l in range(depth):
        lw = _layer_weights(l, g_attn_norm, g_v_a, g_q, g_k, g_out_a, g_out_b, g_ffn_norm,
                            w_coarse, b_coarse, w_fine, b_fine)
        sinks = attn_sinks[l].astype(F32)
        wbd_p, bsp_p = _spatial_tables(w_spatial[l], b_spatial[l], CHUNK, PROMPT_TILE)
        wbd_s, bsp_s = _spatial_tables(w_spatial[l], b_spatial[l], dec, SAMPLE_TILE)

        hp, kp, vp, *weights_bf = _run_prompt(hp, lw, wbd_p, bsp_p, sinks, seq,
                                              w_in[l], w_out[l], w_gate[l], w_up[l], w_down[l])
        kp_l.append(_positions_first(kp.reshape(batch, KV_WIDTH, WINDOW)))
        vp_l.append(_positions_first(vp.reshape(batch, KV_WIDTH, WINDOW)))

        hs, wk, wv, va = _run_sample(hs, _positions_last(cache_k[l]), _positions_last(cache_v[l]),
                                     lw, wbd_s, bsp_s, sinks, dec, *weights_bf)
        ks_l.append(_positions_first(wk))
        vs_l.append(_positions_first(wv))
        cv_l.append(va.reshape(dbatch, dec, A_GROUPS, A_CH))

    return (hp.reshape(batch, seq, D_MODEL), hs.reshape(dbatch, dec, D_MODEL),
            jnp.stack(kp_l, axis=0), jnp.stack(vp_l, axis=0),
            jnp.stack(ks_l, axis=0), jnp.stack(vs_l, axis=0), jnp.stack(cv_l, axis=0))
```

```python
import functools

import numpy as np
import jax
import jax.numpy as jnp
from jax import lax
from jax.experimental import pallas as pl
from jax.experimental.pallas import tpu as pltpu

D_MODEL = 1024
CHUNK = 128
A_GROUPS = 4
A_WIDTH = 512
A_CH = 128
N_HEADS = 8
N_KV = 2
Q_PER_KV = 4
HEAD_DIM = 64
B_WIDTH = 512
KV_WIDTH = 128
WINDOW = 128
IN_COLS = 2 * A_WIDTH + B_WIDTH + 2 * KV_WIDTH
N_EXPERT_GROUPS = 4
EXPERTS_PER_GROUP = 4
N_EXPERTS = 16
D_EXPERT = 256
EPS = 1e-6

LANES = 128
PAIR = 2 * CHUNK
PROMPT_TILE = 512
SAMPLE_TILE = 256
SAMPLE_SUB = 128
VMEM_LIMIT_BYTES = 60 * 1024 * 1024

F32 = jnp.float32
BF16 = jnp.bfloat16
NEG_INF = float("-inf")
LOG2E = 1.4426950408889634
Q_SCALE = (HEAD_DIM ** -0.5) * LOG2E


def _slopes():
    return np.array([2.0 ** (-8.0 * (h + 1) / N_HEADS) for h in range(N_HEADS)], np.float64)


def _prompt_bias():
    t = np.arange(WINDOW)[:, None]
    s = np.arange(2 * WINDOW)[None, :]
    dist = t + WINDOW - s
    valid = (dist >= 0) & (dist < WINDOW)
    sl = _slopes()
    out = np.full((N_KV, Q_PER_KV * WINDOW, 2 * WINDOW), -np.inf, np.float32)
    for kh in range(N_KV):
        for g in range(Q_PER_KV):
            b = np.where(valid, -sl[kh * Q_PER_KV + g] * LOG2E * dist, -np.inf)
            out[kh, g * WINDOW:(g + 1) * WINDOW] = b
    return out


def _sample_bias(nb, dec):
    sl = _slopes()
    t = np.arange(dec)[:, None]
    j = np.arange(WINDOW)[None, :]
    dist_c = t + WINDOW - j
    valid_c = (dist_c >= 0) & (dist_c < WINDOW)
    bc = np.full((N_HEADS * dec, WINDOW), -np.inf, np.float32)
    tp = np.arange(dec)[None, :]
    dist_n = t - tp
    valid_n = dist_n >= 0
    bn = np.full((nb, N_HEADS * dec, nb * dec), -np.inf, np.float32)
    for h in range(N_HEADS):
        bc[h * dec:(h + 1) * dec] = np.where(valid_c, -sl[h] * LOG2E * dist_c, -np.inf)
        blk = np.where(valid_n, -sl[h] * LOG2E * dist_n, -np.inf)
        for b in range(nb):
            bn[b, h * dec:(h + 1) * dec, b * dec:(b + 1) * dec] = blk
    return bc, bn


def _seg_ones(width, seg):
    i = np.arange(width)
    return (i[:, None] // seg == i[None, :] // seg).astype(np.float32)


def _causal_block_mask(period, size):
    i = np.arange(size)
    same = i[:, None] // period == i[None, :] // period
    return (same & (i[None, :] % period <= i[:, None] % period)).astype(np.float32)


def _dot(a, b):
    return jnp.dot(a, b, preferred_element_type=F32)


def _dot_nt(a, b):
    return lax.dot_general(a, b, (((1,), (1,)), ((), ())), preferred_element_type=F32)


def _rms(x, g):
    ms = jnp.mean(x * x, axis=-1, keepdims=True)
    return (x * lax.rsqrt(ms + EPS)) * g


def _seg_rms(x, ones_bf, g, seg):
    ss = _dot((x * x).astype(BF16), ones_bf)
    return (x * lax.rsqrt(ss * (1.0 / seg) + EPS)) * g


GELU_K0 = -2.0 * 0.7978845608028654 * LOG2E
GELU_K1 = GELU_K0 * 0.044715


def _gelu_tanh(x):
    t = (x * x) * GELU_K1 + GELU_K0
    return x * (1.0 / (1.0 + jnp.exp2(x * t)))


VEC_ATTN = 0
VEC_VA = VEC_ATTN + D_MODEL
VEC_Q = VEC_VA + A_WIDTH
VEC_K = VEC_Q + B_WIDTH
VEC_OUT = VEC_K + KV_WIDTH
VEC_FFN = VEC_OUT + D_MODEL
VEC_LANES = VEC_FFN + D_MODEL


def _vec(vec_ref, start, width):
    return vec_ref[:, start:start + width]


def _project(x, vec_ref, w_in_ref, segq_ref, segk_ref):
    xn = _rms(x, _vec(vec_ref, VEC_ATTN, D_MODEL)).astype(BF16)
    z = _dot(xn, w_in_ref[...])
    i1, i2, i3, i4 = A_WIDTH, 2 * A_WIDTH, 2 * A_WIDTH + B_WIDTH, 2 * A_WIDTH + B_WIDTH + KV_WIDTH
    u = _gelu_tanh(z[:, :i1])
    va_pre = _gelu_tanh(z[:, i1:i2])
    g_va = _vec(vec_ref, VEC_VA, A_WIDTH)
    va = jnp.concatenate(
        [_rms(va_pre[:, g * A_CH:(g + 1) * A_CH], g_va[:, g * A_CH:(g + 1) * A_CH]) for g in range(A_GROUPS)],
        axis=1)
    q = _seg_rms(z[:, i2:i3], segq_ref[...], _vec(vec_ref, VEC_Q, B_WIDTH), HEAD_DIM)
    k = _seg_rms(z[:, i3:i4], segk_ref[...], _vec(vec_ref, VEC_K, KV_WIDTH), HEAD_DIM)
    v = z[:, i4:]
    return u, va, q, k, v


def _chunk_mlp(va_bf, wsp_ref, bsp_ref, tile):
    pair = wsp_ref.shape[1]
    n_pairs = tile // pair
    outs = []
    for g in range(A_GROUPS):
        cols = [va_bf[p * pair:(p + 1) * pair, g * A_CH:(g + 1) * A_CH] for p in range(n_pairs)]
        rhs = cols[0] if n_pairs == 1 else jnp.concatenate(cols, axis=1)
        o = _dot(wsp_ref[g], rhs)
        rows = [o[:, p * A_CH:(p + 1) * A_CH] for p in range(n_pairs)]
        outs.append(rows[0] if n_pairs == 1 else jnp.concatenate(rows, axis=0))
    mixed = jnp.concatenate(outs, axis=1)
    bias = bsp_ref[...]
    reps = tile // bias.shape[0]
    return mixed + (bias if reps == 1 else jnp.concatenate([bias] * reps, axis=0))


def _lane_lo():
    return lax.broadcasted_iota(jnp.int32, (1, LANES), 1) < HEAD_DIM


def _swap_halves(x):
    return pltpu.roll(x, HEAD_DIM, axis=1)


def _head_lhs(q_groups, q_groups_swapped, head, lo):
    j, half = divmod(head, 2)
    kv = head // Q_PER_KV
    src = q_groups[j] if half == kv else q_groups_swapped[j]
    return jnp.where(lo if kv == 0 else jnp.logical_not(lo), src, 0.0)


def _merge_heads(o_heads, lo):
    groups = []
    for j in range(N_HEADS // 2):
        kv = (2 * j) // Q_PER_KV
        if kv == 0:
            groups.append(jnp.where(lo, o_heads[2 * j], _swap_halves(o_heads[2 * j + 1])))
        else:
            groups.append(jnp.where(lo, _swap_halves(o_heads[2 * j]), o_heads[2 * j + 1]))
    return jnp.concatenate(groups, axis=1)


def _dup_halves(x, lo):
    xs = _swap_halves(x)
    return jnp.where(lo, x, xs).astype(BF16), jnp.where(lo, xs, x).astype(BF16)


def _swa_prompt(q, k, v, kprev_ref, vprev_ref, bias_ref, sinks_ref, is_first, tile):
    lo = _lane_lo()
    hi = jnp.logical_not(lo)
    col = lax.broadcasted_iota(jnp.int32, (WINDOW, 2 * WINDOW), 1)
    first_mask = jnp.where(col < WINDOW, jnp.where(is_first, NEG_INF, 0.0).astype(F32), 0.0)
    kd = _dup_halves(k, lo)
    vd = _dup_halves(v, lo)
    nblk = tile // WINDOW
    rows_out = []
    for i in range(nblk):
        r0, r1 = i * WINDOW, (i + 1) * WINDOW
        qg = [q[r0:r1, j * LANES:(j + 1) * LANES] for j in range(N_HEADS // 2)]
        o_heads = []
        for kh in range(N_KV):
            if i == 0:
                kp, vp = kprev_ref[kh], vprev_ref[kh]
            else:
                kp, vp = kd[kh][r0 - WINDOW:r0], vd[kh][r0 - WINDOW:r0]
            kb = jnp.concatenate([kp, kd[kh][r0:r1]], axis=0)
            vb = jnp.concatenate([vp, vd[kh][r0:r1]], axis=0)
            heads = [kh * Q_PER_KV + g for g in range(Q_PER_KV)]
            lhs = jnp.concatenate(
                [jnp.where(lo if h % 2 == 0 else hi, qg[h // 2], 0.0) for h in heads], axis=0).astype(BF16)
            s = _dot_nt(lhs, kb) + bias_ref[kh]
            ps, linvs = [], []
            for g in range(Q_PER_KV):
                sg = s[g * WINDOW:(g + 1) * WINDOW]
                if i == 0:
                    sg = sg + first_mask
                sink = sinks_ref[heads[g]] * LOG2E
                mg = jnp.maximum(jnp.max(sg, axis=-1, keepdims=True), sink)
                pg = jnp.exp2(sg - mg)
                lg = jnp.sum(pg, axis=-1, keepdims=True) + jnp.exp2(sink - mg)
                ps.append(pg.astype(BF16))
                linvs.append(1.0 / lg)
            o = _dot(jnp.concatenate(ps, axis=0), vb)
            o_heads += [o[g * WINDOW:(g + 1) * WINDOW] * linvs[g] for g in range(Q_PER_KV)]
        rows_out.append(jnp.concatenate(
            [jnp.where(lo, o_heads[2 * j], o_heads[2 * j + 1]) for j in range(N_HEADS // 2)], axis=1))
    for kh in range(N_KV):
        kprev_ref[kh] = kd[kh][tile - WINDOW:tile]
        vprev_ref[kh] = vd[kh][tile - WINDOW:tile]
    return jnp.concatenate(rows_out, axis=0)


def _shift_window(cache_ref, new_t, out_ref, nb, dec):
    lane = lax.broadcasted_iota(jnp.int32, (1, WINDOW), 1)
    keep = lane < WINDOW - dec
    per_tile = LANES // dec
    for b in range(nb):
        src = new_t[:, (b // per_tile) * LANES:(b // per_tile + 1) * LANES]
        new_cols = pltpu.roll(src, (WINDOW - dec - (b % per_tile) * dec) % LANES, axis=1)
        old = pltpu.roll(cache_ref[b], WINDOW - dec, axis=1)
        out_ref[b] = jnp.where(keep, old, new_cols)


def _swa_sample(q, k, v, ck_ref, cv_ref, wk_ref, wv_ref, bias_c_ref, bias_n_ref, sinks_ref, nb, dec):
    lo = _lane_lo()
    qg = [q[:, j * LANES:(j + 1) * LANES] for j in range(N_HEADS // 2)]
    qgs = [_swap_halves(x) for x in qg]
    qb = jnp.concatenate(
        [_head_lhs(qg, qgs, h, lo).reshape(nb, dec, LANES) for h in range(N_HEADS)], axis=1)
    rows = N_HEADS * dec
    qb_bf = qb.astype(BF16)
    k_bf = k.astype(BF16)
    v_bf = v.astype(BF16)
    ck = ck_ref[...].astype(BF16)
    cv = cv_ref[...].astype(BF16)
    sc = jnp.einsum('bqc,bcp->bqp', qb_bf, ck, preferred_element_type=F32) + bias_c_ref[...][None]
    sn = _dot_nt(qb_bf.reshape(nb * rows, LANES), k_bf).reshape(nb, rows, nb * dec) + bias_n_ref[...]
    sink = jnp.concatenate(
        [jnp.full((1, dec, 1), sinks_ref[h] * LOG2E, F32) for h in range(N_HEADS)], axis=1)
    m = jnp.maximum(jnp.max(sc, axis=-1, keepdims=True), jnp.max(sn, axis=-1, keepdims=True))
    m = jnp.maximum(m, sink)
    pc = jnp.exp2(sc - m)
    pn = jnp.exp2(sn - m)
    l = jnp.sum(pc, axis=-1, keepdims=True) + jnp.sum(pn, axis=-1, keepdims=True) + jnp.exp2(sink - m)
    oc = jnp.einsum('bqp,bcp->bqc', pc.astype(BF16), cv, preferred_element_type=F32)
    on = _dot(pn.reshape(nb * rows, nb * dec).astype(BF16), v_bf).reshape(nb, rows, LANES)
    o = (oc + on) * (1.0 / l)
    o_heads = [o[:, h * dec:(h + 1) * dec, :].reshape(nb * dec, LANES) for h in range(N_HEADS)]
    _shift_window(ck_ref, k.T, wk_ref, nb, dec)
    _shift_window(cv_ref, v.T, wv_ref, nb, dec)
    return _merge_heads(o_heads, lo)


ROUTER_ROWS = 32


def _route_rows(hn_bf, wrt_ref, brt_ref, tile):
    logits = _dot_nt(wrt_ref[...], hn_bf) + jnp.tile(brt_ref[...], (1, tile // LANES))
    row = lax.broadcasted_iota(jnp.int32, logits.shape, 0)
    big = jnp.int32(ROUTER_ROWS)
    lc = jnp.where(row < N_EXPERT_GROUPS, logits, NEG_INF)
    mx = jnp.max(lc, axis=0, keepdims=True)
    g_idx = jnp.min(jnp.where(lc == mx, row, big), axis=0, keepdims=True)
    p_g = 1.0 / jnp.sum(jnp.exp(lc - mx), axis=0, keepdims=True)
    e_row = row - N_EXPERT_GROUPS
    in_group = (e_row >= 0) & (e_row < N_EXPERTS) & ((e_row >> 2) == g_idx)
    lf = jnp.where(in_group, logits, NEG_INF)
    v1 = jnp.max(lf, axis=0, keepdims=True)
    i1 = jnp.min(jnp.where(lf == v1, row, big), axis=0, keepdims=True)
    lf2 = jnp.where(row == i1, NEG_INF, lf)
    v2 = jnp.max(lf2, axis=0, keepdims=True)
    i2 = jnp.min(jnp.where(lf2 == v2, row, big), axis=0, keepdims=True)
    e = jnp.exp(v2 - v1)
    w1 = p_g / (1.0 + e)
    w2 = w1 * e
    return g_idx, i1, i2, w1, w2


def _silu(x):
    return x * (1.0 / (1.0 + jnp.exp2(x * (-LOG2E))))


GROUP_LANE = EXPERTS_PER_GROUP
LOW_SHIFT = 8
SORT_BLOCK = 128


def _moe_grouped(h, vec_ref, wrt_ref, brt_ref, wg_ref, wu_ref, wd_ref, ustrict_ref, xs_ref, es_ref, ys_ref, tile):
    hn_bf = _rms(h, _vec(vec_ref, VEC_FFN, D_MODEL)).astype(BF16)
    g_idx, i1, i2, w1, w2 = _route_rows(hn_bf, wrt_ref, brt_ref, tile)

    row16 = lax.broadcasted_iota(jnp.int32, (16, tile), 0)
    onehot = row16 == g_idx
    gmat = jnp.where(onehot, 1.0, 0.0)
    before = _dot(gmat.astype(BF16), ustrict_ref[...])
    rank = jnp.sum(jnp.where(onehot, before, 0.0), axis=0, keepdims=True)
    offs = [jnp.float32(0.0)]
    for g in range(N_EXPERT_GROUPS - 1):
        offs.append(offs[-1] + jnp.sum(jnp.where(g_idx == g, 1.0, 0.0)))
    offv = jnp.zeros_like(rank)
    for g in range(1, N_EXPERT_GROUPS):
        offv = offv + jnp.where(g_idx == g, offs[g], 0.0)
    pos = (rank + offv).astype(jnp.int32)
    dest = lax.broadcasted_iota(jnp.int32, (tile, tile), 0)
    perm = jnp.where(dest == pos, 1.0, 0.0).astype(BF16)

    row8 = lax.broadcasted_iota(jnp.int32, (8, tile), 0)
    j1 = (i1 - N_EXPERT_GROUPS) & (EXPERTS_PER_GROUP - 1)
    j2 = (i2 - N_EXPERT_GROUPS) & (EXPERTS_PER_GROUP - 1)
    gates = jnp.where(row8 == j1, w1, jnp.where(row8 == j2, w2, 0.0))
    gates_hi = gates.astype(BF16).astype(F32)
    gates_lo = (gates - gates_hi).astype(BF16).astype(F32)
    top = gates_hi + jnp.where(row8 == GROUP_LANE, g_idx.astype(F32), 0.0)
    side = jnp.concatenate([top, gates_lo, jnp.zeros((LANES - 2 * LOW_SHIFT, tile), F32)], axis=0).T
    x_ext = jnp.concatenate([hn_bf, side.astype(BF16)], axis=1)
    srt = _dot(perm, x_ext)
    xs_ref[...] = srt[:, :D_MODEL].astype(BF16)
    es = srt[:, D_MODEL:]
    es_ref[...] = es + pltpu.roll(es, LANES - LOW_SHIFT, axis=1)

    def group_of(row):
        g = jnp.int32(0)
        for k in range(1, N_EXPERT_GROUPS):
            g = g + (jnp.float32(row) >= offs[k]).astype(jnp.int32)
        return g

    def group_pass(rows, g, keep, accumulate):
        xb = xs_ref[rows, :]
        eb = es_ref[rows, :]
        mine = jnp.where(eb[:, GROUP_LANE:GROUP_LANE + 1] == jnp.asarray(g).astype(F32), keep, 0.0)
        parts = []
        for j in range(EXPERTS_PER_GROUP):
            e = g * EXPERTS_PER_GROUP + j
            gate = _dot(xb, wg_ref[e])
            up = _dot(xb, wu_ref[e])
            parts.append((_silu(gate) * up * (eb[:, j:j + 1] * mine)).astype(BF16))
        wd = wd_ref[pl.ds(g * EXPERTS_PER_GROUP, EXPERTS_PER_GROUP)]
        out = _dot(jnp.concatenate(parts, axis=1), wd.reshape(EXPERTS_PER_GROUP * D_EXPERT, D_MODEL))
        if accumulate:
            ys_ref[rows, :] += out
        else:
            ys_ref[rows, :] = out

    nblk = tile // SORT_BLOCK
    for b in range(nblk):
        group_pass(pl.ds(b * SORT_BLOCK, SORT_BLOCK), group_of(b * SORT_BLOCK), 1.0, False)
    for g in range(1, N_EXPERT_GROUPS):
        start = offs[g].astype(jnp.int32)
        blk = jnp.minimum(lax.div(start, jnp.int32(SORT_BLOCK)), nblk - 1)
        inside = jnp.where(lax.rem(start, jnp.int32(SORT_BLOCK)) != 0, 1.0, 0.0)
        group_pass(pl.ds(pl.multiple_of(blk * SORT_BLOCK, SORT_BLOCK), SORT_BLOCK), g, inside, True)
    return lax.dot_general(perm, ys_ref[...].astype(BF16), (((0,), (0,)), ((), ())),
                           preferred_element_type=F32)


def _merge(x, a_out, b_out, vec_ref, w_out_ref):
    g_out = _vec(vec_ref, VEC_OUT, D_MODEL)
    mix = jnp.concatenate(
        [_rms(a_out, g_out[:, :A_WIDTH]), _rms(b_out, g_out[:, A_WIDTH:])], axis=1).astype(BF16)
    return x + _dot(mix, w_out_ref[...])


def _stage_expert_weights(wg_hbm, wu_hbm, wd_hbm, wg_s, wu_s, wd_s, st_a, st_d, sems):
    def gate_cp(e):
        return pltpu.make_async_copy(wg_hbm.at[e], st_a.at[0], sems.at[0])

    def up_cp(e):
        return pltpu.make_async_copy(wu_hbm.at[e], st_a.at[1], sems.at[1])

    def down_cp(e, slot):
        return pltpu.make_async_copy(wd_hbm.at[e], st_d.at[slot], sems.at[2 + slot])

    gate_cp(0).start()
    up_cp(0).start()
    down_cp(0, 0).start()

    def pair(p, carry):
        for slot in (0, 1):
            e = 2 * p + slot
            nxt = e + 1
            more = nxt < N_EXPERTS

            @pl.when(more)
            def _():
                down_cp(nxt, 1 - slot).start()

            gate_cp(e).wait()
            wg_s[e] = st_a[0].astype(BF16)

            @pl.when(more)
            def _():
                gate_cp(nxt).start()

            up_cp(e).wait()
            wu_s[e] = st_a[1].astype(BF16)

            @pl.when(more)
            def _():
                up_cp(nxt).start()

            down_cp(e, slot).wait()
            wd_s[e] = st_d[slot].astype(BF16)
        return carry

    lax.fori_loop(0, N_EXPERTS // 2, pair, 0)


def _stage_dense_weights(w_in_hbm, w_out_hbm, w_in_s, w_out_s, st_a, st_d, sems):
    cols = st_a.shape[2]
    rows = st_d.shape[1]
    n_in = w_in_s.shape[1] // cols
    n_out = w_out_s.shape[0] // rows

    def in_cp(c):
        return pltpu.make_async_copy(w_in_hbm.at[:, pl.ds(c * cols, cols)], st_a.at[c % 2], sems.at[c % 2])

    def out_cp(c):
        return pltpu.make_async_copy(w_out_hbm.at[pl.ds(c * rows, rows), :], st_d.at[c % 2], sems.at[2 + c % 2])

    for c in range(min(2, n_in)):
        in_cp(c).start()
    for c in range(min(2, n_out)):
        out_cp(c).start()
    for c in range(max(n_in, n_out)):
        if c < n_in:
            in_cp(c).wait()
            w_in_s[:, c * cols:(c + 1) * cols] = st_a[c % 2].astype(BF16)
            if c + 2 < n_in:
                in_cp(c + 2).start()
        if c < n_out:
            out_cp(c).wait()
            w_out_s[c * rows:(c + 1) * rows, :] = st_d[c % 2].astype(BF16)
            if c + 2 < n_out:
                out_cp(c + 2).start()


N_STAGE_SEMS = 4


def _bf16_weight_copies(srcs, dsts, sems):
    return tuple(pltpu.make_async_copy(src, dst, sems.at[N_STAGE_SEMS + i])
                 for i, (src, dst) in enumerate(zip(srcs, dsts)))


def _prompt_kernel(x_ref, vec_ref, segq_ref, segk_ref, wsp_ref, bsp_ref, bias_ref, sinks_ref,
                   wrt_ref, brt_ref, ustrict_ref, w_in_hbm, w_out_hbm, wg_hbm, wu_hbm, wd_hbm,
                   y_ref, kwin_ref, vwin_ref, w_in_o, w_out_o, wg_o, wu_o, wd_o,
                   kprev_ref, vprev_ref, xs_ref, es_ref, ys_ref, w_in_ref, w_out_ref, wg_ref, wu_ref, wd_ref,
                   st_a, st_d, sems, *, tile, tiles_per_seq):
    step = pl.program_id(0)
    is_first = (step % tiles_per_seq) == 0

    kept = (w_in_ref, w_out_ref, wg_ref, wu_ref, wd_ref)
    kept_out = (w_in_o, w_out_o, wg_o, wu_o, wd_o)

    @pl.when(step == 0)
    def _():
        _stage_dense_weights(w_in_hbm, w_out_hbm, w_in_ref, w_out_ref, st_a, st_d, sems)
        _stage_expert_weights(wg_hbm, wu_hbm, wd_hbm, wg_ref, wu_ref, wd_ref, st_a, st_d, sems)
        for cp in _bf16_weight_copies(kept, kept_out, sems):
            cp.start()

    @pl.when(step == pl.num_programs(0) - 1)
    def _():
        for cp in _bf16_weight_copies(kept, kept_out, sems):
            cp.wait()

    @pl.when(is_first)
    def _():
        kprev_ref[...] = jnp.zeros_like(kprev_ref)
        vprev_ref[...] = jnp.zeros_like(vprev_ref)

    x = x_ref[...]
    u, va, q, k, v = _project(x, vec_ref, w_in_ref, segq_ref, segk_ref)
    kwin_ref[...] = k[tile - WINDOW:tile].T
    vwin_ref[...] = v[tile - WINDOW:tile].T
    a_out = u * _chunk_mlp(va.astype(BF16), wsp_ref, bsp_ref, tile)
    b_out = _swa_prompt(q * Q_SCALE, k, v, kprev_ref, vprev_ref, bias_ref, sinks_ref, is_first, tile)
    h = _merge(x, a_out, b_out, vec_ref, w_out_ref)
    y_ref[...] = h + _moe_grouped(h, vec_ref, wrt_ref, brt_ref, wg_ref, wu_ref, wd_ref, ustrict_ref,
                                  xs_ref, es_ref, ys_ref, tile)


def _sample_kernel(x_ref, ck_ref, cv_ref, vec_ref, w_in_ref, segq_ref, segk_ref, wsp_ref, bsp_ref,
                   bias_c_ref, bias_n_ref, sinks_ref, w_out_ref, wrt_ref, brt_ref, ustrict_ref,
                   wg_ref, wu_ref, wd_ref, y_ref, wk_ref, wv_ref, va_ref, xs_ref, es_ref, ys_ref,
                   *, tile, sub, nb, dec):
    hs = []
    for part in range(tile // sub):
        r0 = part * sub
        seqs = pl.ds(part * nb, nb)
        x = x_ref[r0:r0 + sub, :]
        u, va, q, k, v = _project(x, vec_ref, w_in_ref, segq_ref, segk_ref)
        for g in range(A_GROUPS):
            va_ref[pl.ds(r0 * A_GROUPS + g, sub, stride=A_GROUPS), :] = va[:, g * A_CH:(g + 1) * A_CH]
        a_out = u * _chunk_mlp(va.astype(BF16), wsp_ref, bsp_ref, sub)
        b_out = _swa_sample(q * Q_SCALE, k, v, ck_ref.at[seqs], cv_ref.at[seqs], wk_ref.at[seqs], wv_ref.at[seqs],
                            bias_c_ref, bias_n_ref, sinks_ref, nb, dec)
        hs.append(_merge(x, a_out, b_out, vec_ref, w_out_ref))
    h = jnp.concatenate(hs, axis=0)
    y_ref[...] = h + _moe_grouped(h, vec_ref, wrt_ref, brt_ref, wg_ref, wu_ref, wd_ref, ustrict_ref,
                                  xs_ref, es_ref, ys_ref, tile)


def _const_spec(shape):
    nd = len(shape)
    return pl.BlockSpec(shape, lambda i: (0,) * nd, pipeline_mode=pl.Buffered(1))


def _row_spec(tile, width):
    return pl.BlockSpec((tile, width), lambda i: (i, 0))


def _smem_spec():
    return pl.BlockSpec(memory_space=pltpu.SMEM)


def _layer_weights(l, g_attn_norm, g_v_a, g_q, g_k, g_out_a, g_out_b, g_ffn_norm,
                   w_coarse, b_coarse, w_fine, b_fine):
    row_pad = ROUTER_ROWS - N_EXPERT_GROUPS - N_EXPERTS
    wrt = jnp.concatenate([w_coarse[l].T, w_fine[l].T, jnp.zeros((row_pad, D_MODEL), F32)], axis=0)
    brt = jnp.concatenate([b_coarse[l], b_fine[l], jnp.zeros((row_pad,), F32)])
    vec = jnp.concatenate([
        g_attn_norm[l], g_v_a[l].reshape(A_WIDTH), jnp.tile(g_q[l], N_HEADS), jnp.tile(g_k[l], N_KV),
        g_out_a[l], g_out_b[l], g_ffn_norm[l]])
    return dict(
        vec=vec.reshape(1, VEC_LANES),
        segq=jnp.asarray(_seg_ones(B_WIDTH, HEAD_DIM), BF16),
        segk=jnp.asarray(_seg_ones(KV_WIDTH, HEAD_DIM), BF16),
        wrt=wrt.astype(BF16),
        brt=jnp.broadcast_to(brt[:, None], (ROUTER_ROWS, LANES)),
    )


def _spatial_tables(ws, bs, period, tile):
    pair = min(PAIR, tile)
    reps = pair // period
    wbd = (jnp.tile(ws[:, :period, :period], (1, reps, reps)) * _causal_block_mask(period, pair)).astype(BF16)
    bsp = jnp.repeat(bs[:, :period].T, A_CH, axis=1)
    return wbd, bsp


def _upper_triangle(n):
    return jnp.asarray(np.triu(np.ones((n, n), np.float32), 1), BF16)


def _run_prompt(x2d, lw, wbd, bsp, sinks, seq_len, w_in, w_out, w_gate, w_up, w_down):
    assert w_in.shape[0] == w_gate.shape[1] and w_in.shape[1] % w_gate.shape[2] == 0
    assert w_out.shape[1] == w_down.shape[2] and w_out.shape[0] % w_down.shape[1] == 0
    n_tok = x2d.shape[0]
    tile = PROMPT_TILE
    tiles_per_seq = seq_len // tile
    n_seq = n_tok // seq_len
    bias = jnp.asarray(_prompt_bias())
    tail = [lw["wrt"], lw["brt"], _upper_triangle(tile)]
    mats = [w_in, w_out, w_gate, w_up, w_down]
    consts = [lw["vec"], lw["segq"], lw["segk"], wbd, bsp, bias]
    any_spec = pl.BlockSpec(memory_space=pl.ANY)
    in_specs = ([_row_spec(tile, D_MODEL)] + [_const_spec(a.shape) for a in consts] + [_smem_spec()]
                + [_const_spec(a.shape) for a in tail] + [any_spec] * len(mats))
    win_spec = pl.BlockSpec((KV_WIDTH, WINDOW), lambda i: (i // tiles_per_seq, 0))
    out_shape = (jax.ShapeDtypeStruct((n_tok, D_MODEL), F32),
                 jax.ShapeDtypeStruct((n_seq * KV_WIDTH, WINDOW), F32),
                 jax.ShapeDtypeStruct((n_seq * KV_WIDTH, WINDOW), F32),
                 *[jax.ShapeDtypeStruct(w.shape, BF16) for w in mats])
    out_specs = (_row_spec(tile, D_MODEL), win_spec, win_spec) + (any_spec,) * len(mats)
    kern = functools.partial(_prompt_kernel, tile=tile, tiles_per_seq=tiles_per_seq)
    return pl.pallas_call(
        kern,
        out_shape=out_shape,
        grid=(n_tok // tile,),
        in_specs=in_specs,
        out_specs=out_specs,
        scratch_shapes=[pltpu.VMEM((N_KV, WINDOW, KV_WIDTH), BF16), pltpu.VMEM((N_KV, WINDOW, KV_WIDTH), BF16),
                        pltpu.VMEM((tile, D_MODEL), BF16), pltpu.VMEM((tile, LANES), F32),
                        pltpu.VMEM((tile, D_MODEL), F32),
                        *[pltpu.VMEM(w.shape, BF16) for w in mats],
                        pltpu.VMEM((2,) + w_gate.shape[1:], F32), pltpu.VMEM((2,) + w_down.shape[1:], F32),
                        pltpu.SemaphoreType.DMA((N_STAGE_SEMS + len(mats),))],
        compiler_params=pltpu.CompilerParams(dimension_semantics=("arbitrary",),
                                             vmem_limit_bytes=VMEM_LIMIT_BYTES),
        name="layer_prompt",
    )(x2d, *consts, sinks, *tail, *mats)


def _run_sample(x2d, ck_t, cv_t, lw, wbd, bsp, sinks, dec, w_in_bf, w_out_bf, wg_bf, wu_bf, wd_bf):
    n_tok = x2d.shape[0]
    n_seq = ck_t.shape[0]
    tile, sub = SAMPLE_TILE, SAMPLE_SUB
    nb = sub // dec
    bc, bn = _sample_bias(nb, dec)
    head = [lw["vec"], w_in_bf, lw["segq"], lw["segk"]]
    tail = [w_out_bf, lw["wrt"], lw["brt"], _upper_triangle(tile), wg_bf, wu_bf, wd_bf]
    consts = head + [wbd, bsp, jnp.asarray(bc), jnp.asarray(bn)]
    cache_spec = pl.BlockSpec((tile // dec, KV_WIDTH, WINDOW), lambda i: (i, 0, 0))
    in_specs = ([_row_spec(tile, D_MODEL), cache_spec, cache_spec] + [_const_spec(a.shape) for a in consts]
                + [_smem_spec()] + [_const_spec(a.shape) for a in tail])
    out_shape = (jax.ShapeDtypeStruct((n_tok, D_MODEL), F32),
                 jax.ShapeDtypeStruct((n_seq, KV_WIDTH, WINDOW), F32),
                 jax.ShapeDtypeStruct((n_seq, KV_WIDTH, WINDOW), F32),
                 jax.ShapeDtypeStruct((n_tok * A_GROUPS, A_CH), F32))
    out_specs = (_row_spec(tile, D_MODEL), cache_spec, cache_spec, _row_spec(tile * A_GROUPS, A_CH))
    kern = functools.partial(_sample_kernel, tile=tile, sub=sub, nb=nb, dec=dec)
    return pl.pallas_call(
        kern,
        out_shape=out_shape,
        grid=(n_tok // tile,),
        in_specs=in_specs,
        out_specs=out_specs,
        scratch_shapes=[pltpu.VMEM((tile, D_MODEL), BF16), pltpu.VMEM((tile, LANES), F32),
                        pltpu.VMEM((tile, D_MODEL), F32)],
        compiler_params=pltpu.CompilerParams(dimension_semantics=("arbitrary",),
                                             vmem_limit_bytes=VMEM_LIMIT_BYTES),
        name="layer_sample",
    )(x2d, ck_t, cv_t, *consts, sinks, *tail)


def _positions_last(c):
    b, w = c.shape[:2]
    return jnp.transpose(c, (0, 2, 3, 1)).reshape(b, KV_WIDTH, w)


def _positions_first(c_t):
    b, _, w = c_t.shape
    return jnp.transpose(c_t.reshape(b, N_KV, HEAD_DIM, w), (0, 3, 1, 2))


def kernel(x_prompt, x_sample, cache_k, cache_v, g_attn_norm, w_in, g_v_a, w_spatial, b_spatial, g_q, g_k, attn_sinks, g_out_a, g_out_b, w_out, g_ffn_norm, w_coarse, b_coarse, w_fine, b_fine, w_gate, w_up, w_down):
    depth = w_in.shape[0]
    batch, seq, _ = x_prompt.shape
    dbatch, dec, _ = x_sample.shape
    win = cache_k.shape[2]
    assert win == WINDOW and seq % PROMPT_TILE == 0 and (dbatch * dec) % SAMPLE_TILE == 0
    assert PAIR % dec == 0 and SAMPLE_SUB % dec == 0 and LANES % dec == 0

    hp = x_prompt.reshape(batch * seq, D_MODEL)
    hs = x_sample.reshape(dbatch * dec, D_MODEL)
    kp_l, vp_l, ks_l, vs_l, cv_l = [], [], [], [], []
    for l in range(depth):
        lw = _layer_weights(l, g_attn_norm, g_v_a, g_q, g_k, g_out_a, g_out_b, g_ffn_norm,
                            w_coarse, b_coarse, w_fine, b_fine)
        sinks = attn_sinks[l].astype(F32)
        wbd_p, bsp_p = _spatial_tables(w_spatial[l], b_spatial[l], CHUNK, PROMPT_TILE)
        wbd_s, bsp_s = _spatial_tables(w_spatial[l], b_spatial[l], dec, SAMPLE_SUB)

        hp, kp, vp, *weights_bf = _run_prompt(hp, lw, wbd_p, bsp_p, sinks, seq,
                                              w_in[l], w_out[l], w_gate[l], w_up[l], w_down[l])
        kp_l.append(_positions_first(kp.reshape(batch, KV_WIDTH, WINDOW)))
        vp_l.append(_positions_first(vp.reshape(batch, KV_WIDTH, WINDOW)))

        hs, wk, wv, va = _run_sample(hs, _positions_last(cache_k[l]), _positions_last(cache_v[l]),
                                     lw, wbd_s, bsp_s, sinks, dec, *weights_bf)
        ks_l.append(_positions_first(wk))
        vs_l.append(_positions_first(wv))
        cv_l.append(va.reshape(dbatch, dec, A_GROUPS, A_CH))

    return (hp.reshape(batch, seq, D_MODEL), hs.reshape(dbatch, dec, D_MODEL),
            jnp.stack(kp_l, axis=0), jnp.stack(vp_l, axis=0),
            jnp.stack(ks_l, axis=0), jnp.stack(vs_l, axis=0), jnp.stack(cv_l, axis=0))
```

```python
import functools

import numpy as np
import jax
import jax.numpy as jnp
from jax import lax
from jax.experimental import pallas as pl
from jax.experimental.pallas import tpu as pltpu

D_MODEL = 1024
CHUNK = 128
A_GROUPS = 4
A_WIDTH = 512
A_CH = 128
N_HEADS = 8
N_KV = 2
Q_PER_KV = 4
HEAD_DIM = 64
B_WIDTH = 512
KV_WIDTH = 128
WINDOW = 128
IN_COLS = 2 * A_WIDTH + B_WIDTH + 2 * KV_WIDTH
N_EXPERT_GROUPS = 4
EXPERTS_PER_GROUP = 4
N_EXPERTS = 16
D_EXPERT = 256
EPS = 1e-6

LANES = 128
PAIR = 2 * CHUNK
PROMPT_TILE = 512
SAMPLE_TILE = 256
SAMPLE_SUB = 128
VMEM_LIMIT_BYTES = 60 * 1024 * 1024

F32 = jnp.float32
BF16 = jnp.bfloat16
NEG_INF = float("-inf")
LOG2E = 1.4426950408889634
Q_SCALE = (HEAD_DIM ** -0.5) * LOG2E


def _slopes():
    return np.array([2.0 ** (-8.0 * (h + 1) / N_HEADS) for h in range(N_HEADS)], np.float64)


def _prompt_bias():
    t = np.arange(WINDOW)[:, None]
    s = np.arange(2 * WINDOW)[None, :]
    dist = t + WINDOW - s
    valid = (dist >= 0) & (dist < WINDOW)
    sl = _slopes()
    out = np.full((N_KV, Q_PER_KV * WINDOW, 2 * WINDOW), -np.inf, np.float32)
    for kh in range(N_KV):
        for g in range(Q_PER_KV):
            b = np.where(valid, -sl[kh * Q_PER_KV + g] * LOG2E * dist, -np.inf)
            out[kh, g * WINDOW:(g + 1) * WINDOW] = b
    return out


def _sample_bias(nb, dec):
    sl = _slopes()
    t = np.arange(dec)[:, None]
    j = np.arange(WINDOW)[None, :]
    dist_c = t + WINDOW - j
    valid_c = (dist_c >= 0) & (dist_c < WINDOW)
    bc = np.full((N_HEADS * dec, WINDOW), -np.inf, np.float32)
    tp = np.arange(dec)[None, :]
    dist_n = t - tp
    valid_n = dist_n >= 0
    bn = np.full((nb, N_HEADS * dec, nb * dec), -np.inf, np.float32)
    for h in range(N_HEADS):
        bc[h * dec:(h + 1) * dec] = np.where(valid_c, -sl[h] * LOG2E * dist_c, -np.inf)
        blk = np.where(valid_n, -sl[h] * LOG2E * dist_n, -np.inf)
        for b in range(nb):
            bn[b, h * dec:(h + 1) * dec, b * dec:(b + 1) * dec] = blk
    return bc, bn


def _seg_ones(width, seg):
    i = np.arange(width)
    return (i[:, None] // seg == i[None, :] // seg).astype(np.float32)


def _causal_block_mask(period, size):
    i = np.arange(size)
    same = i[:, None] // period == i[None, :] // period
    return (same & (i[None, :] % period <= i[:, None] % period)).astype(np.float32)


def _dot(a, b):
    return jnp.dot(a, b, preferred_element_type=F32)


def _dot_nt(a, b):
    return lax.dot_general(a, b, (((1,), (1,)), ((), ())), preferred_element_type=F32)


def _rms(x, g):
    ms = jnp.mean(x * x, axis=-1, keepdims=True)
    return (x * lax.rsqrt(ms + EPS)) * g


def _seg_rms(x, ones_bf, g, seg):
    ss = _dot((x * x).astype(BF16), ones_bf)
    return (x * lax.rsqrt(ss * (1.0 / seg) + EPS)) * g


GELU_K0 = -2.0 * 0.7978845608028654 * LOG2E
GELU_K1 = GELU_K0 * 0.044715


def _gelu_tanh(x):
    t = (x * x) * GELU_K1 + GELU_K0
    return x * (1.0 / (1.0 + jnp.exp2(x * t)))


VEC_ATTN = 0
VEC_VA = VEC_ATTN + D_MODEL
VEC_Q = VEC_VA + A_WIDTH
VEC_K = VEC_Q + B_WIDTH
VEC_OUT = VEC_K + KV_WIDTH
VEC_FFN = VEC_OUT + D_MODEL
VEC_LANES = VEC_FFN + D_MODEL


def _vec(vec_ref, start, width):
    return vec_ref[:, start:start + width]


def _project(x, vec_ref, w_in_ref, segq_ref, segk_ref):
    xn = _rms(x, _vec(vec_ref, VEC_ATTN, D_MODEL)).astype(BF16)
    z = _dot(xn, w_in_ref[...])
    i1, i2, i3, i4 = A_WIDTH, 2 * A_WIDTH, 2 * A_WIDTH + B_WIDTH, 2 * A_WIDTH + B_WIDTH + KV_WIDTH
    u = _gelu_tanh(z[:, :i1])
    va_pre = _gelu_tanh(z[:, i1:i2])
    g_va = _vec(vec_ref, VEC_VA, A_WIDTH)
    va = jnp.concatenate(
        [_rms(va_pre[:, g * A_CH:(g + 1) * A_CH], g_va[:, g * A_CH:(g + 1) * A_CH]) for g in range(A_GROUPS)],
        axis=1)
    q = _seg_rms(z[:, i2:i3], segq_ref[...], _vec(vec_ref, VEC_Q, B_WIDTH), HEAD_DIM)
    k = _seg_rms(z[:, i3:i4], segk_ref[...], _vec(vec_ref, VEC_K, KV_WIDTH), HEAD_DIM)
    v = z[:, i4:]
    return u, va, q, k, v


def _chunk_mlp(va_bf, wsp_ref, bsp_ref, tile):
    pair = wsp_ref.shape[1]
    n_pairs = tile // pair
    outs = []
    for g in range(A_GROUPS):
        cols = [va_bf[p * pair:(p + 1) * pair, g * A_CH:(g + 1) * A_CH] for p in range(n_pairs)]
        rhs = cols[0] if n_pairs == 1 else jnp.concatenate(cols, axis=1)
        o = _dot(wsp_ref[g], rhs)
        rows = [o[:, p * A_CH:(p + 1) * A_CH] for p in range(n_pairs)]
        outs.append(rows[0] if n_pairs == 1 else jnp.concatenate(rows, axis=0))
    mixed = jnp.concatenate(outs, axis=1)
    bias = bsp_ref[...]
    reps = tile // bias.shape[0]
    return mixed + (bias if reps == 1 else jnp.concatenate([bias] * reps, axis=0))


def _lane_lo():
    return lax.broadcasted_iota(jnp.int32, (1, LANES), 1) < HEAD_DIM


def _swap_halves(x):
    return pltpu.roll(x, HEAD_DIM, axis=1)


def _head_lhs(q_groups, q_groups_swapped, head, lo):
    j, half = divmod(head, 2)
    kv = head // Q_PER_KV
    src = q_groups[j] if half == kv else q_groups_swapped[j]
    return jnp.where(lo if kv == 0 else jnp.logical_not(lo), src, 0.0)


def _merge_heads(o_heads, lo):
    groups = []
    for j in range(N_HEADS // 2):
        kv = (2 * j) // Q_PER_KV
        if kv == 0:
            groups.append(jnp.where(lo, o_heads[2 * j], _swap_halves(o_heads[2 * j + 1])))
        else:
            groups.append(jnp.where(lo, _swap_halves(o_heads[2 * j]), o_heads[2 * j + 1]))
    return jnp.concatenate(groups, axis=1)


def _dup_halves(x, lo):
    xs = _swap_halves(x)
    return jnp.where(lo, x, xs).astype(BF16), jnp.where(lo, xs, x).astype(BF16)


def _swa_prompt(q, k, v, kprev_ref, vprev_ref, bias_ref, sinks_ref, is_first, tile):
    lo = _lane_lo()
    hi = jnp.logical_not(lo)
    col = lax.broadcasted_iota(jnp.int32, (WINDOW, 2 * WINDOW), 1)
    first_mask = jnp.where(col < WINDOW, jnp.where(is_first, NEG_INF, 0.0).astype(F32), 0.0)
    kd = _dup_halves(k, lo)
    vd = _dup_halves(v, lo)
    nblk = tile // WINDOW
    rows_out = []
    for i in range(nblk):
        r0, r1 = i * WINDOW, (i + 1) * WINDOW
        qg = [q[r0:r1, j * LANES:(j + 1) * LANES] for j in range(N_HEADS // 2)]
        o_heads = []
        for kh in range(N_KV):
            if i == 0:
                kp, vp = kprev_ref[kh], vprev_ref[kh]
            else:
                kp, vp = kd[kh][r0 - WINDOW:r0], vd[kh][r0 - WINDOW:r0]
            kb = jnp.concatenate([kp, kd[kh][r0:r1]], axis=0)
            vb = jnp.concatenate([vp, vd[kh][r0:r1]], axis=0)
            heads = [kh * Q_PER_KV + g for g in range(Q_PER_KV)]
            lhs = jnp.concatenate(
                [jnp.where(lo if h % 2 == 0 else hi, qg[h // 2], 0.0) for h in heads], axis=0).astype(BF16)
            s = _dot_nt(lhs, kb) + bias_ref[kh]
            ps, linvs = [], []
            for g in range(Q_PER_KV):
                sg = s[g * WINDOW:(g + 1) * WINDOW]
                if i == 0:
                    sg = sg + first_mask
                sink = sinks_ref[heads[g]] * LOG2E
                mg = jnp.maximum(jnp.max(sg, axis=-1, keepdims=True), sink)
                pg = jnp.exp2(sg - mg)
                lg = jnp.sum(pg, axis=-1, keepdims=True) + jnp.exp2(sink - mg)
                ps.append(pg.astype(BF16))
                linvs.append(1.0 / lg)
            o = _dot(jnp.concatenate(ps, axis=0), vb)
            o_heads += [o[g * WINDOW:(g + 1) * WINDOW] * linvs[g] for g in range(Q_PER_KV)]
        rows_out.append(jnp.concatenate(
            [jnp.where(lo, o_heads[2 * j], o_heads[2 * j + 1]) for j in range(N_HEADS // 2)], axis=1))
    for kh in range(N_KV):
        kprev_ref[kh] = kd[kh][tile - WINDOW:tile]
        vprev_ref[kh] = vd[kh][tile - WINDOW:tile]
    return jnp.concatenate(rows_out, axis=0)


def _shift_window(cache_ref, new_t, out_ref, nb, dec):
    lane = lax.broadcasted_iota(jnp.int32, (1, WINDOW), 1)
    keep = lane < WINDOW - dec
    per_tile = LANES // dec
    for b in range(nb):
        src = new_t[:, (b // per_tile) * LANES:(b // per_tile + 1) * LANES]
        new_cols = pltpu.roll(src, (WINDOW - dec - (b % per_tile) * dec) % LANES, axis=1)
        old = pltpu.roll(cache_ref[b], WINDOW - dec, axis=1)
        out_ref[b] = jnp.where(keep, old, new_cols)


def _swa_sample(q, k, v, ck_ref, cv_ref, wk_ref, wv_ref, bias_c_ref, bias_n_ref, sinks_ref, nb, dec):
    lo = _lane_lo()
    qg = [q[:, j * LANES:(j + 1) * LANES] for j in range(N_HEADS // 2)]
    qgs = [_swap_halves(x) for x in qg]
    qb = jnp.concatenate(
        [_head_lhs(qg, qgs, h, lo).reshape(nb, dec, LANES) for h in range(N_HEADS)], axis=1)
    rows = N_HEADS * dec
    qb_bf = qb.astype(BF16)
    k_bf = k.astype(BF16)
    v_bf = v.astype(BF16)
    ck = ck_ref[...].astype(BF16)
    cv = cv_ref[...].astype(BF16)
    sc = jnp.einsum('bqc,bcp->bqp', qb_bf, ck, preferred_element_type=F32) + bias_c_ref[...][None]
    sn = _dot_nt(qb_bf.reshape(nb * rows, LANES), k_bf).reshape(nb, rows, nb * dec) + bias_n_ref[...]
    sink = jnp.concatenate(
        [jnp.full((1, dec, 1), sinks_ref[h] * LOG2E, F32) for h in range(N_HEADS)], axis=1)
    m = jnp.maximum(jnp.max(sc, axis=-1, keepdims=True), jnp.max(sn, axis=-1, keepdims=True))
    m = jnp.maximum(m, sink)
    pc = jnp.exp2(sc - m)
    pn = jnp.exp2(sn - m)
    l = jnp.sum(pc, axis=-1, keepdims=True) + jnp.sum(pn, axis=-1, keepdims=True) + jnp.exp2(sink - m)
    oc = jnp.einsum('bqp,bcp->bqc', pc.astype(BF16), cv, preferred_element_type=F32)
    on = _dot(pn.reshape(nb * rows, nb * dec).astype(BF16), v_bf).reshape(nb, rows, LANES)
    o = (oc + on) * (1.0 / l)
    o_heads = [o[:, h * dec:(h + 1) * dec, :].reshape(nb * dec, LANES) for h in range(N_HEADS)]
    _shift_window(ck_ref, k.T, wk_ref, nb, dec)
    _shift_window(cv_ref, v.T, wv_ref, nb, dec)
    return _merge_heads(o_heads, lo)


ROUTER_ROWS = 32


def _route_rows(hn_bf, wrt_ref, brt_ref, tile):
    logits = _dot_nt(wrt_ref[...], hn_bf) + jnp.tile(brt_ref[...], (1, tile // LANES))
    row = lax.broadcasted_iota(jnp.int32, logits.shape, 0)
    big = jnp.int32(ROUTER_ROWS)
    lc = jnp.where(row < N_EXPERT_GROUPS, logits, NEG_INF)
    mx = jnp.max(lc, axis=0, keepdims=True)
    g_idx = jnp.min(jnp.where(lc == mx, row, big), axis=0, keepdims=True)
    p_g = 1.0 / jnp.sum(jnp.exp(lc - mx), axis=0, keepdims=True)
    e_row = row - N_EXPERT_GROUPS
    in_group = (e_row >= 0) & (e_row < N_EXPERTS) & ((e_row >> 2) == g_idx)
    lf = jnp.where(in_group, logits, NEG_INF)
    v1 = jnp.max(lf, axis=0, keepdims=True)
    i1 = jnp.min(jnp.where(lf == v1, row, big), axis=0, keepdims=True)
    lf2 = jnp.where(row == i1, NEG_INF, lf)
    v2 = jnp.max(lf2, axis=0, keepdims=True)
    i2 = jnp.min(jnp.where(lf2 == v2, row, big), axis=0, keepdims=True)
    e = jnp.exp(v2 - v1)
    w1 = p_g / (1.0 + e)
    w2 = w1 * e
    return g_idx, i1, i2, w1, w2


def _silu(x):
    return x * (1.0 / (1.0 + jnp.exp2(x * (-LOG2E))))


GROUP_LANE = EXPERTS_PER_GROUP
LOW_SHIFT = 8
SORT_BLOCK = 128


def _moe_grouped(h, vec_ref, wrt_ref, brt_ref, wg_ref, wu_ref, wd_ref, ustrict_ref, xs_ref, es_ref, ys_ref, tile):
    hn_bf = _rms(h, _vec(vec_ref, VEC_FFN, D_MODEL)).astype(BF16)
    g_idx, i1, i2, w1, w2 = _route_rows(hn_bf, wrt_ref, brt_ref, tile)

    row16 = lax.broadcasted_iota(jnp.int32, (16, tile), 0)
    onehot = row16 == g_idx
    gmat = jnp.where(onehot, 1.0, 0.0)
    before = _dot(gmat.astype(BF16), ustrict_ref[...])
    rank = jnp.sum(jnp.where(onehot, before, 0.0), axis=0, keepdims=True)
    offs = [jnp.float32(0.0)]
    for g in range(N_EXPERT_GROUPS - 1):
        offs.append(offs[-1] + jnp.sum(jnp.where(g_idx == g, 1.0, 0.0)))
    offv = jnp.zeros_like(rank)
    for g in range(1, N_EXPERT_GROUPS):
        offv = offv + jnp.where(g_idx == g, offs[g], 0.0)
    pos = (rank + offv).astype(jnp.int32)
    dest = lax.broadcasted_iota(jnp.int32, (tile, tile), 0)
    perm = jnp.where(dest == pos, 1.0, 0.0).astype(BF16)

    row8 = lax.broadcasted_iota(jnp.int32, (8, tile), 0)
    j1 = (i1 - N_EXPERT_GROUPS) & (EXPERTS_PER_GROUP - 1)
    j2 = (i2 - N_EXPERT_GROUPS) & (EXPERTS_PER_GROUP - 1)
    gates = jnp.where(row8 == j1, w1, jnp.where(row8 == j2, w2, 0.0))
    gates_hi = gates.astype(BF16).astype(F32)
    gates_lo = (gates - gates_hi).astype(BF16).astype(F32)
    top = gates_hi + jnp.where(row8 == GROUP_LANE, g_idx.astype(F32), 0.0)
    side = jnp.concatenate([top, gates_lo, jnp.zeros((LANES - 2 * LOW_SHIFT, tile), F32)], axis=0).T
    x_ext = jnp.concatenate([hn_bf, side.astype(BF16)], axis=1)
    srt = _dot(perm, x_ext)
    xs_ref[...] = srt[:, :D_MODEL].astype(BF16)
    es = srt[:, D_MODEL:]
    es_ref[...] = es + pltpu.roll(es, LANES - LOW_SHIFT, axis=1)

    def group_of(row):
        g = jnp.int32(0)
        for k in range(1, N_EXPERT_GROUPS):
            g = g + (jnp.float32(row) >= offs[k]).astype(jnp.int32)
        return g

    def group_pass(rows, g, keep, accumulate):
        xb = xs_ref[rows, :]
        eb = es_ref[rows, :]
        mine = jnp.where(eb[:, GROUP_LANE:GROUP_LANE + 1] == jnp.asarray(g).astype(F32), keep, 0.0)
        parts = []
        for j in range(EXPERTS_PER_GROUP):
            e = g * EXPERTS_PER_GROUP + j
            gate = _dot(xb, wg_ref[e])
            up = _dot(xb, wu_ref[e])
            parts.append((_silu(gate) * up * (eb[:, j:j + 1] * mine)).astype(BF16))
        wd = wd_ref[pl.ds(g * EXPERTS_PER_GROUP, EXPERTS_PER_GROUP)]
        out = _dot(jnp.concatenate(parts, axis=1), wd.reshape(EXPERTS_PER_GROUP * D_EXPERT, D_MODEL))
        if accumulate:
            ys_ref[rows, :] += out
        else:
            ys_ref[rows, :] = out

    nblk = tile // SORT_BLOCK
    for b in range(nblk):
        group_pass(pl.ds(b * SORT_BLOCK, SORT_BLOCK), group_of(b * SORT_BLOCK), 1.0, False)
    for g in range(1, N_EXPERT_GROUPS):
        start = offs[g].astype(jnp.int32)
        blk = jnp.minimum(lax.div(start, jnp.int32(SORT_BLOCK)), nblk - 1)
        inside = jnp.where(lax.rem(start, jnp.int32(SORT_BLOCK)) != 0, 1.0, 0.0)
        group_pass(pl.ds(pl.multiple_of(blk * SORT_BLOCK, SORT_BLOCK), SORT_BLOCK), g, inside, True)
    return lax.dot_general(perm, ys_ref[...].astype(BF16), (((0,), (0,)), ((), ())),
                           preferred_element_type=F32)


def _merge(x, a_out, b_out, vec_ref, w_out_ref):
    g_out = _vec(vec_ref, VEC_OUT, D_MODEL)
    mix = jnp.concatenate(
        [_rms(a_out, g_out[:, :A_WIDTH]), _rms(b_out, g_out[:, A_WIDTH:])], axis=1).astype(BF16)
    return x + _dot(mix, w_out_ref[...])


def _stage_expert_weights(wg_hbm, wu_hbm, wd_hbm, wg_s, wu_s, wd_s, st_a, st_d, sems):
    def gate_cp(e):
        return pltpu.make_async_copy(wg_hbm.at[e], st_a.at[0], sems.at[0])

    def up_cp(e):
        return pltpu.make_async_copy(wu_hbm.at[e], st_a.at[1], sems.at[1])

    def down_cp(e, slot):
        return pltpu.make_async_copy(wd_hbm.at[e], st_d.at[slot], sems.at[2 + slot])

    gate_cp(0).start()
    up_cp(0).start()
    down_cp(0, 0).start()

    def pair(p, carry):
        for slot in (0, 1):
            e = 2 * p + slot
            nxt = e + 1
            more = nxt < N_EXPERTS

            @pl.when(more)
            def _():
                down_cp(nxt, 1 - slot).start()

            gate_cp(e).wait()
            wg_s[e] = st_a[0].astype(BF16)

            @pl.when(more)
            def _():
                gate_cp(nxt).start()

            up_cp(e).wait()
            wu_s[e] = st_a[1].astype(BF16)

            @pl.when(more)
            def _():
                up_cp(nxt).start()

            down_cp(e, slot).wait()
            wd_s[e] = st_d[slot].astype(BF16)
        return carry

    lax.fori_loop(0, N_EXPERTS // 2, pair, 0)


def _stage_dense_weights(w_in_hbm, w_out_hbm, w_in_s, w_out_s, st_a, st_d, sems):
    cols = st_a.shape[2]
    rows = st_d.shape[1]
    n_in = w_in_s.shape[1] // cols
    n_out = w_out_s.shape[0] // rows

    def in_cp(c):
        return pltpu.make_async_copy(w_in_hbm.at[:, pl.ds(c * cols, cols)], st_a.at[c % 2], sems.at[c % 2])

    def out_cp(c):
        return pltpu.make_async_copy(w_out_hbm.at[pl.ds(c * rows, rows), :], st_d.at[c % 2], sems.at[2 + c % 2])

    for c in range(min(2, n_in)):
        in_cp(c).start()
    for c in range(min(2, n_out)):
        out_cp(c).start()
    for c in range(max(n_in, n_out)):
        if c < n_in:
            in_cp(c).wait()
            w_in_s[:, c * cols:(c + 1) * cols] = st_a[c % 2].astype(BF16)
            if c + 2 < n_in:
                in_cp(c + 2).start()
        if c < n_out:
            out_cp(c).wait()
            w_out_s[c * rows:(c + 1) * rows, :] = st_d[c % 2].astype(BF16)
            if c + 2 < n_out:
                out_cp(c + 2).start()


N_STAGE_SEMS = 4


def _bf16_weight_copies(srcs, dsts, sems):
    return tuple(pltpu.make_async_copy(src, dst, sems.at[N_STAGE_SEMS + i])
                 for i, (src, dst) in enumerate(zip(srcs, dsts)))


def _prompt_kernel(x_ref, vec_ref, segq_ref, segk_ref, wsp_ref, bsp_ref, bias_ref, sinks_ref,
                   wrt_ref, brt_ref, ustrict_ref, w_in_hbm, w_out_hbm, wg_hbm, wu_hbm, wd_hbm,
                   y_ref, kwin_ref, vwin_ref, w_in_o, w_out_o, wg_o, wu_o, wd_o,
                   kprev_ref, vprev_ref, xs_ref, es_ref, ys_ref, w_in_ref, w_out_ref, wg_ref, wu_ref, wd_ref,
                   st_a, st_d, sems, *, tile, tiles_per_seq):
    step = pl.program_id(0)
    is_first = (step % tiles_per_seq) == 0

    kept = (w_in_ref, w_out_ref, wg_ref, wu_ref, wd_ref)
    kept_out = (w_in_o, w_out_o, wg_o, wu_o, wd_o)

    @pl.when(step == 0)
    def _():
        _stage_dense_weights(w_in_hbm, w_out_hbm, w_in_ref, w_out_ref, st_a, st_d, sems)
        _stage_expert_weights(wg_hbm, wu_hbm, wd_hbm, wg_ref, wu_ref, wd_ref, st_a, st_d, sems)
        for cp in _bf16_weight_copies(kept, kept_out, sems):
            cp.start()

    @pl.when(step == pl.num_programs(0) - 1)
    def _():
        for cp in _bf16_weight_copies(kept, kept_out, sems):
            cp.wait()

    @pl.when(is_first)
    def _():
        kprev_ref[...] = jnp.zeros_like(kprev_ref)
        vprev_ref[...] = jnp.zeros_like(vprev_ref)

    u, va, q, k, v = _project(x_ref[...], vec_ref, w_in_ref, segq_ref, segk_ref)
    kwin_ref[...] = k[tile - WINDOW:tile].T
    vwin_ref[...] = v[tile - WINDOW:tile].T
    a_out = u * _chunk_mlp(va.astype(BF16), wsp_ref, bsp_ref, tile)
    b_out = _swa_prompt(q * Q_SCALE, k, v, kprev_ref, vprev_ref, bias_ref, sinks_ref, is_first, tile)
    y_ref[...] = _merge(x_ref[...], a_out, b_out, vec_ref, w_out_ref)
    y_ref[...] += _moe_grouped(y_ref[...], vec_ref, wrt_ref, brt_ref, wg_ref, wu_ref, wd_ref, ustrict_ref,
                               xs_ref, es_ref, ys_ref, tile)


def _sample_kernel(x_ref, ck_ref, cv_ref, vec_ref, w_in_ref, segq_ref, segk_ref, wsp_ref, bsp_ref,
                   bias_c_ref, bias_n_ref, sinks_ref, w_out_ref, wrt_ref, brt_ref, ustrict_ref,
                   wg_ref, wu_ref, wd_ref, y_ref, wk_ref, wv_ref, va_ref, xs_ref, es_ref, ys_ref,
                   *, tile, sub, nb, dec):
    hs = []
    for part in range(tile // sub):
        r0 = part * sub
        seqs = pl.ds(part * nb, nb)
        x = x_ref[r0:r0 + sub, :]
        u, va, q, k, v = _project(x, vec_ref, w_in_ref, segq_ref, segk_ref)
        for g in range(A_GROUPS):
            va_ref[pl.ds(r0 * A_GROUPS + g, sub, stride=A_GROUPS), :] = va[:, g * A_CH:(g + 1) * A_CH]
        a_out = u * _chunk_mlp(va.astype(BF16), wsp_ref, bsp_ref, sub)
        b_out = _swa_sample(q * Q_SCALE, k, v, ck_ref.at[seqs], cv_ref.at[seqs], wk_ref.at[seqs], wv_ref.at[seqs],
                            bias_c_ref, bias_n_ref, sinks_ref, nb, dec)
        hs.append(_merge(x, a_out, b_out, vec_ref, w_out_ref))
    h = jnp.concatenate(hs, axis=0)
    y_ref[...] = h + _moe_grouped(h, vec_ref, wrt_ref, brt_ref, wg_ref, wu_ref, wd_ref, ustrict_ref,
                                  xs_ref, es_ref, ys_ref, tile)


def _const_spec(shape):
    nd = len(shape)
    return pl.BlockSpec(shape, lambda i: (0,) * nd, pipeline_mode=pl.Buffered(1))


def _row_spec(tile, width):
    return pl.BlockSpec((tile, width), lambda i: (i, 0))


def _smem_spec():
    return pl.BlockSpec(memory_space=pltpu.SMEM)


def _layer_weights(l, g_attn_norm, g_v_a, g_q, g_k, g_out_a, g_out_b, g_ffn_norm,
                   w_coarse, b_coarse, w_fine, b_fine):
    row_pad = ROUTER_ROWS - N_EXPERT_GROUPS - N_EXPERTS
    wrt = jnp.concatenate([w_coarse[l].T, w_fine[l].T, jnp.zeros((row_pad, D_MODEL), F32)], axis=0)
    brt = jnp.concatenate([b_coarse[l], b_fine[l], jnp.zeros((row_pad,), F32)])
    vec = jnp.concatenate([
        g_attn_norm[l], g_v_a[l].reshape(A_WIDTH), jnp.tile(g_q[l], N_HEADS), jnp.tile(g_k[l], N_KV),
        g_out_a[l], g_out_b[l], g_ffn_norm[l]])
    return dict(
        vec=vec.reshape(1, VEC_LANES),
        segq=jnp.asarray(_seg_ones(B_WIDTH, HEAD_DIM), BF16),
        segk=jnp.asarray(_seg_ones(KV_WIDTH, HEAD_DIM), BF16),
        wrt=wrt.astype(BF16),
        brt=jnp.broadcast_to(brt[:, None], (ROUTER_ROWS, LANES)),
    )


def _spatial_tables(ws, bs, period, tile):
    pair = min(PAIR, tile)
    reps = pair // period
    wbd = (jnp.tile(ws[:, :period, :period], (1, reps, reps)) * _causal_block_mask(period, pair)).astype(BF16)
    bsp = jnp.repeat(bs[:, :period].T, A_CH, axis=1)
    return wbd, bsp


def _upper_triangle(n):
    return jnp.asarray(np.triu(np.ones((n, n), np.float32), 1), BF16)


def _run_prompt(x2d, lw, wbd, bsp, sinks, seq_len, w_in, w_out, w_gate, w_up, w_down):
    assert w_in.shape[0] == w_gate.shape[1] and w_in.shape[1] % w_gate.shape[2] == 0
    assert w_out.shape[1] == w_down.shape[2] and w_out.shape[0] % w_down.shape[1] == 0
    n_tok = x2d.shape[0]
    tile = PROMPT_TILE
    tiles_per_seq = seq_len // tile
    n_seq = n_tok // seq_len
    bias = jnp.asarray(_prompt_bias())
    tail = [lw["wrt"], lw["brt"], _upper_triangle(tile)]
    mats = [w_in, w_out, w_gate, w_up, w_down]
    consts = [lw["vec"], lw["segq"], lw["segk"], wbd, bsp, bias]
    any_spec = pl.BlockSpec(memory_space=pl.ANY)
    in_specs = ([_row_spec(tile, D_MODEL)] + [_const_spec(a.shape) for a in consts] + [_smem_spec()]
                + [_const_spec(a.shape) for a in tail] + [any_spec] * len(mats))
    win_spec = pl.BlockSpec((KV_WIDTH, WINDOW), lambda i: (i // tiles_per_seq, 0))
    out_shape = (jax.ShapeDtypeStruct((n_tok, D_MODEL), F32),
                 jax.ShapeDtypeStruct((n_seq * KV_WIDTH, WINDOW), F32),
                 jax.ShapeDtypeStruct((n_seq * KV_WIDTH, WINDOW), F32),
                 *[jax.ShapeDtypeStruct(w.shape, BF16) for w in mats])
    out_specs = (_row_spec(tile, D_MODEL), win_spec, win_spec) + (any_spec,) * len(mats)
    kern = functools.partial(_prompt_kernel, tile=tile, tiles_per_seq=tiles_per_seq)
    return pl.pallas_call(
        kern,
        out_shape=out_shape,
        grid=(n_tok // tile,),
        in_specs=in_specs,
        out_specs=out_specs,
        scratch_shapes=[pltpu.VMEM((N_KV, WINDOW, KV_WIDTH), BF16), pltpu.VMEM((N_KV, WINDOW, KV_WIDTH), BF16),
                        pltpu.VMEM((tile, D_MODEL), BF16), pltpu.VMEM((tile, LANES), F32),
                        pltpu.VMEM((tile, D_MODEL), F32),
                        *[pltpu.VMEM(w.shape, BF16) for w in mats],
                        pltpu.VMEM((2,) + w_gate.shape[1:], F32), pltpu.VMEM((2,) + w_down.shape[1:], F32),
                        pltpu.SemaphoreType.DMA((N_STAGE_SEMS + len(mats),))],
        compiler_params=pltpu.CompilerParams(dimension_semantics=("arbitrary",),
                                             vmem_limit_bytes=VMEM_LIMIT_BYTES),
        name="layer_prompt",
    )(x2d, *consts, sinks, *tail, *mats)


def _run_sample(x2d, ck_t, cv_t, lw, wbd, bsp, sinks, dec, w_in_bf, w_out_bf, wg_bf, wu_bf, wd_bf):
    n_tok = x2d.shape[0]
    n_seq = ck_t.shape[0]
    tile, sub = SAMPLE_TILE, SAMPLE_SUB
    nb = sub // dec
    bc, bn = _sample_bias(nb, dec)
    head = [lw["vec"], w_in_bf, lw["segq"], lw["segk"]]
    tail = [w_out_bf, lw["wrt"], lw["brt"], _upper_triangle(tile), wg_bf, wu_bf, wd_bf]
    consts = head + [wbd, bsp, jnp.asarray(bc), jnp.asarray(bn)]
    cache_spec = pl.BlockSpec((tile // dec, KV_WIDTH, WINDOW), lambda i: (i, 0, 0))
    in_specs = ([_row_spec(tile, D_MODEL), cache_spec, cache_spec] + [_const_spec(a.shape) for a in consts]
                + [_smem_spec()] + [_const_spec(a.shape) for a in tail])
    out_shape = (jax.ShapeDtypeStruct((n_tok, D_MODEL), F32),
                 jax.ShapeDtypeStruct((n_seq, KV_WIDTH, WINDOW), F32),
                 jax.ShapeDtypeStruct((n_seq, KV_WIDTH, WINDOW), F32),
                 jax.ShapeDtypeStruct((n_tok * A_GROUPS, A_CH), F32))
    out_specs = (_row_spec(tile, D_MODEL), cache_spec, cache_spec, _row_spec(tile * A_GROUPS, A_CH))
    kern = functools.partial(_sample_kernel, tile=tile, sub=sub, nb=nb, dec=dec)
    return pl.pallas_call(
        kern,
        out_shape=out_shape,
        grid=(n_tok // tile,),
        in_specs=in_specs,
        out_specs=out_specs,
        scratch_shapes=[pltpu.VMEM((tile, D_MODEL), BF16), pltpu.VMEM((tile, LANES), F32),
                        pltpu.VMEM((tile, D_MODEL), F32)],
        compiler_params=pltpu.CompilerParams(dimension_semantics=("arbitrary",),
                                             vmem_limit_bytes=VMEM_LIMIT_BYTES),
        name="layer_sample",
    )(x2d, ck_t, cv_t, *consts, sinks, *tail)


def _positions_last(c):
    b, w = c.shape[:2]
    return jnp.transpose(c, (0, 2, 3, 1)).reshape(b, KV_WIDTH, w)


def _positions_first(c_t):
    b, _, w = c_t.shape
    return jnp.transpose(c_t.reshape(b, N_KV, HEAD_DIM, w), (0, 3, 1, 2))


def kernel(x_prompt, x_sample, cache_k, cache_v, g_attn_norm, w_in, g_v_a, w_spatial, b_spatial, g_q, g_k, attn_sinks, g_out_a, g_out_b, w_out, g_ffn_norm, w_coarse, b_coarse, w_fine, b_fine, w_gate, w_up, w_down):
    depth = w_in.shape[0]
    batch, seq, _ = x_prompt.shape
    dbatch, dec, _ = x_sample.shape
    win = cache_k.shape[2]
    assert win == WINDOW and seq % PROMPT_TILE == 0 and (dbatch * dec) % SAMPLE_TILE == 0
    assert PAIR % dec == 0 and SAMPLE_SUB % dec == 0 and LANES % dec == 0

    hp = x_prompt.reshape(batch * seq, D_MODEL)
    hs = x_sample.reshape(dbatch * dec, D_MODEL)
    kp_l, vp_l, ks_l, vs_l, cv_l = [], [], [], [], []
    for l in range(depth):
        lw = _layer_weights(l, g_attn_norm, g_v_a, g_q, g_k, g_out_a, g_out_b, g_ffn_norm,
                            w_coarse, b_coarse, w_fine, b_fine)
        sinks = attn_sinks[l].astype(F32)
        wbd_p, bsp_p = _spatial_tables(w_spatial[l], b_spatial[l], CHUNK, PROMPT_TILE)
        wbd_s, bsp_s = _spatial_tables(w_spatial[l], b_spatial[l], dec, SAMPLE_SUB)

        hp, kp, vp, *weights_bf = _run_prompt(hp, lw, wbd_p, bsp_p, sinks, seq,
                                              w_in[l], w_out[l], w_gate[l], w_up[l], w_down[l])
        kp_l.append(_positions_first(kp.reshape(batch, KV_WIDTH, WINDOW)))
        vp_l.append(_positions_first(vp.reshape(batch, KV_WIDTH, WINDOW)))

        hs, wk, wv, va = _run_sample(hs, _positions_last(cache_k[l]), _positions_last(cache_v[l]),
                                     lw, wbd_s, bsp_s, sinks, dec, *weights_bf)
        ks_l.append(_positions_first(wk))
        vs_l.append(_positions_first(wv))
        cv_l.append(va.reshape(dbatch, dec, A_GROUPS, A_CH))

    return (hp.reshape(batch, seq, D_MODEL), hs.reshape(dbatch, dec, D_MODEL),
            jnp.stack(kp_l, axis=0), jnp.stack(vp_l, axis=0),
            jnp.stack(ks_l, axis=0), jnp.stack(vs_l, axis=0), jnp.stack(cv_l, axis=0))
```

```python
import functools

import numpy as np
import jax
import jax.numpy as jnp
from jax import lax
from jax.experimental import pallas as pl
from jax.experimental.pallas import tpu as pltpu

D_MODEL = 1024
CHUNK = 128
A_GROUPS = 4
A_WIDTH = 512
A_CH = 128
N_HEADS = 8
N_KV = 2
Q_PER_KV = 4
HEAD_DIM = 64
B_WIDTH = 512
KV_WIDTH = 128
WINDOW = 128
IN_COLS = 2 * A_WIDTH + B_WIDTH + 2 * KV_WIDTH
N_EXPERT_GROUPS = 4
EXPERTS_PER_GROUP = 4
N_EXPERTS = 16
D_EXPERT = 256
EPS = 1e-6

LANES = 128
PAIR = 2 * CHUNK
PROMPT_TILE = 512
SAMPLE_TILE = 256
SAMPLE_SUB = 128
VMEM_LIMIT_BYTES = 60 * 1024 * 1024

F32 = jnp.float32
BF16 = jnp.bfloat16
NEG_INF = float("-inf")
LOG2E = 1.4426950408889634
Q_SCALE = (HEAD_DIM ** -0.5) * LOG2E


def _slopes():
    return np.array([2.0 ** (-8.0 * (h + 1) / N_HEADS) for h in range(N_HEADS)], np.float64)


def _prompt_bias():
    t = np.arange(WINDOW)[:, None]
    s = np.arange(2 * WINDOW)[None, :]
    dist = t + WINDOW - s
    valid = (dist >= 0) & (dist < WINDOW)
    sl = _slopes()
    out = np.full((N_KV, Q_PER_KV * WINDOW, 2 * WINDOW), -np.inf, np.float32)
    for kh in range(N_KV):
        for g in range(Q_PER_KV):
            b = np.where(valid, -sl[kh * Q_PER_KV + g] * LOG2E * dist, -np.inf)
            out[kh, g * WINDOW:(g + 1) * WINDOW] = b
    return out


def _sample_bias(nb, dec):
    sl = _slopes()
    t = np.arange(dec)[:, None]
    j = np.arange(WINDOW)[None, :]
    dist_c = t + WINDOW - j
    valid_c = (dist_c >= 0) & (dist_c < WINDOW)
    bc = np.full((N_HEADS * dec, WINDOW), -np.inf, np.float32)
    tp = np.arange(dec)[None, :]
    dist_n = t - tp
    valid_n = dist_n >= 0
    bn = np.full((nb, N_HEADS * dec, nb * dec), -np.inf, np.float32)
    for h in range(N_HEADS):
        bc[h * dec:(h + 1) * dec] = np.where(valid_c, -sl[h] * LOG2E * dist_c, -np.inf)
        blk = np.where(valid_n, -sl[h] * LOG2E * dist_n, -np.inf)
        for b in range(nb):
            bn[b, h * dec:(h + 1) * dec, b * dec:(b + 1) * dec] = blk
    return bc, bn


def _seg_ones(width, seg):
    i = np.arange(width)
    return (i[:, None] // seg == i[None, :] // seg).astype(np.float32)


def _causal_block_mask(period, size):
    i = np.arange(size)
    same = i[:, None] // period == i[None, :] // period
    return (same & (i[None, :] % period <= i[:, None] % period)).astype(np.float32)


def _dot(a, b):
    return jnp.dot(a, b, preferred_element_type=F32)


def _dot_nt(a, b):
    return lax.dot_general(a, b, (((1,), (1,)), ((), ())), preferred_element_type=F32)


def _rms(x, g):
    ms = jnp.mean(x * x, axis=-1, keepdims=True)
    return (x * lax.rsqrt(ms + EPS)) * g


def _seg_rms(x, ones_bf, g, seg):
    ss = _dot((x * x).astype(BF16), ones_bf)
    return (x * lax.rsqrt(ss * (1.0 / seg) + EPS)) * g


GELU_K0 = -2.0 * 0.7978845608028654 * LOG2E
GELU_K1 = GELU_K0 * 0.044715


def _gelu_tanh(x):
    t = (x * x) * GELU_K1 + GELU_K0
    return x * (1.0 / (1.0 + jnp.exp2(x * t)))


VEC_ATTN = 0
VEC_VA = VEC_ATTN + D_MODEL
VEC_Q = VEC_VA + A_WIDTH
VEC_K = VEC_Q + B_WIDTH
VEC_OUT = VEC_K + KV_WIDTH
VEC_FFN = VEC_OUT + D_MODEL
VEC_LANES = VEC_FFN + D_MODEL


def _vec(vec_ref, start, width):
    return vec_ref[:, start:start + width]


def _project(x, vec_ref, w_in_ref, segq_ref, segk_ref):
    xn = _rms(x, _vec(vec_ref, VEC_ATTN, D_MODEL)).astype(BF16)
    z = _dot(xn, w_in_ref[...])
    i1, i2, i3, i4 = A_WIDTH, 2 * A_WIDTH, 2 * A_WIDTH + B_WIDTH, 2 * A_WIDTH + B_WIDTH + KV_WIDTH
    u = _gelu_tanh(z[:, :i1])
    va_pre = _gelu_tanh(z[:, i1:i2])
    g_va = _vec(vec_ref, VEC_VA, A_WIDTH)
    va = jnp.concatenate(
        [_rms(va_pre[:, g * A_CH:(g + 1) * A_CH], g_va[:, g * A_CH:(g + 1) * A_CH]) for g in range(A_GROUPS)],
        axis=1)
    q = _seg_rms(z[:, i2:i3], segq_ref[...], _vec(vec_ref, VEC_Q, B_WIDTH), HEAD_DIM)
    k = _seg_rms(z[:, i3:i4], segk_ref[...], _vec(vec_ref, VEC_K, KV_WIDTH), HEAD_DIM)
    v = z[:, i4:]
    return u, va, q, k, v


def _chunk_mlp(va_bf, wsp_ref, bsp_ref, tile):
    pair = wsp_ref.shape[1]
    n_pairs = tile // pair
    outs = []
    for g in range(A_GROUPS):
        cols = [va_bf[p * pair:(p + 1) * pair, g * A_CH:(g + 1) * A_CH] for p in range(n_pairs)]
        rhs = cols[0] if n_pairs == 1 else jnp.concatenate(cols, axis=1)
        o = _dot(wsp_ref[g], rhs)
        rows = [o[:, p * A_CH:(p + 1) * A_CH] for p in range(n_pairs)]
        outs.append(rows[0] if n_pairs == 1 else jnp.concatenate(rows, axis=0))
    mixed = jnp.concatenate(outs, axis=1)
    bias = bsp_ref[...]
    reps = tile // bias.shape[0]
    return mixed + (bias if reps == 1 else jnp.concatenate([bias] * reps, axis=0))


def _lane_lo():
    return lax.broadcasted_iota(jnp.int32, (1, LANES), 1) < HEAD_DIM


def _swap_halves(x):
    return pltpu.roll(x, HEAD_DIM, axis=1)


def _head_lhs(q_groups, q_groups_swapped, head, lo):
    j, half = divmod(head, 2)
    kv = head // Q_PER_KV
    src = q_groups[j] if half == kv else q_groups_swapped[j]
    return jnp.where(lo if kv == 0 else jnp.logical_not(lo), src, 0.0)


def _merge_heads(o_heads, lo):
    groups = []
    for j in range(N_HEADS // 2):
        kv = (2 * j) // Q_PER_KV
        if kv == 0:
            groups.append(jnp.where(lo, o_heads[2 * j], _swap_halves(o_heads[2 * j + 1])))
        else:
            groups.append(jnp.where(lo, _swap_halves(o_heads[2 * j]), o_heads[2 * j + 1]))
    return jnp.concatenate(groups, axis=1)


def _dup_halves(x, lo):
    xs = _swap_halves(x)
    return jnp.where(lo, x, xs).astype(BF16), jnp.where(lo, xs, x).astype(BF16)


def _swa_prompt(q, k, v, kprev_ref, vprev_ref, bias_ref, sinks_ref, is_first, tile):
    lo = _lane_lo()
    hi = jnp.logical_not(lo)
    col = lax.broadcasted_iota(jnp.int32, (WINDOW, 2 * WINDOW), 1)
    first_mask = jnp.where(col < WINDOW, jnp.where(is_first, NEG_INF, 0.0).astype(F32), 0.0)
    kd = _dup_halves(k, lo)
    vd = _dup_halves(v, lo)
    nblk = tile // WINDOW
    rows_out = []
    for i in range(nblk):
        r0, r1 = i * WINDOW, (i + 1) * WINDOW
        qg = [q[r0:r1, j * LANES:(j + 1) * LANES] for j in range(N_HEADS // 2)]
        o_heads = []
        for kh in range(N_KV):
            if i == 0:
                kp, vp = kprev_ref[kh], vprev_ref[kh]
            else:
                kp, vp = kd[kh][r0 - WINDOW:r0], vd[kh][r0 - WINDOW:r0]
            kb = jnp.concatenate([kp, kd[kh][r0:r1]], axis=0)
            vb = jnp.concatenate([vp, vd[kh][r0:r1]], axis=0)
            heads = [kh * Q_PER_KV + g for g in range(Q_PER_KV)]
            lhs = jnp.concatenate(
                [jnp.where(lo if h % 2 == 0 else hi, qg[h // 2], 0.0) for h in heads], axis=0).astype(BF16)
            s = _dot_nt(lhs, kb) + bias_ref[kh]
            ps, linvs = [], []
            for g in range(Q_PER_KV):
                sg = s[g * WINDOW:(g + 1) * WINDOW]
                if i == 0:
                    sg = sg + first_mask
                sink = sinks_ref[heads[g]] * LOG2E
                mg = jnp.maximum(jnp.max(sg, axis=-1, keepdims=True), sink)
                pg = jnp.exp2(sg - mg)
                lg = jnp.sum(pg, axis=-1, keepdims=True) + jnp.exp2(sink - mg)
                ps.append(pg.astype(BF16))
                linvs.append(1.0 / lg)
            o = _dot(jnp.concatenate(ps, axis=0), vb)
            o_heads += [o[g * WINDOW:(g + 1) * WINDOW] * linvs[g] for g in range(Q_PER_KV)]
        rows_out.append(jnp.concatenate(
            [jnp.where(lo, o_heads[2 * j], o_heads[2 * j + 1]) for j in range(N_HEADS // 2)], axis=1))
    for kh in range(N_KV):
        kprev_ref[kh] = kd[kh][tile - WINDOW:tile]
        vprev_ref[kh] = vd[kh][tile - WINDOW:tile]
    return jnp.concatenate(rows_out, axis=0)


def _shift_window(cache_ref, new_t, out_ref, nb, dec):
    lane = lax.broadcasted_iota(jnp.int32, (1, WINDOW), 1)
    keep = lane < WINDOW - dec
    per_tile = LANES // dec
    for b in range(nb):
        src = new_t[:, (b // per_tile) * LANES:(b // per_tile + 1) * LANES]
        new_cols = pltpu.roll(src, (WINDOW - dec - (b % per_tile) * dec) % LANES, axis=1)
        old = pltpu.roll(cache_ref[b], WINDOW - dec, axis=1)
        out_ref[b] = jnp.where(keep, old, new_cols)


def _swa_sample(q, k, v, ck_ref, cv_ref, wk_ref, wv_ref, bias_c_ref, bias_n_ref, sinks_ref, nb, dec):
    lo = _lane_lo()
    qg = [q[:, j * LANES:(j + 1) * LANES] for j in range(N_HEADS // 2)]
    qgs = [_swap_halves(x) for x in qg]
    qb = jnp.concatenate(
        [_head_lhs(qg, qgs, h, lo).reshape(nb, dec, LANES) for h in range(N_HEADS)], axis=1)
    rows = N_HEADS * dec
    qb_bf = qb.astype(BF16)
    k_bf = k.astype(BF16)
    v_bf = v.astype(BF16)
    ck = ck_ref[...].astype(BF16)
    cv = cv_ref[...].astype(BF16)
    sc = jnp.einsum('bqc,bcp->bqp', qb_bf, ck, preferred_element_type=F32) + bias_c_ref[...][None]
    sn = _dot_nt(qb_bf.reshape(nb * rows, LANES), k_bf).reshape(nb, rows, nb * dec) + bias_n_ref[...]
    sink = jnp.concatenate(
        [jnp.full((1, dec, 1), sinks_ref[h] * LOG2E, F32) for h in range(N_HEADS)], axis=1)
    m = jnp.maximum(jnp.max(sc, axis=-1, keepdims=True), jnp.max(sn, axis=-1, keepdims=True))
    m = jnp.maximum(m, sink)
    pc = jnp.exp2(sc - m)
    pn = jnp.exp2(sn - m)
    l = jnp.sum(pc, axis=-1, keepdims=True) + jnp.sum(pn, axis=-1, keepdims=True) + jnp.exp2(sink - m)
    oc = jnp.einsum('bqp,bcp->bqc', pc.astype(BF16), cv, preferred_element_type=F32)
    on = _dot(pn.reshape(nb * rows, nb * dec).astype(BF16), v_bf).reshape(nb, rows, LANES)
    o = (oc + on) * (1.0 / l)
    o_heads = [o[:, h * dec:(h + 1) * dec, :].reshape(nb * dec, LANES) for h in range(N_HEADS)]
    _shift_window(ck_ref, k.T, wk_ref, nb, dec)
    _shift_window(cv_ref, v.T, wv_ref, nb, dec)
    return _merge_heads(o_heads, lo)


ROUTER_ROWS = 32


def _route_rows(hn_bf, wrt_ref, brt_ref, tile):
    logits = _dot_nt(wrt_ref[...], hn_bf) + jnp.tile(brt_ref[...], (1, tile // LANES))
    row = lax.broadcasted_iota(jnp.int32, logits.shape, 0)
    big = jnp.int32(ROUTER_ROWS)
    lc = jnp.where(row < N_EXPERT_GROUPS, logits, NEG_INF)
    mx = jnp.max(lc, axis=0, keepdims=True)
    g_idx = jnp.min(jnp.where(lc == mx, row, big), axis=0, keepdims=True)
    p_g = 1.0 / jnp.sum(jnp.exp(lc - mx), axis=0, keepdims=True)
    e_row = row - N_EXPERT_GROUPS
    in_group = (e_row >= 0) & (e_row < N_EXPERTS) & ((e_row >> 2) == g_idx)
    lf = jnp.where(in_group, logits, NEG_INF)
    v1 = jnp.max(lf, axis=0, keepdims=True)
    i1 = jnp.min(jnp.where(lf == v1, row, big), axis=0, keepdims=True)
    lf2 = jnp.where(row == i1, NEG_INF, lf)
    v2 = jnp.max(lf2, axis=0, keepdims=True)
    i2 = jnp.min(jnp.where(lf2 == v2, row, big), axis=0, keepdims=True)
    e = jnp.exp(v2 - v1)
    w1 = p_g / (1.0 + e)
    w2 = w1 * e
    return g_idx, i1, i2, w1, w2


def _silu(x):
    return x * (1.0 / (1.0 + jnp.exp2(x * (-LOG2E))))


GROUP_LANE = EXPERTS_PER_GROUP
LOW_SHIFT = 8
SORT_BLOCK = 128


def _moe_grouped(h, vec_ref, wrt_ref, brt_ref, wg_ref, wu_ref, wd_ref, ustrict_ref, xs_ref, es_ref, ys_ref, tile):
    hn_bf = _rms(h, _vec(vec_ref, VEC_FFN, D_MODEL)).astype(BF16)
    g_idx, i1, i2, w1, w2 = _route_rows(hn_bf, wrt_ref, brt_ref, tile)

    row16 = lax.broadcasted_iota(jnp.int32, (16, tile), 0)
    onehot = row16 == g_idx
    gmat = jnp.where(onehot, 1.0, 0.0)
    before = _dot(gmat.astype(BF16), ustrict_ref[...])
    rank = jnp.sum(jnp.where(onehot, before, 0.0), axis=0, keepdims=True)
    offs = [jnp.float32(0.0)]
    for g in range(N_EXPERT_GROUPS - 1):
        offs.append(offs[-1] + jnp.sum(jnp.where(g_idx == g, 1.0, 0.0)))
    offv = jnp.zeros_like(rank)
    for g in range(1, N_EXPERT_GROUPS):
        offv = offv + jnp.where(g_idx == g, offs[g], 0.0)
    pos = (rank + offv).astype(jnp.int32)
    dest = lax.broadcasted_iota(jnp.int32, (tile, tile), 0)
    perm = jnp.where(dest == pos, 1.0, 0.0).astype(BF16)

    row8 = lax.broadcasted_iota(jnp.int32, (8, tile), 0)
    j1 = (i1 - N_EXPERT_GROUPS) & (EXPERTS_PER_GROUP - 1)
    j2 = (i2 - N_EXPERT_GROUPS) & (EXPERTS_PER_GROUP - 1)
    gates = jnp.where(row8 == j1, w1, jnp.where(row8 == j2, w2, 0.0))
    gates_hi = gates.astype(BF16).astype(F32)
    gates_lo = (gates - gates_hi).astype(BF16).astype(F32)
    top = gates_hi + jnp.where(row8 == GROUP_LANE, g_idx.astype(F32), 0.0)
    side = jnp.concatenate([top, gates_lo, jnp.zeros((LANES - 2 * LOW_SHIFT, tile), F32)], axis=0).T
    x_ext = jnp.concatenate([hn_bf, side.astype(BF16)], axis=1)
    srt = _dot(perm, x_ext)
    xs_ref[...] = srt[:, :D_MODEL].astype(BF16)
    es = srt[:, D_MODEL:]
    es_ref[...] = es + pltpu.roll(es, LANES - LOW_SHIFT, axis=1)

    def group_of(row):
        g = jnp.int32(0)
        for k in range(1, N_EXPERT_GROUPS):
            g = g + (jnp.float32(row) >= offs[k]).astype(jnp.int32)
        return g

    def group_pass(rows, g, keep, accumulate):
        xb = xs_ref[rows, :]
        eb = es_ref[rows, :]
        mine = jnp.where(eb[:, GROUP_LANE:GROUP_LANE + 1] == jnp.asarray(g).astype(F32), keep, 0.0)
        parts = []
        for j in range(EXPERTS_PER_GROUP):
            e = g * EXPERTS_PER_GROUP + j
            gate = _dot(xb, wg_ref[e])
            up = _dot(xb, wu_ref[e])
            parts.append((_silu(gate) * up * (eb[:, j:j + 1] * mine)).astype(BF16))
        wd = wd_ref[pl.ds(g * EXPERTS_PER_GROUP, EXPERTS_PER_GROUP)]
        out = _dot(jnp.concatenate(parts, axis=1), wd.reshape(EXPERTS_PER_GROUP * D_EXPERT, D_MODEL))
        if accumulate:
            ys_ref[rows, :] += out
        else:
            ys_ref[rows, :] = out

    nblk = tile // SORT_BLOCK
    for b in range(nblk):
        group_pass(pl.ds(b * SORT_BLOCK, SORT_BLOCK), group_of(b * SORT_BLOCK), 1.0, False)
    for g in range(1, N_EXPERT_GROUPS):
        start = offs[g].astype(jnp.int32)
        blk = jnp.minimum(lax.div(start, jnp.int32(SORT_BLOCK)), nblk - 1)
        inside = jnp.where(lax.rem(start, jnp.int32(SORT_BLOCK)) != 0, 1.0, 0.0)
        group_pass(pl.ds(pl.multiple_of(blk * SORT_BLOCK, SORT_BLOCK), SORT_BLOCK), g, inside, True)
    return lax.dot_general(perm, ys_ref[...].astype(BF16), (((0,), (0,)), ((), ())),
                           preferred_element_type=F32)


def _merge(x, a_out, b_out, vec_ref, w_out_ref):
    g_out = _vec(vec_ref, VEC_OUT, D_MODEL)
    mix = jnp.concatenate(
        [_rms(a_out, g_out[:, :A_WIDTH]), _rms(b_out, g_out[:, A_WIDTH:])], axis=1).astype(BF16)
    return x + _dot(mix, w_out_ref[...])


def _stage_expert_weights(wg_hbm, wu_hbm, wd_hbm, wg_s, wu_s, wd_s, st_a, st_d, sems):
    def gate_cp(e):
        return pltpu.make_async_copy(wg_hbm.at[e], st_a.at[0], sems.at[0])

    def up_cp(e):
        return pltpu.make_async_copy(wu_hbm.at[e], st_a.at[1], sems.at[1])

    def down_cp(e, slot):
        return pltpu.make_async_copy(wd_hbm.at[e], st_d.at[slot], sems.at[2 + slot])

    gate_cp(0).start()
    up_cp(0).start()
    down_cp(0, 0).start()

    def pair(p, carry):
        for slot in (0, 1):
            e = 2 * p + slot
            nxt = e + 1
            more = nxt < N_EXPERTS

            @pl.when(more)
            def _():
                down_cp(nxt, 1 - slot).start()

            gate_cp(e).wait()
            wg_s[e] = st_a[0].astype(BF16)

            @pl.when(more)
            def _():
                gate_cp(nxt).start()

            up_cp(e).wait()
            wu_s[e] = st_a[1].astype(BF16)

            @pl.when(more)
            def _():
                up_cp(nxt).start()

            down_cp(e, slot).wait()
            wd_s[e] = st_d[slot].astype(BF16)
        return carry

    lax.fori_loop(0, N_EXPERTS // 2, pair, 0)


N_STAGE_SEMS = 4


def _bf16_weight_copies(srcs, dsts, sems):
    return tuple(pltpu.make_async_copy(src, dst, sems.at[N_STAGE_SEMS + i])
                 for i, (src, dst) in enumerate(zip(srcs, dsts)))


def _prompt_kernel(x_ref, vec_ref, w_in_ref, segq_ref, segk_ref, wsp_ref, bsp_ref, bias_ref, sinks_ref,
                   w_out_ref, wrt_ref, brt_ref, ustrict_ref, wg_hbm, wu_hbm, wd_hbm,
                   y_ref, kwin_ref, vwin_ref, wg_o, wu_o, wd_o,
                   kprev_ref, vprev_ref, xs_ref, es_ref, ys_ref, wg_ref, wu_ref, wd_ref,
                   st_a, st_d, sems, *, tile, tiles_per_seq):
    step = pl.program_id(0)
    is_first = (step % tiles_per_seq) == 0

    kept = (wg_ref, wu_ref, wd_ref)
    kept_out = (wg_o, wu_o, wd_o)

    @pl.when(step == 0)
    def _():
        _stage_expert_weights(wg_hbm, wu_hbm, wd_hbm, wg_ref, wu_ref, wd_ref, st_a, st_d, sems)
        for cp in _bf16_weight_copies(kept, kept_out, sems):
            cp.start()

    @pl.when(step == pl.num_programs(0) - 1)
    def _():
        for cp in _bf16_weight_copies(kept, kept_out, sems):
            cp.wait()

    @pl.when(is_first)
    def _():
        kprev_ref[...] = jnp.zeros_like(kprev_ref)
        vprev_ref[...] = jnp.zeros_like(vprev_ref)

    x = x_ref[...]
    u, va, q, k, v = _project(x, vec_ref, w_in_ref, segq_ref, segk_ref)
    kwin_ref[...] = k[tile - WINDOW:tile].T
    vwin_ref[...] = v[tile - WINDOW:tile].T
    a_out = u * _chunk_mlp(va.astype(BF16), wsp_ref, bsp_ref, tile)
    b_out = _swa_prompt(q * Q_SCALE, k, v, kprev_ref, vprev_ref, bias_ref, sinks_ref, is_first, tile)
    h = _merge(x, a_out, b_out, vec_ref, w_out_ref)
    y_ref[...] = h + _moe_grouped(h, vec_ref, wrt_ref, brt_ref, wg_ref, wu_ref, wd_ref, ustrict_ref,
                                  xs_ref, es_ref, ys_ref, tile)


def _sample_kernel(x_ref, ck_ref, cv_ref, vec_ref, w_in_ref, segq_ref, segk_ref, wsp_ref, bsp_ref,
                   bias_c_ref, bias_n_ref, sinks_ref, w_out_ref, wrt_ref, brt_ref, ustrict_ref,
                   wg_ref, wu_ref, wd_ref, y_ref, wk_ref, wv_ref, va_ref, xs_ref, es_ref, ys_ref,
                   *, tile, sub, nb, dec):
    hs = []
    for part in range(tile // sub):
        r0 = part * sub
        seqs = pl.ds(part * nb, nb)
        x = x_ref[r0:r0 + sub, :]
        u, va, q, k, v = _project(x, vec_ref, w_in_ref, segq_ref, segk_ref)
        for g in range(A_GROUPS):
            va_ref[pl.ds(r0 * A_GROUPS + g, sub, stride=A_GROUPS), :] = va[:, g * A_CH:(g + 1) * A_CH]
        a_out = u * _chunk_mlp(va.astype(BF16), wsp_ref, bsp_ref, sub)
        b_out = _swa_sample(q * Q_SCALE, k, v, ck_ref.at[seqs], cv_ref.at[seqs], wk_ref.at[seqs], wv_ref.at[seqs],
                            bias_c_ref, bias_n_ref, sinks_ref, nb, dec)
        hs.append(_merge(x, a_out, b_out, vec_ref, w_out_ref))
    h = jnp.concatenate(hs, axis=0)
    y_ref[...] = h + _moe_grouped(h, vec_ref, wrt_ref, brt_ref, wg_ref, wu_ref, wd_ref, ustrict_ref,
                                  xs_ref, es_ref, ys_ref, tile)


def _const_spec(shape):
    nd = len(shape)
    return pl.BlockSpec(shape, lambda i: (0,) * nd, pipeline_mode=pl.Buffered(1))


def _row_spec(tile, width):
    return pl.BlockSpec((tile, width), lambda i: (i, 0))


def _smem_spec():
    return pl.BlockSpec(memory_space=pltpu.SMEM)


def _layer_weights(l, g_attn_norm, w_in, g_v_a, g_q, g_k, g_out_a, g_out_b, w_out, g_ffn_norm,
                   w_coarse, b_coarse, w_fine, b_fine):
    row_pad = ROUTER_ROWS - N_EXPERT_GROUPS - N_EXPERTS
    wrt = jnp.concatenate([w_coarse[l].T, w_fine[l].T, jnp.zeros((row_pad, D_MODEL), F32)], axis=0)
    brt = jnp.concatenate([b_coarse[l], b_fine[l], jnp.zeros((row_pad,), F32)])
    vec = jnp.concatenate([
        g_attn_norm[l], g_v_a[l].reshape(A_WIDTH), jnp.tile(g_q[l], N_HEADS), jnp.tile(g_k[l], N_KV),
        g_out_a[l], g_out_b[l], g_ffn_norm[l]])
    return dict(
        vec=vec.reshape(1, VEC_LANES),
        segq=jnp.asarray(_seg_ones(B_WIDTH, HEAD_DIM), BF16),
        segk=jnp.asarray(_seg_ones(KV_WIDTH, HEAD_DIM), BF16),
        wrt=wrt.astype(BF16),
        brt=jnp.broadcast_to(brt[:, None], (ROUTER_ROWS, LANES)),
        w_in=w_in[l].astype(BF16),
        w_out=w_out[l].astype(BF16),
    )


def _spatial_tables(ws, bs, period, tile):
    pair = min(PAIR, tile)
    reps = pair // period
    wbd = (jnp.tile(ws[:, :period, :period], (1, reps, reps)) * _causal_block_mask(period, pair)).astype(BF16)
    bsp = jnp.repeat(bs[:, :period].T, A_CH, axis=1)
    return wbd, bsp


def _upper_triangle(n):
    return jnp.asarray(np.triu(np.ones((n, n), np.float32), 1), BF16)


def _run_prompt(x2d, lw, wbd, bsp, sinks, seq_len, w_gate, w_up, w_down):
    n_tok = x2d.shape[0]
    tile = PROMPT_TILE
    tiles_per_seq = seq_len // tile
    n_seq = n_tok // seq_len
    bias = jnp.asarray(_prompt_bias())
    tail = [lw["w_out"], lw["wrt"], lw["brt"], _upper_triangle(tile)]
    mats = [w_gate, w_up, w_down]
    consts = [lw["vec"], lw["w_in"], lw["segq"], lw["segk"], wbd, bsp, bias]
    any_spec = pl.BlockSpec(memory_space=pl.ANY)
    in_specs = ([_row_spec(tile, D_MODEL)] + [_const_spec(a.shape) for a in consts] + [_smem_spec()]
                + [_const_spec(a.shape) for a in tail] + [any_spec] * len(mats))
    win_spec = pl.BlockSpec((KV_WIDTH, WINDOW), lambda i: (i // tiles_per_seq, 0))
    out_shape = (jax.ShapeDtypeStruct((n_tok, D_MODEL), F32),
                 jax.ShapeDtypeStruct((n_seq * KV_WIDTH, WINDOW), F32),
                 jax.ShapeDtypeStruct((n_seq * KV_WIDTH, WINDOW), F32),
                 *[jax.ShapeDtypeStruct(w.shape, BF16) for w in mats])
    out_specs = (_row_spec(tile, D_MODEL), win_spec, win_spec) + (any_spec,) * len(mats)
    kern = functools.partial(_prompt_kernel, tile=tile, tiles_per_seq=tiles_per_seq)
    return pl.pallas_call(
        kern,
        out_shape=out_shape,
        grid=(n_tok // tile,),
        in_specs=in_specs,
        out_specs=out_specs,
        scratch_shapes=[pltpu.VMEM((N_KV, WINDOW, KV_WIDTH), BF16), pltpu.VMEM((N_KV, WINDOW, KV_WIDTH), BF16),
                        pltpu.VMEM((tile, D_MODEL), BF16), pltpu.VMEM((tile, LANES), F32),
                        pltpu.VMEM((tile, D_MODEL), F32),
                        *[pltpu.VMEM(w.shape, BF16) for w in mats],
                        pltpu.VMEM((2,) + w_gate.shape[1:], F32), pltpu.VMEM((2,) + w_down.shape[1:], F32),
                        pltpu.SemaphoreType.DMA((N_STAGE_SEMS + len(mats),))],
        compiler_params=pltpu.CompilerParams(dimension_semantics=("arbitrary",),
                                             vmem_limit_bytes=VMEM_LIMIT_BYTES),
        name="layer_prompt",
    )(x2d, *consts, sinks, *tail, *mats)


def _run_sample(x2d, ck_t, cv_t, lw, wbd, bsp, sinks, dec, wg_bf, wu_bf, wd_bf):
    n_tok = x2d.shape[0]
    n_seq = ck_t.shape[0]
    tile, sub = SAMPLE_TILE, SAMPLE_SUB
    nb = sub // dec
    bc, bn = _sample_bias(nb, dec)
    head = [lw["vec"], lw["w_in"], lw["segq"], lw["segk"]]
    tail = [lw["w_out"], lw["wrt"], lw["brt"], _upper_triangle(tile), wg_bf, wu_bf, wd_bf]
    consts = head + [wbd, bsp, jnp.asarray(bc), jnp.asarray(bn)]
    cache_spec = pl.BlockSpec((tile // dec, KV_WIDTH, WINDOW), lambda i: (i, 0, 0))
    in_specs = ([_row_spec(tile, D_MODEL), cache_spec, cache_spec] + [_const_spec(a.shape) for a in consts]
                + [_smem_spec()] + [_const_spec(a.shape) for a in tail])
    out_shape = (jax.ShapeDtypeStruct((n_tok, D_MODEL), F32),
                 jax.ShapeDtypeStruct((n_seq, KV_WIDTH, WINDOW), F32),
                 jax.ShapeDtypeStruct((n_seq, KV_WIDTH, WINDOW), F32),
                 jax.ShapeDtypeStruct((n_tok * A_GROUPS, A_CH), F32))
    out_specs = (_row_spec(tile, D_MODEL), cache_spec, cache_spec, _row_spec(tile * A_GROUPS, A_CH))
    kern = functools.partial(_sample_kernel, tile=tile, sub=sub, nb=nb, dec=dec)
    return pl.pallas_call(
        kern,
        out_shape=out_shape,
        grid=(n_tok // tile,),
        in_specs=in_specs,
        out_specs=out_specs,
        scratch_shapes=[pltpu.VMEM((tile, D_MODEL), BF16), pltpu.VMEM((tile, LANES), F32),
                        pltpu.VMEM((tile, D_MODEL), F32)],
        compiler_params=pltpu.CompilerParams(dimension_semantics=("arbitrary",),
                                             vmem_limit_bytes=VMEM_LIMIT_BYTES),
        name="layer_sample",
    )(x2d, ck_t, cv_t, *consts, sinks, *tail)


def _positions_last(c):
    b, w = c.shape[:2]
    return jnp.transpose(c, (0, 2, 3, 1)).reshape(b, KV_WIDTH, w)


def _positions_first(c_t):
    b, _, w = c_t.shape
    return jnp.transpose(c_t.reshape(b, N_KV, HEAD_DIM, w), (0, 3, 1, 2))


def kernel(x_prompt, x_sample, cache_k, cache_v, g_attn_norm, w_in, g_v_a, w_spatial, b_spatial, g_q, g_k, attn_sinks, g_out_a, g_out_b, w_out, g_ffn_norm, w_coarse, b_coarse, w_fine, b_fine, w_gate, w_up, w_down):
    depth = w_in.shape[0]
    batch, seq, _ = x_prompt.shape
    dbatch, dec, _ = x_sample.shape
    win = cache_k.shape[2]
    assert win == WINDOW and seq % PROMPT_TILE == 0 and (dbatch * dec) % SAMPLE_TILE == 0
    assert PAIR % dec == 0 and SAMPLE_SUB % dec == 0 and LANES % dec == 0

    hp = x_prompt.reshape(batch * seq, D_MODEL)
    hs = x_sample.reshape(dbatch * dec, D_MODEL)
    kp_l, vp_l, ks_l, vs_l, cv_l = [], [], [], [], []
    for l in range(depth):
        lw = _layer_weights(l, g_attn_norm, w_in, g_v_a, g_q, g_k, g_out_a, g_out_b, w_out, g_ffn_norm,
                            w_coarse, b_coarse, w_fine, b_fine)
        sinks = attn_sinks[l].astype(F32)
        wbd_p, bsp_p = _spatial_tables(w_spatial[l], b_spatial[l], CHUNK, PROMPT_TILE)
        wbd_s, bsp_s = _spatial_tables(w_spatial[l], b_spatial[l], dec, SAMPLE_SUB)

        hp, kp, vp, *weights_bf = _run_prompt(hp, lw, wbd_p, bsp_p, sinks, seq,
                                              w_gate[l], w_up[l], w_down[l])
        kp_l.append(_positions_first(kp.reshape(batch, KV_WIDTH, WINDOW)))
        vp_l.append(_positions_first(vp.reshape(batch, KV_WIDTH, WINDOW)))

        hs, wk, wv, va = _run_sample(hs, _positions_last(cache_k[l]), _positions_last(cache_v[l]),
                                     lw, wbd_s, bsp_s, sinks, dec, *weights_bf)
        ks_l.append(_positions_first(wk))
        vs_l.append(_positions_first(wv))
        cv_l.append(va.reshape(dbatch, dec, A_GROUPS, A_CH))

    return (hp.reshape(batch, seq, D_MODEL), hs.reshape(dbatch, dec, D_MODEL),
            jnp.stack(kp_l, axis=0), jnp.stack(vp_l, axis=0),
            jnp.stack(ks_l, axis=0), jnp.stack(vs_l, axis=0), jnp.stack(cv_l, axis=0))
```

```python
import functools

import numpy as np
import jax
import jax.numpy as jnp
from jax import lax
from jax.experimental import pallas as pl
from jax.experimental.pallas import tpu as pltpu

D_MODEL = 1024
CHUNK = 128
A_GROUPS = 4
A_WIDTH = 512
A_CH = 128
N_HEADS = 8
N_KV = 2
Q_PER_KV = 4
HEAD_DIM = 64
B_WIDTH = 512
KV_WIDTH = 128
WINDOW = 128
IN_COLS = 2 * A_WIDTH + B_WIDTH + 2 * KV_WIDTH
N_EXPERT_GROUPS = 4
EXPERTS_PER_GROUP = 4
N_EXPERTS = 16
D_EXPERT = 256
EPS = 1e-6

LANES = 128
PAIR = 2 * CHUNK
PROMPT_TILE = 512
SAMPLE_TILE = 256
SAMPLE_SUB = 128
VMEM_LIMIT_BYTES = 60 * 1024 * 1024

F32 = jnp.float32
BF16 = jnp.bfloat16
NEG_INF = float("-inf")
LOG2E = 1.4426950408889634
Q_SCALE = (HEAD_DIM ** -0.5) * LOG2E


def _slopes():
    return np.array([2.0 ** (-8.0 * (h + 1) / N_HEADS) for h in range(N_HEADS)], np.float64)


def _prompt_bias():
    t = np.arange(WINDOW)[:, None]
    s = np.arange(2 * WINDOW)[None, :]
    dist = t + WINDOW - s
    valid = (dist >= 0) & (dist < WINDOW)
    sl = _slopes()
    out = np.full((N_KV, Q_PER_KV * WINDOW, 2 * WINDOW), -np.inf, np.float32)
    for kh in range(N_KV):
        for g in range(Q_PER_KV):
            b = np.where(valid, -sl[kh * Q_PER_KV + g] * LOG2E * dist, -np.inf)
            out[kh, g * WINDOW:(g + 1) * WINDOW] = b
    return out


def _sample_bias(nb, dec):
    sl = _slopes()
    t = np.arange(dec)[:, None]
    j = np.arange(WINDOW)[None, :]
    dist_c = t + WINDOW - j
    valid_c = (dist_c >= 0) & (dist_c < WINDOW)
    bc = np.full((N_HEADS * dec, WINDOW), -np.inf, np.float32)
    tp = np.arange(dec)[None, :]
    dist_n = t - tp
    valid_n = dist_n >= 0
    bn = np.full((nb, N_HEADS * dec, nb * dec), -np.inf, np.float32)
    for h in range(N_HEADS):
        bc[h * dec:(h + 1) * dec] = np.where(valid_c, -sl[h] * LOG2E * dist_c, -np.inf)
        blk = np.where(valid_n, -sl[h] * LOG2E * dist_n, -np.inf)
        for b in range(nb):
            bn[b, h * dec:(h + 1) * dec, b * dec:(b + 1) * dec] = blk
    return bc, bn


def _seg_ones(width, seg):
    i = np.arange(width)
    return (i[:, None] // seg == i[None, :] // seg).astype(np.float32)


def _causal_block_mask(period, size):
    i = np.arange(size)
    same = i[:, None] // period == i[None, :] // period
    return (same & (i[None, :] % period <= i[:, None] % period)).astype(np.float32)


def _dot(a, b):
    return jnp.dot(a, b, preferred_element_type=F32)


def _dot_nt(a, b):
    return lax.dot_general(a, b, (((1,), (1,)), ((), ())), preferred_element_type=F32)


def _rms(x, g):
    ms = jnp.mean(x * x, axis=-1, keepdims=True)
    return (x * lax.rsqrt(ms + EPS)) * g


def _seg_rms(x, ones_bf, g, seg):
    ss = _dot((x * x).astype(BF16), ones_bf)
    return (x * lax.rsqrt(ss * (1.0 / seg) + EPS)) * g


GELU_K0 = -2.0 * 0.7978845608028654 * LOG2E
GELU_K1 = GELU_K0 * 0.044715


def _gelu_tanh(x):
    t = (x * x) * GELU_K1 + GELU_K0
    return x * (1.0 / (1.0 + jnp.exp2(x * t)))


VEC_ATTN = 0
VEC_VA = VEC_ATTN + D_MODEL
VEC_Q = VEC_VA + A_WIDTH
VEC_K = VEC_Q + B_WIDTH
VEC_OUT = VEC_K + KV_WIDTH
VEC_FFN = VEC_OUT + D_MODEL
VEC_LANES = VEC_FFN + D_MODEL


def _vec(vec_ref, start, width):
    return vec_ref[:, start:start + width]


def _project(x, vec_ref, w_in_ref, segq_ref, segk_ref):
    xn = _rms(x, _vec(vec_ref, VEC_ATTN, D_MODEL)).astype(BF16)
    z = _dot(xn, w_in_ref[...])
    i1, i2, i3, i4 = A_WIDTH, 2 * A_WIDTH, 2 * A_WIDTH + B_WIDTH, 2 * A_WIDTH + B_WIDTH + KV_WIDTH
    u = _gelu_tanh(z[:, :i1])
    va_pre = _gelu_tanh(z[:, i1:i2])
    g_va = _vec(vec_ref, VEC_VA, A_WIDTH)
    va = jnp.concatenate(
        [_rms(va_pre[:, g * A_CH:(g + 1) * A_CH], g_va[:, g * A_CH:(g + 1) * A_CH]) for g in range(A_GROUPS)],
        axis=1)
    q = _seg_rms(z[:, i2:i3], segq_ref[...], _vec(vec_ref, VEC_Q, B_WIDTH), HEAD_DIM)
    k = _seg_rms(z[:, i3:i4], segk_ref[...], _vec(vec_ref, VEC_K, KV_WIDTH), HEAD_DIM)
    v = z[:, i4:]
    return u, va, q, k, v


def _chunk_mlp(va_bf, wsp_ref, bsp_ref, tile):
    pair = wsp_ref.shape[1]
    n_pairs = tile // pair
    outs = []
    for g in range(A_GROUPS):
        cols = [va_bf[p * pair:(p + 1) * pair, g * A_CH:(g + 1) * A_CH] for p in range(n_pairs)]
        rhs = cols[0] if n_pairs == 1 else jnp.concatenate(cols, axis=1)
        o = _dot(wsp_ref[g], rhs)
        rows = [o[:, p * A_CH:(p + 1) * A_CH] for p in range(n_pairs)]
        outs.append(rows[0] if n_pairs == 1 else jnp.concatenate(rows, axis=0))
    mixed = jnp.concatenate(outs, axis=1)
    bias = bsp_ref[...]
    reps = tile // bias.shape[0]
    return mixed + (bias if reps == 1 else jnp.concatenate([bias] * reps, axis=0))


def _lane_lo():
    return lax.broadcasted_iota(jnp.int32, (1, LANES), 1) < HEAD_DIM


def _swap_halves(x):
    return pltpu.roll(x, HEAD_DIM, axis=1)


def _head_lhs(q_groups, q_groups_swapped, head, lo):
    j, half = divmod(head, 2)
    kv = head // Q_PER_KV
    src = q_groups[j] if half == kv else q_groups_swapped[j]
    return jnp.where(lo if kv == 0 else jnp.logical_not(lo), src, 0.0)


def _merge_heads(o_heads, lo):
    groups = []
    for j in range(N_HEADS // 2):
        kv = (2 * j) // Q_PER_KV
        if kv == 0:
            groups.append(jnp.where(lo, o_heads[2 * j], _swap_halves(o_heads[2 * j + 1])))
        else:
            groups.append(jnp.where(lo, _swap_halves(o_heads[2 * j]), o_heads[2 * j + 1]))
    return jnp.concatenate(groups, axis=1)


def _dup_halves(x, lo):
    xs = _swap_halves(x)
    return jnp.where(lo, x, xs).astype(BF16), jnp.where(lo, xs, x).astype(BF16)


def _swa_prompt(q, k, v, kprev_ref, vprev_ref, bias_ref, sinks_ref, is_first, tile):
    lo = _lane_lo()
    hi = jnp.logical_not(lo)
    col = lax.broadcasted_iota(jnp.int32, (WINDOW, 2 * WINDOW), 1)
    first_mask = jnp.where(col < WINDOW, jnp.where(is_first, NEG_INF, 0.0).astype(F32), 0.0)
    kd = _dup_halves(k, lo)
    vd = _dup_halves(v, lo)
    nblk = tile // WINDOW
    rows_out = []
    for i in range(nblk):
        r0, r1 = i * WINDOW, (i + 1) * WINDOW
        qg = [q[r0:r1, j * LANES:(j + 1) * LANES] for j in range(N_HEADS // 2)]
        o_heads = []
        for kh in range(N_KV):
            if i == 0:
                kp, vp = kprev_ref[kh], vprev_ref[kh]
            else:
                kp, vp = kd[kh][r0 - WINDOW:r0], vd[kh][r0 - WINDOW:r0]
            kb = jnp.concatenate([kp, kd[kh][r0:r1]], axis=0)
            vb = jnp.concatenate([vp, vd[kh][r0:r1]], axis=0)
            heads = [kh * Q_PER_KV + g for g in range(Q_PER_KV)]
            lhs = jnp.concatenate(
                [jnp.where(lo if h % 2 == 0 else hi, qg[h // 2], 0.0) for h in heads], axis=0).astype(BF16)
            s = _dot_nt(lhs, kb) + bias_ref[kh]
            ps, linvs = [], []
            for g in range(Q_PER_KV):
                sg = s[g * WINDOW:(g + 1) * WINDOW]
                if i == 0:
                    sg = sg + first_mask
                sink = sinks_ref[heads[g]] * LOG2E
                mg = jnp.maximum(jnp.max(sg, axis=-1, keepdims=True), sink)
                pg = jnp.exp2(sg - mg)
                lg = jnp.sum(pg, axis=-1, keepdims=True) + jnp.exp2(sink - mg)
                ps.append(pg.astype(BF16))
                linvs.append(1.0 / lg)
            o = _dot(jnp.concatenate(ps, axis=0), vb)
            o_heads += [o[g * WINDOW:(g + 1) * WINDOW] * linvs[g] for g in range(Q_PER_KV)]
        rows_out.append(jnp.concatenate(
            [jnp.where(lo, o_heads[2 * j], o_heads[2 * j + 1]) for j in range(N_HEADS // 2)], axis=1))
    for kh in range(N_KV):
        kprev_ref[kh] = kd[kh][tile - WINDOW:tile]
        vprev_ref[kh] = vd[kh][tile - WINDOW:tile]
    return jnp.concatenate(rows_out, axis=0)


def _shift_window(cache_ref, new_t, out_ref, nb, dec):
    lane = lax.broadcasted_iota(jnp.int32, (1, WINDOW), 1)
    keep = lane < WINDOW - dec
    per_tile = LANES // dec
    for b in range(nb):
        src = new_t[:, (b // per_tile) * LANES:(b // per_tile + 1) * LANES]
        new_cols = pltpu.roll(src, (WINDOW - dec - (b % per_tile) * dec) % LANES, axis=1)
        old = pltpu.roll(cache_ref[b], WINDOW - dec, axis=1)
        out_ref[b] = jnp.where(keep, old, new_cols)


def _swa_sample(q, k, v, ck_ref, cv_ref, wk_ref, wv_ref, bias_c_ref, bias_n_ref, sinks_ref, nb, dec):
    lo = _lane_lo()
    qg = [q[:, j * LANES:(j + 1) * LANES] for j in range(N_HEADS // 2)]
    qgs = [_swap_halves(x) for x in qg]
    qb = jnp.concatenate(
        [_head_lhs(qg, qgs, h, lo).reshape(nb, dec, LANES) for h in range(N_HEADS)], axis=1)
    rows = N_HEADS * dec
    qb_bf = qb.astype(BF16)
    k_bf = k.astype(BF16)
    v_bf = v.astype(BF16)
    ck = ck_ref[...].astype(BF16)
    cv = cv_ref[...].astype(BF16)
    sc = jnp.einsum('bqc,bcp->bqp', qb_bf, ck, preferred_element_type=F32) + bias_c_ref[...][None]
    sn = _dot_nt(qb_bf.reshape(nb * rows, LANES), k_bf).reshape(nb, rows, nb * dec) + bias_n_ref[...]
    sink = jnp.concatenate(
        [jnp.full((1, dec, 1), sinks_ref[h] * LOG2E, F32) for h in range(N_HEADS)], axis=1)
    m = jnp.maximum(jnp.max(sc, axis=-1, keepdims=True), jnp.max(sn, axis=-1, keepdims=True))
    m = jnp.maximum(m, sink)
    pc = jnp.exp2(sc - m)
    pn = jnp.exp2(sn - m)
    l = jnp.sum(pc, axis=-1, keepdims=True) + jnp.sum(pn, axis=-1, keepdims=True) + jnp.exp2(sink - m)
    oc = jnp.einsum('bqp,bcp->bqc', pc.astype(BF16), cv, preferred_element_type=F32)
    on = _dot(pn.reshape(nb * rows, nb * dec).astype(BF16), v_bf).reshape(nb, rows, LANES)
    o = (oc + on) * (1.0 / l)
    o_heads = [o[:, h * dec:(h + 1) * dec, :].reshape(nb * dec, LANES) for h in range(N_HEADS)]
    _shift_window(ck_ref, k.T, wk_ref, nb, dec)
    _shift_window(cv_ref, v.T, wv_ref, nb, dec)
    return _merge_heads(o_heads, lo)


ROUTER_ROWS = 32


def _route_rows(hn_bf, wrt_ref, brt_ref, tile):
    logits = _dot_nt(wrt_ref[...], hn_bf) + jnp.tile(brt_ref[...], (1, tile // LANES))
    row = lax.broadcasted_iota(jnp.int32, logits.shape, 0)
    big = jnp.int32(ROUTER_ROWS)
    lc = jnp.where(row < N_EXPERT_GROUPS, logits, NEG_INF)
    mx = jnp.max(lc, axis=0, keepdims=True)
    g_idx = jnp.min(jnp.where(lc == mx, row, big), axis=0, keepdims=True)
    p_g = 1.0 / jnp.sum(jnp.exp(lc - mx), axis=0, keepdims=True)
    e_row = row - N_EXPERT_GROUPS
    in_group = (e_row >= 0) & (e_row < N_EXPERTS) & ((e_row >> 2) == g_idx)
    lf = jnp.where(in_group, logits, NEG_INF)
    v1 = jnp.max(lf, axis=0, keepdims=True)
    i1 = jnp.min(jnp.where(lf == v1, row, big), axis=0, keepdims=True)
    lf2 = jnp.where(row == i1, NEG_INF, lf)
    v2 = jnp.max(lf2, axis=0, keepdims=True)
    i2 = jnp.min(jnp.where(lf2 == v2, row, big), axis=0, keepdims=True)
    e = jnp.exp(v2 - v1)
    w1 = p_g / (1.0 + e)
    w2 = w1 * e
    return g_idx, i1, i2, w1, w2


def _silu(x):
    return x * (1.0 / (1.0 + jnp.exp2(x * (-LOG2E))))


GROUP_LANE = EXPERTS_PER_GROUP
LOW_SHIFT = 8
SORT_BLOCK = 128


def _moe_grouped(h, vec_ref, wrt_ref, brt_ref, wg_ref, wu_ref, wd_ref, ustrict_ref, xs_ref, es_ref, ys_ref, tile):
    hn_bf = _rms(h, _vec(vec_ref, VEC_FFN, D_MODEL)).astype(BF16)
    g_idx, i1, i2, w1, w2 = _route_rows(hn_bf, wrt_ref, brt_ref, tile)

    row16 = lax.broadcasted_iota(jnp.int32, (16, tile), 0)
    onehot = row16 == g_idx
    gmat = jnp.where(onehot, 1.0, 0.0)
    before = _dot(gmat.astype(BF16), ustrict_ref[...])
    rank = jnp.sum(jnp.where(onehot, before, 0.0), axis=0, keepdims=True)
    offs = [jnp.float32(0.0)]
    for g in range(N_EXPERT_GROUPS - 1):
        offs.append(offs[-1] + jnp.sum(jnp.where(g_idx == g, 1.0, 0.0)))
    offv = jnp.zeros_like(rank)
    for g in range(1, N_EXPERT_GROUPS):
        offv = offv + jnp.where(g_idx == g, offs[g], 0.0)
    pos = (rank + offv).astype(jnp.int32)
    dest = lax.broadcasted_iota(jnp.int32, (tile, tile), 0)
    perm = jnp.where(dest == pos, 1.0, 0.0).astype(BF16)

    row8 = lax.broadcasted_iota(jnp.int32, (8, tile), 0)
    j1 = (i1 - N_EXPERT_GROUPS) & (EXPERTS_PER_GROUP - 1)
    j2 = (i2 - N_EXPERT_GROUPS) & (EXPERTS_PER_GROUP - 1)
    gates = jnp.where(row8 == j1, w1, jnp.where(row8 == j2, w2, 0.0))
    gates_hi = gates.astype(BF16).astype(F32)
    gates_lo = (gates - gates_hi).astype(BF16).astype(F32)
    top = gates_hi + jnp.where(row8 == GROUP_LANE, g_idx.astype(F32), 0.0)
    side = jnp.concatenate([top, gates_lo, jnp.zeros((LANES - 2 * LOW_SHIFT, tile), F32)], axis=0).T
    x_ext = jnp.concatenate([hn_bf, side.astype(BF16)], axis=1)
    srt = _dot(perm, x_ext)
    xs_ref[...] = srt[:, :D_MODEL].astype(BF16)
    es = srt[:, D_MODEL:]
    es_ref[...] = es + pltpu.roll(es, LANES - LOW_SHIFT, axis=1)

    def group_of(row):
        g = jnp.int32(0)
        for k in range(1, N_EXPERT_GROUPS):
            g = g + (jnp.float32(row) >= offs[k]).astype(jnp.int32)
        return g

    def group_pass(rows, g, keep, accumulate):
        xb = xs_ref[rows, :]
        eb = es_ref[rows, :]
        mine = jnp.where(eb[:, GROUP_LANE:GROUP_LANE + 1] == jnp.asarray(g).astype(F32), keep, 0.0)
        parts = []
        for j in range(EXPERTS_PER_GROUP):
            e = g * EXPERTS_PER_GROUP + j
            gate = _dot(xb, wg_ref[e])
            up = _dot(xb, wu_ref[e])
            parts.append((_silu(gate) * up * (eb[:, j:j + 1] * mine)).astype(BF16))
        wd = wd_ref[pl.ds(g * EXPERTS_PER_GROUP, EXPERTS_PER_GROUP)]
        out = _dot(jnp.concatenate(parts, axis=1), wd.reshape(EXPERTS_PER_GROUP * D_EXPERT, D_MODEL))
        if accumulate:
            ys_ref[rows, :] += out
        else:
            ys_ref[rows, :] = out

    nblk = tile // SORT_BLOCK
    for b in range(nblk):
        group_pass(pl.ds(b * SORT_BLOCK, SORT_BLOCK), group_of(b * SORT_BLOCK), 1.0, False)
    for g in range(1, N_EXPERT_GROUPS):
        start = offs[g].astype(jnp.int32)
        blk = jnp.minimum(lax.div(start, jnp.int32(SORT_BLOCK)), nblk - 1)
        inside = jnp.where(lax.rem(start, jnp.int32(SORT_BLOCK)) != 0, 1.0, 0.0)
        group_pass(pl.ds(pl.multiple_of(blk * SORT_BLOCK, SORT_BLOCK), SORT_BLOCK), g, inside, True)
    return lax.dot_general(perm, ys_ref[...].astype(BF16), (((0,), (0,)), ((), ())),
                           preferred_element_type=F32)


def _merge(x, a_out, b_out, vec_ref, w_out_ref):
    g_out = _vec(vec_ref, VEC_OUT, D_MODEL)
    mix = jnp.concatenate(
        [_rms(a_out, g_out[:, :A_WIDTH]), _rms(b_out, g_out[:, A_WIDTH:])], axis=1).astype(BF16)
    return x + _dot(mix, w_out_ref[...])


def _stage_expert_weights(wg_hbm, wu_hbm, wd_hbm, wg_s, wu_s, wd_s, st_a, st_d, sems):
    def gate_cp(e):
        return pltpu.make_async_copy(wg_hbm.at[e], st_a.at[0], sems.at[0])

    def up_cp(e):
        return pltpu.make_async_copy(wu_hbm.at[e], st_a.at[1], sems.at[1])

    def down_cp(e, slot):
        return pltpu.make_async_copy(wd_hbm.at[e], st_d.at[slot], sems.at[2 + slot])

    gate_cp(0).start()
    up_cp(0).start()
    down_cp(0, 0).start()

    def pair(p, carry):
        for slot in (0, 1):
            e = 2 * p + slot
            nxt = e + 1
            more = nxt < N_EXPERTS

            @pl.when(more)
            def _():
                down_cp(nxt, 1 - slot).start()

            gate_cp(e).wait()
            wg_s[e] = st_a[0].astype(BF16)

            @pl.when(more)
            def _():
                gate_cp(nxt).start()

            up_cp(e).wait()
            wu_s[e] = st_a[1].astype(BF16)

            @pl.when(more)
            def _():
                up_cp(nxt).start()

            down_cp(e, slot).wait()
            wd_s[e] = st_d[slot].astype(BF16)
        return carry

    lax.fori_loop(0, N_EXPERTS // 2, pair, 0)


def _stage_dense_weights(w_in_hbm, w_out_hbm, w_in_s, w_out_s, st_a, st_d, sems):
    cols = st_a.shape[2]
    rows = st_d.shape[1]
    n_in = w_in_s.shape[1] // cols
    n_out = w_out_s.shape[0] // rows

    def in_cp(c):
        return pltpu.make_async_copy(w_in_hbm.at[:, pl.ds(c * cols, cols)], st_a.at[c % 2], sems.at[c % 2])

    def out_cp(c):
        return pltpu.make_async_copy(w_out_hbm.at[pl.ds(c * rows, rows), :], st_d.at[c % 2], sems.at[2 + c % 2])

    for c in range(min(2, n_in)):
        in_cp(c).start()
    for c in range(min(2, n_out)):
        out_cp(c).start()
    for c in range(max(n_in, n_out)):
        if c < n_in:
            in_cp(c).wait()
            w_in_s[:, c * cols:(c + 1) * cols] = st_a[c % 2].astype(BF16)
            if c + 2 < n_in:
                in_cp(c + 2).start()
        if c < n_out:
            out_cp(c).wait()
            w_out_s[c * rows:(c + 1) * rows, :] = st_d[c % 2].astype(BF16)
            if c + 2 < n_out:
                out_cp(c + 2).start()


N_STAGE_SEMS = 4


def _bf16_weight_copies(srcs, dsts, sems):
    return tuple(pltpu.make_async_copy(src, dst, sems.at[N_STAGE_SEMS + i])
                 for i, (src, dst) in enumerate(zip(srcs, dsts)))


def _prompt_kernel(x_ref, vec_ref, segq_ref, segk_ref, wsp_ref, bsp_ref, bias_ref, sinks_ref,
                   wrt_ref, brt_ref, ustrict_ref, w_in_hbm, w_out_hbm, wg_hbm, wu_hbm, wd_hbm,
                   y_ref, kwin_ref, vwin_ref, w_in_o, w_out_o, wg_o, wu_o, wd_o,
                   kprev_ref, vprev_ref, xs_ref, es_ref, ys_ref, w_in_ref, w_out_ref, wg_ref, wu_ref, wd_ref,
                   st_a, st_d, sems, *, tile, tiles_per_seq):
    step = pl.program_id(0)
    is_first = (step % tiles_per_seq) == 0

    kept = (w_in_ref, w_out_ref, wg_ref, wu_ref, wd_ref)
    kept_out = (w_in_o, w_out_o, wg_o, wu_o, wd_o)

    @pl.when(step == 0)
    def _():
        _stage_dense_weights(w_in_hbm, w_out_hbm, w_in_ref, w_out_ref, st_a, st_d, sems)
        _stage_expert_weights(wg_hbm, wu_hbm, wd_hbm, wg_ref, wu_ref, wd_ref, st_a, st_d, sems)
        for cp in _bf16_weight_copies(kept, kept_out, sems):
            cp.start(priority=1)

    @pl.when(step == pl.num_programs(0) - 1)
    def _():
        for cp in _bf16_weight_copies(kept, kept_out, sems):
            cp.wait()

    @pl.when(is_first)
    def _():
        kprev_ref[...] = jnp.zeros_like(kprev_ref)
        vprev_ref[...] = jnp.zeros_like(vprev_ref)

    x = x_ref[...]
    u, va, q, k, v = _project(x, vec_ref, w_in_ref, segq_ref, segk_ref)
    kwin_ref[...] = k[tile - WINDOW:tile].T
    vwin_ref[...] = v[tile - WINDOW:tile].T
    a_out = u * _chunk_mlp(va.astype(BF16), wsp_ref, bsp_ref, tile)
    b_out = _swa_prompt(q * Q_SCALE, k, v, kprev_ref, vprev_ref, bias_ref, sinks_ref, is_first, tile)
    h = _merge(x, a_out, b_out, vec_ref, w_out_ref)
    y_ref[...] = h + _moe_grouped(h, vec_ref, wrt_ref, brt_ref, wg_ref, wu_ref, wd_ref, ustrict_ref,
                                  xs_ref, es_ref, ys_ref, tile)


def _sample_kernel(x_ref, ck_ref, cv_ref, vec_ref, w_in_ref, segq_ref, segk_ref, wsp_ref, bsp_ref,
                   bias_c_ref, bias_n_ref, sinks_ref, w_out_ref, wrt_ref, brt_ref, ustrict_ref,
                   wg_ref, wu_ref, wd_ref, y_ref, wk_ref, wv_ref, va_ref, xs_ref, es_ref, ys_ref,
                   *, tile, sub, nb, dec):
    hs = []
    for part in range(tile // sub):
        r0 = part * sub
        seqs = pl.ds(part * nb, nb)
        x = x_ref[r0:r0 + sub, :]
        u, va, q, k, v = _project(x, vec_ref, w_in_ref, segq_ref, segk_ref)
        for g in range(A_GROUPS):
            va_ref[pl.ds(r0 * A_GROUPS + g, sub, stride=A_GROUPS), :] = va[:, g * A_CH:(g + 1) * A_CH]
        a_out = u * _chunk_mlp(va.astype(BF16), wsp_ref, bsp_ref, sub)
        b_out = _swa_sample(q * Q_SCALE, k, v, ck_ref.at[seqs], cv_ref.at[seqs], wk_ref.at[seqs], wv_ref.at[seqs],
                            bias_c_ref, bias_n_ref, sinks_ref, nb, dec)
        hs.append(_merge(x, a_out, b_out, vec_ref, w_out_ref))
    h = jnp.concatenate(hs, axis=0)
    y_ref[...] = h + _moe_grouped(h, vec_ref, wrt_ref, brt_ref, wg_ref, wu_ref, wd_ref, ustrict_ref,
                                  xs_ref, es_ref, ys_ref, tile)


def _const_spec(shape):
    nd = len(shape)
    return pl.BlockSpec(shape, lambda i: (0,) * nd, pipeline_mode=pl.Buffered(1))


def _row_spec(tile, width):
    return pl.BlockSpec((tile, width), lambda i: (i, 0))


def _smem_spec():
    return pl.BlockSpec(memory_space=pltpu.SMEM)


def _layer_weights(l, g_attn_norm, g_v_a, g_q, g_k, g_out_a, g_out_b, g_ffn_norm,
                   w_coarse, b_coarse, w_fine, b_fine):
    row_pad = ROUTER_ROWS - N_EXPERT_GROUPS - N_EXPERTS
    wrt = jnp.concatenate([w_coarse[l].T, w_fine[l].T, jnp.zeros((row_pad, D_MODEL), F32)], axis=0)
    brt = jnp.concatenate([b_coarse[l], b_fine[l], jnp.zeros((row_pad,), F32)])
    vec = jnp.concatenate([
        g_attn_norm[l], g_v_a[l].reshape(A_WIDTH), jnp.tile(g_q[l], N_HEADS), jnp.tile(g_k[l], N_KV),
        g_out_a[l], g_out_b[l], g_ffn_norm[l]])
    return dict(
        vec=vec.reshape(1, VEC_LANES),
        segq=jnp.asarray(_seg_ones(B_WIDTH, HEAD_DIM), BF16),
        segk=jnp.asarray(_seg_ones(KV_WIDTH, HEAD_DIM), BF16),
        wrt=wrt.astype(BF16),
        brt=jnp.broadcast_to(brt[:, None], (ROUTER_ROWS, LANES)),
    )


def _spatial_tables(ws, bs, period, tile):
    pair = min(PAIR, tile)
    reps = pair // period
    wbd = (jnp.tile(ws[:, :period, :period], (1, reps, reps)) * _causal_block_mask(period, pair)).astype(BF16)
    bsp = jnp.repeat(bs[:, :period].T, A_CH, axis=1)
    return wbd, bsp


def _upper_triangle(n):
    return jnp.asarray(np.triu(np.ones((n, n), np.float32), 1), BF16)


def _run_prompt(x2d, lw, wbd, bsp, sinks, seq_len, w_in, w_out, w_gate, w_up, w_down):
    assert w_in.shape[0] == w_gate.shape[1] and w_in.shape[1] % w_gate.shape[2] == 0
    assert w_out.shape[1] == w_down.shape[2] and w_out.shape[0] % w_down.shape[1] == 0
    n_tok = x2d.shape[0]
    tile = PROMPT_TILE
    tiles_per_seq = seq_len // tile
    n_seq = n_tok // seq_len
    bias = jnp.asarray(_prompt_bias())
    tail = [lw["wrt"], lw["brt"], _upper_triangle(tile)]
    mats = [w_in, w_out, w_gate, w_up, w_down]
    consts = [lw["vec"], lw["segq"], lw["segk"], wbd, bsp, bias]
    any_spec = pl.BlockSpec(memory_space=pl.ANY)
    in_specs = ([_row_spec(tile, D_MODEL)] + [_const_spec(a.shape) for a in consts] + [_smem_spec()]
                + [_const_spec(a.shape) for a in tail] + [any_spec] * len(mats))
    win_spec = pl.BlockSpec((KV_WIDTH, WINDOW), lambda i: (i // tiles_per_seq, 0))
    out_shape = (jax.ShapeDtypeStruct((n_tok, D_MODEL), F32),
                 jax.ShapeDtypeStruct((n_seq * KV_WIDTH, WINDOW), F32),
                 jax.ShapeDtypeStruct((n_seq * KV_WIDTH, WINDOW), F32),
                 *[jax.ShapeDtypeStruct(w.shape, BF16) for w in mats])
    out_specs = (_row_spec(tile, D_MODEL), win_spec, win_spec) + (any_spec,) * len(mats)
    kern = functools.partial(_prompt_kernel, tile=tile, tiles_per_seq=tiles_per_seq)
    return pl.pallas_call(
        kern,
        out_shape=out_shape,
        grid=(n_tok // tile,),
        in_specs=in_specs,
        out_specs=out_specs,
        scratch_shapes=[pltpu.VMEM((N_KV, WINDOW, KV_WIDTH), BF16), pltpu.VMEM((N_KV, WINDOW, KV_WIDTH), BF16),
                        pltpu.VMEM((tile, D_MODEL), BF16), pltpu.VMEM((tile, LANES), F32),
                        pltpu.VMEM((tile, D_MODEL), F32),
                        *[pltpu.VMEM(w.shape, BF16) for w in mats],
                        pltpu.VMEM((2,) + w_gate.shape[1:], F32), pltpu.VMEM((2,) + w_down.shape[1:], F32),
                        pltpu.SemaphoreType.DMA((N_STAGE_SEMS + len(mats),))],
        compiler_params=pltpu.CompilerParams(dimension_semantics=("arbitrary",),
                                             vmem_limit_bytes=VMEM_LIMIT_BYTES),
        name="layer_prompt",
    )(x2d, *consts, sinks, *tail, *mats)


def _run_sample(x2d, ck_t, cv_t, lw, wbd, bsp, sinks, dec, w_in_bf, w_out_bf, wg_bf, wu_bf, wd_bf):
    n_tok = x2d.shape[0]
    n_seq = ck_t.shape[0]
    tile, sub = SAMPLE_TILE, SAMPLE_SUB
    nb = sub // dec
    bc, bn = _sample_bias(nb, dec)
    head = [lw["vec"], w_in_bf, lw["segq"], lw["segk"]]
    tail = [w_out_bf, lw["wrt"], lw["brt"], _upper_triangle(tile), wg_bf, wu_bf, wd_bf]
    consts = head + [wbd, bsp, jnp.asarray(bc), jnp.asarray(bn)]
    cache_spec = pl.BlockSpec((tile // dec, KV_WIDTH, WINDOW), lambda i: (i, 0, 0))
    in_specs = ([_row_spec(tile, D_MODEL), cache_spec, cache_spec] + [_const_spec(a.shape) for a in consts]
                + [_smem_spec()] + [_const_spec(a.shape) for a in tail])
    out_shape = (jax.ShapeDtypeStruct((n_tok, D_MODEL), F32),
                 jax.ShapeDtypeStruct((n_seq, KV_WIDTH, WINDOW), F32),
                 jax.ShapeDtypeStruct((n_seq, KV_WIDTH, WINDOW), F32),
                 jax.ShapeDtypeStruct((n_tok * A_GROUPS, A_CH), F32))
    out_specs = (_row_spec(tile, D_MODEL), cache_spec, cache_spec, _row_spec(tile * A_GROUPS, A_CH))
    kern = functools.partial(_sample_kernel, tile=tile, sub=sub, nb=nb, dec=dec)
    return pl.pallas_call(
        kern,
        out_shape=out_shape,
        grid=(n_tok // tile,),
        in_specs=in_specs,
        out_specs=out_specs,
        scratch_shapes=[pltpu.VMEM((tile, D_MODEL), BF16), pltpu.VMEM((tile, LANES), F32),
                        pltpu.VMEM((tile, D_MODEL), F32)],
        compiler_params=pltpu.CompilerParams(dimension_semantics=("arbitrary",),
                                             vmem_limit_bytes=VMEM_LIMIT_BYTES),
        name="layer_sample",
    )(x2d, ck_t, cv_t, *consts, sinks, *tail)


def _positions_last(c):
    b, w = c.shape[:2]
    return jnp.transpose(c, (0, 2, 3, 1)).reshape(b, KV_WIDTH, w)


def _positions_first(c_t):
    b, _, w = c_t.shape
    return jnp.transpose(c_t.reshape(b, N_KV, HEAD_DIM, w), (0, 3, 1, 2))


def kernel(x_prompt, x_sample, cache_k, cache_v, g_attn_norm, w_in, g_v_a, w_spatial, b_spatial, g_q, g_k, attn_sinks, g_out_a, g_out_b, w_out, g_ffn_norm, w_coarse, b_coarse, w_fine, b_fine, w_gate, w_up, w_down):
    depth = w_in.shape[0]
    batch, seq, _ = x_prompt.shape
    dbatch, dec, _ = x_sample.shape
    win = cache_k.shape[2]
    assert win == WINDOW and seq % PROMPT_TILE == 0 and (dbatch * dec) % SAMPLE_TILE == 0
    assert PAIR % dec == 0 and SAMPLE_SUB % dec == 0 and LANES % dec == 0

    hp = x_prompt.reshape(batch * seq, D_MODEL)
    hs = x_sample.reshape(dbatch * dec, D_MODEL)
    kp_l, vp_l, ks_l, vs_l, cv_l = [], [], [], [], []
    for l in range(depth):
        lw = _layer_weights(l, g_attn_norm, g_v_a, g_q, g_k, g_out_a, g_out_b, g_ffn_norm,
                            w_coarse, b_coarse, w_fine, b_fine)
        sinks = attn_sinks[l].astype(F32)
        wbd_p, bsp_p = _spatial_tables(w_spatial[l], b_spatial[l], CHUNK, PROMPT_TILE)
        wbd_s, bsp_s = _spatial_tables(w_spatial[l], b_spatial[l], dec, SAMPLE_SUB)

        hp, kp, vp, *weights_bf = _run_prompt(hp, lw, wbd_p, bsp_p, sinks, seq,
                                              w_in[l], w_out[l], w_gate[l], w_up[l], w_down[l])
        kp_l.append(_positions_first(kp.reshape(batch, KV_WIDTH, WINDOW)))
        vp_l.append(_positions_first(vp.reshape(batch, KV_WIDTH, WINDOW)))

        hs, wk, wv, va = _run_sample(hs, _positions_last(cache_k[l]), _positions_last(cache_v[l]),
                                     lw, wbd_s, bsp_s, sinks, dec, *weights_bf)
        ks_l.append(_positions_first(wk))
        vs_l.append(_positions_first(wv))
        cv_l.append(va.reshape(dbatch, dec, A_GROUPS, A_CH))

    return (hp.reshape(batch, seq, D_MODEL), hs.reshape(dbatch, dec, D_MODEL),
            jnp.stack(kp_l, axis=0), jnp.stack(vp_l, axis=0),
            jnp.stack(ks_l, axis=0), jnp.stack(vs_l, axis=0), jnp.stack(cv_l, axis=0))
```

```python
import functools

import numpy as np
import jax
import jax.numpy as jnp
from jax import lax
from jax.experimental import pallas as pl
from jax.experimental.pallas import tpu as pltpu

D_MODEL = 1024
CHUNK = 128
A_GROUPS = 4
A_WIDTH = 512
A_CH = 128
N_HEADS = 8
N_KV = 2
Q_PER_KV = 4
HEAD_DIM = 64
B_WIDTH = 512
KV_WIDTH = 128
WINDOW = 128
IN_COLS = 2 * A_WIDTH + B_WIDTH + 2 * KV_WIDTH
N_EXPERT_GROUPS = 4
EXPERTS_PER_GROUP = 4
N_EXPERTS = 16
D_EXPERT = 256
EPS = 1e-6

LANES = 128
PAIR = 2 * CHUNK
PROMPT_TILE = 512
SAMPLE_TILE = 256
SAMPLE_SUB = 128
VMEM_LIMIT_BYTES = 60 * 1024 * 1024

F32 = jnp.float32
BF16 = jnp.bfloat16
NEG_INF = float("-inf")
LOG2E = 1.4426950408889634
Q_SCALE = (HEAD_DIM ** -0.5) * LOG2E


def _slopes():
    return np.array([2.0 ** (-8.0 * (h + 1) / N_HEADS) for h in range(N_HEADS)], np.float64)


def _prompt_bias():
    t = np.arange(WINDOW)[:, None]
    s = np.arange(2 * WINDOW)[None, :]
    dist = t + WINDOW - s
    valid = (dist >= 0) & (dist < WINDOW)
    sl = _slopes()
    out = np.full((N_KV, Q_PER_KV * WINDOW, 2 * WINDOW), -np.inf, np.float32)
    for kh in range(N_KV):
        for g in range(Q_PER_KV):
            b = np.where(valid, -sl[kh * Q_PER_KV + g] * LOG2E * dist, -np.inf)
            out[kh, g * WINDOW:(g + 1) * WINDOW] = b
    return out


def _sample_bias(nb, dec):
    sl = _slopes()
    t = np.arange(dec)[:, None]
    j = np.arange(WINDOW)[None, :]
    dist_c = t + WINDOW - j
    valid_c = (dist_c >= 0) & (dist_c < WINDOW)
    bc = np.full((N_HEADS * dec, WINDOW), -np.inf, np.float32)
    tp = np.arange(dec)[None, :]
    dist_n = t - tp
    valid_n = dist_n >= 0
    bn = np.full((nb, N_HEADS * dec, nb * dec), -np.inf, np.float32)
    for h in range(N_HEADS):
        bc[h * dec:(h + 1) * dec] = np.where(valid_c, -sl[h] * LOG2E * dist_c, -np.inf)
        blk = np.where(valid_n, -sl[h] * LOG2E * dist_n, -np.inf)
        for b in range(nb):
            bn[b, h * dec:(h + 1) * dec, b * dec:(b + 1) * dec] = blk
    return bc, bn


def _seg_ones(width, seg):
    i = np.arange(width)
    return (i[:, None] // seg == i[None, :] // seg).astype(np.float32)


def _causal_block_mask(period, size):
    i = np.arange(size)
    same = i[:, None] // period == i[None, :] // period
    return (same & (i[None, :] % period <= i[:, None] % period)).astype(np.float32)


def _dot(a, b):
    return jnp.dot(a, b, preferred_element_type=F32)


def _dot_nt(a, b):
    return lax.dot_general(a, b, (((1,), (1,)), ((), ())), preferred_element_type=F32)


def _rms(x, g):
    ms = jnp.mean(x * x, axis=-1, keepdims=True)
    return (x * lax.rsqrt(ms + EPS)) * g


def _seg_rms(x, ones_bf, g, seg):
    ss = _dot((x * x).astype(BF16), ones_bf)
    return (x * lax.rsqrt(ss * (1.0 / seg) + EPS)) * g


GELU_K0 = -2.0 * 0.7978845608028654 * LOG2E
GELU_K1 = GELU_K0 * 0.044715


def _gelu_tanh(x):
    t = (x * x) * GELU_K1 + GELU_K0
    return x * (1.0 / (1.0 + jnp.exp2(x * t)))


VEC_ATTN = 0
VEC_VA = VEC_ATTN + D_MODEL
VEC_Q = VEC_VA + A_WIDTH
VEC_K = VEC_Q + B_WIDTH
VEC_OUT = VEC_K + KV_WIDTH
VEC_FFN = VEC_OUT + D_MODEL
VEC_LANES = VEC_FFN + D_MODEL


def _vec(vec_ref, start, width):
    return vec_ref[:, start:start + width]


def _project(x, vec_ref, w_in_ref, segq_ref, segk_ref):
    xn = _rms(x, _vec(vec_ref, VEC_ATTN, D_MODEL)).astype(BF16)
    z = _dot(xn, w_in_ref[...])
    i1, i2, i3, i4 = A_WIDTH, 2 * A_WIDTH, 2 * A_WIDTH + B_WIDTH, 2 * A_WIDTH + B_WIDTH + KV_WIDTH
    u = _gelu_tanh(z[:, :i1])
    va_pre = _gelu_tanh(z[:, i1:i2])
    g_va = _vec(vec_ref, VEC_VA, A_WIDTH)
    va = jnp.concatenate(
        [_rms(va_pre[:, g * A_CH:(g + 1) * A_CH], g_va[:, g * A_CH:(g + 1) * A_CH]) for g in range(A_GROUPS)],
        axis=1)
    q = _seg_rms(z[:, i2:i3], segq_ref[...], _vec(vec_ref, VEC_Q, B_WIDTH), HEAD_DIM)
    k = _seg_rms(z[:, i3:i4], segk_ref[...], _vec(vec_ref, VEC_K, KV_WIDTH), HEAD_DIM)
    v = z[:, i4:]
    return u, va, q, k, v


def _chunk_mlp(va_bf, wsp_ref, bsp_ref, tile):
    pair = wsp_ref.shape[1]
    n_pairs = tile // pair
    outs = []
    for g in range(A_GROUPS):
        cols = [va_bf[p * pair:(p + 1) * pair, g * A_CH:(g + 1) * A_CH] for p in range(n_pairs)]
        rhs = cols[0] if n_pairs == 1 else jnp.concatenate(cols, axis=1)
        o = _dot(wsp_ref[g], rhs)
        rows = [o[:, p * A_CH:(p + 1) * A_CH] for p in range(n_pairs)]
        outs.append(rows[0] if n_pairs == 1 else jnp.concatenate(rows, axis=0))
    mixed = jnp.concatenate(outs, axis=1)
    bias = bsp_ref[...]
    reps = tile // bias.shape[0]
    return mixed + (bias if reps == 1 else jnp.concatenate([bias] * reps, axis=0))


def _lane_lo():
    return lax.broadcasted_iota(jnp.int32, (1, LANES), 1) < HEAD_DIM


def _swap_halves(x):
    return pltpu.roll(x, HEAD_DIM, axis=1)


def _head_lhs(q_groups, q_groups_swapped, head, lo):
    j, half = divmod(head, 2)
    kv = head // Q_PER_KV
    src = q_groups[j] if half == kv else q_groups_swapped[j]
    return jnp.where(lo if kv == 0 else jnp.logical_not(lo), src, 0.0)


def _merge_heads(o_heads, lo):
    groups = []
    for j in range(N_HEADS // 2):
        kv = (2 * j) // Q_PER_KV
        if kv == 0:
            groups.append(jnp.where(lo, o_heads[2 * j], _swap_halves(o_heads[2 * j + 1])))
        else:
            groups.append(jnp.where(lo, _swap_halves(o_heads[2 * j]), o_heads[2 * j + 1]))
    return jnp.concatenate(groups, axis=1)


def _dup_halves(x, lo):
    xs = _swap_halves(x)
    return jnp.where(lo, x, xs).astype(BF16), jnp.where(lo, xs, x).astype(BF16)


def _swa_prompt(q, k, v, kprev_ref, vprev_ref, bias_ref, sinks_ref, is_first, tile):
    lo = _lane_lo()
    hi = jnp.logical_not(lo)
    col = lax.broadcasted_iota(jnp.int32, (WINDOW, 2 * WINDOW), 1)
    first_mask = jnp.where(col < WINDOW, jnp.where(is_first, NEG_INF, 0.0).astype(F32), 0.0)
    kd = _dup_halves(k, lo)
    vd = _dup_halves(v, lo)
    nblk = tile // WINDOW
    rows_out = []
    for i in range(nblk):
        r0, r1 = i * WINDOW, (i + 1) * WINDOW
        qg = [q[r0:r1, j * LANES:(j + 1) * LANES] for j in range(N_HEADS // 2)]
        o_heads = []
        for kh in range(N_KV):
            if i == 0:
                kp, vp = kprev_ref[kh], vprev_ref[kh]
            else:
                kp, vp = kd[kh][r0 - WINDOW:r0], vd[kh][r0 - WINDOW:r0]
            kb = jnp.concatenate([kp, kd[kh][r0:r1]], axis=0)
            vb = jnp.concatenate([vp, vd[kh][r0:r1]], axis=0)
            heads = [kh * Q_PER_KV + g for g in range(Q_PER_KV)]
            lhs = jnp.concatenate(
                [jnp.where(lo if h % 2 == 0 else hi, qg[h // 2], 0.0) for h in heads], axis=0).astype(BF16)
            s = _dot_nt(lhs, kb) + bias_ref[kh]
            ps, linvs = [], []
            for g in range(Q_PER_KV):
                sg = s[g * WINDOW:(g + 1) * WINDOW]
                if i == 0:
                    sg = sg + first_mask
                sink = sinks_ref[heads[g]] * LOG2E
                mg = jnp.maximum(jnp.max(sg, axis=-1, keepdims=True), sink)
                pg = jnp.exp2(sg - mg)
                lg = jnp.sum(pg, axis=-1, keepdims=True) + jnp.exp2(sink - mg)
                ps.append(pg.astype(BF16))
                linvs.append(1.0 / lg)
            o = _dot(jnp.concatenate(ps, axis=0), vb)
            o_heads += [o[g * WINDOW:(g + 1) * WINDOW] * linvs[g] for g in range(Q_PER_KV)]
        rows_out.append(jnp.concatenate(
            [jnp.where(lo, o_heads[2 * j], o_heads[2 * j + 1]) for j in range(N_HEADS // 2)], axis=1))
    for kh in range(N_KV):
        kprev_ref[kh] = kd[kh][tile - WINDOW:tile]
        vprev_ref[kh] = vd[kh][tile - WINDOW:tile]
    return jnp.concatenate(rows_out, axis=0)


def _shift_window(cache_ref, new_t, out_ref, nb, dec):
    lane = lax.broadcasted_iota(jnp.int32, (1, WINDOW), 1)
    keep = lane < WINDOW - dec
    per_tile = LANES // dec
    for b in range(nb):
        src = new_t[:, (b // per_tile) * LANES:(b // per_tile + 1) * LANES]
        new_cols = pltpu.roll(src, (WINDOW - dec - (b % per_tile) * dec) % LANES, axis=1)
        old = pltpu.roll(cache_ref[b], WINDOW - dec, axis=1)
        out_ref[b] = jnp.where(keep, old, new_cols)


def _swa_sample(q, k, v, ck_ref, cv_ref, wk_ref, wv_ref, bias_c_ref, bias_n_ref, sinks_ref, nb, dec):
    lo = _lane_lo()
    qg = [q[:, j * LANES:(j + 1) * LANES] for j in range(N_HEADS // 2)]
    qgs = [_swap_halves(x) for x in qg]
    qb = jnp.concatenate(
        [_head_lhs(qg, qgs, h, lo).reshape(nb, dec, LANES) for h in range(N_HEADS)], axis=1)
    rows = N_HEADS * dec
    qb_bf = qb.astype(BF16)
    k_bf = k.astype(BF16)
    v_bf = v.astype(BF16)
    ck = ck_ref[...].astype(BF16)
    cv = cv_ref[...].astype(BF16)
    sc = jnp.einsum('bqc,bcp->bqp', qb_bf, ck, preferred_element_type=F32) + bias_c_ref[...][None]
    sn = _dot_nt(qb_bf.reshape(nb * rows, LANES), k_bf).reshape(nb, rows, nb * dec) + bias_n_ref[...]
    sink = jnp.concatenate(
        [jnp.full((1, dec, 1), sinks_ref[h] * LOG2E, F32) for h in range(N_HEADS)], axis=1)
    m = jnp.maximum(jnp.max(sc, axis=-1, keepdims=True), jnp.max(sn, axis=-1, keepdims=True))
    m = jnp.maximum(m, sink)
    pc = jnp.exp2(sc - m)
    pn = jnp.exp2(sn - m)
    l = jnp.sum(pc, axis=-1, keepdims=True) + jnp.sum(pn, axis=-1, keepdims=True) + jnp.exp2(sink - m)
    oc = jnp.einsum('bqp,bcp->bqc', pc.astype(BF16), cv, preferred_element_type=F32)
    on = _dot(pn.reshape(nb * rows, nb * dec).astype(BF16), v_bf).reshape(nb, rows, LANES)
    o = (oc + on) * (1.0 / l)
    o_heads = [o[:, h * dec:(h + 1) * dec, :].reshape(nb * dec, LANES) for h in range(N_HEADS)]
    _shift_window(ck_ref, k.T, wk_ref, nb, dec)
    _shift_window(cv_ref, v.T, wv_ref, nb, dec)
    return _merge_heads(o_heads, lo)


ROUTER_ROWS = 32


def _route_rows(hn_bf, wrt_ref, brt_ref, tile):
    logits = _dot_nt(wrt_ref[...], hn_bf) + jnp.tile(brt_ref[...], (1, tile // LANES))
    row = lax.broadcasted_iota(jnp.int32, logits.shape, 0)
    big = jnp.int32(ROUTER_ROWS)
    lc = jnp.where(row < N_EXPERT_GROUPS, logits, NEG_INF)
    mx = jnp.max(lc, axis=0, keepdims=True)
    g_idx = jnp.min(jnp.where(lc == mx, row, big), axis=0, keepdims=True)
    p_g = 1.0 / jnp.sum(jnp.exp(lc - mx), axis=0, keepdims=True)
    e_row = row - N_EXPERT_GROUPS
    in_group = (e_row >= 0) & (e_row < N_EXPERTS) & ((e_row >> 2) == g_idx)
    lf = jnp.where(in_group, logits, NEG_INF)
    v1 = jnp.max(lf, axis=0, keepdims=True)
    i1 = jnp.min(jnp.where(lf == v1, row, big), axis=0, keepdims=True)
    lf2 = jnp.where(row == i1, NEG_INF, lf)
    v2 = jnp.max(lf2, axis=0, keepdims=True)
    i2 = jnp.min(jnp.where(lf2 == v2, row, big), axis=0, keepdims=True)
    e = jnp.exp(v2 - v1)
    w1 = p_g / (1.0 + e)
    w2 = w1 * e
    return g_idx, i1, i2, w1, w2


def _silu(x):
    return x * (1.0 / (1.0 + jnp.exp2(x * (-LOG2E))))


GROUP_LANE = EXPERTS_PER_GROUP
LOW_SHIFT = 8
SORT_BLOCK = 128


def _moe_grouped(h, vec_ref, wrt_ref, brt_ref, wg_ref, wu_ref, wd_ref, ustrict_ref, xs_ref, es_ref, ys_ref, tile):
    hn_bf = _rms(h, _vec(vec_ref, VEC_FFN, D_MODEL)).astype(BF16)
    g_idx, i1, i2, w1, w2 = _route_rows(hn_bf, wrt_ref, brt_ref, tile)

    row16 = lax.broadcasted_iota(jnp.int32, (16, tile), 0)
    onehot = row16 == g_idx
    gmat = jnp.where(onehot, 1.0, 0.0)
    before = _dot(gmat.astype(BF16), ustrict_ref[...])
    rank = jnp.sum(jnp.where(onehot, before, 0.0), axis=0, keepdims=True)
    offs = [jnp.float32(0.0)]
    for g in range(N_EXPERT_GROUPS - 1):
        offs.append(offs[-1] + jnp.sum(jnp.where(g_idx == g, 1.0, 0.0)))
    offv = jnp.zeros_like(rank)
    for g in range(1, N_EXPERT_GROUPS):
        offv = offv + jnp.where(g_idx == g, offs[g], 0.0)
    pos = (rank + offv).astype(jnp.int32)
    dest = lax.broadcasted_iota(jnp.int32, (tile, tile), 0)
    perm = jnp.where(dest == pos, 1.0, 0.0).astype(BF16)

    row8 = lax.broadcasted_iota(jnp.int32, (8, tile), 0)
    j1 = (i1 - N_EXPERT_GROUPS) & (EXPERTS_PER_GROUP - 1)
    j2 = (i2 - N_EXPERT_GROUPS) & (EXPERTS_PER_GROUP - 1)
    gates = jnp.where(row8 == j1, w1, jnp.where(row8 == j2, w2, 0.0))
    gates_hi = gates.astype(BF16).astype(F32)
    gates_lo = (gates - gates_hi).astype(BF16).astype(F32)
    top = gates_hi + jnp.where(row8 == GROUP_LANE, g_idx.astype(F32), 0.0)
    side = jnp.concatenate([top, gates_lo, jnp.zeros((LANES - 2 * LOW_SHIFT, tile), F32)], axis=0).T
    x_ext = jnp.concatenate([hn_bf, side.astype(BF16)], axis=1)
    srt = _dot(perm, x_ext)
    xs_ref[...] = srt[:, :D_MODEL].astype(BF16)
    es = srt[:, D_MODEL:]
    es_ref[...] = es + pltpu.roll(es, LANES - LOW_SHIFT, axis=1)

    def group_of(row):
        g = jnp.int32(0)
        for k in range(1, N_EXPERT_GROUPS):
            g = g + (jnp.float32(row) >= offs[k]).astype(jnp.int32)
        return g

    def group_pass(rows, g, keep, accumulate):
        xb = xs_ref[rows, :]
        eb = es_ref[rows, :]
        mine = jnp.where(eb[:, GROUP_LANE:GROUP_LANE + 1] == jnp.asarray(g).astype(F32), keep, 0.0)
        parts = []
        for j in range(EXPERTS_PER_GROUP):
            e = g * EXPERTS_PER_GROUP + j
            gate = _dot(xb, wg_ref[e])
            up = _dot(xb, wu_ref[e])
            parts.append((_silu(gate) * up * (eb[:, j:j + 1] * mine)).astype(BF16))
        wd = wd_ref[pl.ds(g * EXPERTS_PER_GROUP, EXPERTS_PER_GROUP)]
        out = _dot(jnp.concatenate(parts, axis=1), wd.reshape(EXPERTS_PER_GROUP * D_EXPERT, D_MODEL))
        if accumulate:
            ys_ref[rows, :] += out
        else:
            ys_ref[rows, :] = out

    nblk = tile // SORT_BLOCK
    for b in range(nblk):
        group_pass(pl.ds(b * SORT_BLOCK, SORT_BLOCK), group_of(b * SORT_BLOCK), 1.0, False)
    for g in range(1, N_EXPERT_GROUPS):
        start = offs[g].astype(jnp.int32)
        blk = jnp.minimum(lax.div(start, jnp.int32(SORT_BLOCK)), nblk - 1)
        inside = jnp.where(lax.rem(start, jnp.int32(SORT_BLOCK)) != 0, 1.0, 0.0)
        group_pass(pl.ds(pl.multiple_of(blk * SORT_BLOCK, SORT_BLOCK), SORT_BLOCK), g, inside, True)
    return lax.dot_general(perm, ys_ref[...].astype(BF16), (((0,), (0,)), ((), ())),
                           preferred_element_type=F32)


def _merge(x, a_out, b_out, vec_ref, w_out_ref):
    g_out = _vec(vec_ref, VEC_OUT, D_MODEL)
    mix = jnp.concatenate(
        [_rms(a_out, g_out[:, :A_WIDTH]), _rms(b_out, g_out[:, A_WIDTH:])], axis=1).astype(BF16)
    return x + _dot(mix, w_out_ref[...])


def _stage_expert_weights(wg_hbm, wu_hbm, wd_hbm, wg_s, wu_s, wd_s, st_a, st_d, sems):
    def gate_cp(e):
        return pltpu.make_async_copy(wg_hbm.at[e], st_a.at[0], sems.at[0])

    def up_cp(e):
        return pltpu.make_async_copy(wu_hbm.at[e], st_a.at[1], sems.at[1])

    def down_cp(e, slot):
        return pltpu.make_async_copy(wd_hbm.at[e], st_d.at[slot], sems.at[2 + slot])

    gate_cp(0).start()
    up_cp(0).start(priority=1)
    down_cp(0, 0).start()

    def pair(p, carry):
        for slot in (0, 1):
            e = 2 * p + slot
            nxt = e + 1
            more = nxt < N_EXPERTS

            @pl.when(more)
            def _():
                down_cp(nxt, 1 - slot).start()

            gate_cp(e).wait()
            wg_s[e] = st_a[0].astype(BF16)

            @pl.when(more)
            def _():
                gate_cp(nxt).start()

            up_cp(e).wait()
            wu_s[e] = st_a[1].astype(BF16)

            @pl.when(more)
            def _():
                up_cp(nxt).start(priority=1)

            down_cp(e, slot).wait()
            wd_s[e] = st_d[slot].astype(BF16)
        return carry

    lax.fori_loop(0, N_EXPERTS // 2, pair, 0)


def _stage_dense_weights(w_in_hbm, w_out_hbm, w_in_s, w_out_s, st_a, st_d, sems):
    cols = st_a.shape[2]
    rows = st_d.shape[1]
    n_in = w_in_s.shape[1] // cols
    n_out = w_out_s.shape[0] // rows

    def in_cp(c):
        return pltpu.make_async_copy(w_in_hbm.at[:, pl.ds(c * cols, cols)], st_a.at[c % 2], sems.at[c % 2])

    def out_cp(c):
        return pltpu.make_async_copy(w_out_hbm.at[pl.ds(c * rows, rows), :], st_d.at[c % 2], sems.at[2 + c % 2])

    for c in range(min(2, n_in)):
        in_cp(c).start()
    for c in range(min(2, n_out)):
        out_cp(c).start()
    for c in range(max(n_in, n_out)):
        if c < n_in:
            in_cp(c).wait()
            w_in_s[:, c * cols:(c + 1) * cols] = st_a[c % 2].astype(BF16)
            if c + 2 < n_in:
                in_cp(c + 2).start()
        if c < n_out:
            out_cp(c).wait()
            w_out_s[c * rows:(c + 1) * rows, :] = st_d[c % 2].astype(BF16)
            if c + 2 < n_out:
                out_cp(c + 2).start()


N_STAGE_SEMS = 4


def _bf16_weight_copies(srcs, dsts, sems):
    return tuple(pltpu.make_async_copy(src, dst, sems.at[N_STAGE_SEMS + i])
                 for i, (src, dst) in enumerate(zip(srcs, dsts)))


def _prompt_kernel(x_ref, vec_ref, segq_ref, segk_ref, wsp_ref, bsp_ref, bias_ref, sinks_ref,
                   wrt_ref, brt_ref, ustrict_ref, w_in_hbm, w_out_hbm, wg_hbm, wu_hbm, wd_hbm,
                   y_ref, kwin_ref, vwin_ref, w_in_o, w_out_o, wg_o, wu_o, wd_o,
                   kprev_ref, vprev_ref, xs_ref, es_ref, ys_ref, w_in_ref, w_out_ref, wg_ref, wu_ref, wd_ref,
                   st_a, st_d, sems, *, tile, tiles_per_seq):
    step = pl.program_id(0)
    is_first = (step % tiles_per_seq) == 0

    kept = (w_in_ref, w_out_ref, wg_ref, wu_ref, wd_ref)
    kept_out = (w_in_o, w_out_o, wg_o, wu_o, wd_o)

    @pl.when(step == 0)
    def _():
        _stage_dense_weights(w_in_hbm, w_out_hbm, w_in_ref, w_out_ref, st_a, st_d, sems)
        _stage_expert_weights(wg_hbm, wu_hbm, wd_hbm, wg_ref, wu_ref, wd_ref, st_a, st_d, sems)
        for cp in _bf16_weight_copies(kept, kept_out, sems):
            cp.start()

    @pl.when(step == pl.num_programs(0) - 1)
    def _():
        for cp in _bf16_weight_copies(kept, kept_out, sems):
            cp.wait()

    @pl.when(is_first)
    def _():
        kprev_ref[...] = jnp.zeros_like(kprev_ref)
        vprev_ref[...] = jnp.zeros_like(vprev_ref)

    x = x_ref[...]
    u, va, q, k, v = _project(x, vec_ref, w_in_ref, segq_ref, segk_ref)
    kwin_ref[...] = k[tile - WINDOW:tile].T
    vwin_ref[...] = v[tile - WINDOW:tile].T
    a_out = u * _chunk_mlp(va.astype(BF16), wsp_ref, bsp_ref, tile)
    b_out = _swa_prompt(q * Q_SCALE, k, v, kprev_ref, vprev_ref, bias_ref, sinks_ref, is_first, tile)
    h = _merge(x, a_out, b_out, vec_ref, w_out_ref)
    y_ref[...] = h + _moe_grouped(h, vec_ref, wrt_ref, brt_ref, wg_ref, wu_ref, wd_ref, ustrict_ref,
                                  xs_ref, es_ref, ys_ref, tile)


def _sample_kernel(x_ref, ck_ref, cv_ref, vec_ref, w_in_ref, segq_ref, segk_ref, wsp_ref, bsp_ref,
                   bias_c_ref, bias_n_ref, sinks_ref, w_out_ref, wrt_ref, brt_ref, ustrict_ref,
                   wg_ref, wu_ref, wd_ref, y_ref, wk_ref, wv_ref, va_ref, xs_ref, es_ref, ys_ref,
                   *, tile, sub, nb, dec):
    hs = []
    for part in range(tile // sub):
        r0 = part * sub
        seqs = pl.ds(part * nb, nb)
        x = x_ref[r0:r0 + sub, :]
        u, va, q, k, v = _project(x, vec_ref, w_in_ref, segq_ref, segk_ref)
        for g in range(A_GROUPS):
            va_ref[pl.ds(r0 * A_GROUPS + g, sub, stride=A_GROUPS), :] = va[:, g * A_CH:(g + 1) * A_CH]
        a_out = u * _chunk_mlp(va.astype(BF16), wsp_ref, bsp_ref, sub)
        b_out = _swa_sample(q * Q_SCALE, k, v, ck_ref.at[seqs], cv_ref.at[seqs], wk_ref.at[seqs], wv_ref.at[seqs],
                            bias_c_ref, bias_n_ref, sinks_ref, nb, dec)
        hs.append(_merge(x, a_out, b_out, vec_ref, w_out_ref))
    h = jnp.concatenate(hs, axis=0)
    y_ref[...] = h + _moe_grouped(h, vec_ref, wrt_ref, brt_ref, wg_ref, wu_ref, wd_ref, ustrict_ref,
                                  xs_ref, es_ref, ys_ref, tile)


def _const_spec(shape):
    nd = len(shape)
    return pl.BlockSpec(shape, lambda i: (0,) * nd, pipeline_mode=pl.Buffered(1))


def _row_spec(tile, width):
    return pl.BlockSpec((tile, width), lambda i: (i, 0))


def _smem_spec():
    return pl.BlockSpec(memory_space=pltpu.SMEM)


def _layer_weights(l, g_attn_norm, g_v_a, g_q, g_k, g_out_a, g_out_b, g_ffn_norm,
                   w_coarse, b_coarse, w_fine, b_fine):
    row_pad = ROUTER_ROWS - N_EXPERT_GROUPS - N_EXPERTS
    wrt = jnp.concatenate([w_coarse[l].T, w_fine[l].T, jnp.zeros((row_pad, D_MODEL), F32)], axis=0)
    brt = jnp.concatenate([b_coarse[l], b_fine[l], jnp.zeros((row_pad,), F32)])
    vec = jnp.concatenate([
        g_attn_norm[l], g_v_a[l].reshape(A_WIDTH), jnp.tile(g_q[l], N_HEADS), jnp.tile(g_k[l], N_KV),
        g_out_a[l], g_out_b[l], g_ffn_norm[l]])
    return dict(
        vec=vec.reshape(1, VEC_LANES),
        segq=jnp.asarray(_seg_ones(B_WIDTH, HEAD_DIM), BF16),
        segk=jnp.asarray(_seg_ones(KV_WIDTH, HEAD_DIM), BF16),
        wrt=wrt.astype(BF16),
        brt=jnp.broadcast_to(brt[:, None], (ROUTER_ROWS, LANES)),
    )


def _spatial_tables(ws, bs, period, tile):
    pair = min(PAIR, tile)
    reps = pair // period
    wbd = (jnp.tile(ws[:, :period, :period], (1, reps, reps)) * _causal_block_mask(period, pair)).astype(BF16)
    bsp = jnp.repeat(bs[:, :period].T, A_CH, axis=1)
    return wbd, bsp


def _upper_triangle(n):
    return jnp.asarray(np.triu(np.ones((n, n), np.float32), 1), BF16)


def _run_prompt(x2d, lw, wbd, bsp, sinks, seq_len, w_in, w_out, w_gate, w_up, w_down):
    assert w_in.shape[0] == w_gate.shape[1] and w_in.shape[1] % w_gate.shape[2] == 0
    assert w_out.shape[1] == w_down.shape[2] and w_out.shape[0] % w_down.shape[1] == 0
    n_tok = x2d.shape[0]
    tile = PROMPT_TILE
    tiles_per_seq = seq_len // tile
    n_seq = n_tok // seq_len
    bias = jnp.asarray(_prompt_bias())
    tail = [lw["wrt"], lw["brt"], _upper_triangle(tile)]
    mats = [w_in, w_out, w_gate, w_up, w_down]
    consts = [lw["vec"], lw["segq"], lw["segk"], wbd, bsp, bias]
    any_spec = pl.BlockSpec(memory_space=pl.ANY)
    in_specs = ([_row_spec(tile, D_MODEL)] + [_const_spec(a.shape) for a in consts] + [_smem_spec()]
                + [_const_spec(a.shape) for a in tail] + [any_spec] * len(mats))
    win_spec = pl.BlockSpec((KV_WIDTH, WINDOW), lambda i: (i // tiles_per_seq, 0))
    out_shape = (jax.ShapeDtypeStruct((n_tok, D_MODEL), F32),
                 jax.ShapeDtypeStruct((n_seq * KV_WIDTH, WINDOW), F32),
                 jax.ShapeDtypeStruct((n_seq * KV_WIDTH, WINDOW), F32),
                 *[jax.ShapeDtypeStruct(w.shape, BF16) for w in mats])
    out_specs = (_row_spec(tile, D_MODEL), win_spec, win_spec) + (any_spec,) * len(mats)
    kern = functools.partial(_prompt_kernel, tile=tile, tiles_per_seq=tiles_per_seq)
    return pl.pallas_call(
        kern,
        out_shape=out_shape,
        grid=(n_tok // tile,),
        in_specs=in_specs,
        out_specs=out_specs,
        scratch_shapes=[pltpu.VMEM((N_KV, WINDOW, KV_WIDTH), BF16), pltpu.VMEM((N_KV, WINDOW, KV_WIDTH), BF16),
                        pltpu.VMEM((tile, D_MODEL), BF16), pltpu.VMEM((tile, LANES), F32),
                        pltpu.VMEM((tile, D_MODEL), F32),
                        *[pltpu.VMEM(w.shape, BF16) for w in mats],
                        pltpu.VMEM((2,) + w_gate.shape[1:], F32), pltpu.VMEM((2,) + w_down.shape[1:], F32),
                        pltpu.SemaphoreType.DMA((N_STAGE_SEMS + len(mats),))],
        compiler_params=pltpu.CompilerParams(dimension_semantics=("arbitrary",),
                                             vmem_limit_bytes=VMEM_LIMIT_BYTES),
        name="layer_prompt",
    )(x2d, *consts, sinks, *tail, *mats)


def _run_sample(x2d, ck_t, cv_t, lw, wbd, bsp, sinks, dec, w_in_bf, w_out_bf, wg_bf, wu_bf, wd_bf):
    n_tok = x2d.shape[0]
    n_seq = ck_t.shape[0]
    tile, sub = SAMPLE_TILE, SAMPLE_SUB
    nb = sub // dec
    bc, bn = _sample_bias(nb, dec)
    head = [lw["vec"], w_in_bf, lw["segq"], lw["segk"]]
    tail = [w_out_bf, lw["wrt"], lw["brt"], _upper_triangle(tile), wg_bf, wu_bf, wd_bf]
    consts = head + [wbd, bsp, jnp.asarray(bc), jnp.asarray(bn)]
    cache_spec = pl.BlockSpec((tile // dec, KV_WIDTH, WINDOW), lambda i: (i, 0, 0))
    in_specs = ([_row_spec(tile, D_MODEL), cache_spec, cache_spec] + [_const_spec(a.shape) for a in consts]
                + [_smem_spec()] + [_const_spec(a.shape) for a in tail])
    out_shape = (jax.ShapeDtypeStruct((n_tok, D_MODEL), F32),
                 jax.ShapeDtypeStruct((n_seq, KV_WIDTH, WINDOW), F32),
                 jax.ShapeDtypeStruct((n_seq, KV_WIDTH, WINDOW), F32),
                 jax.ShapeDtypeStruct((n_tok * A_GROUPS, A_CH), F32))
    out_specs = (_row_spec(tile, D_MODEL), cache_spec, cache_spec, _row_spec(tile * A_GROUPS, A_CH))
    kern = functools.partial(_sample_kernel, tile=tile, sub=sub, nb=nb, dec=dec)
    return pl.pallas_call(
        kern,
        out_shape=out_shape,
        grid=(n_tok // tile,),
        in_specs=in_specs,
        out_specs=out_specs,
        scratch_shapes=[pltpu.VMEM((tile, D_MODEL), BF16), pltpu.VMEM((tile, LANES), F32),
                        pltpu.VMEM((tile, D_MODEL), F32)],
        compiler_params=pltpu.CompilerParams(dimension_semantics=("arbitrary",),
                                             vmem_limit_bytes=VMEM_LIMIT_BYTES),
        name="layer_sample",
    )(x2d, ck_t, cv_t, *consts, sinks, *tail)


def _positions_last(c):
    b, w = c.shape[:2]
    return jnp.transpose(c, (0, 2, 3, 1)).reshape(b, KV_WIDTH, w)


def _positions_first(c_t):
    b, _, w = c_t.shape
    return jnp.transpose(c_t.reshape(b, N_KV, HEAD_DIM, w), (0, 3, 1, 2))


def kernel(x_prompt, x_sample, cache_k, cache_v, g_attn_norm, w_in, g_v_a, w_spatial, b_spatial, g_q, g_k, attn_sinks, g_out_a, g_out_b, w_out, g_ffn_norm, w_coarse, b_coarse, w_fine, b_fine, w_gate, w_up, w_down):
    depth = w_in.shape[0]
    batch, seq, _ = x_prompt.shape
    dbatch, dec, _ = x_sample.shape
    win = cache_k.shape[2]
    assert win == WINDOW and seq % PROMPT_TILE == 0 and (dbatch * dec) % SAMPLE_TILE == 0
    assert PAIR % dec == 0 and SAMPLE_SUB % dec == 0 and LANES % dec == 0

    hp = x_prompt.reshape(batch * seq, D_MODEL)
    hs = x_sample.reshape(dbatch * dec, D_MODEL)
    kp_l, vp_l, ks_l, vs_l, cv_l = [], [], [], [], []
    for l in range(depth):
        lw = _layer_weights(l, g_attn_norm, g_v_a, g_q, g_k, g_out_a, g_out_b, g_ffn_norm,
                            w_coarse, b_coarse, w_fine, b_fine)
        sinks = attn_sinks[l].astype(F32)
        wbd_p, bsp_p = _spatial_tables(w_spatial[l], b_spatial[l], CHUNK, PROMPT_TILE)
        wbd_s, bsp_s = _spatial_tables(w_spatial[l], b_spatial[l], dec, SAMPLE_SUB)

        hp, kp, vp, *weights_bf = _run_prompt(hp, lw, wbd_p, bsp_p, sinks, seq,
                                              w_in[l], w_out[l], w_gate[l], w_up[l], w_down[l])
        kp_l.append(_positions_first(kp.reshape(batch, KV_WIDTH, WINDOW)))
        vp_l.append(_positions_first(vp.reshape(batch, KV_WIDTH, WINDOW)))

        hs, wk, wv, va = _run_sample(hs, _positions_last(cache_k[l]), _positions_last(cache_v[l]),
                                     lw, wbd_s, bsp_s, sinks, dec, *weights_bf)
        ks_l.append(_positions_first(wk))
        vs_l.append(_positions_first(wv))
        cv_l.append(va.reshape(dbatch, dec, A_GROUPS, A_CH))

    return (hp.reshape(batch, seq, D_MODEL), hs.reshape(dbatch, dec, D_MODEL),
            jnp.stack(kp_l, axis=0), jnp.stack(vp_l, axis=0),
            jnp.stack(ks_l, axis=0), jnp.stack(vs_l, axis=0), jnp.stack(cv_l, axis=0))
```
